```python
import math
import jax, jax.numpy as jnp
from jax import lax
import numpy as np

D_MODEL = 2048
BATCH = 2
SEQ = 4096
DEPTH = 2

CTX_LEN = 256
GRID_W = 64
MIX_WIDTH = D_MODEL
N_MIXERS = 4
GROUP_W = MIX_WIDTH // N_MIXERS
S5_CH = 16
S5_GROUPS = GROUP_W // S5_CH
S5_STATE = 64
S5_DT_MIN = 1e-3
S5_DT_MAX = 1e-1
SG_HEADS = 4
SG_HEAD_DIM = GROUP_W // SG_HEADS
SG_CHUNK = 128
POOL_WINDOWS = (2, 4, 8, 16)
POOL_DIM = GROUP_W // len(POOL_WINDOWS)
M2_HEAD_DIM = 64
M2_HEADS = GROUP_W // M2_HEAD_DIM
M2_STATE = 128
M2_GROUPS = 2
M2_CONV = 4
M2_PAD = (M2_CONV // 2, M2_CONV - 1 - M2_CONV // 2)
M2_CHUNK = 128
M2_XBC = GROUP_W + 2 * M2_GROUPS * M2_STATE
M2_DT_MIN = 1e-3
M2_DT_MAX = 1e-1
IN_SIZES = (GROUP_W, GROUP_W, GROUP_W, GROUP_W, GROUP_W, M2_XBC, 2 * M2_HEADS)
IN_WIDTH = 5 * GROUP_W + M2_XBC + 2 * M2_HEADS
FFN_HIDDEN = ((8 * D_MODEL // 3 + 255) // 256) * 256
FFN_CONV = 3
DEEPNORM_ALPHA = (2 * DEPTH) ** 0.25
DEEPNORM_BETA = (8 * DEPTH) ** -0.25
LN_EPS = 1e-5
RMS_EPS = 1e-5

kernel_name = "hybrid_parallel_s5_gmlp_pool_ssd_dit"


def layer_norm(x, g, b):
    xf = x.astype(jnp.float32)
    mu = jnp.mean(xf, -1, keepdims=True)
    var = jnp.mean(jnp.square(xf - mu), -1, keepdims=True)
    return ((xf - mu) * lax.rsqrt(var + LN_EPS) * g + b).astype(x.dtype)


def modulate(x, shift, scale):
    return x * (1 + scale) + shift


def split_in(p):
    offs, o = [], 0
    for s in IN_SIZES[:-1]:
        o += s
        offs.append(o)
    return jnp.split(p, offs, axis=-1)


def dw_conv1d(x, w, b, pad):
    y = lax.conv_general_dilated(x, w[:, None, :].astype(x.dtype), window_strides=(1,), padding=[pad],
                                 dimension_numbers=('NWC', 'WIO', 'NWC'), feature_group_count=x.shape[-1])
    return y + b.astype(x.dtype)


def _lin_combine(e1, e2):
    a1, b1 = e1
    a2, b2 = e2
    return a2 * a1, a2 * b1 + b2


def s5_states(u, a_bar, b_bar, h0, reverse):
    bu = jnp.einsum('blgh,gph->blgp', u.astype(jnp.complex64), b_bar)
    a = jnp.broadcast_to(a_bar, (1, u.shape[1]) + a_bar.shape)
    a_cum, h = lax.associative_scan(_lin_combine, (a, bu), reverse=reverse, axis=1)
    return h + a_cum * h0[:, None]


def s5_mixer(u_ctx, u_lat, a_re, a_im, b_re, b_im, c_re, c_im, log_step, d, glu_w, glu_b, need_ctx):
    f32 = jnp.float32

    def groups(u):
        return u.astype(f32).reshape(u.shape[0], u.shape[1], S5_GROUPS, S5_CH)

    uc, ul = groups(u_ctx), groups(u_lat)
    ys_ctx, ys_lat = [], []
    for direction, reverse in enumerate((False, True)):
        lam = lax.complex(a_re[direction].astype(f32), a_im[direction].astype(f32))
        step = jnp.exp(log_step[direction].astype(f32))[:, None]
        a_bar = jnp.exp(lam * step)
        b_bar = ((a_bar - 1.0) / lam)[..., None] * lax.complex(b_re[direction].astype(f32),
                                                               b_im[direction].astype(f32))
        c_mat = lax.complex(c_re[direction].astype(f32), c_im[direction].astype(f32))
        h0 = jnp.zeros((uc.shape[0], S5_GROUPS, S5_STATE), jnp.complex64)
        hc = s5_states(uc, a_bar, b_bar, h0, reverse)
        hc_final = hc[:, 0] if reverse else hc[:, -1]
        hl = s5_states(ul, a_bar, b_bar, hc_final, reverse)
        ys_lat.append(jnp.einsum('blgp,ghp->blgh', hl, c_mat).real)
        if need_ctx:
            ys_ctx.append(jnp.einsum('blgp,ghp->blgh', hc, c_mat).real)

    def finish(ys, u, u_orig):
        y = (ys[0] + ys[1] + d.astype(f32).reshape(S5_GROUPS, S5_CH) * u).reshape(u_orig.shape)
        z = jax.nn.gelu(y)
        return (z * jax.nn.sigmoid(z @ glu_w.astype(f32) + glu_b.astype(f32))).astype(u_orig.dtype)

    y_lat = finish(ys_lat, ul, u_lat)
    y_ctx = finish(ys_ctx, uc, u_ctx) if need_ctx else None
    return y_ctx, y_lat


def chunk_gating(u, v, ln_g, ln_b, w_s, b_s):
    bsz, l, _ = u.shape
    u = jax.nn.gelu(u)
    v = jax.nn.gelu(v).reshape(bsz, l // SG_CHUNK, SG_CHUNK, SG_HEADS, SG_HEAD_DIM)
    v = layer_norm(v, ln_g.reshape(SG_HEADS, SG_HEAD_DIM), ln_b.reshape(SG_HEADS, SG_HEAD_DIM))
    s = jnp.einsum('hij,bcjhd->bcihd', w_s, v) + b_s.T[None, None, :, :, None]
    return u * s.reshape(bsz, l, GROUP_W)


def pool_mixer(p, w, bias, scale):
    bsz, l, _ = p.shape
    pf = p.astype(jnp.float32)
    csum = jnp.pad(jnp.cumsum(pf, axis=1), ((0, 0), (1, 0), (0, 0)))
    t = jnp.arange(l)
    outs = []
    for g, win in enumerate(POOL_WINDOWS):
        lo = jnp.clip(t - win // 2, 0, l - 1)
        hi = jnp.clip(t + win // 2 - 1, 0, l - 1)
        sl = slice(g * POOL_DIM, (g + 1) * POOL_DIM)
        cs = csum[..., sl]
        total = jnp.take(cs, hi + 1, axis=1) - jnp.take(cs, lo, axis=1)
        mean = total / (hi - lo + 1).astype(jnp.float32)[None, :, None]
        outs.append(jnp.einsum('blc,cd->bld', mean - pf[..., sl], w[g]))
    y = jnp.concatenate(outs, axis=-1) + bias
    return (y * scale).astype(p.dtype)


def m2_prepare(xbc, dt_raw, conv_w, conv_b, dt_bias):
    f32 = jnp.float32
    xbc = jax.nn.silu(dw_conv1d(xbc, conv_w, conv_b, M2_PAD)).astype(f32)
    bsz, l, _ = xbc.shape
    n_bc = M2_GROUPS * M2_STATE
    rep = M2_HEADS // M2_GROUPS
    xs = xbc[..., :GROUP_W].reshape(bsz, l, M2_HEADS, M2_HEAD_DIM)
    bm = jnp.repeat(xbc[..., GROUP_W:GROUP_W + n_bc].reshape(bsz, l, M2_GROUPS, M2_STATE), rep, axis=2)
    cm = jnp.repeat(xbc[..., GROUP_W + n_bc:].reshape(bsz, l, M2_GROUPS, M2_STATE), rep, axis=2)
    dt = jax.nn.softplus(dt_raw.astype(f32).reshape(bsz, l, 2, M2_HEADS) + dt_bias.astype(f32))
    return xs, bm, cm, dt


def ssd_scan(xs, dt, a, bm, cm, h0, need_y):
    bsz, l, nh, hp = xs.shape
    nc = l // M2_CHUNK

    def chunks(t):
        return t.reshape((bsz, nc, M2_CHUNK) + t.shape[2:])

    xd = chunks(xs * dt[..., None])
    bc, cc = chunks(bm), chunks(cm)
    a_cum = jnp.cumsum(chunks(dt * a), axis=2)
    a_tot = a_cum[:, :, -1]
    decay_end = jnp.exp(a_tot[:, :, None] - a_cum)
    chunk_states = jnp.einsum('bcqhn,bcqh,bcqhp->bchpn', bc, decay_end, xd)

    def step(s, inp):
        tot, st = inp
        return jnp.exp(tot)[..., None, None] * s + st, s

    h_final, h_prev = lax.scan(step, h0, (jnp.moveaxis(a_tot, 1, 0), jnp.moveaxis(chunk_states, 1, 0)))
    if not need_y:
        return None, h_final
    h_prev = jnp.moveaxis(h_prev, 0, 1)
    seg = a_cum[:, :, :, None, :] - a_cum[:, :, None, :, :]
    lower = jnp.tril(jnp.ones((M2_CHUNK, M2_CHUNK), bool))[None, None, :, :, None]
    decay = jnp.exp(jnp.where(lower, seg, -jnp.inf))
    scores = jnp.einsum('bcihn,bcjhn->bcijh', cc, bc) * decay
    y = (jnp.einsum('bcijh,bcjhp->bcihp', scores, xd)
         + jnp.einsum('bcihn,bchpn->bcihp', cc, h_prev) * jnp.exp(a_cum)[..., None])
    return y.reshape(bsz, l, nh, hp), h_final


def ssd_direction(inputs, direction, a, h0, need_y):
    xs, bm, cm, dt = inputs
    dt = dt[:, :, direction]
    if direction == 1:
        xs, bm, cm, dt = (jnp.flip(t, 1) for t in (xs, bm, cm, dt))
    y, h_final = ssd_scan(xs, dt, a, bm, cm, h0, need_y)
    if direction == 1 and y is not None:
        y = jnp.flip(y, 1)
    return y, h_final


def gated_rmsnorm(y, z, w):
    bsz, l, _ = z.shape
    g = (y * jax.nn.silu(z.astype(jnp.float32))).reshape(bsz, l, M2_GROUPS, GROUP_W // M2_GROUPS)
    g = g * lax.rsqrt(jnp.mean(jnp.square(g), -1, keepdims=True) + RMS_EPS)
    return (g.reshape(bsz, l, GROUP_W) * w.astype(jnp.float32)).astype(z.dtype)


def mamba2_mixer(z_ctx, xbc_ctx, dt_ctx, z_lat, xbc_lat, dt_lat, conv_w, conv_b, dt_bias, a_log, d,
                 norm_w, need_ctx):
    ctx_in = m2_prepare(xbc_ctx, dt_ctx, conv_w, conv_b, dt_bias)
    lat_in = m2_prepare(xbc_lat, dt_lat, conv_w, conv_b, dt_bias)
    a = -jnp.exp(a_log.astype(jnp.float32))
    ys_ctx, ys_lat = [], []
    for direction in range(2):
        h0 = jnp.zeros((ctx_in[0].shape[0], M2_HEADS, M2_HEAD_DIM, M2_STATE), jnp.float32)
        y_c, h_c = ssd_direction(ctx_in, direction, a[direction], h0, need_ctx)
        y_l, _ = ssd_direction(lat_in, direction, a[direction], h_c, True)
        ys_lat.append(y_l)
        if need_ctx:
            ys_ctx.append(y_c)
    d_h = d.astype(jnp.float32)[None, None, :, None]

    def finish(ys, xs, z):
        y = (ys[0] + ys[1] + d_h * xs).reshape(z.shape[0], z.shape[1], GROUP_W)
        return gated_rmsnorm(y, z, norm_w)

    y_lat = finish(ys_lat, lat_in[0], z_lat)
    y_ctx = finish(ys_ctx, ctx_in[0], z_ctx) if need_ctx else None
    return y_ctx, y_lat


def conv_ffn(h, w_up, conv_w, conv_b, w_down, rows):
    gate, val = jnp.split(h @ w_up, 2, axis=-1)
    bsz, l, f = gate.shape
    if rows is None:
        gate = dw_conv1d(gate, conv_w[FFN_CONV // 2], conv_b, (FFN_CONV // 2, FFN_CONV // 2))
    else:
        g2 = gate.reshape(bsz, rows, GRID_W, f)
        g2 = lax.conv_general_dilated(g2, conv_w[:, :, None, :].astype(g2.dtype), (1, 1), 'SAME',
                                      dimension_numbers=('NHWC', 'HWIO', 'NHWC'), feature_group_count=f)
        gate = g2.reshape(bsz, l, f) + conv_b.astype(gate.dtype)
    return (jax.nn.gelu(gate) * val) @ w_down


def setup_inputs(seed: int = 0) -> dict:
    key = jax.random.key(seed)
    ks = iter(jax.random.split(key, 64))
    f32 = jnp.float32
    L, D, F = DEPTH, D_MODEL, FFN_HIDDEN

    def nrm(shape, s):
        return s * jax.random.normal(next(ks), shape, f32)

    def unif(shape, lo, hi):
        return jax.random.uniform(next(ks), shape, f32, lo, hi)

    n_idx = jnp.arange(S5_STATE, dtype=f32)
    m2_dt = jnp.exp(unif((L, 2, M2_HEADS), math.log(M2_DT_MIN), math.log(M2_DT_MAX)))
    return {
        'x': nrm((BATCH, SEQ, D), 1.0),
        'c': nrm((BATCH, D), 1.0),
        'ctx': nrm((BATCH, CTX_LEN, D), 1.0),
        'c_ctx': nrm((D,), 1.0),
        'w_ada': nrm((L, D, 6 * D), 0.5 * D ** -0.5),
        'b_ada': nrm((L, 6 * D), 0.01),
        'w_in': nrm((L, D, IN_WIDTH), D ** -0.5),
        'w_out': nrm((L, MIX_WIDTH, D), DEEPNORM_BETA * MIX_WIDTH ** -0.5),
        'ln1_g': 1.0 + nrm((L, D), 0.02),
        'ln1_b': nrm((L, D), 0.02),
        'ln2_g': 1.0 + nrm((L, D), 0.02),
        'ln2_b': nrm((L, D), 0.02),
        's5_a_re': -0.5 + nrm((L, 2, S5_GROUPS, S5_STATE), 0.01),
        's5_a_im': math.pi * n_idx + nrm((L, 2, S5_GROUPS, S5_STATE), 0.01),
        's5_b_re': nrm((L, 2, S5_GROUPS, S5_STATE, S5_CH), (2 * S5_CH) ** -0.5),
        's5_b_im': nrm((L, 2, S5_GROUPS, S5_STATE, S5_CH), (2 * S5_CH) ** -0.5),
        's5_c_re': nrm((L, 2, S5_GROUPS, S5_CH, S5_STATE), 0.5),
        's5_c_im': nrm((L, 2, S5_GROUPS, S5_CH, S5_STATE), 0.5),
        's5_log_step': unif((L, 2, S5_GROUPS), math.log(S5_DT_MIN), math.log(S5_DT_MAX)),
        's5_d': nrm((L, GROUP_W), 1.0),
        's5_glu_w': nrm((L, GROUP_W, GROUP_W), GROUP_W ** -0.5),
        's5_glu_b': nrm((L, GROUP_W), 0.01),
        'sg_ln_g': 1.0 + nrm((L, GROUP_W), 0.02),
        'sg_ln_b': nrm((L, GROUP_W), 0.02),
        'sg_w': nrm((L, SG_HEADS, SG_CHUNK, SG_CHUNK), SG_CHUNK ** -0.5),
        'sg_b': 1.0 + nrm((L, SG_HEADS, SG_CHUNK), 0.01),
        'pool_w': nrm((L, len(POOL_WINDOWS), POOL_DIM, POOL_DIM), POOL_DIM ** -0.5),
        'pool_b': nrm((L, GROUP_W), 0.01),
        'pool_scale': 1.0 + nrm((L, GROUP_W), 0.02),
        'm2_conv_w': nrm((L, M2_CONV, M2_XBC), M2_CONV ** -0.5),
        'm2_conv_b': nrm((L, M2_XBC), 0.01),
        'm2_dt_bias': m2_dt + jnp.log(-jnp.expm1(-m2_dt)),
        'm2_a_log': jnp.log(unif((L, 2, M2_HEADS), 1.0, 16.0)),
        'm2_d': 1.0 + nrm((L, M2_HEADS), 0.01),
        'm2_norm_w': 1.0 + nrm((L, GROUP_W), 0.02),
        'ffn_w_up': nrm((L, D, 2 * F), D ** -0.5),
        'ffn_conv_w': nrm((L, FFN_CONV, FFN_CONV, F), 1.0 / FFN_CONV),
        'ffn_conv_b': nrm((L, F), 0.01),
        'ffn_w_down': nrm((L, F, D), DEEPNORM_BETA * F ** -0.5),
    }


def reference(x, c, ctx, c_ctx, w_ada, b_ada, w_in, w_out, ln1_g, ln1_b, ln2_g, ln2_b,
              s5_a_re, s5_a_im, s5_b_re, s5_b_im, s5_c_re, s5_c_im, s5_log_step, s5_d, s5_glu_w, s5_glu_b,
              sg_ln_g, sg_ln_b, sg_w, sg_b, pool_w, pool_b, pool_scale,
              m2_conv_w, m2_conv_b, m2_dt_bias, m2_a_log, m2_d, m2_norm_w,
              ffn_w_up, ffn_conv_w, ffn_conv_b, ffn_w_down):
    h_lat, h_ctx = x, ctx
    rows = h_lat.shape[1] // GRID_W
    silu_c = jax.nn.silu(c)[:, None, :]
    silu_cc = jax.nn.silu(c_ctx)[None, None, :]
    for i in range(DEPTH):
        need_ctx = i < DEPTH - 1
        m_lat = jnp.split(silu_c @ w_ada[i] + b_ada[i], 6, axis=-1)
        m_ctx = jnp.split(silu_cc @ w_ada[i] + b_ada[i], 6, axis=-1)

        pl = split_in(modulate(h_lat, m_lat[0], m_lat[1]) @ w_in[i])
        pc = split_in(modulate(h_ctx, m_ctx[0], m_ctx[1]) @ w_in[i])
        ya_ctx, ya_lat = s5_mixer(pc[0], pl[0], s5_a_re[i], s5_a_im[i], s5_b_re[i], s5_b_im[i],
                                  s5_c_re[i], s5_c_im[i], s5_log_step[i], s5_d[i], s5_glu_w[i],
                                  s5_glu_b[i], need_ctx)
        yd_ctx, yd_lat = mamba2_mixer(pc[4], pc[5], pc[6], pl[4], pl[5], pl[6], m2_conv_w[i], m2_conv_b[i],
                                      m2_dt_bias[i], m2_a_log[i], m2_d[i], m2_norm_w[i], need_ctx)
        mix_lat = jnp.concatenate([
            ya_lat,
            chunk_gating(pl[1], pl[2], sg_ln_g[i], sg_ln_b[i], sg_w[i], sg_b[i]),
            pool_mixer(pl[3], pool_w[i], pool_b[i], pool_scale[i]),
            yd_lat], axis=-1) @ w_out[i]
        h_lat = layer_norm(DEEPNORM_ALPHA * h_lat + m_lat[2] * mix_lat, ln1_g[i], ln1_b[i])
        if need_ctx:
            mix_ctx = jnp.concatenate([
                ya_ctx,
                chunk_gating(pc[1], pc[2], sg_ln_g[i], sg_ln_b[i], sg_w[i], sg_b[i]),
                pool_mixer(pc[3], pool_w[i], pool_b[i], pool_scale[i]),
                yd_ctx], axis=-1) @ w_out[i]
            h_ctx = layer_norm(DEEPNORM_ALPHA * h_ctx + m_ctx[2] * mix_ctx, ln1_g[i], ln1_b[i])

        f_lat = conv_ffn(modulate(h_lat, m_lat[3], m_lat[4]), ffn_w_up[i], ffn_conv_w[i], ffn_conv_b[i],
                         ffn_w_down[i], rows)
        h_lat = layer_norm(DEEPNORM_ALPHA * h_lat + m_lat[5] * f_lat, ln2_g[i], ln2_b[i])
        if need_ctx:
            f_ctx = conv_ffn(modulate(h_ctx, m_ctx[3], m_ctx[4]), ffn_w_up[i], ffn_conv_w[i], ffn_conv_b[i],
                             ffn_w_down[i], None)
            h_ctx = layer_norm(DEEPNORM_ALPHA * h_ctx + m_ctx[5] * f_ctx, ln2_g[i], ln2_b[i])
    return h_lat
```

```python
import functools
import math

import jax
import jax.numpy as jnp
from jax import lax
from jax.experimental import pallas as pl
from jax.experimental.pallas import tpu as pltpu

F32 = jnp.float32
BF16 = jnp.bfloat16
HIGHEST = lax.Precision.HIGHEST

GRID_W = 64
GROUP_W = 512
S5_CH = 16
S5_Q = 16
SG_HEADS = 4
SG_CHUNK = 128
POOL_WINDOWS = (2, 4, 8, 16)
POOL_DIM = GROUP_W // len(POOL_WINDOWS)
POOL_HALO = 8
M2_HEAD_DIM = 64
M2_HEADS = GROUP_W // M2_HEAD_DIM
M2_STATE = 128
M2_GROUPS = 2
M2_CONV = 4
M2_CHUNK = 128
M2_XBC = GROUP_W + 2 * M2_GROUPS * M2_STATE
M2_HALO = 8
LN_EPS = 1e-5
RMS_EPS = 1e-5

LANE = 128
SUBLANE = 8
VMEM_LIMIT = 56 * 1024 * 1024
ROW_TILE = 256
FFN_ROW_TILE = 512
FFN_COL_TILE = 512
ADA_COL_TILE = 1024
S5_GROUP_BLOCK = 4
MOD_ROWS = 8


def _cparams(sem):
    return pltpu.CompilerParams(dimension_semantics=sem, vmem_limit_bytes=VMEM_LIMIT)


def _bdot(a, b):
    return jnp.dot(a, b, preferred_element_type=F32)


def _hdot(a, b):
    return jnp.dot(a, b, preferred_element_type=F32, precision=HIGHEST)


def _gelu(x):
    return jax.nn.gelu(x)


def _silu(x):
    return x * jax.nn.sigmoid(x)


def _ada_kernel(c_ref, w_ref, b_ref, o_ref):
    c = c_ref[...]
    s = _silu(c).astype(BF16)
    o_ref[...] = _bdot(s, w_ref[...].astype(BF16)) + b_ref[...]


def _ada(cmat, w_ada, b_ada):
    depth, d, n = w_ada.shape
    tn = ADA_COL_TILE
    return pl.pallas_call(
        _ada_kernel,
        grid=(depth, n // tn),
        in_specs=[
            pl.BlockSpec((MOD_ROWS, d), lambda l, j: (0, 0)),
            pl.BlockSpec((None, d, tn), lambda l, j: (l, 0, j)),
            pl.BlockSpec((None, 1, tn), lambda l, j: (l, 0, j)),
        ],
        out_specs=pl.BlockSpec((None, MOD_ROWS, tn), lambda l, j: (l, 0, j)),
        out_shape=jax.ShapeDtypeStruct((depth, MOD_ROWS, n), F32),
        compiler_params=_cparams(("arbitrary", "arbitrary")),
        name="ada",
    )(cmat, w_ada, b_ada.reshape(depth, 1, n))


def _mod_spec(d, k, nbatch, nlat_blocks):
    return pl.BlockSpec((None, 1, d), lambda b, j, *_: (jnp.where(j >= nlat_blocks, nbatch, b), 0, k))


def _inproj_kernel(h_ref, sh_ref, sc_ref, wa_ref, wb_ref, wd_ref, oa_ref, ob_ref, od_ref):
    xm = (h_ref[...] * (1.0 + sc_ref[...]) + sh_ref[...]).astype(BF16)
    oa_ref[...] = _bdot(xm, wa_ref[...])
    ob_ref[...] = _bdot(xm, wb_ref[...])
    od_ref[...] = _bdot(xm, wd_ref[...])


def _inproj(h, mods, wa, wb, wd, nlat):
    nb, t, d = h.shape
    tm = ROW_TILE
    na, nbw, nd = wa.shape[1], wb.shape[1], wd.shape[1]

    def row(n):
        return pl.BlockSpec((None, tm, n), lambda b, j: (b, j, 0))

    def full(n):
        return pl.BlockSpec((d, n), lambda b, j: (0, 0))

    return pl.pallas_call(
        _inproj_kernel,
        grid=(nb, t // tm),
        in_specs=[row(d), _mod_spec(d, 0, nb, nlat // tm), _mod_spec(d, 1, nb, nlat // tm),
                  full(na), full(nbw), full(nd)],
        out_specs=[row(na), row(nbw), row(nd)],
        out_shape=[jax.ShapeDtypeStruct((nb, t, n), F32) for n in (na, nbw, nd)],
        compiler_params=_cparams(("arbitrary", "arbitrary")),
        name="inproj",
    )(h, mods, mods, wa, wb, wd)


def _s5_mats(a_re, a_im, b_re, b_im, c_re, c_im, log_step, d_skip):
    q, hch = S5_Q, S5_CH
    g, p = a_re.shape[1], a_re.shape[2]
    lam = lax.complex(a_re.astype(F32), a_im.astype(F32))
    step = jnp.exp(log_step.astype(F32))[..., None]
    a_bar = jnp.exp(lam * step)
    b_bar = ((a_bar - 1.0) / lam)[..., None] * lax.complex(b_re.astype(F32), b_im.astype(F32))
    c_mat = lax.complex(c_re.astype(F32), c_im.astype(F32))
    ks = jnp.arange(q + 1, dtype=F32)
    a_pow = jnp.exp(lam[None] * step[None] * ks[:, None, None, None])

    ca = c_mat[None] * a_pow[:, :, :, None, :]
    br, bi = jnp.real(b_bar), jnp.imag(b_bar)
    lagk = (jnp.einsum('kdgop,dgpi->dgkoi', jnp.real(ca[:q]), br, precision=HIGHEST)
            - jnp.einsum('kdgop,dgpi->dgkoi', jnp.imag(ca[:q]), bi, precision=HIGHEST))
    s_idx = jnp.arange(q)[:, None]
    t_idx = jnp.arange(q)[None, :]
    lag_f = t_idx - s_idx
    lag_r = s_idx - t_idx
    kf = jnp.where((lag_f >= 0)[None, :, :, None, None], lagk[0][:, jnp.clip(lag_f, 0, q - 1)], 0.0)
    kr = jnp.where((lag_r >= 0)[None, :, :, None, None], lagk[1][:, jnp.clip(lag_r, 0, q - 1)], 0.0)
    eye_t = (s_idx == t_idx).astype(F32)[None, :, :, None, None]
    dmat = jnp.eye(hch, dtype=F32)[None] * d_skip.astype(F32).reshape(g, hch)[:, :, None]
    tmat = kf + kr + eye_t * dmat[:, None, None]
    tmat = jnp.transpose(tmat, (0, 1, 4, 2, 3)).reshape(g, q * hch, q * hch)

    def state_in(direction, exps):
        m = a_pow[exps, direction][:, :, :, None] * b_bar[direction][None]
        m = jnp.transpose(m, (1, 0, 3, 2)).reshape(g, q * hch, p)
        return jnp.real(m), jnp.imag(m)

    sf_re, sf_im = state_in(0, q - 1 - jnp.arange(q))
    sr_re, sr_im = state_in(1, jnp.arange(q))
    w1 = jnp.concatenate([tmat, sf_re, sf_im, sf_im, sf_re, sr_re, sr_im, sr_im, sr_re], axis=-1)

    def state_out(direction, exps):
        m = ca[exps, direction]
        m = jnp.transpose(m, (1, 3, 0, 2)).reshape(g, p, q * hch)
        return jnp.real(m), -jnp.imag(m)

    of_re, of_im = state_out(0, 1 + jnp.arange(q))
    or_re, or_im = state_out(1, q - jnp.arange(q))
    w2 = jnp.concatenate([of_re, of_im, or_re, or_im], axis=1)

    aq = a_pow[q]
    a1 = jnp.concatenate([jnp.real(aq), jnp.real(aq)], axis=-1).reshape(2, g * 2 * p)
    a2 = jnp.concatenate([-jnp.imag(aq), jnp.imag(aq)], axis=-1).reshape(2, g * 2 * p)
    acoef = jnp.stack([a1[0], a2[0], a1[1], a2[1]], axis=0)
    return w1, w2, acoef


def _s5_kernel(x_ref, w1_ref, w2_ref, ac_ref, y_ref, s_scr, h_scr, *, nbatch, nrows, nctx, gb):
    qh = x_ref.shape[-1]
    sw = w2_ref.shape[1] // 2
    nlat = nrows - nctx
    for gi in range(gb):
        r = _hdot(x_ref[gi], w1_ref[gi])
        y_ref[gi] = r[:, :qh]
        for k in range(4):
            s_scr[k, :, gi * sw:(gi + 1) * sw] = r[:, qh + k * sw:qh + (k + 1) * sw]

    a1f, a2f, a1r, a2r = (ac_ref[k:k + 1, :] for k in range(4))
    zero = jnp.zeros((1, gb * sw), F32)

    def body(s, carry):
        rf = jnp.where(s < nctx, s + nlat, s - nctx)
        rr = nrows - 1 - s
        out = []
        for b in range(nbatch):
            hf, hfs, hr, hrs = carry[4 * b:4 * b + 4]
            f_row = b * nrows + rf
            r_row = b * nrows + rr
            h_scr[0, pl.ds(f_row, 1), :] = hf
            h_scr[1, pl.ds(r_row, 1), :] = hr
            nhf = a1f * hf + a2f * hfs + s_scr[0, pl.ds(f_row, 1), :]
            nhfs = a1f * hfs - a2f * hf + s_scr[1, pl.ds(f_row, 1), :]
            nhr = a1r * hr + a2r * hrs + s_scr[2, pl.ds(r_row, 1), :]
            nhrs = a1r * hrs - a2r * hr + s_scr[3, pl.ds(r_row, 1), :]
            out += [nhf, nhfs, nhr, nhrs]
        return tuple(out)

    lax.fori_loop(0, nrows, body, (zero,) * (4 * nbatch))

    for gi in range(gb):
        hin = jnp.concatenate([h_scr[0, :, gi * sw:(gi + 1) * sw], h_scr[1, :, gi * sw:(gi + 1) * sw]], axis=1)
        y_ref[gi] = y_ref[gi] + _hdot(hin, w2_ref[gi])


def _s5(pa, mats, nlat):
    w1, w2, acoef = mats
    nb, t, gw = pa.shape
    g = gw // S5_CH
    q = S5_Q
    nrows = t // q
    qh = q * S5_CH
    gb = S5_GROUP_BLOCK
    sw = w2.shape[1] // 2
    x = pa.reshape(nb, nrows, q, g, S5_CH).transpose(3, 0, 1, 2, 4).reshape(g, nb * nrows, qh)
    y = pl.pallas_call(
        functools.partial(_s5_kernel, nbatch=nb, nrows=nrows, nctx=(t - nlat) // q, gb=gb),
        grid=(g // gb,),
        in_specs=[
            pl.BlockSpec((gb, nb * nrows, qh), lambda i: (i, 0, 0)),
            pl.BlockSpec((gb, qh, w1.shape[2]), lambda i: (i, 0, 0)),
            pl.BlockSpec((gb, w2.shape[1], qh), lambda i: (i, 0, 0)),
            pl.BlockSpec((4, gb * sw), lambda i: (0, i)),
        ],
        out_specs=pl.BlockSpec((gb, nb * nrows, qh), lambda i: (i, 0, 0)),
        out_shape=jax.ShapeDtypeStruct((g, nb * nrows, qh), F32),
        scratch_shapes=[pltpu.VMEM((4, nb * nrows, gb * sw), F32), pltpu.VMEM((2, nb * nrows, gb * sw), F32)],
        compiler_params=_cparams(("arbitrary",)),
        name="s5",
    )(x, w1, w2, acoef)
    return y.reshape(g, nb, nrows, q, S5_CH).transpose(1, 2, 3, 0, 4).reshape(nb, t, gw)


def _mixers_kernel(ya_ref, pb_ref, prev_ref, next_ref, gluw_ref, glub_ref, lng_ref, lnb_ref, sgw_ref,
                   sgb_ref, pw_ref, pbias_ref, pscale_ref, o_ref, ext_scr, *, nlat_blocks, nblocks, nlat, nctx):
    j = pl.program_id(1)
    tm = ya_ref.shape[0]
    gw = GROUP_W

    z = _gelu(ya_ref[...])
    gate = jax.nn.sigmoid(_bdot(z.astype(BF16), gluw_ref[...]) + glub_ref[...])
    o_ref[:, 0:gw] = z * gate

    hd = gw // SG_HEADS
    u = _gelu(pb_ref[:, 0:gw])
    v = _gelu(pb_ref[:, gw:2 * gw])
    for hh in range(SG_HEADS):
        vh = v[:, hh * hd:(hh + 1) * hd]
        mu = jnp.mean(vh, axis=-1, keepdims=True)
        var = jnp.mean(jnp.square(vh - mu), axis=-1, keepdims=True)
        vn = ((vh - mu) * lax.rsqrt(var + LN_EPS) * lng_ref[:, hh * hd:(hh + 1) * hd]
              + lnb_ref[:, hh * hd:(hh + 1) * hd]).astype(BF16)
        for cchunk in range(tm // SG_CHUNK):
            rows = slice(cchunk * SG_CHUNK, (cchunk + 1) * SG_CHUNK)
            s = _bdot(sgw_ref[hh], vn[rows]) + sgb_ref[:, hh * hd:(hh + 1) * hd]
            o_ref[rows, gw + hh * hd:gw + (hh + 1) * hd] = u[rows, hh * hd:(hh + 1) * hd] * s

    is_first = jnp.logical_or(j == 0, j == nlat_blocks)
    is_last = jnp.logical_or(j == nlat_blocks - 1, j == nblocks - 1)
    p = pb_ref[:, 2 * gw:3 * gw]
    hal = POOL_HALO
    ext_scr[0:hal, :] = jnp.where(is_first, 0.0, prev_ref[...])
    ext_scr[hal:hal + tm, :] = p
    ext_scr[hal + tm:hal + tm + hal, :] = jnp.where(is_last, 0.0, next_ref[...])
    in_ctx = j >= nlat_blocks
    seq_len = jnp.where(in_ctx, nctx, nlat)
    t0 = (j - jnp.where(in_ctx, nlat_blocks, 0)) * tm
    tpos = t0 + lax.broadcasted_iota(jnp.int32, (tm, 1), 0)
    pd = POOL_DIM
    for gi, win in enumerate(POOL_WINDOWS):
        cols = slice(gi * pd, (gi + 1) * pd)
        tot = jnp.zeros((tm, pd), F32)
        for off in range(-(win // 2), win // 2):
            tot = tot + ext_scr[hal + off:hal + off + tm, cols]
        lo = jnp.maximum(tpos - win // 2, 0)
        hi = jnp.minimum(tpos + win // 2 - 1, seq_len - 1)
        mean = tot / (hi - lo + 1).astype(F32)
        yv = _bdot((mean - p[:, cols]).astype(BF16), pw_ref[gi]) + pbias_ref[:, cols]
        o_ref[:, 2 * gw + gi * pd:2 * gw + (gi + 1) * pd] = yv * pscale_ref[:, cols]


def _mixers(ya, pb, glu_w, glu_b, ln_g, ln_b, sg_w, sg_btile, pool_w, pool_b, pool_scale, nlat):
    nb, t, gw = ya.shape
    tm = ROW_TILE
    nblocks = t // tm
    nlat_blocks = nlat // tm
    hb = tm // POOL_HALO
    nh = t // POOL_HALO

    def vec():
        return pl.BlockSpec((1, gw), lambda b, j: (0, 0))

    return pl.pallas_call(
        functools.partial(_mixers_kernel, nlat_blocks=nlat_blocks, nblocks=nblocks, nlat=nlat, nctx=t - nlat),
        grid=(nb, nblocks),
        in_specs=[
            pl.BlockSpec((None, tm, gw), lambda b, j: (b, j, 0)),
            pl.BlockSpec((None, tm, 3 * gw), lambda b, j: (b, j, 0)),
            pl.BlockSpec((None, POOL_HALO, gw), lambda b, j: (b, jnp.maximum(j * hb - 1, 0), 2)),
            pl.BlockSpec((None, POOL_HALO, gw), lambda b, j: (b, jnp.minimum((j + 1) * hb, nh - 1), 2)),
            pl.BlockSpec((gw, gw), lambda b, j: (0, 0)),
            vec(), vec(), vec(),
            pl.BlockSpec((SG_HEADS, SG_CHUNK, SG_CHUNK), lambda b, j: (0, 0, 0)),
            pl.BlockSpec((SG_CHUNK, gw), lambda b, j: (0, 0)),
            pl.BlockSpec((len(POOL_WINDOWS), POOL_DIM, POOL_DIM), lambda b, j: (0, 0, 0)),
            vec(), vec(),
        ],
        out_specs=pl.BlockSpec((None, tm, 3 * gw), lambda b, j: (b, j, 0)),
        out_shape=jax.ShapeDtypeStruct((nb, t, 3 * gw), F32),
        scratch_shapes=[pltpu.VMEM((tm + 2 * POOL_HALO, gw), F32)],
        compiler_params=_cparams(("arbitrary", "arbitrary")),
        name="mixers",
    )(ya, pb, pb, pb, glu_w, glu_b, ln_g, ln_b, sg_w, sg_btile, pool_w, pool_b, pool_scale)


def _expand_heads(m, width, lane0):
    qn = m.shape[0]
    head_of_lane = lax.broadcasted_iota(jnp.int32, (qn, width), 1) // M2_HEAD_DIM
    out = jnp.zeros((qn, width), F32)
    for hh in range(width // M2_HEAD_DIM):
        out = jnp.where(head_of_lane == hh, jnp.broadcast_to(m[:, lane0 + hh:lane0 + hh + 1], (qn, width)), out)
    return out


def _ssd_kernel(*refs, reverse, nlat_chunks, nchunks):
    if reverse:
        (pd_ref, prev_ref, next_ref, cw_ref, cb_ref, dtb_ref, a_ref, dsk_ref, nw_ref, yf_ref,
         o_ref, ext_scr, st_scr) = refs
    else:
        (pd_ref, prev_ref, next_ref, cw_ref, cb_ref, dtb_ref, a_ref, dsk_ref,
         o_ref, ext_scr, st_scr) = refs
    k = pl.program_id(1)
    qn = M2_CHUNK
    gw = GROUP_W
    nx = M2_XBC
    hal = M2_HALO
    if reverse:
        chunk = nchunks - 1 - k
    else:
        chunk = jnp.where(k < nchunks - nlat_chunks, k + nlat_chunks, k - (nchunks - nlat_chunks))

    @pl.when(k == 0)
    def _():
        st_scr[...] = jnp.zeros_like(st_scr)

    is_first = jnp.logical_or(chunk == 0, chunk == nlat_chunks)
    is_last = jnp.logical_or(chunk == nlat_chunks - 1, chunk == nchunks - 1)
    ext_scr[0:hal, :] = jnp.where(is_first, 0.0, prev_ref[...])
    ext_scr[hal:hal + qn, :] = pd_ref[:, 0:nx]
    ext_scr[hal + qn:hal + qn + hal, :] = jnp.where(is_last, 0.0, next_ref[...])
    acc = jnp.zeros((qn, nx), F32) + cb_ref[...]
    for tap in range(M2_CONV):
        off = hal + tap - M2_CONV // 2
        acc = acc + ext_scr[off:off + qn, :] * cw_ref[tap:tap + 1, :]
    xbc = _silu(acc)
    xs = xbc[:, 0:gw]
    nbc = M2_GROUPS * M2_STATE
    bm = xbc[:, gw:gw + nbc]
    cm = xbc[:, gw + nbc:gw + 2 * nbc]

    lane0 = M2_HEADS if reverse else 0
    xdt = pd_ref[:, nx + gw:nx + gw + LANE] + dtb_ref[...]
    dt = jnp.maximum(xdt, 0.0) + jnp.log1p(jnp.exp(-jnp.abs(xdt)))
    da = dt * a_ref[...]
    ri = lax.broadcasted_iota(jnp.int32, (qn, qn), 0)
    ci = lax.broadcasted_iota(jnp.int32, (qn, qn), 1)
    causal = (ri <= ci) if reverse else (ri >= ci)
    a_col = _hdot(causal.astype(F32), da)
    a_row = a_col.T
    tot_row = a_col[0:1, :] if reverse else a_col[qn - 1:qn, :]

    dt_x = _expand_heads(dt, gw, lane0)
    ea_x = _expand_heads(jnp.exp(a_col), gw, lane0)
    de_x = _expand_heads(jnp.exp(tot_row - a_col), gw, lane0)
    et_x = _expand_heads(jnp.exp(tot_row), gw, lane0)
    xd = xs * dt_x
    xdw = (xd * de_x).astype(BF16)
    xdb = xd.astype(BF16)

    hpg = M2_HEADS // M2_GROUPS
    gcols = hpg * M2_HEAD_DIM
    for g in range(M2_GROUPS):
        bg = bm[:, g * M2_STATE:(g + 1) * M2_STATE]
        cg = cm[:, g * M2_STATE:(g + 1) * M2_STATE].astype(BF16)
        bgt = bg.T.astype(BF16)
        scores = _bdot(cg, bgt)
        st_old = st_scr[:, g * gcols:(g + 1) * gcols]
        y_inter = _bdot(cg, st_old.astype(BF16)) * ea_x[:, g * gcols:(g + 1) * gcols]
        st_new = _bdot(bgt, xdw[:, g * gcols:(g + 1) * gcols])
        st_scr[:, g * gcols:(g + 1) * gcols] = et_x[:, g * gcols:(g + 1) * gcols] * st_old + st_new
        for hl in range(hpg):
            hh = g * hpg + hl
            seg = a_col[:, lane0 + hh:lane0 + hh + 1] - a_row[lane0 + hh:lane0 + hh + 1, :]
            decay = jnp.exp(jnp.where(causal, seg, -jnp.inf))
            cols = slice(hh * M2_HEAD_DIM, (hh + 1) * M2_HEAD_DIM)
            y_h = _bdot((scores * decay).astype(BF16), xdb[:, cols])
            y_h = y_h + y_inter[:, hl * M2_HEAD_DIM:(hl + 1) * M2_HEAD_DIM]
            if reverse:
                o_ref[:, cols] = y_h + yf_ref[:, cols]
            else:
                o_ref[:, cols] = y_h + dsk_ref[:, cols] * xs[:, cols]

    if reverse:
        zg = pd_ref[:, nx:nx + gw]
        gv = o_ref[...] * _silu(zg)
        gc = gw // M2_GROUPS
        for g in range(M2_GROUPS):
            part = gv[:, g * gc:(g + 1) * gc]
            ms = jnp.mean(jnp.square(part), axis=-1, keepdims=True)
            o_ref[:, g * gc:(g + 1) * gc] = part * lax.rsqrt(ms + RMS_EPS) * nw_ref[:, g * gc:(g + 1) * gc]


def _ssd(pd, conv_w, conv_b, dt_bias, a_log, d_skip, norm_w, nlat):
    nb, t, wd = pd.shape
    qn = M2_CHUNK
    nchunks = t // qn
    nlat_chunks = nlat // qn
    hb = qn // M2_HALO
    nh = t // M2_HALO
    gw = GROUP_W
    a = -jnp.exp(a_log.astype(F32))
    dsk = jnp.repeat(d_skip.astype(F32), M2_HEAD_DIM).reshape(1, gw)

    def chunk_of(k, reverse):
        if reverse:
            return nchunks - 1 - k
        return jnp.where(k < nchunks - nlat_chunks, k + nlat_chunks, k - (nchunks - nlat_chunks))

    def call(reverse, yf):
        def rows(n):
            return pl.BlockSpec((None, qn, n), lambda b, k: (b, chunk_of(k, reverse), 0))

        def const(shape):
            return pl.BlockSpec(shape, lambda b, k: tuple(0 for _ in shape))

        in_specs = [
            rows(wd),
            pl.BlockSpec((None, M2_HALO, M2_XBC),
                         lambda b, k: (b, jnp.maximum(chunk_of(k, reverse) * hb - 1, 0), 0)),
            pl.BlockSpec((None, M2_HALO, M2_XBC),
                         lambda b, k: (b, jnp.minimum((chunk_of(k, reverse) + 1) * hb, nh - 1), 0)),
            const((M2_CONV, M2_XBC)), const((1, M2_XBC)), const((1, LANE)), const((1, LANE)),
            const((1, gw)),
        ]
        d = 1 if reverse else 0
        lane_pad = ((0, 0), (0, LANE - 2 * M2_HEADS))
        dtb = jnp.pad(dt_bias.astype(F32).reshape(1, 2 * M2_HEADS), lane_pad)
        a_dir = jnp.pad(jnp.where(jnp.arange(2)[:, None] == d, a, 0.0).reshape(1, 2 * M2_HEADS), lane_pad)
        args = [pd, pd, pd, conv_w.astype(F32), conv_b.astype(F32).reshape(1, M2_XBC), dtb, a_dir, dsk]
        if reverse:
            in_specs += [const((1, gw)), rows(gw)]
            args += [norm_w.astype(F32).reshape(1, gw), yf]
        return pl.pallas_call(
            functools.partial(_ssd_kernel, reverse=reverse, nlat_chunks=nlat_chunks, nchunks=nchunks),
            grid=(nb, nchunks),
            in_specs=in_specs,
            out_specs=rows(gw),
            out_shape=jax.ShapeDtypeStruct((nb, t, gw), F32),
            scratch_shapes=[pltpu.VMEM((qn + 2 * M2_HALO, M2_XBC), F32), pltpu.VMEM((M2_STATE, gw), F32)],
            compiler_params=_cparams(("arbitrary", "arbitrary")),
            name="ssd_rev" if reverse else "ssd_fwd",
        )(*args)

    return call(True, call(False, None))


def _layer_norm_rows(v, g, b):
    mu = jnp.mean(v, axis=-1, keepdims=True)
    var = jnp.mean(jnp.square(v - mu), axis=-1, keepdims=True)
    return (v - mu) * lax.rsqrt(var + LN_EPS) * g + b


def _outproj_kernel(mabc_ref, md_ref, h_ref, gate_ref, w1_ref, w2_ref, lg_ref, lb_ref, o_ref, *, alpha):
    mix = _bdot(mabc_ref[...].astype(BF16), w1_ref[...]) + _bdot(md_ref[...].astype(BF16), w2_ref[...])
    o_ref[...] = _layer_norm_rows(alpha * h_ref[...] + gate_ref[...] * mix, lg_ref[...], lb_ref[...])


def _outproj(mabc, md, h, mods, w_abc, w_d, ln_g, ln_b, nlat, rows_used, alpha):
    nb, t, d = h.shape
    tm = ROW_TILE
    k1, k2 = mabc.shape[2], md.shape[2]

    def row(n):
        return pl.BlockSpec((None, tm, n), lambda b, j: (b, j, 0))

    def const(shape):
        return pl.BlockSpec(shape, lambda b, j: (0, 0))

    return pl.pallas_call(
        functools.partial(_outproj_kernel, alpha=alpha),
        grid=(nb, rows_used // tm),
        in_specs=[row(k1), row(k2), row(d), _mod_spec(d, 2, nb, nlat // tm),
                  const((k1, d)), const((k2, d)), const((1, d)), const((1, d))],
        out_specs=row(d),
        out_shape=jax.ShapeDtypeStruct((nb, rows_used, d), F32),
        compiler_params=_cparams(("arbitrary", "arbitrary")),
        name="outproj",
    )(mabc, md, h, mods, w_abc, w_d, ln_g, ln_b)


def _ffn_kernel(*refs, alpha, grid_conv, halo):
    if grid_conv:
        (h_ref, prev_ref, next_ref, sh_ref, sc_ref, gate_ref, wg_ref, wv_ref, cw_ref, cb_ref, wd_ref,
         lg_ref, lb_ref, o_ref, xm_scr, g_scr, acc_scr) = refs
    else:
        (h_ref, sh_ref, sc_ref, gate_ref, wg_ref, wv_ref, cw_ref, cb_ref, wd_ref,
         lg_ref, lb_ref, o_ref, xm_scr, g_scr, acc_scr) = refs
    j = pl.program_id(1)
    f = pl.program_id(2)
    nj = pl.num_programs(1)
    nf = pl.num_programs(2)
    tm = h_ref.shape[0]
    pad = SUBLANE

    @pl.when(f == 0)
    def _():
        def modulated(v):
            return (v * (1.0 + sc_ref[...]) + sh_ref[...]).astype(BF16)
        xm_scr[halo:halo + tm, :] = modulated(h_ref[...])
        if grid_conv:
            xm_scr[0:halo, :] = modulated(prev_ref[...])
            xm_scr[halo + tm:halo + tm + halo, :] = modulated(next_ref[...])
        acc_scr[...] = jnp.zeros_like(acc_scr)
        g_scr[0:pad, :] = jnp.zeros((pad, g_scr.shape[1]), F32)
        g_scr[pad + tm + 2 * halo:pad + tm + 2 * halo + pad, :] = jnp.zeros((pad, g_scr.shape[1]), F32)

    gate_ext = _bdot(xm_scr[...], wg_ref[...])
    val = _bdot(xm_scr[halo:halo + tm, :], wv_ref[...])
    g_scr[pad:pad + tm + 2 * halo, :] = gate_ext
    if grid_conv:
        @pl.when(j == 0)
        def _():
            g_scr[pad:pad + halo, :] = jnp.zeros((halo, g_scr.shape[1]), F32)

        @pl.when(j == nj - 1)
        def _():
            g_scr[pad + halo + tm:pad + halo + tm + halo, :] = jnp.zeros((halo, g_scr.shape[1]), F32)

    base = pad + halo
    col = lax.broadcasted_iota(jnp.int32, (tm, 1), 0) % GRID_W
    conv = jnp.zeros((tm, g_scr.shape[1]), F32) + cb_ref[...]
    if grid_conv:
        for dr in (-1, 0, 1):
            for dc in (-1, 0, 1):
                off = base + dr * GRID_W + dc
                tap = g_scr[off:off + tm, :] * cw_ref[(dr + 1) * 3 + (dc + 1):(dr + 1) * 3 + (dc + 1) + 1, :]
                if dc == -1:
                    tap = jnp.where(col == 0, 0.0, tap)
                elif dc == 1:
                    tap = jnp.where(col == GRID_W - 1, 0.0, tap)
                conv = conv + tap
    else:
        for dc in (-1, 0, 1):
            off = base + dc
            conv = conv + g_scr[off:off + tm, :] * cw_ref[3 + (dc + 1):3 + (dc + 1) + 1, :]
    act = (_gelu(conv) * val).astype(BF16)
    acc_scr[...] += _bdot(act, wd_ref[...])

    @pl.when(f == nf - 1)
    def _():
        o_ref[...] = _layer_norm_rows(alpha * h_ref[...] + gate_ref[...] * acc_scr[...], lg_ref[...], lb_ref[...])


def _ffn(h, mods, w_up, conv_w9, conv_b, w_down, ln_g, ln_b, nlat, alpha, grid_conv, out_rows, prev_out=None):
    nb, t, d = h.shape
    fh = w_down.shape[0]
    tf = FFN_COL_TILE
    nf = fh // tf
    if grid_conv:
        tm, halo = FFN_ROW_TILE, GRID_W
        nblk, blk0 = nlat // tm, 0
    else:
        tm, halo = t - nlat, 0
        nblk, blk0 = 1, nlat // tm
    hb = tm // GRID_W
    nhalo_blocks = nlat // GRID_W

    def row(n):
        return pl.BlockSpec((None, tm, n), lambda b, j, f: (b, j + blk0, 0))

    def vec(k):
        return pl.BlockSpec((None, 1, d), lambda b, j, f: (nb if not grid_conv else b, 0, k))

    def const(shape):
        return pl.BlockSpec(shape, lambda b, j, f: (0, 0))

    in_specs = [row(d)]
    args = [h]
    if grid_conv:
        in_specs += [
            pl.BlockSpec((None, halo, d), lambda b, j, f: (b, jnp.maximum(j * hb - 1, 0), 0)),
            pl.BlockSpec((None, halo, d), lambda b, j, f: (b, jnp.minimum((j + 1) * hb, nhalo_blocks - 1), 0)),
        ]
        args += [h, h]
    in_specs += [
        vec(3), vec(4), vec(5),
        pl.BlockSpec((d, tf), lambda b, j, f: (0, f)),
        pl.BlockSpec((d, tf), lambda b, j, f: (0, nf + f)),
        pl.BlockSpec((9, tf), lambda b, j, f: (0, f)),
        pl.BlockSpec((1, tf), lambda b, j, f: (0, f)),
        pl.BlockSpec((tf, d), lambda b, j, f: (f, 0)),
        const((1, d)), const((1, d)),
    ]
    args += [mods, mods, mods, w_up, w_up, conv_w9, conv_b, w_down, ln_g, ln_b]
    aliases = {}
    if prev_out is not None:
        in_specs.append(pl.BlockSpec(memory_space=pl.ANY))
        args.append(prev_out)
        aliases = {len(args) - 1: 0}

    kern = functools.partial(_ffn_kernel, alpha=alpha, grid_conv=grid_conv, halo=halo)
    if prev_out is not None:
        inner = kern

        def kern(*refs):
            nin = len(args)
            inner(*refs[:nin - 1], *refs[nin:])

    return pl.pallas_call(
        kern,
        grid=(nb, nblk, nf),
        in_specs=in_specs,
        out_specs=row(d),
        out_shape=jax.ShapeDtypeStruct((nb, out_rows, d), F32),
        scratch_shapes=[pltpu.VMEM((tm + 2 * halo, d), BF16),
                        pltpu.VMEM((tm + 2 * halo + 2 * SUBLANE, tf), F32),
                        pltpu.VMEM((tm, d), F32)],
        input_output_aliases=aliases,
        compiler_params=_cparams(("arbitrary", "arbitrary", "arbitrary")),
        name="ffn_lat" if grid_conv else "ffn_ctx",
    )(*args)


def kernel(x, c, ctx, c_ctx, w_ada, b_ada, w_in, w_out, ln1_g, ln1_b, ln2_g, ln2_b, s5_a_re, s5_a_im, s5_b_re, s5_b_im, s5_c_re, s5_c_im, s5_log_step, s5_d, s5_glu_w, s5_glu_b, sg_ln_g, sg_ln_b, sg_w, sg_b, pool_w, pool_b, pool_scale, m2_conv_w, m2_conv_b, m2_dt_bias, m2_a_log, m2_d, m2_norm_w, ffn_w_up, ffn_conv_w, ffn_conv_b, ffn_w_down):
    nb, nlat, d = x.shape
    nctx = ctx.shape[1]
    depth = w_ada.shape[0]
    gw = GROUP_W
    alpha = (2 * depth) ** 0.25
    assert nb + 1 <= MOD_ROWS and nlat % FFN_ROW_TILE == 0 and nctx % ROW_TILE == 0
    assert nlat % (GRID_W * SUBLANE) == 0 and FFN_ROW_TILE % GRID_W == 0

    h = jnp.concatenate([x, ctx], axis=1)
    cmat = jnp.concatenate([c, c_ctx[None, :], jnp.zeros((MOD_ROWS - nb - 1, d), F32)], axis=0)
    mods_all = _ada(cmat, w_ada, b_ada)

    for i in range(depth):
        need_ctx = i < depth - 1
        mods = mods_all[i].reshape(MOD_ROWS, 1, 6 * d)

        wi = w_in[i].astype(BF16)
        o = [0, gw, 2 * gw, 3 * gw, 4 * gw, 5 * gw, 5 * gw + M2_XBC, 5 * gw + M2_XBC + 2 * M2_HEADS]
        wa = wi[:, o[0]:o[1]]
        wb = wi[:, o[1]:o[4]]
        wd = jnp.concatenate([wi[:, o[5]:o[6]], wi[:, o[4]:o[5]], wi[:, o[6]:o[7]],
                              jnp.zeros((d, LANE - 2 * M2_HEADS), BF16)], axis=1)
        pa, pb, pd = _inproj(h, mods, wa, wb, wd, nlat)

        ya = _s5(pa, _s5_mats(s5_a_re[i], s5_a_im[i], s5_b_re[i], s5_b_im[i], s5_c_re[i], s5_c_im[i],
                              s5_log_step[i], s5_d[i]), nlat)
        sg_btile = jnp.repeat(sg_b[i].astype(F32).T, gw // SG_HEADS, axis=1)
        mabc = _mixers(ya, pb, s5_glu_w[i].astype(BF16), s5_glu_b[i].reshape(1, gw),
                       sg_ln_g[i].reshape(1, gw), sg_ln_b[i].reshape(1, gw), sg_w[i].astype(BF16), sg_btile,
                       pool_w[i].astype(BF16), pool_b[i].reshape(1, gw), pool_scale[i].reshape(1, gw), nlat)
        md = _ssd(pd, m2_conv_w[i], m2_conv_b[i], m2_dt_bias[i], m2_a_log[i], m2_d[i], m2_norm_w[i], nlat)

        wo = w_out[i].astype(BF16)
        rows_used = nlat + nctx if need_ctx else nlat
        h1 = _outproj(mabc, md, h, mods, wo[:3 * gw], wo[3 * gw:], ln1_g[i].reshape(1, d),
                      ln1_b[i].reshape(1, d), nlat, rows_used, alpha)

        w_up = ffn_w_up[i].astype(BF16)
        w_down = ffn_w_down[i].astype(BF16)
        fh = w_down.shape[0]
        cw9 = ffn_conv_w[i].astype(F32).reshape(9, fh)
        cb = ffn_conv_b[i].astype(F32).reshape(1, fh)
        lg, lb = ln2_g[i].reshape(1, d), ln2_b[i].reshape(1, d)
        if need_ctx:
            h2 = _ffn(h1, mods, w_up, cw9, cb, w_down, lg, lb, nlat, alpha, True, nlat + nctx)
            h = _ffn(h1, mods, w_up, cw9, cb, w_down, lg, lb, nlat, alpha, False, nlat + nctx, prev_out=h2)
        else:
            h = _ffn(h1, mods, w_up, cw9, cb, w_down, lg, lb, nlat, alpha, True, nlat)
    return h
```

```python
import functools
import math

import jax
import jax.numpy as jnp
from jax import lax
from jax.experimental import pallas as pl
from jax.experimental.pallas import tpu as pltpu

F32 = jnp.float32
BF16 = jnp.bfloat16
HIGHEST = lax.Precision.HIGHEST

GRID_W = 64
GROUP_W = 512
S5_CH = 16
S5_Q = 16
SG_HEADS = 4
SG_CHUNK = 128
POOL_WINDOWS = (2, 4, 8, 16)
POOL_DIM = GROUP_W // len(POOL_WINDOWS)
POOL_HALO = 8
M2_HEAD_DIM = 64
M2_HEADS = GROUP_W // M2_HEAD_DIM
M2_STATE = 128
M2_GROUPS = 2
M2_CONV = 4
M2_CHUNK = 128
M2_XBC = GROUP_W + 2 * M2_GROUPS * M2_STATE
M2_HALO = 8
LN_EPS = 1e-5
RMS_EPS = 1e-5

LANE = 128
SUBLANE = 8
VMEM_LIMIT = 56 * 1024 * 1024
ROW_TILE = 256
FFN_ROW_TILE = 512
FFN_COL_TILE = 512
FFN_SUB_TILE = 256
ADA_COL_TILE = 1024
S5_GROUP_BLOCK = 4
MOD_ROWS = 8


def _cparams(sem):
    return pltpu.CompilerParams(dimension_semantics=sem, vmem_limit_bytes=VMEM_LIMIT)


def _bdot(a, b):
    return jnp.dot(a, b, preferred_element_type=F32)


def _hdot(a, b):
    return jnp.dot(a, b, preferred_element_type=F32, precision=HIGHEST)


def _gelu(x):
    return jax.nn.gelu(x)


def _silu(x):
    return x * jax.nn.sigmoid(x)


def _ada_kernel(c_ref, w_ref, b_ref, o_ref):
    c = c_ref[...]
    s = _silu(c).astype(BF16)
    o_ref[...] = _bdot(s, w_ref[...].astype(BF16)) + b_ref[...]


def _ada(cmat, w_ada, b_ada):
    depth, d, n = w_ada.shape
    tn = ADA_COL_TILE
    return pl.pallas_call(
        _ada_kernel,
        grid=(depth, n // tn),
        in_specs=[
            pl.BlockSpec((MOD_ROWS, d), lambda l, j: (0, 0)),
            pl.BlockSpec((None, d, tn), lambda l, j: (l, 0, j)),
            pl.BlockSpec((None, 1, tn), lambda l, j: (l, 0, j)),
        ],
        out_specs=pl.BlockSpec((None, MOD_ROWS, tn), lambda l, j: (l, 0, j)),
        out_shape=jax.ShapeDtypeStruct((depth, MOD_ROWS, n), F32),
        compiler_params=_cparams(("arbitrary", "arbitrary")),
        name="ada",
    )(cmat, w_ada, b_ada.reshape(depth, 1, n))


def _mod_spec(d, k, nbatch, nlat_blocks):
    return pl.BlockSpec((None, 1, d), lambda b, j, *_: (jnp.where(j >= nlat_blocks, nbatch, b), 0, k))


def _stream_specs(tm, d, nlat_blocks):
    lat = pl.BlockSpec((None, tm, d), lambda b, j: (b, jnp.minimum(j, nlat_blocks - 1), 0))
    ctx = pl.BlockSpec((None, tm, d), lambda b, j: (b, jnp.maximum(j - nlat_blocks, 0), 0))
    return lat, ctx


def _inproj_kernel(hl_ref, hc_ref, sh_ref, sc_ref, w_ref, oa_ref, ob_ref, od_ref, xm_scr, *, nlat_blocks):
    j = pl.program_id(1)

    def modulated(h_ref):
        return (h_ref[...] * (1.0 + sc_ref[...]) + sh_ref[...]).astype(BF16)

    @pl.when(j < nlat_blocks)
    def _():
        xm_scr[...] = modulated(hl_ref)

    @pl.when(j >= nlat_blocks)
    def _():
        xm_scr[...] = modulated(hc_ref)

    xm = xm_scr[...]
    gw = GROUP_W
    oa_ref[...] = _bdot(xm, w_ref[:, 0:gw])
    ob_ref[...] = _bdot(xm, w_ref[:, gw:4 * gw])
    od_ref[:, 0:M2_XBC] = _bdot(xm, w_ref[:, 5 * gw:5 * gw + M2_XBC])
    od_ref[:, M2_XBC:M2_XBC + gw] = _bdot(xm, w_ref[:, 4 * gw:5 * gw])
    od_ref[:, M2_XBC + gw:M2_XBC + gw + LANE] = _bdot(xm, w_ref[:, 5 * gw + M2_XBC:5 * gw + M2_XBC + LANE])


def _inproj(h_lat, h_ctx, mods, w):
    nb, nlat, d = h_lat.shape
    t = nlat + h_ctx.shape[1]
    tm = ROW_TILE
    na, nbw, nd = GROUP_W, 3 * GROUP_W, M2_XBC + GROUP_W + LANE
    lat_spec, ctx_spec = _stream_specs(tm, d, nlat // tm)

    def row(n):
        return pl.BlockSpec((None, tm, n), lambda b, j: (b, j, 0))

    return pl.pallas_call(
        functools.partial(_inproj_kernel, nlat_blocks=nlat // tm),
        grid=(nb, t // tm),
        in_specs=[lat_spec, ctx_spec, _mod_spec(d, 0, nb, nlat // tm), _mod_spec(d, 1, nb, nlat // tm),
                  pl.BlockSpec(w.shape, lambda b, j: (0, 0))],
        out_specs=[row(na), row(nbw), row(nd)],
        out_shape=[jax.ShapeDtypeStruct((nb, t, n), F32) for n in (na, nbw, nd)],
        scratch_shapes=[pltpu.VMEM((tm, d), BF16)],
        compiler_params=_cparams(("arbitrary", "arbitrary")),
        name="inproj",
    )(h_lat, h_ctx, mods, mods, w)


def _s5_mats(a_re, a_im, b_re, b_im, c_re, c_im, log_step, d_skip):
    q, hch = S5_Q, S5_CH
    g, p = a_re.shape[1], a_re.shape[2]
    lam = lax.complex(a_re.astype(F32), a_im.astype(F32))
    step = jnp.exp(log_step.astype(F32))[..., None]
    a_bar = jnp.exp(lam * step)
    b_bar = ((a_bar - 1.0) / lam)[..., None] * lax.complex(b_re.astype(F32), b_im.astype(F32))
    c_mat = lax.complex(c_re.astype(F32), c_im.astype(F32))
    ks = jnp.arange(q + 1, dtype=F32)
    a_pow = jnp.exp(lam[None] * step[None] * ks[:, None, None, None])

    ca = c_mat[None] * a_pow[:, :, :, None, :]
    br, bi = jnp.real(b_bar), jnp.imag(b_bar)
    lagk = (jnp.einsum('kdgop,dgpi->dgkoi', jnp.real(ca[:q]), br, precision=HIGHEST)
            - jnp.einsum('kdgop,dgpi->dgkoi', jnp.imag(ca[:q]), bi, precision=HIGHEST))
    s_idx = jnp.arange(q)[:, None]
    t_idx = jnp.arange(q)[None, :]
    lag_f = t_idx - s_idx
    lag_r = s_idx - t_idx
    kf = jnp.where((lag_f >= 0)[None, :, :, None, None], lagk[0][:, jnp.clip(lag_f, 0, q - 1)], 0.0)
    kr = jnp.where((lag_r >= 0)[None, :, :, None, None], lagk[1][:, jnp.clip(lag_r, 0, q - 1)], 0.0)
    eye_t = (s_idx == t_idx).astype(F32)[None, :, :, None, None]
    dmat = jnp.eye(hch, dtype=F32)[None] * d_skip.astype(F32).reshape(g, hch)[:, :, None]
    tmat = kf + kr + eye_t * dmat[:, None, None]
    tmat = jnp.transpose(tmat, (0, 1, 4, 2, 3)).reshape(g, q * hch, q * hch)

    def state_in(direction, exps):
        m = a_pow[exps, direction][:, :, :, None] * b_bar[direction][None]
        m = jnp.transpose(m, (1, 0, 3, 2)).reshape(g, q * hch, p)
        return jnp.real(m), jnp.imag(m)

    sf_re, sf_im = state_in(0, q - 1 - jnp.arange(q))
    sr_re, sr_im = state_in(1, jnp.arange(q))
    w1 = jnp.concatenate([tmat, sf_re, sf_im, sf_im, sf_re, sr_re, sr_im, sr_im, sr_re], axis=-1)

    def state_out(direction, exps):
        m = ca[exps, direction]
        m = jnp.transpose(m, (1, 3, 0, 2)).reshape(g, p, q * hch)
        return jnp.real(m), -jnp.imag(m)

    of_re, of_im = state_out(0, 1 + jnp.arange(q))
    or_re, or_im = state_out(1, q - jnp.arange(q))
    w2 = jnp.concatenate([of_re, of_im, or_re, or_im], axis=1)

    aq = a_pow[q]
    a1 = jnp.concatenate([jnp.real(aq), jnp.real(aq)], axis=-1).reshape(2, g * 2 * p)
    a2 = jnp.concatenate([-jnp.imag(aq), jnp.imag(aq)], axis=-1).reshape(2, g * 2 * p)
    acoef = jnp.stack([a1[0], a2[0], a1[1], a2[1]], axis=0)
    return w1, w2, acoef


def _s5_kernel(x_ref, w1_ref, w2_ref, ac_ref, y_ref, s_scr, h_scr, *, nbatch, nrows, nctx, gb):
    qh = x_ref.shape[-1]
    sw = w2_ref.shape[1] // 2
    nlat = nrows - nctx
    for gi in range(gb):
        r = _hdot(x_ref[gi], w1_ref[gi])
        y_ref[gi] = r[:, :qh]
        for k in range(4):
            s_scr[k, :, gi * sw:(gi + 1) * sw] = r[:, qh + k * sw:qh + (k + 1) * sw]

    a1f, a2f, a1r, a2r = (ac_ref[k:k + 1, :] for k in range(4))
    zero = jnp.zeros((1, gb * sw), F32)

    def body(s, carry):
        rf = jnp.where(s < nctx, s + nlat, s - nctx)
        rr = nrows - 1 - s
        out = []
        for b in range(nbatch):
            hf, hfs, hr, hrs = carry[4 * b:4 * b + 4]
            f_row = b * nrows + rf
            r_row = b * nrows + rr
            h_scr[0, pl.ds(f_row, 1), :] = hf
            h_scr[1, pl.ds(r_row, 1), :] = hr
            nhf = a1f * hf + a2f * hfs + s_scr[0, pl.ds(f_row, 1), :]
            nhfs = a1f * hfs - a2f * hf + s_scr[1, pl.ds(f_row, 1), :]
            nhr = a1r * hr + a2r * hrs + s_scr[2, pl.ds(r_row, 1), :]
            nhrs = a1r * hrs - a2r * hr + s_scr[3, pl.ds(r_row, 1), :]
            out += [nhf, nhfs, nhr, nhrs]
        return tuple(out)

    lax.fori_loop(0, nrows, body, (zero,) * (4 * nbatch))

    for gi in range(gb):
        hin = jnp.concatenate([h_scr[0, :, gi * sw:(gi + 1) * sw], h_scr[1, :, gi * sw:(gi + 1) * sw]], axis=1)
        y_ref[gi] = y_ref[gi] + _hdot(hin, w2_ref[gi])


def _s5(pa, mats, nlat):
    w1, w2, acoef = mats
    nb, t, gw = pa.shape
    g = gw // S5_CH
    q = S5_Q
    nrows = t // q
    qh = q * S5_CH
    gb = S5_GROUP_BLOCK
    sw = w2.shape[1] // 2
    x = pa.reshape(nb, nrows, q, g, S5_CH).transpose(3, 0, 1, 2, 4).reshape(g, nb * nrows, qh)
    y = pl.pallas_call(
        functools.partial(_s5_kernel, nbatch=nb, nrows=nrows, nctx=(t - nlat) // q, gb=gb),
        grid=(g // gb,),
        in_specs=[
            pl.BlockSpec((gb, nb * nrows, qh), lambda i: (i, 0, 0)),
            pl.BlockSpec((gb, qh, w1.shape[2]), lambda i: (i, 0, 0)),
            pl.BlockSpec((gb, w2.shape[1], qh), lambda i: (i, 0, 0)),
            pl.BlockSpec((4, gb * sw), lambda i: (0, i)),
        ],
        out_specs=pl.BlockSpec((gb, nb * nrows, qh), lambda i: (i, 0, 0)),
        out_shape=jax.ShapeDtypeStruct((g, nb * nrows, qh), F32),
        scratch_shapes=[pltpu.VMEM((4, nb * nrows, gb * sw), F32), pltpu.VMEM((2, nb * nrows, gb * sw), F32)],
        compiler_params=_cparams(("arbitrary",)),
        name="s5",
    )(x, w1, w2, acoef)
    return y.reshape(g, nb, nrows, q, S5_CH).transpose(1, 2, 3, 0, 4).reshape(nb, t, gw)


def _mixers_kernel(ya_ref, pb_ref, prev_ref, next_ref, gluw_ref, glub_ref, lng_ref, lnb_ref, sgw_ref,
                   sgb_ref, pw_ref, pbias_ref, pscale_ref, o_ref, ext_scr, *, nlat_blocks, nblocks, nlat, nctx):
    j = pl.program_id(1)
    tm = ya_ref.shape[0]
    gw = GROUP_W

    z = _gelu(ya_ref[...])
    gate = jax.nn.sigmoid(_bdot(z.astype(BF16), gluw_ref[...]) + glub_ref[...])
    o_ref[:, 0:gw] = z * gate

    hd = gw // SG_HEADS
    u = _gelu(pb_ref[:, 0:gw])
    v = _gelu(pb_ref[:, gw:2 * gw])
    for hh in range(SG_HEADS):
        vh = v[:, hh * hd:(hh + 1) * hd]
        mu = jnp.mean(vh, axis=-1, keepdims=True)
        var = jnp.mean(jnp.square(vh - mu), axis=-1, keepdims=True)
        vn = ((vh - mu) * lax.rsqrt(var + LN_EPS) * lng_ref[:, hh * hd:(hh + 1) * hd]
              + lnb_ref[:, hh * hd:(hh + 1) * hd]).astype(BF16)
        for cchunk in range(tm // SG_CHUNK):
            rows = slice(cchunk * SG_CHUNK, (cchunk + 1) * SG_CHUNK)
            s = _bdot(sgw_ref[hh], vn[rows]) + sgb_ref[:, hh * hd:(hh + 1) * hd]
            o_ref[rows, gw + hh * hd:gw + (hh + 1) * hd] = u[rows, hh * hd:(hh + 1) * hd] * s

    is_first = jnp.logical_or(j == 0, j == nlat_blocks)
    is_last = jnp.logical_or(j == nlat_blocks - 1, j == nblocks - 1)
    p = pb_ref[:, 2 * gw:3 * gw]
    hal = POOL_HALO
    ext_scr[0:hal, :] = jnp.where(is_first, 0.0, prev_ref[...])
    ext_scr[hal:hal + tm, :] = p
    ext_scr[hal + tm:hal + tm + hal, :] = jnp.where(is_last, 0.0, next_ref[...])
    in_ctx = j >= nlat_blocks
    seq_len = jnp.where(in_ctx, nctx, nlat)
    t0 = (j - jnp.where(in_ctx, nlat_blocks, 0)) * tm
    tpos = t0 + lax.broadcasted_iota(jnp.int32, (tm, 1), 0)
    pd = POOL_DIM
    for gi, win in enumerate(POOL_WINDOWS):
        cols = slice(gi * pd, (gi + 1) * pd)
        tot = jnp.zeros((tm, pd), F32)
        for off in range(-(win // 2), win // 2):
            tot = tot + ext_scr[hal + off:hal + off + tm, cols]
        lo = jnp.maximum(tpos - win // 2, 0)
        hi = jnp.minimum(tpos + win // 2 - 1, seq_len - 1)
        mean = tot / (hi - lo + 1).astype(F32)
        yv = _bdot((mean - p[:, cols]).astype(BF16), pw_ref[gi]) + pbias_ref[:, cols]
        o_ref[:, 2 * gw + gi * pd:2 * gw + (gi + 1) * pd] = yv * pscale_ref[:, cols]


def _mixers(ya, pb, glu_w, glu_b, ln_g, ln_b, sg_w, sg_btile, pool_w, pool_b, pool_scale, nlat):
    nb, t, gw = ya.shape
    tm = ROW_TILE
    nblocks = t // tm
    nlat_blocks = nlat // tm
    hb = tm // POOL_HALO
    nh = t // POOL_HALO

    def vec():
        return pl.BlockSpec((1, gw), lambda b, j: (0, 0))

    return pl.pallas_call(
        functools.partial(_mixers_kernel, nlat_blocks=nlat_blocks, nblocks=nblocks, nlat=nlat, nctx=t - nlat),
        grid=(nb, nblocks),
        in_specs=[
            pl.BlockSpec((None, tm, gw), lambda b, j: (b, j, 0)),
            pl.BlockSpec((None, tm, 3 * gw), lambda b, j: (b, j, 0)),
            pl.BlockSpec((None, POOL_HALO, gw), lambda b, j: (b, jnp.maximum(j * hb - 1, 0), 2)),
            pl.BlockSpec((None, POOL_HALO, gw), lambda b, j: (b, jnp.minimum((j + 1) * hb, nh - 1), 2)),
            pl.BlockSpec((gw, gw), lambda b, j: (0, 0)),
            vec(), vec(), vec(),
            pl.BlockSpec((SG_HEADS, SG_CHUNK, SG_CHUNK), lambda b, j: (0, 0, 0)),
            pl.BlockSpec((SG_CHUNK, gw), lambda b, j: (0, 0)),
            pl.BlockSpec((len(POOL_WINDOWS), POOL_DIM, POOL_DIM), lambda b, j: (0, 0, 0)),
            vec(), vec(),
        ],
        out_specs=pl.BlockSpec((None, tm, 3 * gw), lambda b, j: (b, j, 0)),
        out_shape=jax.ShapeDtypeStruct((nb, t, 3 * gw), F32),
        scratch_shapes=[pltpu.VMEM((tm + 2 * POOL_HALO, gw), F32)],
        compiler_params=_cparams(("arbitrary", "arbitrary")),
        name="mixers",
    )(ya, pb, pb, pb, glu_w, glu_b, ln_g, ln_b, sg_w, sg_btile, pool_w, pool_b, pool_scale)


def _expand_heads(m, width, lane0):
    qn = m.shape[0]
    head_of_lane = lax.broadcasted_iota(jnp.int32, (qn, width), 1) // M2_HEAD_DIM
    out = jnp.zeros((qn, width), F32)
    for hh in range(width // M2_HEAD_DIM):
        out = jnp.where(head_of_lane == hh, jnp.broadcast_to(m[:, lane0 + hh:lane0 + hh + 1], (qn, width)), out)
    return out


def _ssd_kernel(*refs, reverse, nlat_chunks, nchunks):
    if reverse:
        (pd_ref, prev_ref, next_ref, cw_ref, cb_ref, dtb_ref, a_ref, dsk_ref, nw_ref, yf_ref,
         o_ref, ext_scr, st_scr) = refs
    else:
        (pd_ref, prev_ref, next_ref, cw_ref, cb_ref, dtb_ref, a_ref, dsk_ref,
         o_ref, ext_scr, st_scr) = refs
    k = pl.program_id(1)
    qn = M2_CHUNK
    gw = GROUP_W
    nx = M2_XBC
    hal = M2_HALO
    if reverse:
        chunk = nchunks - 1 - k
    else:
        chunk = jnp.where(k < nchunks - nlat_chunks, k + nlat_chunks, k - (nchunks - nlat_chunks))

    @pl.when(k == 0)
    def _():
        st_scr[...] = jnp.zeros_like(st_scr)

    is_first = jnp.logical_or(chunk == 0, chunk == nlat_chunks)
    is_last = jnp.logical_or(chunk == nlat_chunks - 1, chunk == nchunks - 1)
    ext_scr[0:hal, :] = jnp.where(is_first, 0.0, prev_ref[...])
    ext_scr[hal:hal + qn, :] = pd_ref[:, 0:nx]
    ext_scr[hal + qn:hal + qn + hal, :] = jnp.where(is_last, 0.0, next_ref[...])
    acc = jnp.zeros((qn, nx), F32) + cb_ref[...]
    for tap in range(M2_CONV):
        off = hal + tap - M2_CONV // 2
        acc = acc + ext_scr[off:off + qn, :] * cw_ref[tap:tap + 1, :]
    xbc = _silu(acc)
    xs = xbc[:, 0:gw]
    nbc = M2_GROUPS * M2_STATE
    bm = xbc[:, gw:gw + nbc]
    cm = xbc[:, gw + nbc:gw + 2 * nbc]

    lane0 = M2_HEADS if reverse else 0
    xdt = pd_ref[:, nx + gw:nx + gw + LANE] + dtb_ref[...]
    dt = jnp.maximum(xdt, 0.0) + jnp.log1p(jnp.exp(-jnp.abs(xdt)))
    da = dt * a_ref[...]
    ri = lax.broadcasted_iota(jnp.int32, (qn, qn), 0)
    ci = lax.broadcasted_iota(jnp.int32, (qn, qn), 1)
    causal = (ri <= ci) if reverse else (ri >= ci)
    a_col = _hdot(causal.astype(F32), da)
    a_row = a_col.T
    tot_row = a_col[0:1, :] if reverse else a_col[qn - 1:qn, :]

    dt_x = _expand_heads(dt, gw, lane0)
    ea_x = _expand_heads(jnp.exp(a_col), gw, lane0)
    de_x = _expand_heads(jnp.exp(tot_row - a_col), gw, lane0)
    et_x = _expand_heads(jnp.exp(tot_row), gw, lane0)
    xd = xs * dt_x
    xdw = (xd * de_x).astype(BF16)
    xdb = xd.astype(BF16)

    hpg = M2_HEADS // M2_GROUPS
    gcols = hpg * M2_HEAD_DIM
    for g in range(M2_GROUPS):
        bg = bm[:, g * M2_STATE:(g + 1) * M2_STATE]
        cg = cm[:, g * M2_STATE:(g + 1) * M2_STATE].astype(BF16)
        bgt = bg.T.astype(BF16)
        scores = _bdot(cg, bgt)
        st_old = st_scr[:, g * gcols:(g + 1) * gcols]
        y_inter = _bdot(cg, st_old.astype(BF16)) * ea_x[:, g * gcols:(g + 1) * gcols]
        st_new = _bdot(bgt, xdw[:, g * gcols:(g + 1) * gcols])
        st_scr[:, g * gcols:(g + 1) * gcols] = et_x[:, g * gcols:(g + 1) * gcols] * st_old + st_new
        for hl in range(hpg):
            hh = g * hpg + hl
            seg = a_col[:, lane0 + hh:lane0 + hh + 1] - a_row[lane0 + hh:lane0 + hh + 1, :]
            decay = jnp.exp(jnp.where(causal, seg, -jnp.inf))
            cols = slice(hh * M2_HEAD_DIM, (hh + 1) * M2_HEAD_DIM)
            y_h = _bdot((scores * decay).astype(BF16), xdb[:, cols])
            y_h = y_h + y_inter[:, hl * M2_HEAD_DIM:(hl + 1) * M2_HEAD_DIM]
            if reverse:
                o_ref[:, cols] = y_h + yf_ref[:, cols]
            else:
                o_ref[:, cols] = y_h + dsk_ref[:, cols] * xs[:, cols]

    if reverse:
        zg = pd_ref[:, nx:nx + gw]
        gv = o_ref[...] * _silu(zg)
        gc = gw // M2_GROUPS
        for g in range(M2_GROUPS):
            part = gv[:, g * gc:(g + 1) * gc]
            ms = jnp.mean(jnp.square(part), axis=-1, keepdims=True)
            o_ref[:, g * gc:(g + 1) * gc] = part * lax.rsqrt(ms + RMS_EPS) * nw_ref[:, g * gc:(g + 1) * gc]


def _ssd(pd, conv_w, conv_b, dt_bias, a_log, d_skip, norm_w, nlat):
    nb, t, wd = pd.shape
    qn = M2_CHUNK
    nchunks = t // qn
    nlat_chunks = nlat // qn
    hb = qn // M2_HALO
    nh = t // M2_HALO
    gw = GROUP_W
    a = -jnp.exp(a_log.astype(F32))
    dsk = jnp.repeat(d_skip.astype(F32), M2_HEAD_DIM).reshape(1, gw)

    def chunk_of(k, reverse):
        if reverse:
            return nchunks - 1 - k
        return jnp.where(k < nchunks - nlat_chunks, k + nlat_chunks, k - (nchunks - nlat_chunks))

    def call(reverse, yf):
        def rows(n):
            return pl.BlockSpec((None, qn, n), lambda b, k: (b, chunk_of(k, reverse), 0))

        def const(shape):
            return pl.BlockSpec(shape, lambda b, k: tuple(0 for _ in shape))

        in_specs = [
            rows(wd),
            pl.BlockSpec((None, M2_HALO, M2_XBC),
                         lambda b, k: (b, jnp.maximum(chunk_of(k, reverse) * hb - 1, 0), 0)),
            pl.BlockSpec((None, M2_HALO, M2_XBC),
                         lambda b, k: (b, jnp.minimum((chunk_of(k, reverse) + 1) * hb, nh - 1), 0)),
            const((M2_CONV, M2_XBC)), const((1, M2_XBC)), const((1, LANE)), const((1, LANE)),
            const((1, gw)),
        ]
        d = 1 if reverse else 0
        lane_pad = ((0, 0), (0, LANE - 2 * M2_HEADS))
        dtb = jnp.pad(dt_bias.astype(F32).reshape(1, 2 * M2_HEADS), lane_pad)
        a_dir = jnp.pad(jnp.where(jnp.arange(2)[:, None] == d, a, 0.0).reshape(1, 2 * M2_HEADS), lane_pad)
        args = [pd, pd, pd, conv_w.astype(F32), conv_b.astype(F32).reshape(1, M2_XBC), dtb, a_dir, dsk]
        if reverse:
            in_specs += [const((1, gw)), rows(gw)]
            args += [norm_w.astype(F32).reshape(1, gw), yf]
        return pl.pallas_call(
            functools.partial(_ssd_kernel, reverse=reverse, nlat_chunks=nlat_chunks, nchunks=nchunks),
            grid=(nb, nchunks),
            in_specs=in_specs,
            out_specs=rows(gw),
            out_shape=jax.ShapeDtypeStruct((nb, t, gw), F32),
            scratch_shapes=[pltpu.VMEM((qn + 2 * M2_HALO, M2_XBC), F32), pltpu.VMEM((M2_STATE, gw), F32)],
            compiler_params=_cparams(("arbitrary", "arbitrary")),
            name="ssd_rev" if reverse else "ssd_fwd",
        )(*args)

    return call(True, call(False, None))


def _layer_norm_rows(v, g, b):
    mu = jnp.mean(v, axis=-1, keepdims=True)
    var = jnp.mean(jnp.square(v - mu), axis=-1, keepdims=True)
    return (v - mu) * lax.rsqrt(var + LN_EPS) * g + b


def _outproj_kernel(mabc_ref, md_ref, hl_ref, hc_ref, gate_ref, w_ref, lg_ref, lb_ref, o_ref,
                    *, alpha, nlat_blocks):
    j = pl.program_id(1)
    k1 = mabc_ref.shape[1]
    mix = (_bdot(mabc_ref[...].astype(BF16), w_ref[0:k1, :])
           + _bdot(md_ref[...].astype(BF16), w_ref[k1:w_ref.shape[0], :]))
    gated = gate_ref[...] * mix

    @pl.when(j < nlat_blocks)
    def _():
        o_ref[...] = _layer_norm_rows(alpha * hl_ref[...] + gated, lg_ref[...], lb_ref[...])

    @pl.when(j >= nlat_blocks)
    def _():
        o_ref[...] = _layer_norm_rows(alpha * hc_ref[...] + gated, lg_ref[...], lb_ref[...])


def _outproj(mabc, md, h_lat, h_ctx, mods, w, ln_g, ln_b, rows_used, alpha):
    nb, nlat, d = h_lat.shape
    tm = ROW_TILE
    k1, k2 = mabc.shape[2], md.shape[2]
    lat_spec, ctx_spec = _stream_specs(tm, d, nlat // tm)

    def row(n):
        return pl.BlockSpec((None, tm, n), lambda b, j: (b, j, 0))

    def const(shape):
        return pl.BlockSpec(shape, lambda b, j: (0, 0))

    return pl.pallas_call(
        functools.partial(_outproj_kernel, alpha=alpha, nlat_blocks=nlat // tm),
        grid=(nb, rows_used // tm),
        in_specs=[row(k1), row(k2), lat_spec, ctx_spec, _mod_spec(d, 2, nb, nlat // tm),
                  const((k1 + k2, d)), const((1, d)), const((1, d))],
        out_specs=row(d),
        out_shape=jax.ShapeDtypeStruct((nb, rows_used, d), F32),
        compiler_params=_cparams(("arbitrary", "arbitrary")),
        name="outproj",
    )(mabc, md, h_lat, h_ctx, mods, w, ln_g, ln_b)


def _ffn_kernel(*refs, alpha, grid_conv, halo):
    if grid_conv:
        (h_ref, prev_ref, next_ref, sh_ref, sc_ref, gate_ref, wg_ref, wv_ref, cw_ref, cb_ref, wd_ref,
         lg_ref, lb_ref, o_ref, xm_scr, g_scr, v_scr, side_scr, acc_scr) = refs
    else:
        (h_ref, sh_ref, sc_ref, gate_ref, wg_ref, wv_ref, cw_ref, cb_ref, wd_ref,
         lg_ref, lb_ref, o_ref, xm_scr, g_scr, v_scr, side_scr, acc_scr) = refs
    j = pl.program_id(1)
    f = pl.program_id(2)
    nj = pl.num_programs(1)
    nf = pl.num_programs(2)
    tm = h_ref.shape[0]
    sub = g_scr.shape[2]
    nsub = g_scr.shape[0]
    pad = SUBLANE

    @pl.when(f == 0)
    def _():
        def modulated(v):
            return (v * (1.0 + sc_ref[...]) + sh_ref[...]).astype(BF16)
        xm_scr[halo:halo + tm, :] = modulated(h_ref[...])
        if grid_conv:
            xm_scr[0:halo, :] = modulated(prev_ref[...])
            xm_scr[halo + tm:halo + tm + halo, :] = modulated(next_ref[...])
        acc_scr[...] = jnp.zeros_like(acc_scr)
        side_scr[:, :, 0:pad, :] = jnp.zeros((nsub, 2, pad, sub), F32)
        side_scr[:, :, pad + tm:pad + tm + pad, :] = jnp.zeros((nsub, 2, pad, sub), F32)

    pos = lax.broadcasted_iota(jnp.int32, (tm, 1), 0)
    if grid_conv:
        keep_prev = (j > 0).astype(F32)
        keep_next = (j < nj - 1).astype(F32)
        pos = pos % GRID_W
        left_ok, right_ok = pos != 0, pos != GRID_W - 1
        row_taps = (-1, 0, 1)
    else:
        left_ok, right_ok = pos != 0, pos != tm - 1
        row_taps = (0,)

    for si in range(nsub):
        cs = slice(si * sub, (si + 1) * sub)
        ge = _bdot(xm_scr[...], wg_ref[:, cs])
        if grid_conv:
            g_scr[si, 0:halo, :] = ge[0:halo] * keep_prev
            g_scr[si, halo:halo + tm, :] = ge[halo:halo + tm]
            g_scr[si, halo + tm:halo + tm + halo, :] = ge[halo + tm:halo + tm + halo] * keep_next
        else:
            g_scr[si] = ge
        v_scr[si] = _bdot(xm_scr[halo:halo + tm, :], wv_ref[:, cs])

    acts = []
    for si in range(nsub):
        cs = slice(si * sub, (si + 1) * sub)

        def tap_sum(dc):
            tot = None
            for dr in row_taps:
                k = (dr + 1) * 3 + (dc + 1)
                r0 = halo + dr * GRID_W
                term = g_scr[si, r0:r0 + tm, :] * cw_ref[k:k + 1, cs]
                tot = term if tot is None else tot + term
            return tot

        side_scr[si, 0, pad:pad + tm, :] = tap_sum(-1)
        side_scr[si, 1, pad:pad + tm, :] = tap_sum(1)
        conv = tap_sum(0) + cb_ref[:, cs]
        conv = conv + jnp.where(left_ok, side_scr[si, 0, pad - 1:pad - 1 + tm, :], 0.0)
        conv = conv + jnp.where(right_ok, side_scr[si, 1, pad + 1:pad + 1 + tm, :], 0.0)
        acts.append((_gelu(conv) * v_scr[si]).astype(BF16))
    act = acts[0] if nsub == 1 else jnp.concatenate(acts, axis=1)
    acc_scr[...] += _bdot(act, wd_ref[...])

    @pl.when(f == nf - 1)
    def _():
        o_ref[...] = _layer_norm_rows(alpha * h_ref[...] + gate_ref[...] * acc_scr[...], lg_ref[...], lb_ref[...])


def _ffn(h, mods, w_up, conv_w9, conv_b, w_down, ln_g, ln_b, nlat, alpha, grid_conv):
    nb, t, d = h.shape
    fh = w_down.shape[0]
    tf = FFN_COL_TILE
    sub = FFN_SUB_TILE
    nf = fh // tf
    if grid_conv:
        tm, halo = FFN_ROW_TILE, GRID_W
        nblk, blk0 = nlat // tm, 0
    else:
        tm, halo = t - nlat, 0
        nblk, blk0 = 1, nlat // tm
    hb = tm // GRID_W
    nhalo_blocks = nlat // GRID_W

    def vec(k):
        return pl.BlockSpec((None, 1, d), lambda b, j, f: (nb if not grid_conv else b, 0, k))

    def const(shape):
        return pl.BlockSpec(shape, lambda b, j, f: (0, 0))

    in_specs = [pl.BlockSpec((None, tm, d), lambda b, j, f: (b, j + blk0, 0))]
    args = [h]
    if grid_conv:
        in_specs += [
            pl.BlockSpec((None, halo, d), lambda b, j, f: (b, jnp.maximum(j * hb - 1, 0), 0)),
            pl.BlockSpec((None, halo, d), lambda b, j, f: (b, jnp.minimum((j + 1) * hb, nhalo_blocks - 1), 0)),
        ]
        args += [h, h]
    in_specs += [
        vec(3), vec(4), vec(5),
        pl.BlockSpec((d, tf), lambda b, j, f: (0, f)),
        pl.BlockSpec((d, tf), lambda b, j, f: (0, nf + f)),
        pl.BlockSpec((9, tf), lambda b, j, f: (0, f)),
        pl.BlockSpec((1, tf), lambda b, j, f: (0, f)),
        pl.BlockSpec((tf, d), lambda b, j, f: (f, 0)),
        const((1, d)), const((1, d)),
    ]
    args += [mods, mods, mods, w_up, w_up, conv_w9, conv_b, w_down, ln_g, ln_b]
    return pl.pallas_call(
        functools.partial(_ffn_kernel, alpha=alpha, grid_conv=grid_conv, halo=halo),
        grid=(nb, nblk, nf),
        in_specs=in_specs,
        out_specs=pl.BlockSpec((None, tm, d), lambda b, j, f: (b, j, 0)),
        out_shape=jax.ShapeDtypeStruct((nb, nblk * tm, d), F32),
        scratch_shapes=[pltpu.VMEM((tm + 2 * halo, d), BF16),
                        pltpu.VMEM((tf // sub, tm + 2 * halo, sub), F32),
                        pltpu.VMEM((tf // sub, tm, sub), F32),
                        pltpu.VMEM((tf // sub, 2, tm + 2 * SUBLANE, sub), F32),
                        pltpu.VMEM((tm, d), F32)],
        compiler_params=_cparams(("arbitrary", "arbitrary", "arbitrary")),
        name="ffn_lat" if grid_conv else "ffn_ctx",
    )(*args)


def kernel(x, c, ctx, c_ctx, w_ada, b_ada, w_in, w_out, ln1_g, ln1_b, ln2_g, ln2_b, s5_a_re, s5_a_im, s5_b_re, s5_b_im, s5_c_re, s5_c_im, s5_log_step, s5_d, s5_glu_w, s5_glu_b, sg_ln_g, sg_ln_b, sg_w, sg_b, pool_w, pool_b, pool_scale, m2_conv_w, m2_conv_b, m2_dt_bias, m2_a_log, m2_d, m2_norm_w, ffn_w_up, ffn_conv_w, ffn_conv_b, ffn_w_down):
    nb, nlat, d = x.shape
    nctx = ctx.shape[1]
    depth = w_ada.shape[0]
    gw = GROUP_W
    alpha = (2 * depth) ** 0.25
    assert nb + 1 <= MOD_ROWS and nlat % FFN_ROW_TILE == 0 and nctx % ROW_TILE == 0
    assert nlat % (GRID_W * SUBLANE) == 0 and FFN_ROW_TILE % GRID_W == 0
    assert w_in.shape[2] == 5 * gw + M2_XBC + 2 * M2_HEADS

    cmat = jnp.concatenate([c, c_ctx[None, :], jnp.zeros((MOD_ROWS - nb - 1, d), F32)], axis=0)
    mods_all = _ada(cmat, w_ada, b_ada)

    h_lat, h_ctx = x, ctx
    for i in range(depth):
        need_ctx = i < depth - 1
        mods = mods_all[i].reshape(MOD_ROWS, 1, 6 * d)

        wi = jnp.pad(w_in[i], ((0, 0), (0, LANE - 2 * M2_HEADS))).astype(BF16)
        pa, pb, pd = _inproj(h_lat, h_ctx, mods, wi)

        ya = _s5(pa, _s5_mats(s5_a_re[i], s5_a_im[i], s5_b_re[i], s5_b_im[i], s5_c_re[i], s5_c_im[i],
                              s5_log_step[i], s5_d[i]), nlat)
        sg_btile = jnp.repeat(sg_b[i].astype(F32).T, gw // SG_HEADS, axis=1)
        mabc = _mixers(ya, pb, s5_glu_w[i].astype(BF16), s5_glu_b[i].reshape(1, gw),
                       sg_ln_g[i].reshape(1, gw), sg_ln_b[i].reshape(1, gw), sg_w[i].astype(BF16), sg_btile,
                       pool_w[i].astype(BF16), pool_b[i].reshape(1, gw), pool_scale[i].reshape(1, gw), nlat)
        md = _ssd(pd, m2_conv_w[i], m2_conv_b[i], m2_dt_bias[i], m2_a_log[i], m2_d[i], m2_norm_w[i], nlat)

        rows_used = nlat + nctx if need_ctx else nlat
        h1 = _outproj(mabc, md, h_lat, h_ctx, mods, w_out[i].astype(BF16), ln1_g[i].reshape(1, d),
                      ln1_b[i].reshape(1, d), rows_used, alpha)

        w_up = ffn_w_up[i].astype(BF16)
        w_down = ffn_w_down[i].astype(BF16)
        fh = w_down.shape[0]
        cw9 = ffn_conv_w[i].astype(F32).reshape(9, fh)
        cb = ffn_conv_b[i].astype(F32).reshape(1, fh)
        lg, lb = ln2_g[i].reshape(1, d), ln2_b[i].reshape(1, d)
        h_lat = _ffn(h1, mods, w_up, cw9, cb, w_down, lg, lb, nlat, alpha, True)
        if need_ctx:
            h_ctx = _ffn(h1, mods, w_up, cw9, cb, w_down, lg, lb, nlat, alpha, False)
    return h_lat
```

```python
import functools
import math

import jax
import jax.numpy as jnp
from jax import lax
from jax.experimental import pallas as pl
from jax.experimental.pallas import tpu as pltpu

F32 = jnp.float32
BF16 = jnp.bfloat16
HIGHEST = lax.Precision.HIGHEST

GRID_W = 64
GROUP_W = 512
S5_CH = 16
S5_Q = 16
S5_RELAYOUT_UNROLL = 4
SG_HEADS = 4
SG_CHUNK = 128
POOL_WINDOWS = (2, 4, 8, 16)
POOL_DIM = GROUP_W // len(POOL_WINDOWS)
POOL_HALO = 8
M2_HEAD_DIM = 64
M2_HEADS = GROUP_W // M2_HEAD_DIM
M2_STATE = 128
M2_GROUPS = 2
M2_CONV = 4
M2_CHUNK = 128
M2_XBC = GROUP_W + 2 * M2_GROUPS * M2_STATE
M2_HALO = 8
LN_EPS = 1e-5
RMS_EPS = 1e-5

LANE = 128
SUBLANE = 8
VMEM_LIMIT = 56 * 1024 * 1024
ROW_TILE = 256
FFN_ROW_TILE = 512
FFN_COL_TILE = 512
FFN_SUB_TILE = 256
ADA_COL_TILE = 1024
MOD_ROWS = 8


def _cparams(sem):
    return pltpu.CompilerParams(dimension_semantics=sem, vmem_limit_bytes=VMEM_LIMIT)


def _bdot(a, b):
    return jnp.dot(a, b, preferred_element_type=F32)


def _hdot(a, b):
    return jnp.dot(a, b, preferred_element_type=F32, precision=HIGHEST)


def _gelu(x):
    return jax.nn.gelu(x)


def _silu(x):
    return x * jax.nn.sigmoid(x)


def _ada_kernel(c_ref, w_ref, b_ref, o_ref):
    c = c_ref[...]
    s = _silu(c).astype(BF16)
    o_ref[...] = _bdot(s, w_ref[...].astype(BF16)) + b_ref[...]


def _ada(cmat, w_ada, b_ada):
    depth, d, n = w_ada.shape
    tn = ADA_COL_TILE
    return pl.pallas_call(
        _ada_kernel,
        grid=(depth, n // tn),
        in_specs=[
            pl.BlockSpec((MOD_ROWS, d), lambda l, j: (0, 0)),
            pl.BlockSpec((None, d, tn), lambda l, j: (l, 0, j)),
            pl.BlockSpec((None, 1, tn), lambda l, j: (l, 0, j)),
        ],
        out_specs=pl.BlockSpec((None, MOD_ROWS, tn), lambda l, j: (l, 0, j)),
        out_shape=jax.ShapeDtypeStruct((depth, MOD_ROWS, n), F32),
        compiler_params=_cparams(("arbitrary", "arbitrary")),
        name="ada",
    )(cmat, w_ada, b_ada.reshape(depth, 1, n))


def _mod_spec(d, k, nbatch, nlat_blocks):
    return pl.BlockSpec((None, 1, d), lambda b, j, *_: (jnp.where(j >= nlat_blocks, nbatch, b), 0, k))


def _stream_specs(tm, d, nlat_blocks):
    lat = pl.BlockSpec((None, tm, d), lambda b, j: (b, jnp.minimum(j, nlat_blocks - 1), 0))
    ctx = pl.BlockSpec((None, tm, d), lambda b, j: (b, jnp.maximum(j - nlat_blocks, 0), 0))
    return lat, ctx


def _inproj_kernel(hl_ref, hc_ref, sh_ref, sc_ref, w_ref, oa_ref, ob_ref, od_ref, xm_scr, *, nlat_blocks):
    j = pl.program_id(1)

    def modulated(h_ref):
        return (h_ref[...] * (1.0 + sc_ref[...]) + sh_ref[...]).astype(BF16)

    @pl.when(j < nlat_blocks)
    def _():
        xm_scr[...] = modulated(hl_ref)

    @pl.when(j >= nlat_blocks)
    def _():
        xm_scr[...] = modulated(hc_ref)

    xm = xm_scr[...]
    gw = GROUP_W
    oa_ref[...] = _bdot(xm, w_ref[:, 0:gw])
    ob_ref[...] = _bdot(xm, w_ref[:, gw:4 * gw])
    od_ref[:, 0:M2_XBC] = _bdot(xm, w_ref[:, 5 * gw:5 * gw + M2_XBC])
    od_ref[:, M2_XBC:M2_XBC + gw] = _bdot(xm, w_ref[:, 4 * gw:5 * gw])
    od_ref[:, M2_XBC + gw:M2_XBC + gw + LANE] = _bdot(xm, w_ref[:, 5 * gw + M2_XBC:5 * gw + M2_XBC + LANE])


def _inproj(h_lat, h_ctx, mods, w):
    nb, nlat, d = h_lat.shape
    t = nlat + h_ctx.shape[1]
    tm = ROW_TILE
    na, nbw, nd = GROUP_W, 3 * GROUP_W, M2_XBC + GROUP_W + LANE
    lat_spec, ctx_spec = _stream_specs(tm, d, nlat // tm)

    def row(n):
        return pl.BlockSpec((None, tm, n), lambda b, j: (b, j, 0))

    return pl.pallas_call(
        functools.partial(_inproj_kernel, nlat_blocks=nlat // tm),
        grid=(nb, t // tm),
        in_specs=[lat_spec, ctx_spec, _mod_spec(d, 0, nb, nlat // tm), _mod_spec(d, 1, nb, nlat // tm),
                  pl.BlockSpec(w.shape, lambda b, j: (0, 0))],
        out_specs=[row(na), row(nbw), row(nd)],
        out_shape=[jax.ShapeDtypeStruct((nb, t, n), F32) for n in (na, nbw, nd)],
        scratch_shapes=[pltpu.VMEM((tm, d), BF16)],
        compiler_params=_cparams(("arbitrary", "arbitrary")),
        name="inproj",
    )(h_lat, h_ctx, mods, mods, w)


def _s5_mats(a_re, a_im, b_re, b_im, c_re, c_im, log_step, d_skip):
    q, hch = S5_Q, S5_CH
    g, p = a_re.shape[1], a_re.shape[2]
    lam = lax.complex(a_re.astype(F32), a_im.astype(F32))
    step = jnp.exp(log_step.astype(F32))[..., None]
    a_bar = jnp.exp(lam * step)
    b_bar = ((a_bar - 1.0) / lam)[..., None] * lax.complex(b_re.astype(F32), b_im.astype(F32))
    c_mat = lax.complex(c_re.astype(F32), c_im.astype(F32))
    ks = jnp.arange(q + 1, dtype=F32)
    a_pow = jnp.exp(lam[None] * step[None] * ks[:, None, None, None])

    ca = c_mat[None] * a_pow[:, :, :, None, :]
    br, bi = jnp.real(b_bar), jnp.imag(b_bar)
    lagk = (jnp.einsum('kdgop,dgpi->dgkoi', jnp.real(ca[:q]), br, precision=HIGHEST)
            - jnp.einsum('kdgop,dgpi->dgkoi', jnp.imag(ca[:q]), bi, precision=HIGHEST))
    s_idx = jnp.arange(q)[:, None]
    t_idx = jnp.arange(q)[None, :]
    lag_f = t_idx - s_idx
    lag_r = s_idx - t_idx
    kf = jnp.where((lag_f >= 0)[None, :, :, None, None], lagk[0][:, jnp.clip(lag_f, 0, q - 1)], 0.0)
    kr = jnp.where((lag_r >= 0)[None, :, :, None, None], lagk[1][:, jnp.clip(lag_r, 0, q - 1)], 0.0)
    eye_t = (s_idx == t_idx).astype(F32)[None, :, :, None, None]
    dmat = jnp.eye(hch, dtype=F32)[None] * d_skip.astype(F32).reshape(g, hch)[:, :, None]
    tmat = kf + kr + eye_t * dmat[:, None, None]
    tmat = jnp.transpose(tmat, (0, 1, 4, 2, 3)).reshape(g, q * hch, q * hch)

    def state_in(direction, exps):
        m = a_pow[exps, direction][:, :, :, None] * b_bar[direction][None]
        m = jnp.transpose(m, (1, 0, 3, 2)).reshape(g, q * hch, p)
        return jnp.real(m), jnp.imag(m)

    sf_re, sf_im = state_in(0, q - 1 - jnp.arange(q))
    sr_re, sr_im = state_in(1, jnp.arange(q))
    w1 = jnp.concatenate([tmat, sf_re, sf_im, sr_re, sr_im], axis=-1)

    def state_out(direction, exps):
        m = ca[exps, direction]
        m = jnp.transpose(m, (1, 3, 0, 2)).reshape(g, p, q * hch)
        return jnp.real(m), -jnp.imag(m)

    of_re, of_im = state_out(0, 1 + jnp.arange(q))
    or_re, or_im = state_out(1, q - jnp.arange(q))
    w2 = jnp.concatenate([of_re, of_im, or_re, or_im], axis=1)

    aq = a_pow[q]
    a1 = jnp.concatenate([jnp.real(aq), jnp.real(aq)], axis=-1).reshape(2, g * 2 * p)
    a2 = jnp.concatenate([-jnp.imag(aq), jnp.imag(aq)], axis=-1).reshape(2, g * 2 * p)
    acoef = jnp.stack([a1[0], a2[0], a1[1], a2[1]], axis=0)

    def hi_lo(w):
        hi = w.astype(BF16)
        return jnp.stack([hi, (w - hi.astype(F32)).astype(BF16)])

    return hi_lo(w1), hi_lo(w2), acoef


def _granule_transpose(v):
    n = len(v)
    slot = lax.broadcasted_iota(jnp.int32, v[0].shape, 1) // S5_CH
    v = list(v)
    j = n // 2
    while j >= 1:
        keep = (slot & j) == 0
        for i in range(n):
            if i & j == 0:
                a, b = v[i], v[i + j]
                v[i] = jnp.where(keep, a, pltpu.roll(b, j * S5_CH, axis=1))
                v[i + j] = jnp.where(keep, pltpu.roll(a, LANE - j * S5_CH, axis=1), b)
        j //= 2
    return v


def _split_bf16(x):
    hi = x.astype(BF16)
    return hi, (x - hi.astype(F32)).astype(BF16)


def _dot3(x, w_hi, w_lo):
    x_hi, x_lo = _split_bf16(x)
    return _bdot(x_hi, w_hi) + (_bdot(x_lo, w_hi) + _bdot(x_hi, w_lo))


def _s5_kernel(pa_ref, w1_ref, w2_ref, ac_ref, ya_ref, x_scr, s_scr, h_scr, *, nbatch, nrows, nctx):
    gb = LANE // S5_CH
    halves = S5_Q // gb
    qh = S5_Q * S5_CH
    sw = LANE
    nlat = nrows - nctx
    nrc = nrows // SUBLANE
    tok_per_rc = SUBLANE * S5_Q

    def relayout(it, to_rows):
        for u in range(S5_RELAYOUT_UNROLL):
            idx = it * S5_RELAYOUT_UNROLL + u
            b = idx // nrc
            rc = idx % nrc
            r0 = pl.multiple_of(b * nrows + rc * SUBLANE, SUBLANE)
            for half in range(halves):
                lanes = slice(half * LANE, (half + 1) * LANE)

                def tok(s8, half=half, rc=rc):
                    return pl.ds(rc * tok_per_rc + half * gb + s8, SUBLANE, stride=S5_Q)

                if to_rows:
                    v = _granule_transpose([pa_ref[b, tok(s8), :] for s8 in range(gb)])
                    for gi in range(gb):
                        x_scr[gi, pl.ds(r0, SUBLANE), lanes] = v[gi]
                else:
                    v = _granule_transpose([x_scr[gi, pl.ds(r0, SUBLANE), lanes] for gi in range(gb)])
                    for s8 in range(gb):
                        ya_ref[b, tok(s8), :] = v[s8]
        return 0

    nblocks = nbatch * nrc // S5_RELAYOUT_UNROLL
    lax.fori_loop(0, nblocks, lambda it, c: relayout(it, True), 0)

    for gi in range(gb):
        r = _dot3(x_scr[gi], w1_ref[0, gi], w1_ref[1, gi])
        x_scr[gi] = r[:, :qh]
        for d in range(2):
            sd = r[:, qh + d * sw:qh + (d + 1) * sw]
            s_scr[d, gi] = sd
            s_scr[2 + d, gi] = pltpu.roll(sd, sw // 2, axis=1)

    zero = jnp.zeros((nbatch, sw), F32)

    def body(s, carry):
        f_rows = pl.ds(jnp.where(s < nctx, s + nlat, s - nctx), nbatch, stride=nrows)
        r_rows = pl.ds(nrows - 1 - s, nbatch, stride=nrows)
        out = []
        for gi in range(gb):
            hf, hfs, hr, hrs = carry[4 * gi:4 * gi + 4]
            a1f, a2f, a1r, a2r = (ac_ref[k:k + 1, gi * sw:(gi + 1) * sw] for k in range(4))
            h_scr[0, gi, f_rows, :] = hf
            h_scr[1, gi, r_rows, :] = hr
            out += [a1f * hf + a2f * hfs + s_scr[0, gi, f_rows, :],
                    a1f * hfs - a2f * hf + s_scr[2, gi, f_rows, :],
                    a1r * hr + a2r * hrs + s_scr[1, gi, r_rows, :],
                    a1r * hrs - a2r * hr + s_scr[3, gi, r_rows, :]]
        return tuple(out)

    lax.fori_loop(0, nrows, body, (zero,) * (4 * gb))

    for gi in range(gb):
        hin = jnp.concatenate([h_scr[0, gi], h_scr[1, gi]], axis=1)
        x_scr[gi] = x_scr[gi] + _dot3(hin, w2_ref[0, gi], w2_ref[1, gi])

    lax.fori_loop(0, nblocks, lambda it, c: relayout(it, False), 0)


def _s5(pa, mats, nlat):
    w1, w2, acoef = mats
    nb, t, gw = pa.shape
    q = S5_Q
    nrows = t // q
    qh = q * S5_CH
    gb = LANE // S5_CH
    assert nrows % SUBLANE == 0 and q % gb == 0 and w2.shape[2] == 2 * LANE
    assert (nb * nrows // SUBLANE) % S5_RELAYOUT_UNROLL == 0
    return pl.pallas_call(
        functools.partial(_s5_kernel, nbatch=nb, nrows=nrows, nctx=(t - nlat) // q),
        grid=(gw // LANE,),
        in_specs=[
            pl.BlockSpec((nb, t, LANE), lambda i: (0, 0, i)),
            pl.BlockSpec((2, gb, qh, w1.shape[3]), lambda i: (0, i, 0, 0)),
            pl.BlockSpec((2, gb, w2.shape[2], qh), lambda i: (0, i, 0, 0)),
            pl.BlockSpec((4, gb * LANE), lambda i: (0, i)),
        ],
        out_specs=pl.BlockSpec((nb, t, LANE), lambda i: (0, 0, i)),
        out_shape=jax.ShapeDtypeStruct((nb, t, gw), F32),
        scratch_shapes=[pltpu.VMEM((gb, nb * nrows, qh), F32),
                        pltpu.VMEM((4, gb, nb * nrows, LANE), F32),
                        pltpu.VMEM((2, gb, nb * nrows, LANE), F32)],
        compiler_params=_cparams(("arbitrary",)),
        name="s5",
    )(pa, w1, w2, acoef)


def _mixers_kernel(ya_ref, pb_ref, prev_ref, next_ref, gluw_ref, glub_ref, lng_ref, lnb_ref, sgw_ref,
                   sgb_ref, pw_ref, pbias_ref, pscale_ref, o_ref, ext_scr, *, nlat_blocks, nblocks, nlat, nctx):
    j = pl.program_id(1)
    tm = ya_ref.shape[0]
    gw = GROUP_W

    z = _gelu(ya_ref[...])
    gate = jax.nn.sigmoid(_bdot(z.astype(BF16), gluw_ref[...]) + glub_ref[...])
    o_ref[:, 0:gw] = z * gate

    hd = gw // SG_HEADS
    u = _gelu(pb_ref[:, 0:gw])
    v = _gelu(pb_ref[:, gw:2 * gw])
    for hh in range(SG_HEADS):
        vh = v[:, hh * hd:(hh + 1) * hd]
        mu = jnp.mean(vh, axis=-1, keepdims=True)
        var = jnp.mean(jnp.square(vh - mu), axis=-1, keepdims=True)
        vn = ((vh - mu) * lax.rsqrt(var + LN_EPS) * lng_ref[:, hh * hd:(hh + 1) * hd]
              + lnb_ref[:, hh * hd:(hh + 1) * hd]).astype(BF16)
        for cchunk in range(tm // SG_CHUNK):
            rows = slice(cchunk * SG_CHUNK, (cchunk + 1) * SG_CHUNK)
            s = _bdot(sgw_ref[hh], vn[rows]) + sgb_ref[:, hh * hd:(hh + 1) * hd]
            o_ref[rows, gw + hh * hd:gw + (hh + 1) * hd] = u[rows, hh * hd:(hh + 1) * hd] * s

    is_first = jnp.logical_or(j == 0, j == nlat_blocks)
    is_last = jnp.logical_or(j == nlat_blocks - 1, j == nblocks - 1)
    p = pb_ref[:, 2 * gw:3 * gw]
    hal = POOL_HALO
    ext_scr[0:hal, :] = jnp.where(is_first, 0.0, prev_ref[...])
    ext_scr[hal:hal + tm, :] = p
    ext_scr[hal + tm:hal + tm + hal, :] = jnp.where(is_last, 0.0, next_ref[...])
    in_ctx = j >= nlat_blocks
    seq_len = jnp.where(in_ctx, nctx, nlat)
    t0 = (j - jnp.where(in_ctx, nlat_blocks, 0)) * tm
    tpos = t0 + lax.broadcasted_iota(jnp.int32, (tm, 1), 0)
    pd = POOL_DIM
    for gi, win in enumerate(POOL_WINDOWS):
        cols = slice(gi * pd, (gi + 1) * pd)
        tot = jnp.zeros((tm, pd), F32)
        for off in range(-(win // 2), win // 2):
            tot = tot + ext_scr[hal + off:hal + off + tm, cols]
        lo = jnp.maximum(tpos - win // 2, 0)
        hi = jnp.minimum(tpos + win // 2 - 1, seq_len - 1)
        mean = tot / (hi - lo + 1).astype(F32)
        yv = _bdot((mean - p[:, cols]).astype(BF16), pw_ref[gi]) + pbias_ref[:, cols]
        o_ref[:, 2 * gw + gi * pd:2 * gw + (gi + 1) * pd] = yv * pscale_ref[:, cols]


def _mixers(ya, pb, glu_w, glu_b, ln_g, ln_b, sg_w, sg_btile, pool_w, pool_b, pool_scale, nlat):
    nb, t, gw = ya.shape
    tm = ROW_TILE
    nblocks = t // tm
    nlat_blocks = nlat // tm
    hb = tm // POOL_HALO
    nh = t // POOL_HALO

    def vec():
        return pl.BlockSpec((1, gw), lambda b, j: (0, 0))

    return pl.pallas_call(
        functools.partial(_mixers_kernel, nlat_blocks=nlat_blocks, nblocks=nblocks, nlat=nlat, nctx=t - nlat),
        grid=(nb, nblocks),
        in_specs=[
            pl.BlockSpec((None, tm, gw), lambda b, j: (b, j, 0)),
            pl.BlockSpec((None, tm, 3 * gw), lambda b, j: (b, j, 0)),
            pl.BlockSpec((None, POOL_HALO, gw), lambda b, j: (b, jnp.maximum(j * hb - 1, 0), 2)),
            pl.BlockSpec((None, POOL_HALO, gw), lambda b, j: (b, jnp.minimum((j + 1) * hb, nh - 1), 2)),
            pl.BlockSpec((gw, gw), lambda b, j: (0, 0)),
            vec(), vec(), vec(),
            pl.BlockSpec((SG_HEADS, SG_CHUNK, SG_CHUNK), lambda b, j: (0, 0, 0)),
            pl.BlockSpec((SG_CHUNK, gw), lambda b, j: (0, 0)),
            pl.BlockSpec((len(POOL_WINDOWS), POOL_DIM, POOL_DIM), lambda b, j: (0, 0, 0)),
            vec(), vec(),
        ],
        out_specs=pl.BlockSpec((None, tm, 3 * gw), lambda b, j: (b, j, 0)),
        out_shape=jax.ShapeDtypeStruct((nb, t, 3 * gw), F32),
        scratch_shapes=[pltpu.VMEM((tm + 2 * POOL_HALO, gw), F32)],
        compiler_params=_cparams(("arbitrary", "arbitrary")),
        name="mixers",
    )(ya, pb, pb, pb, glu_w, glu_b, ln_g, ln_b, sg_w, sg_btile, pool_w, pool_b, pool_scale)


def _expand_heads(m, width, lane0):
    qn = m.shape[0]
    head_of_lane = lax.broadcasted_iota(jnp.int32, (qn, width), 1) // M2_HEAD_DIM
    out = jnp.zeros((qn, width), F32)
    for hh in range(width // M2_HEAD_DIM):
        out = jnp.where(head_of_lane == hh, jnp.broadcast_to(m[:, lane0 + hh:lane0 + hh + 1], (qn, width)), out)
    return out


def _ssd_kernel(*refs, reverse, nlat_chunks, nchunks):
    if reverse:
        (pd_ref, prev_ref, next_ref, cw_ref, cb_ref, dtb_ref, a_ref, dsk_ref, nw_ref, yf_ref,
         o_ref, ext_scr, st_scr) = refs
    else:
        (pd_ref, prev_ref, next_ref, cw_ref, cb_ref, dtb_ref, a_ref, dsk_ref,
         o_ref, ext_scr, st_scr) = refs
    k = pl.program_id(1)
    qn = M2_CHUNK
    gw = GROUP_W
    nx = M2_XBC
    hal = M2_HALO
    if reverse:
        chunk = nchunks - 1 - k
    else:
        chunk = jnp.where(k < nchunks - nlat_chunks, k + nlat_chunks, k - (nchunks - nlat_chunks))

    @pl.when(k == 0)
    def _():
        st_scr[...] = jnp.zeros_like(st_scr)

    is_first = jnp.logical_or(chunk == 0, chunk == nlat_chunks)
    is_last = jnp.logical_or(chunk == nlat_chunks - 1, chunk == nchunks - 1)
    ext_scr[0:hal, :] = jnp.where(is_first, 0.0, prev_ref[...])
    ext_scr[hal:hal + qn, :] = pd_ref[:, 0:nx]
    ext_scr[hal + qn:hal + qn + hal, :] = jnp.where(is_last, 0.0, next_ref[...])
    acc = jnp.zeros((qn, nx), F32) + cb_ref[...]
    for tap in range(M2_CONV):
        off = hal + tap - M2_CONV // 2
        acc = acc + ext_scr[off:off + qn, :] * cw_ref[tap:tap + 1, :]
    xbc = _silu(acc)
    xs = xbc[:, 0:gw]
    nbc = M2_GROUPS * M2_STATE
    bm = xbc[:, gw:gw + nbc]
    cm = xbc[:, gw + nbc:gw + 2 * nbc]

    lane0 = M2_HEADS if reverse else 0
    xdt = pd_ref[:, nx + gw:nx + gw + LANE] + dtb_ref[...]
    dt = jnp.maximum(xdt, 0.0) + jnp.log1p(jnp.exp(-jnp.abs(xdt)))
    da = dt * a_ref[...]
    ri = lax.broadcasted_iota(jnp.int32, (qn, qn), 0)
    ci = lax.broadcasted_iota(jnp.int32, (qn, qn), 1)
    causal = (ri <= ci) if reverse else (ri >= ci)
    a_col = _hdot(causal.astype(F32), da)
    a_row = a_col.T
    tot_row = a_col[0:1, :] if reverse else a_col[qn - 1:qn, :]

    dt_x = _expand_heads(dt, gw, lane0)
    ea_x = _expand_heads(jnp.exp(a_col), gw, lane0)
    de_x = _expand_heads(jnp.exp(tot_row - a_col), gw, lane0)
    et_x = _expand_heads(jnp.exp(tot_row), gw, lane0)
    xd = xs * dt_x
    xdw = (xd * de_x).astype(BF16)
    xdb = xd.astype(BF16)

    hpg = M2_HEADS // M2_GROUPS
    gcols = hpg * M2_HEAD_DIM
    for g in range(M2_GROUPS):
        bg = bm[:, g * M2_STATE:(g + 1) * M2_STATE]
        cg = cm[:, g * M2_STATE:(g + 1) * M2_STATE].astype(BF16)
        bgt = bg.T.astype(BF16)
        scores = _bdot(cg, bgt)
        st_old = st_scr[:, g * gcols:(g + 1) * gcols]
        y_inter = _bdot(cg, st_old.astype(BF16)) * ea_x[:, g * gcols:(g + 1) * gcols]
        st_new = _bdot(bgt, xdw[:, g * gcols:(g + 1) * gcols])
        st_scr[:, g * gcols:(g + 1) * gcols] = et_x[:, g * gcols:(g + 1) * gcols] * st_old + st_new
        for hl in range(hpg):
            hh = g * hpg + hl
            seg = a_col[:, lane0 + hh:lane0 + hh + 1] - a_row[lane0 + hh:lane0 + hh + 1, :]
            decay = jnp.exp(jnp.where(causal, seg, -jnp.inf))
            cols = slice(hh * M2_HEAD_DIM, (hh + 1) * M2_HEAD_DIM)
            y_h = _bdot((scores * decay).astype(BF16), xdb[:, cols])
            y_h = y_h + y_inter[:, hl * M2_HEAD_DIM:(hl + 1) * M2_HEAD_DIM]
            if reverse:
                o_ref[:, cols] = y_h + yf_ref[:, cols]
            else:
                o_ref[:, cols] = y_h + dsk_ref[:, cols] * xs[:, cols]

    if reverse:
        zg = pd_ref[:, nx:nx + gw]
        gv = o_ref[...] * _silu(zg)
        gc = gw // M2_GROUPS
        for g in range(M2_GROUPS):
            part = gv[:, g * gc:(g + 1) * gc]
            ms = jnp.mean(jnp.square(part), axis=-1, keepdims=True)
            o_ref[:, g * gc:(g + 1) * gc] = part * lax.rsqrt(ms + RMS_EPS) * nw_ref[:, g * gc:(g + 1) * gc]


def _ssd(pd, conv_w, conv_b, dt_bias, a_log, d_skip, norm_w, nlat):
    nb, t, wd = pd.shape
    qn = M2_CHUNK
    nchunks = t // qn
    nlat_chunks = nlat // qn
    hb = qn // M2_HALO
    nh = t // M2_HALO
    gw = GROUP_W
    a = -jnp.exp(a_log.astype(F32))
    dsk = jnp.repeat(d_skip.astype(F32), M2_HEAD_DIM).reshape(1, gw)

    def chunk_of(k, reverse):
        if reverse:
            return nchunks - 1 - k
        return jnp.where(k < nchunks - nlat_chunks, k + nlat_chunks, k - (nchunks - nlat_chunks))

    def call(reverse, yf):
        def rows(n):
            return pl.BlockSpec((None, qn, n), lambda b, k: (b, chunk_of(k, reverse), 0))

        def const(shape):
            return pl.BlockSpec(shape, lambda b, k: tuple(0 for _ in shape))

        in_specs = [
            rows(wd),
            pl.BlockSpec((None, M2_HALO, M2_XBC),
                         lambda b, k: (b, jnp.maximum(chunk_of(k, reverse) * hb - 1, 0), 0)),
            pl.BlockSpec((None, M2_HALO, M2_XBC),
                         lambda b, k: (b, jnp.minimum((chunk_of(k, reverse) + 1) * hb, nh - 1), 0)),
            const((M2_CONV, M2_XBC)), const((1, M2_XBC)), const((1, LANE)), const((1, LANE)),
            const((1, gw)),
        ]
        d = 1 if reverse else 0
        lane_pad = ((0, 0), (0, LANE - 2 * M2_HEADS))
        dtb = jnp.pad(dt_bias.astype(F32).reshape(1, 2 * M2_HEADS), lane_pad)
        a_dir = jnp.pad(jnp.where(jnp.arange(2)[:, None] == d, a, 0.0).reshape(1, 2 * M2_HEADS), lane_pad)
        args = [pd, pd, pd, conv_w.astype(F32), conv_b.astype(F32).reshape(1, M2_XBC), dtb, a_dir, dsk]
        if reverse:
            in_specs += [const((1, gw)), rows(gw)]
            args += [norm_w.astype(F32).reshape(1, gw), yf]
        return pl.pallas_call(
            functools.partial(_ssd_kernel, reverse=reverse, nlat_chunks=nlat_chunks, nchunks=nchunks),
            grid=(nb, nchunks),
            in_specs=in_specs,
            out_specs=rows(gw),
            out_shape=jax.ShapeDtypeStruct((nb, t, gw), F32),
            scratch_shapes=[pltpu.VMEM((qn + 2 * M2_HALO, M2_XBC), F32), pltpu.VMEM((M2_STATE, gw), F32)],
            compiler_params=_cparams(("arbitrary", "arbitrary")),
            name="ssd_rev" if reverse else "ssd_fwd",
        )(*args)

    return call(True, call(False, None))


def _layer_norm_rows(v, g, b):
    mu = jnp.mean(v, axis=-1, keepdims=True)
    var = jnp.mean(jnp.square(v - mu), axis=-1, keepdims=True)
    return (v - mu) * lax.rsqrt(var + LN_EPS) * g + b


def _outproj_kernel(mabc_ref, md_ref, hl_ref, hc_ref, gate_ref, w_ref, lg_ref, lb_ref, o_ref,
                    *, alpha, nlat_blocks):
    j = pl.program_id(1)
    k1 = mabc_ref.shape[1]
    mix = (_bdot(mabc_ref[...].astype(BF16), w_ref[0:k1, :])
           + _bdot(md_ref[...].astype(BF16), w_ref[k1:w_ref.shape[0], :]))
    gated = gate_ref[...] * mix

    @pl.when(j < nlat_blocks)
    def _():
        o_ref[...] = _layer_norm_rows(alpha * hl_ref[...] + gated, lg_ref[...], lb_ref[...])

    @pl.when(j >= nlat_blocks)
    def _():
        o_ref[...] = _layer_norm_rows(alpha * hc_ref[...] + gated, lg_ref[...], lb_ref[...])


def _outproj(mabc, md, h_lat, h_ctx, mods, w, ln_g, ln_b, rows_used, alpha):
    nb, nlat, d = h_lat.shape
    tm = ROW_TILE
    k1, k2 = mabc.shape[2], md.shape[2]
    lat_spec, ctx_spec = _stream_specs(tm, d, nlat // tm)

    def row(n):
        return pl.BlockSpec((None, tm, n), lambda b, j: (b, j, 0))

    def const(shape):
        return pl.BlockSpec(shape, lambda b, j: (0, 0))

    return pl.pallas_call(
        functools.partial(_outproj_kernel, alpha=alpha, nlat_blocks=nlat // tm),
        grid=(nb, rows_used // tm),
        in_specs=[row(k1), row(k2), lat_spec, ctx_spec, _mod_spec(d, 2, nb, nlat // tm),
                  const((k1 + k2, d)), const((1, d)), const((1, d))],
        out_specs=row(d),
        out_shape=jax.ShapeDtypeStruct((nb, rows_used, d), F32),
        compiler_params=_cparams(("arbitrary", "arbitrary")),
        name="outproj",
    )(mabc, md, h_lat, h_ctx, mods, w, ln_g, ln_b)


def _ffn_kernel(*refs, alpha, grid_conv, halo):
    if grid_conv:
        (h_ref, prev_ref, next_ref, sh_ref, sc_ref, gate_ref, wg_ref, wv_ref, cw_ref, cb_ref, wd_ref,
         lg_ref, lb_ref, o_ref, xm_scr, g_scr, v_scr, side_scr, acc_scr) = refs
    else:
        (h_ref, sh_ref, sc_ref, gate_ref, wg_ref, wv_ref, cw_ref, cb_ref, wd_ref,
         lg_ref, lb_ref, o_ref, xm_scr, g_scr, v_scr, side_scr, acc_scr) = refs
    j = pl.program_id(1)
    f = pl.program_id(2)
    nj = pl.num_programs(1)
    nf = pl.num_programs(2)
    tm = h_ref.shape[0]
    sub = g_scr.shape[2]
    nsub = g_scr.shape[0]
    pad = SUBLANE

    @pl.when(f == 0)
    def _():
        def modulated(v):
            return (v * (1.0 + sc_ref[...]) + sh_ref[...]).astype(BF16)
        xm_scr[halo:halo + tm, :] = modulated(h_ref[...])
        if grid_conv:
            xm_scr[0:halo, :] = modulated(prev_ref[...])
            xm_scr[halo + tm:halo + tm + halo, :] = modulated(next_ref[...])
        acc_scr[...] = jnp.zeros_like(acc_scr)
        side_scr[:, :, 0:pad, :] = jnp.zeros((nsub, 2, pad, sub), F32)
        side_scr[:, :, pad + tm:pad + tm + pad, :] = jnp.zeros((nsub, 2, pad, sub), F32)

    pos = lax.broadcasted_iota(jnp.int32, (tm, 1), 0)
    if grid_conv:
        keep_prev = (j > 0).astype(F32)
        keep_next = (j < nj - 1).astype(F32)
        pos = pos % GRID_W
        left_ok, right_ok = pos != 0, pos != GRID_W - 1
        row_taps = (-1, 0, 1)
    else:
        left_ok, right_ok = pos != 0, pos != tm - 1
        row_taps = (0,)

    for si in range(nsub):
        cs = slice(si * sub, (si + 1) * sub)
        ge = _bdot(xm_scr[...], wg_ref[:, cs])
        if grid_conv:
            g_scr[si, 0:halo, :] = ge[0:halo] * keep_prev
            g_scr[si, halo:halo + tm, :] = ge[halo:halo + tm]
            g_scr[si, halo + tm:halo + tm + halo, :] = ge[halo + tm:halo + tm + halo] * keep_next
        else:
            g_scr[si] = ge
        v_scr[si] = _bdot(xm_scr[halo:halo + tm, :], wv_ref[:, cs])

    acts = []
    for si in range(nsub):
        cs = slice(si * sub, (si + 1) * sub)

        def tap_sum(dc):
            tot = None
            for dr in row_taps:
                k = (dr + 1) * 3 + (dc + 1)
                r0 = halo + dr * GRID_W
                term = g_scr[si, r0:r0 + tm, :] * cw_ref[k:k + 1, cs]
                tot = term if tot is None else tot + term
            return tot

        side_scr[si, 0, pad:pad + tm, :] = tap_sum(-1)
        side_scr[si, 1, pad:pad + tm, :] = tap_sum(1)
        conv = tap_sum(0) + cb_ref[:, cs]
        conv = conv + jnp.where(left_ok, side_scr[si, 0, pad - 1:pad - 1 + tm, :], 0.0)
        conv = conv + jnp.where(right_ok, side_scr[si, 1, pad + 1:pad + 1 + tm, :], 0.0)
        acts.append((_gelu(conv) * v_scr[si]).astype(BF16))
    act = acts[0] if nsub == 1 else jnp.concatenate(acts, axis=1)
    acc_scr[...] += _bdot(act, wd_ref[...])

    @pl.when(f == nf - 1)
    def _():
        o_ref[...] = _layer_norm_rows(alpha * h_ref[...] + gate_ref[...] * acc_scr[...], lg_ref[...], lb_ref[...])


def _ffn(h, mods, w_up, conv_w9, conv_b, w_down, ln_g, ln_b, nlat, alpha, grid_conv):
    nb, t, d = h.shape
    fh = w_down.shape[0]
    tf = FFN_COL_TILE
    sub = FFN_SUB_TILE
    nf = fh // tf
    if grid_conv:
        tm, halo = FFN_ROW_TILE, GRID_W
        nblk, blk0 = nlat // tm, 0
    else:
        tm, halo = t - nlat, 0
        nblk, blk0 = 1, nlat // tm
    hb = tm // GRID_W
    nhalo_blocks = nlat // GRID_W

    def vec(k):
        return pl.BlockSpec((None, 1, d), lambda b, j, f: (nb if not grid_conv else b, 0, k))

    def const(shape):
        return pl.BlockSpec(shape, lambda b, j, f: (0, 0))

    in_specs = [pl.BlockSpec((None, tm, d), lambda b, j, f: (b, j + blk0, 0))]
    args = [h]
    if grid_conv:
        in_specs += [
            pl.BlockSpec((None, halo, d), lambda b, j, f: (b, jnp.maximum(j * hb - 1, 0), 0)),
            pl.BlockSpec((None, halo, d), lambda b, j, f: (b, jnp.minimum((j + 1) * hb, nhalo_blocks - 1), 0)),
        ]
        args += [h, h]
    in_specs += [
        vec(3), vec(4), vec(5),
        pl.BlockSpec((d, tf), lambda b, j, f: (0, f)),
        pl.BlockSpec((d, tf), lambda b, j, f: (0, nf + f)),
        pl.BlockSpec((9, tf), lambda b, j, f: (0, f)),
        pl.BlockSpec((1, tf), lambda b, j, f: (0, f)),
        pl.BlockSpec((tf, d), lambda b, j, f: (f, 0)),
        const((1, d)), const((1, d)),
    ]
    args += [mods, mods, mods, w_up, w_up, conv_w9, conv_b, w_down, ln_g, ln_b]
    return pl.pallas_call(
        functools.partial(_ffn_kernel, alpha=alpha, grid_conv=grid_conv, halo=halo),
        grid=(nb, nblk, nf),
        in_specs=in_specs,
        out_specs=pl.BlockSpec((None, tm, d), lambda b, j, f: (b, j, 0)),
        out_shape=jax.ShapeDtypeStruct((nb, nblk * tm, d), F32),
        scratch_shapes=[pltpu.VMEM((tm + 2 * halo, d), BF16),
                        pltpu.VMEM((tf // sub, tm + 2 * halo, sub), F32),
                        pltpu.VMEM((tf // sub, tm, sub), F32),
                        pltpu.VMEM((tf // sub, 2, tm + 2 * SUBLANE, sub), F32),
                        pltpu.VMEM((tm, d), F32)],
        compiler_params=_cparams(("arbitrary", "arbitrary", "arbitrary")),
        name="ffn_lat" if grid_conv else "ffn_ctx",
    )(*args)


def kernel(x, c, ctx, c_ctx, w_ada, b_ada, w_in, w_out, ln1_g, ln1_b, ln2_g, ln2_b, s5_a_re, s5_a_im, s5_b_re, s5_b_im, s5_c_re, s5_c_im, s5_log_step, s5_d, s5_glu_w, s5_glu_b, sg_ln_g, sg_ln_b, sg_w, sg_b, pool_w, pool_b, pool_scale, m2_conv_w, m2_conv_b, m2_dt_bias, m2_a_log, m2_d, m2_norm_w, ffn_w_up, ffn_conv_w, ffn_conv_b, ffn_w_down):
    nb, nlat, d = x.shape
    nctx = ctx.shape[1]
    depth = w_ada.shape[0]
    gw = GROUP_W
    alpha = (2 * depth) ** 0.25
    assert nb + 1 <= MOD_ROWS and nlat % FFN_ROW_TILE == 0 and nctx % ROW_TILE == 0
    assert nlat % (GRID_W * SUBLANE) == 0 and FFN_ROW_TILE % GRID_W == 0
    assert w_in.shape[2] == 5 * gw + M2_XBC + 2 * M2_HEADS

    cmat = jnp.concatenate([c, c_ctx[None, :], jnp.zeros((MOD_ROWS - nb - 1, d), F32)], axis=0)
    mods_all = _ada(cmat, w_ada, b_ada)

    h_lat, h_ctx = x, ctx
    for i in range(depth):
        need_ctx = i < depth - 1
        mods = mods_all[i].reshape(MOD_ROWS, 1, 6 * d)

        wi = jnp.pad(w_in[i], ((0, 0), (0, LANE - 2 * M2_HEADS))).astype(BF16)
        pa, pb, pd = _inproj(h_lat, h_ctx, mods, wi)

        ya = _s5(pa, _s5_mats(s5_a_re[i], s5_a_im[i], s5_b_re[i], s5_b_im[i], s5_c_re[i], s5_c_im[i],
                              s5_log_step[i], s5_d[i]), nlat)
        sg_btile = jnp.repeat(sg_b[i].astype(F32).T, gw // SG_HEADS, axis=1)
        mabc = _mixers(ya, pb, s5_glu_w[i].astype(BF16), s5_glu_b[i].reshape(1, gw),
                       sg_ln_g[i].reshape(1, gw), sg_ln_b[i].reshape(1, gw), sg_w[i].astype(BF16), sg_btile,
                       pool_w[i].astype(BF16), pool_b[i].reshape(1, gw), pool_scale[i].reshape(1, gw), nlat)
        md = _ssd(pd, m2_conv_w[i], m2_conv_b[i], m2_dt_bias[i], m2_a_log[i], m2_d[i], m2_norm_w[i], nlat)

        rows_used = nlat + nctx if need_ctx else nlat
        h1 = _outproj(mabc, md, h_lat, h_ctx, mods, w_out[i].astype(BF16), ln1_g[i].reshape(1, d),
                      ln1_b[i].reshape(1, d), rows_used, alpha)

        w_up = ffn_w_up[i].astype(BF16)
        w_down = ffn_w_down[i].astype(BF16)
        fh = w_down.shape[0]
        cw9 = ffn_conv_w[i].astype(F32).reshape(9, fh)
        cb = ffn_conv_b[i].astype(F32).reshape(1, fh)
        lg, lb = ln2_g[i].reshape(1, d), ln2_b[i].reshape(1, d)
        h_lat = _ffn(h1, mods, w_up, cw9, cb, w_down, lg, lb, nlat, alpha, True)
        if need_ctx:
            h_ctx = _ffn(h1, mods, w_up, cw9, cb, w_down, lg, lb, nlat, alpha, False)
    return h_lat
```

```python
import functools
import math

import jax
import jax.numpy as jnp
from jax import lax
from jax.experimental import pallas as pl
from jax.experimental.pallas import tpu as pltpu

F32 = jnp.float32
BF16 = jnp.bfloat16
HIGHEST = lax.Precision.HIGHEST

GRID_W = 64
GROUP_W = 512
S5_CH = 16
S5_Q = 16
S5_RELAYOUT_UNROLL = 4
SG_HEADS = 4
SG_CHUNK = 128
POOL_WINDOWS = (2, 4, 8, 16)
POOL_DIM = GROUP_W // len(POOL_WINDOWS)
POOL_HALO = 8
M2_HEAD_DIM = 64
M2_HEADS = GROUP_W // M2_HEAD_DIM
M2_STATE = 128
M2_GROUPS = 2
M2_CONV = 4
M2_CHUNK = 128
M2_XBC = GROUP_W + 2 * M2_GROUPS * M2_STATE
M2_HALO = 8
LN_EPS = 1e-5
RMS_EPS = 1e-5

LANE = 128
SUBLANE = 8
VMEM_LIMIT = 56 * 1024 * 1024
ROW_TILE = 256
FFN_ROW_TILE = 512
FFN_COL_TILE = 512
FFN_SUB_TILE = 256
ADA_COL_TILE = 1024
MOD_ROWS = 8


def _cparams(sem):
    return pltpu.CompilerParams(dimension_semantics=sem, vmem_limit_bytes=VMEM_LIMIT)


def _bdot(a, b):
    return jnp.dot(a, b, preferred_element_type=F32)


def _hdot(a, b):
    return jnp.dot(a, b, preferred_element_type=F32, precision=HIGHEST)


def _gelu(x):
    return jax.nn.gelu(x)


def _silu(x):
    return x * jax.nn.sigmoid(x)


def _ada_kernel(c_ref, w_ref, b_ref, o_ref):
    c = c_ref[...]
    s = _silu(c).astype(BF16)
    o_ref[...] = _bdot(s, w_ref[...].astype(BF16)) + b_ref[...]


def _ada(cmat, w_ada, b_ada):
    depth, d, n = w_ada.shape
    tn = ADA_COL_TILE
    return pl.pallas_call(
        _ada_kernel,
        grid=(depth, n // tn),
        in_specs=[
            pl.BlockSpec((MOD_ROWS, d), lambda l, j: (0, 0)),
            pl.BlockSpec((None, d, tn), lambda l, j: (l, 0, j)),
            pl.BlockSpec((None, 1, tn), lambda l, j: (l, 0, j)),
        ],
        out_specs=pl.BlockSpec((None, MOD_ROWS, tn), lambda l, j: (l, 0, j)),
        out_shape=jax.ShapeDtypeStruct((depth, MOD_ROWS, n), F32),
        compiler_params=_cparams(("arbitrary", "arbitrary")),
        name="ada",
    )(cmat, w_ada, b_ada.reshape(depth, 1, n))


def _mod_spec(d, k, layer, nbatch, nlat_blocks):
    return pl.BlockSpec((None, None, 1, d),
                        lambda b, j, *_: (layer, jnp.where(j >= nlat_blocks, nbatch, b), 0, k))


def _layer_spec(arr, layer):
    tail = arr.shape[1:]
    return pl.BlockSpec((None,) + tail, lambda *_: (layer,) + (0,) * len(tail))


def _stream_specs(tm, d, nlat_blocks):
    lat = pl.BlockSpec((None, tm, d), lambda b, j: (b, jnp.minimum(j, nlat_blocks - 1), 0))
    ctx = pl.BlockSpec((None, tm, d), lambda b, j: (b, jnp.maximum(j - nlat_blocks, 0), 0))
    return lat, ctx


def _inproj_kernel(hl_ref, hc_ref, sh_ref, sc_ref, w_ref, oa_ref, ob_ref, od_ref, xm_scr, *, nlat_blocks):
    j = pl.program_id(1)

    def modulated(h_ref):
        return (h_ref[...] * (1.0 + sc_ref[...]) + sh_ref[...]).astype(BF16)

    @pl.when(j < nlat_blocks)
    def _():
        xm_scr[...] = modulated(hl_ref)

    @pl.when(j >= nlat_blocks)
    def _():
        xm_scr[...] = modulated(hc_ref)

    xm = xm_scr[...]
    gw = GROUP_W
    oa_ref[...] = _bdot(xm, w_ref[:, 0:gw])
    ob_ref[...] = _bdot(xm, w_ref[:, gw:4 * gw])
    od_ref[:, 0:M2_XBC] = _bdot(xm, w_ref[:, 5 * gw:5 * gw + M2_XBC])
    od_ref[:, M2_XBC:M2_XBC + gw] = _bdot(xm, w_ref[:, 4 * gw:5 * gw])
    od_ref[:, M2_XBC + gw:M2_XBC + gw + LANE] = _bdot(xm, w_ref[:, 5 * gw + M2_XBC:5 * gw + M2_XBC + LANE])


def _inproj(h_lat, h_ctx, mods, w, layer):
    nb, nlat, d = h_lat.shape
    t = nlat + h_ctx.shape[1]
    tm = ROW_TILE
    na, nbw, nd = GROUP_W, 3 * GROUP_W, M2_XBC + GROUP_W + LANE
    lat_spec, ctx_spec = _stream_specs(tm, d, nlat // tm)

    def row(n):
        return pl.BlockSpec((None, tm, n), lambda b, j: (b, j, 0))

    return pl.pallas_call(
        functools.partial(_inproj_kernel, nlat_blocks=nlat // tm),
        grid=(nb, t // tm),
        in_specs=[lat_spec, ctx_spec, _mod_spec(d, 0, layer, nb, nlat // tm),
                  _mod_spec(d, 1, layer, nb, nlat // tm), _layer_spec(w, layer)],
        out_specs=[row(na), row(nbw), row(nd)],
        out_shape=[jax.ShapeDtypeStruct((nb, t, n), F32) for n in (na, nbw, nd)],
        scratch_shapes=[pltpu.VMEM((tm, d), BF16)],
        compiler_params=_cparams(("arbitrary", "arbitrary")),
        name="inproj",
    )(h_lat, h_ctx, mods, mods, w)


def _s5_mats(a_re, a_im, b_re, b_im, c_re, c_im, log_step, d_skip):
    q, hch = S5_Q, S5_CH
    nl, _, g, p = a_re.shape
    lre, lim = a_re.astype(F32), a_im.astype(F32)
    step = jnp.exp(log_step.astype(F32))[..., None]
    mag = jnp.exp(lre * step)
    ar, ai = mag * jnp.cos(lim * step), mag * jnp.sin(lim * step)
    den = lre * lre + lim * lim
    qr = ((ar - 1.0) * lre + ai * lim) / den
    qi = (ai * lre - (ar - 1.0) * lim) / den
    bre, bim = b_re.astype(F32), b_im.astype(F32)
    bbr = qr[..., None] * bre - qi[..., None] * bim
    bbi = qr[..., None] * bim + qi[..., None] * bre
    cre, cim = c_re.astype(F32), c_im.astype(F32)

    pw = [(jnp.ones_like(ar), jnp.zeros_like(ai))]
    for _ in range(q):
        pr, pi = pw[-1]
        pw.append((pr * ar - pi * ai, pr * ai + pi * ar))
    pk_re = jnp.stack([v[0] for v in pw], axis=3)
    pk_im = jnp.stack([v[1] for v in pw], axis=3)
    pt_re = jnp.stack([v[0] for v in pw], axis=4)
    pt_im = jnp.stack([v[1] for v in pw], axis=4)

    ca_re = cre[:, :, :, None] * pk_re[:, :, :, :q, None] - cim[:, :, :, None] * pk_im[:, :, :, :q, None]
    ca_im = cre[:, :, :, None] * pk_im[:, :, :, :q, None] + cim[:, :, :, None] * pk_re[:, :, :, :q, None]
    lagk = (jnp.einsum('ldgpi,ldgkop->ldgiko', bbr, ca_re, precision=HIGHEST)
            - jnp.einsum('ldgpi,ldgkop->ldgiko', bbi, ca_im, precision=HIGHEST))
    dmat = jnp.eye(hch, dtype=F32) * d_skip.astype(F32).reshape(nl, g, 1, hch)
    mid = lagk[:, 0, :, :, 0] + lagk[:, 1, :, :, 0] + dmat
    kk = jnp.concatenate([jnp.flip(lagk[:, 1, :, :, 1:], axis=3), mid[:, :, :, None], lagk[:, 0, :, :, 1:]],
                         axis=3).reshape(nl, g, hch, (2 * q - 1) * hch)
    tmat = jnp.stack([kk[..., (q - 1 - s) * hch:(2 * q - 1 - s) * hch] for s in range(q)], axis=2)
    tmat = tmat.reshape(nl, g, q * hch, q * hch)

    def state_in(d, pr, pi):
        btr = jnp.swapaxes(bbr[:, d], -1, -2)[:, :, None]
        bti = jnp.swapaxes(bbi[:, d], -1, -2)[:, :, None]
        pr, pi = pr[:, :, :, None], pi[:, :, :, None]
        m = jnp.concatenate([pr * btr - pi * bti, pr * bti + pi * btr], axis=-1)
        return m.reshape(nl, g, q * hch, 2 * p)

    sb = jnp.concatenate([state_in(0, jnp.flip(pk_re[:, 0, :, :q], axis=2), jnp.flip(pk_im[:, 0, :, :q], axis=2)),
                          state_in(1, pk_re[:, 1, :, :q], pk_im[:, 1, :, :q])], axis=-1)

    def state_out(d, pr, pi):
        ctr = jnp.swapaxes(cre[:, d], -1, -2)[:, :, :, None]
        cti = jnp.swapaxes(cim[:, d], -1, -2)[:, :, :, None]
        pr, pi = pr[..., None], pi[..., None]
        m = jnp.concatenate([ctr * pr - cti * pi, -(ctr * pi + cti * pr)], axis=2)
        return m.reshape(nl, g, 2 * p, q * hch)

    w2 = jnp.concatenate([state_out(0, pt_re[:, 0, :, :, 1:], pt_im[:, 0, :, :, 1:]),
                          state_out(1, jnp.flip(pt_re[:, 1, :, :, 1:], axis=3), jnp.flip(pt_im[:, 1, :, :, 1:], axis=3))],
                         axis=2)

    aqr, aqi = pw[q]
    a1 = jnp.concatenate([aqr, aqr], axis=-1).reshape(nl, 2, g * 2 * p)
    a2 = jnp.concatenate([-aqi, aqi], axis=-1).reshape(nl, 2, g * 2 * p)
    acoef = jnp.stack([a1[:, 0], a2[:, 0], a1[:, 1], a2[:, 1]], axis=1)
    return tmat, sb, w2, acoef


def _granule_transpose(v):
    n = len(v)
    slot = lax.broadcasted_iota(jnp.int32, v[0].shape, 1) // S5_CH
    v = list(v)
    j = n // 2
    while j >= 1:
        keep = (slot & j) == 0
        for i in range(n):
            if i & j == 0:
                a, b = v[i], v[i + j]
                v[i] = jnp.where(keep, a, pltpu.roll(b, j * S5_CH, axis=1))
                v[i + j] = jnp.where(keep, pltpu.roll(a, LANE - j * S5_CH, axis=1), b)
        j //= 2
    return v


def _split_bf16(x):
    hi = x.astype(BF16)
    return hi, (x - hi.astype(F32)).astype(BF16)


def _dot3(xs, w):
    x_hi, x_lo = xs
    w_hi, w_lo = _split_bf16(w)
    return _bdot(x_hi, w_hi) + (_bdot(x_lo, w_hi) + _bdot(x_hi, w_lo))


def _s5_kernel(pa_ref, t_ref, sb_ref, w2_ref, ac_ref, ya_ref, x_scr, s_scr, h_scr, *, nbatch, nrows, nctx):
    gb = LANE // S5_CH
    halves = S5_Q // gb
    qh = S5_Q * S5_CH
    sw = LANE
    nlat = nrows - nctx
    nrc = nrows // SUBLANE
    tok_per_rc = SUBLANE * S5_Q

    def relayout(it, to_rows):
        for u in range(S5_RELAYOUT_UNROLL):
            idx = it * S5_RELAYOUT_UNROLL + u
            b = idx // nrc
            rc = idx % nrc
            r0 = pl.multiple_of(b * nrows + rc * SUBLANE, SUBLANE)
            for half in range(halves):
                lanes = slice(half * LANE, (half + 1) * LANE)

                def tok(s8, half=half, rc=rc):
                    return pl.ds(rc * tok_per_rc + half * gb + s8, SUBLANE, stride=S5_Q)

                if to_rows:
                    v = _granule_transpose([pa_ref[b, tok(s8), :] for s8 in range(gb)])
                    for gi in range(gb):
                        x_scr[gi, pl.ds(r0, SUBLANE), lanes] = v[gi]
                else:
                    v = _granule_transpose([x_scr[gi, pl.ds(r0, SUBLANE), lanes] for gi in range(gb)])
                    for s8 in range(gb):
                        ya_ref[b, tok(s8), :] = v[s8]
        return 0

    nblocks = nbatch * nrc // S5_RELAYOUT_UNROLL
    lax.fori_loop(0, nblocks, lambda it, c: relayout(it, True), 0)

    for gi in range(gb):
        xs = _split_bf16(x_scr[gi])
        x_scr[gi] = _dot3(xs, t_ref[gi])
        rs = _dot3(xs, sb_ref[gi])
        for d in range(2):
            sd = rs[:, d * sw:(d + 1) * sw]
            s_scr[d, gi] = sd
            s_scr[2 + d, gi] = pltpu.roll(sd, sw // 2, axis=1)

    zero = jnp.zeros((nbatch, sw), F32)

    def body(s, carry):
        f_rows = pl.ds(jnp.where(s < nctx, s + nlat, s - nctx), nbatch, stride=nrows)
        r_rows = pl.ds(nrows - 1 - s, nbatch, stride=nrows)
        out = []
        for gi in range(gb):
            hf, hfs, hr, hrs = carry[4 * gi:4 * gi + 4]
            a1f, a2f, a1r, a2r = (ac_ref[k:k + 1, gi * sw:(gi + 1) * sw] for k in range(4))
            h_scr[0, gi, f_rows, :] = hf
            h_scr[1, gi, r_rows, :] = hr
            out += [a1f * hf + a2f * hfs + s_scr[0, gi, f_rows, :],
                    a1f * hfs - a2f * hf + s_scr[2, gi, f_rows, :],
                    a1r * hr + a2r * hrs + s_scr[1, gi, r_rows, :],
                    a1r * hrs - a2r * hr + s_scr[3, gi, r_rows, :]]
        return tuple(out)

    lax.fori_loop(0, nrows, body, (zero,) * (4 * gb))

    for gi in range(gb):
        hin = jnp.concatenate([h_scr[0, gi], h_scr[1, gi]], axis=1)
        x_scr[gi] = x_scr[gi] + _dot3(_split_bf16(hin), w2_ref[gi])

    lax.fori_loop(0, nblocks, lambda it, c: relayout(it, False), 0)


def _s5(pa, mats, layer, nlat):
    tmat, sb, w2, acoef = mats
    nb, t, gw = pa.shape
    q = S5_Q
    nrows = t // q
    qh = q * S5_CH
    gb = LANE // S5_CH
    assert nrows % SUBLANE == 0 and q % gb == 0 and w2.shape[2] == 2 * LANE and sb.shape[3] == 2 * LANE
    assert (nb * nrows // SUBLANE) % S5_RELAYOUT_UNROLL == 0
    return pl.pallas_call(
        functools.partial(_s5_kernel, nbatch=nb, nrows=nrows, nctx=(t - nlat) // q),
        grid=(gw // LANE,),
        in_specs=[
            pl.BlockSpec((nb, t, LANE), lambda i: (0, 0, i)),
            pl.BlockSpec((None, gb, qh, qh), lambda i: (layer, i, 0, 0)),
            pl.BlockSpec((None, gb, qh, 2 * LANE), lambda i: (layer, i, 0, 0)),
            pl.BlockSpec((None, gb, 2 * LANE, qh), lambda i: (layer, i, 0, 0)),
            pl.BlockSpec((None, 4, gb * LANE), lambda i: (layer, 0, i)),
        ],
        out_specs=pl.BlockSpec((nb, t, LANE), lambda i: (0, 0, i)),
        out_shape=jax.ShapeDtypeStruct((nb, t, gw), F32),
        scratch_shapes=[pltpu.VMEM((gb, nb * nrows, qh), F32),
                        pltpu.VMEM((4, gb, nb * nrows, LANE), F32),
                        pltpu.VMEM((2, gb, nb * nrows, LANE), F32)],
        compiler_params=_cparams(("arbitrary",)),
        name="s5",
    )(pa, tmat, sb, w2, acoef)


def _mixers_kernel(ya_ref, pb_ref, prev_ref, next_ref, gluw_ref, glub_ref, lng_ref, lnb_ref, sgw_ref,
                   sgb_ref, pw_ref, pbias_ref, pscale_ref, o_ref, ext_scr, *, nlat_blocks, nblocks, nlat, nctx):
    j = pl.program_id(1)
    tm = ya_ref.shape[0]
    gw = GROUP_W

    z = _gelu(ya_ref[...])
    gate = jax.nn.sigmoid(_bdot(z.astype(BF16), gluw_ref[...]) + glub_ref[...])
    o_ref[:, 0:gw] = z * gate

    hd = gw // SG_HEADS
    u = _gelu(pb_ref[:, 0:gw])
    v = _gelu(pb_ref[:, gw:2 * gw])
    for hh in range(SG_HEADS):
        vh = v[:, hh * hd:(hh + 1) * hd]
        mu = jnp.mean(vh, axis=-1, keepdims=True)
        var = jnp.mean(jnp.square(vh - mu), axis=-1, keepdims=True)
        vn = ((vh - mu) * lax.rsqrt(var + LN_EPS) * lng_ref[:, hh * hd:(hh + 1) * hd]
              + lnb_ref[:, hh * hd:(hh + 1) * hd]).astype(BF16)
        for cchunk in range(tm // SG_CHUNK):
            rows = slice(cchunk * SG_CHUNK, (cchunk + 1) * SG_CHUNK)
            s = _bdot(sgw_ref[hh], vn[rows]) + sgb_ref[:, hh * hd:(hh + 1) * hd]
            o_ref[rows, gw + hh * hd:gw + (hh + 1) * hd] = u[rows, hh * hd:(hh + 1) * hd] * s

    is_first = jnp.logical_or(j == 0, j == nlat_blocks)
    is_last = jnp.logical_or(j == nlat_blocks - 1, j == nblocks - 1)
    p = pb_ref[:, 2 * gw:3 * gw]
    hal = POOL_HALO
    ext_scr[0:hal, :] = jnp.where(is_first, 0.0, prev_ref[...])
    ext_scr[hal:hal + tm, :] = p
    ext_scr[hal + tm:hal + tm + hal, :] = jnp.where(is_last, 0.0, next_ref[...])
    in_ctx = j >= nlat_blocks
    seq_len = jnp.where(in_ctx, nctx, nlat)
    t0 = (j - jnp.where(in_ctx, nlat_blocks, 0)) * tm
    tpos = t0 + lax.broadcasted_iota(jnp.int32, (tm, 1), 0)
    pd = POOL_DIM
    for gi, win in enumerate(POOL_WINDOWS):
        cols = slice(gi * pd, (gi + 1) * pd)
        tot = jnp.zeros((tm, pd), F32)
        for off in range(-(win // 2), win // 2):
            tot = tot + ext_scr[hal + off:hal + off + tm, cols]
        lo = jnp.maximum(tpos - win // 2, 0)
        hi = jnp.minimum(tpos + win // 2 - 1, seq_len - 1)
        mean = tot / (hi - lo + 1).astype(F32)
        yv = _bdot((mean - p[:, cols]).astype(BF16), pw_ref[gi]) + pbias_ref[:, cols]
        o_ref[:, 2 * gw + gi * pd:2 * gw + (gi + 1) * pd] = yv * pscale_ref[:, cols]


def _mixers(ya, pb, params, layer, nlat):
    nb, t, gw = ya.shape
    tm = ROW_TILE
    nblocks = t // tm
    nlat_blocks = nlat // tm
    hb = tm // POOL_HALO
    nh = t // POOL_HALO

    return pl.pallas_call(
        functools.partial(_mixers_kernel, nlat_blocks=nlat_blocks, nblocks=nblocks, nlat=nlat, nctx=t - nlat),
        grid=(nb, nblocks),
        in_specs=[
            pl.BlockSpec((None, tm, gw), lambda b, j: (b, j, 0)),
            pl.BlockSpec((None, tm, 3 * gw), lambda b, j: (b, j, 0)),
            pl.BlockSpec((None, POOL_HALO, gw), lambda b, j: (b, jnp.maximum(j * hb - 1, 0), 2)),
            pl.BlockSpec((None, POOL_HALO, gw), lambda b, j: (b, jnp.minimum((j + 1) * hb, nh - 1), 2)),
        ] + [_layer_spec(v, layer) for v in params],
        out_specs=pl.BlockSpec((None, tm, 3 * gw), lambda b, j: (b, j, 0)),
        out_shape=jax.ShapeDtypeStruct((nb, t, 3 * gw), F32),
        scratch_shapes=[pltpu.VMEM((tm + 2 * POOL_HALO, gw), F32)],
        compiler_params=_cparams(("arbitrary", "arbitrary")),
        name="mixers",
    )(ya, pb, pb, pb, *params)


def _expand_heads(m, width, lane0):
    qn = m.shape[0]
    head_of_lane = lax.broadcasted_iota(jnp.int32, (qn, width), 1) // M2_HEAD_DIM
    out = jnp.zeros((qn, width), F32)
    for hh in range(width // M2_HEAD_DIM):
        out = jnp.where(head_of_lane == hh, jnp.broadcast_to(m[:, lane0 + hh:lane0 + hh + 1], (qn, width)), out)
    return out


def _ssd_kernel(*refs, reverse, nlat_chunks, nchunks):
    if reverse:
        (pd_ref, prev_ref, next_ref, cw_ref, cb_ref, dtb_ref, a_ref, dsk_ref, nw_ref, yf_ref,
         o_ref, ext_scr, st_scr) = refs
    else:
        (pd_ref, prev_ref, next_ref, cw_ref, cb_ref, dtb_ref, a_ref, dsk_ref,
         o_ref, ext_scr, st_scr) = refs
    k = pl.program_id(1)
    qn = M2_CHUNK
    gw = GROUP_W
    nx = M2_XBC
    hal = M2_HALO
    if reverse:
        chunk = nchunks - 1 - k
    else:
        chunk = jnp.where(k < nchunks - nlat_chunks, k + nlat_chunks, k - (nchunks - nlat_chunks))

    @pl.when(k == 0)
    def _():
        st_scr[...] = jnp.zeros_like(st_scr)

    is_first = jnp.logical_or(chunk == 0, chunk == nlat_chunks)
    is_last = jnp.logical_or(chunk == nlat_chunks - 1, chunk == nchunks - 1)
    ext_scr[0:hal, :] = jnp.where(is_first, 0.0, prev_ref[...])
    ext_scr[hal:hal + qn, :] = pd_ref[:, 0:nx]
    ext_scr[hal + qn:hal + qn + hal, :] = jnp.where(is_last, 0.0, next_ref[...])
    acc = jnp.zeros((qn, nx), F32) + cb_ref[...]
    for tap in range(M2_CONV):
        off = hal + tap - M2_CONV // 2
        acc = acc + ext_scr[off:off + qn, :] * cw_ref[tap:tap + 1, :]
    xbc = _silu(acc)
    xs = xbc[:, 0:gw]
    nbc = M2_GROUPS * M2_STATE
    bm = xbc[:, gw:gw + nbc]
    cm = xbc[:, gw + nbc:gw + 2 * nbc]

    lane0 = M2_HEADS if reverse else 0
    xdt = pd_ref[:, nx + gw:nx + gw + LANE] + dtb_ref[...]
    dt = jnp.maximum(xdt, 0.0) + jnp.log1p(jnp.exp(-jnp.abs(xdt)))
    da = dt * a_ref[...]
    ri = lax.broadcasted_iota(jnp.int32, (qn, qn), 0)
    ci = lax.broadcasted_iota(jnp.int32, (qn, qn), 1)
    causal = (ri <= ci) if reverse else (ri >= ci)
    a_col = _hdot(causal.astype(F32), da)
    a_row = a_col.T
    tot_row = a_col[0:1, :] if reverse else a_col[qn - 1:qn, :]

    dt_x = _expand_heads(dt, gw, lane0)
    ea_x = _expand_heads(jnp.exp(a_col), gw, lane0)
    de_x = _expand_heads(jnp.exp(tot_row - a_col), gw, lane0)
    et_x = _expand_heads(jnp.exp(tot_row), gw, lane0)
    xd = xs * dt_x
    xdw = (xd * de_x).astype(BF16)
    xdb = xd.astype(BF16)

    hpg = M2_HEADS // M2_GROUPS
    gcols = hpg * M2_HEAD_DIM
    for g in range(M2_GROUPS):
        bg = bm[:, g * M2_STATE:(g + 1) * M2_STATE]
        cg = cm[:, g * M2_STATE:(g + 1) * M2_STATE].astype(BF16)
        bgt = bg.T.astype(BF16)
        scores = _bdot(cg, bgt)
        st_old = st_scr[:, g * gcols:(g + 1) * gcols]
        y_inter = _bdot(cg, st_old.astype(BF16)) * ea_x[:, g * gcols:(g + 1) * gcols]
        st_new = _bdot(bgt, xdw[:, g * gcols:(g + 1) * gcols])
        st_scr[:, g * gcols:(g + 1) * gcols] = et_x[:, g * gcols:(g + 1) * gcols] * st_old + st_new
        for hl in range(hpg):
            hh = g * hpg + hl
            seg = a_col[:, lane0 + hh:lane0 + hh + 1] - a_row[lane0 + hh:lane0 + hh + 1, :]
            decay = jnp.exp(jnp.where(causal, seg, -jnp.inf))
            cols = slice(hh * M2_HEAD_DIM, (hh + 1) * M2_HEAD_DIM)
            y_h = _bdot((scores * decay).astype(BF16), xdb[:, cols])
            y_h = y_h + y_inter[:, hl * M2_HEAD_DIM:(hl + 1) * M2_HEAD_DIM]
            if reverse:
                o_ref[:, cols] = y_h + yf_ref[:, cols]
            else:
                o_ref[:, cols] = y_h + dsk_ref[:, cols] * xs[:, cols]

    if reverse:
        zg = pd_ref[:, nx:nx + gw]
        gv = o_ref[...] * _silu(zg)
        gc = gw // M2_GROUPS
        for g in range(M2_GROUPS):
            part = gv[:, g * gc:(g + 1) * gc]
            ms = jnp.mean(jnp.square(part), axis=-1, keepdims=True)
            o_ref[:, g * gc:(g + 1) * gc] = part * lax.rsqrt(ms + RMS_EPS) * nw_ref[:, g * gc:(g + 1) * gc]


def _ssd(pd, params, layer, nlat):
    conv_w, conv_b, dtb, a_dir, dsk, norm_w = params
    nb, t, wd = pd.shape
    qn = M2_CHUNK
    nchunks = t // qn
    nlat_chunks = nlat // qn
    hb = qn // M2_HALO
    nh = t // M2_HALO
    gw = GROUP_W

    def chunk_of(k, reverse):
        if reverse:
            return nchunks - 1 - k
        return jnp.where(k < nchunks - nlat_chunks, k + nlat_chunks, k - (nchunks - nlat_chunks))

    def call(reverse, yf):
        def rows(n):
            return pl.BlockSpec((None, qn, n), lambda b, k: (b, chunk_of(k, reverse), 0))

        in_specs = [
            rows(wd),
            pl.BlockSpec((None, M2_HALO, M2_XBC),
                         lambda b, k: (b, jnp.maximum(chunk_of(k, reverse) * hb - 1, 0), 0)),
            pl.BlockSpec((None, M2_HALO, M2_XBC),
                         lambda b, k: (b, jnp.minimum((chunk_of(k, reverse) + 1) * hb, nh - 1), 0)),
            _layer_spec(conv_w, layer), _layer_spec(conv_b, layer), _layer_spec(dtb, layer),
            pl.BlockSpec((None, None, 1, LANE), lambda b, k: (layer, 1 if reverse else 0, 0, 0)),
            _layer_spec(dsk, layer),
        ]
        args = [pd, pd, pd, conv_w, conv_b, dtb, a_dir, dsk]
        if reverse:
            in_specs += [_layer_spec(norm_w, layer), rows(gw)]
            args += [norm_w, yf]
        return pl.pallas_call(
            functools.partial(_ssd_kernel, reverse=reverse, nlat_chunks=nlat_chunks, nchunks=nchunks),
            grid=(nb, nchunks),
            in_specs=in_specs,
            out_specs=rows(gw),
            out_shape=jax.ShapeDtypeStruct((nb, t, gw), F32),
            scratch_shapes=[pltpu.VMEM((qn + 2 * M2_HALO, M2_XBC), F32), pltpu.VMEM((M2_STATE, gw), F32)],
            compiler_params=_cparams(("arbitrary", "arbitrary")),
            name="ssd_rev" if reverse else "ssd_fwd",
        )(*args)

    return call(True, call(False, None))


def _layer_norm_rows(v, g, b):
    mu = jnp.mean(v, axis=-1, keepdims=True)
    var = jnp.mean(jnp.square(v - mu), axis=-1, keepdims=True)
    return (v - mu) * lax.rsqrt(var + LN_EPS) * g + b


def _outproj_kernel(mabc_ref, md_ref, hl_ref, hc_ref, gate_ref, w_ref, lg_ref, lb_ref, o_ref,
                    *, alpha, nlat_blocks):
    j = pl.program_id(1)
    k1 = mabc_ref.shape[1]
    mix = (_bdot(mabc_ref[...].astype(BF16), w_ref[0:k1, :])
           + _bdot(md_ref[...].astype(BF16), w_ref[k1:w_ref.shape[0], :]))
    gated = gate_ref[...] * mix

    @pl.when(j < nlat_blocks)
    def _():
        o_ref[...] = _layer_norm_rows(alpha * hl_ref[...] + gated, lg_ref[...], lb_ref[...])

    @pl.when(j >= nlat_blocks)
    def _():
        o_ref[...] = _layer_norm_rows(alpha * hc_ref[...] + gated, lg_ref[...], lb_ref[...])


def _outproj(mabc, md, h_lat, h_ctx, mods, w, ln_g, ln_b, layer, rows_used, alpha):
    nb, nlat, d = h_lat.shape
    tm = ROW_TILE
    k1, k2 = mabc.shape[2], md.shape[2]
    lat_spec, ctx_spec = _stream_specs(tm, d, nlat // tm)

    def row(n):
        return pl.BlockSpec((None, tm, n), lambda b, j: (b, j, 0))

    return pl.pallas_call(
        functools.partial(_outproj_kernel, alpha=alpha, nlat_blocks=nlat // tm),
        grid=(nb, rows_used // tm),
        in_specs=[row(k1), row(k2), lat_spec, ctx_spec, _mod_spec(d, 2, layer, nb, nlat // tm),
                  _layer_spec(w, layer), _layer_spec(ln_g, layer), _layer_spec(ln_b, layer)],
        out_specs=row(d),
        out_shape=jax.ShapeDtypeStruct((nb, rows_used, d), F32),
        compiler_params=_cparams(("arbitrary", "arbitrary")),
        name="outproj",
    )(mabc, md, h_lat, h_ctx, mods, w, ln_g, ln_b)


def _ffn_kernel(*refs, alpha, grid_conv, halo):
    if grid_conv:
        (h_ref, prev_ref, next_ref, sh_ref, sc_ref, gate_ref, wg_ref, wv_ref, cw_ref, cb_ref, wd_ref,
         lg_ref, lb_ref, o_ref, xm_scr, g_scr, v_scr, side_scr, acc_scr) = refs
    else:
        (h_ref, sh_ref, sc_ref, gate_ref, wg_ref, wv_ref, cw_ref, cb_ref, wd_ref,
         lg_ref, lb_ref, o_ref, xm_scr, g_scr, v_scr, side_scr, acc_scr) = refs
    j = pl.program_id(1)
    f = pl.program_id(2)
    nj = pl.num_programs(1)
    nf = pl.num_programs(2)
    tm = h_ref.shape[0]
    sub = g_scr.shape[2]
    nsub = g_scr.shape[0]
    pad = SUBLANE

    @pl.when(f == 0)
    def _():
        def modulated(v):
            return (v * (1.0 + sc_ref[...]) + sh_ref[...]).astype(BF16)
        xm_scr[halo:halo + tm, :] = modulated(h_ref[...])
        if grid_conv:
            xm_scr[0:halo, :] = modulated(prev_ref[...])
            xm_scr[halo + tm:halo + tm + halo, :] = modulated(next_ref[...])
        acc_scr[...] = jnp.zeros_like(acc_scr)
        side_scr[:, :, 0:pad, :] = jnp.zeros((nsub, 2, pad, sub), F32)
        side_scr[:, :, pad + tm:pad + tm + pad, :] = jnp.zeros((nsub, 2, pad, sub), F32)

    pos = lax.broadcasted_iota(jnp.int32, (tm, 1), 0)
    if grid_conv:
        keep_prev = (j > 0).astype(F32)
        keep_next = (j < nj - 1).astype(F32)
        pos = pos % GRID_W
        left_ok, right_ok = pos != 0, pos != GRID_W - 1
        row_taps = (-1, 0, 1)
    else:
        left_ok, right_ok = pos != 0, pos != tm - 1
        row_taps = (0,)

    for si in range(nsub):
        cs = slice(si * sub, (si + 1) * sub)
        ge = _bdot(xm_scr[...], wg_ref[:, cs])
        if grid_conv:
            g_scr[si, 0:halo, :] = ge[0:halo] * keep_prev
            g_scr[si, halo:halo + tm, :] = ge[halo:halo + tm]
            g_scr[si, halo + tm:halo + tm + halo, :] = ge[halo + tm:halo + tm + halo] * keep_next
        else:
            g_scr[si] = ge
        v_scr[si] = _bdot(xm_scr[halo:halo + tm, :], wv_ref[:, cs])

    acts = []
    for si in range(nsub):
        cs = slice(si * sub, (si + 1) * sub)

        def tap_sum(dc):
            tot = None
            for dr in row_taps:
                k = (dr + 1) * 3 + (dc + 1)
                r0 = halo + dr * GRID_W
                term = g_scr[si, r0:r0 + tm, :] * cw_ref[k:k + 1, cs]
                tot = term if tot is None else tot + term
            return tot

        side_scr[si, 0, pad:pad + tm, :] = tap_sum(-1)
        side_scr[si, 1, pad:pad + tm, :] = tap_sum(1)
        conv = tap_sum(0) + cb_ref[:, cs]
        conv = conv + jnp.where(left_ok, side_scr[si, 0, pad - 1:pad - 1 + tm, :], 0.0)
        conv = conv + jnp.where(right_ok, side_scr[si, 1, pad + 1:pad + 1 + tm, :], 0.0)
        acts.append((_gelu(conv) * v_scr[si]).astype(BF16))
    act = acts[0] if nsub == 1 else jnp.concatenate(acts, axis=1)
    acc_scr[...] += _bdot(act, wd_ref[...])

    @pl.when(f == nf - 1)
    def _():
        o_ref[...] = _layer_norm_rows(alpha * h_ref[...] + gate_ref[...] * acc_scr[...], lg_ref[...], lb_ref[...])


def _ffn(h, mods, w_up, conv_w9, conv_b, w_down, ln_g, ln_b, layer, nlat, alpha, grid_conv):
    nb, t, d = h.shape
    fh = w_down.shape[1]
    tf = FFN_COL_TILE
    sub = FFN_SUB_TILE
    nf = fh // tf
    if grid_conv:
        tm, halo = FFN_ROW_TILE, GRID_W
        nblk, blk0 = nlat // tm, 0
    else:
        tm, halo = t - nlat, 0
        nblk, blk0 = 1, nlat // tm
    hb = tm // GRID_W
    nhalo_blocks = nlat // GRID_W

    def vec(k):
        return pl.BlockSpec((None, None, 1, d), lambda b, j, f: (layer, b if grid_conv else nb, 0, k))

    in_specs = [pl.BlockSpec((None, tm, d), lambda b, j, f: (b, j + blk0, 0))]
    args = [h]
    if grid_conv:
        in_specs += [
            pl.BlockSpec((None, halo, d), lambda b, j, f: (b, jnp.maximum(j * hb - 1, 0), 0)),
            pl.BlockSpec((None, halo, d), lambda b, j, f: (b, jnp.minimum((j + 1) * hb, nhalo_blocks - 1), 0)),
        ]
        args += [h, h]
    in_specs += [
        vec(3), vec(4), vec(5),
        pl.BlockSpec((None, d, tf), lambda b, j, f: (layer, 0, f)),
        pl.BlockSpec((None, d, tf), lambda b, j, f: (layer, 0, nf + f)),
        pl.BlockSpec((None, 9, tf), lambda b, j, f: (layer, 0, f)),
        pl.BlockSpec((None, 1, tf), lambda b, j, f: (layer, 0, f)),
        pl.BlockSpec((None, tf, d), lambda b, j, f: (layer, f, 0)),
        _layer_spec(ln_g, layer), _layer_spec(ln_b, layer),
    ]
    args += [mods, mods, mods, w_up, w_up, conv_w9, conv_b, w_down, ln_g, ln_b]
    return pl.pallas_call(
        functools.partial(_ffn_kernel, alpha=alpha, grid_conv=grid_conv, halo=halo),
        grid=(nb, nblk, nf),
        in_specs=in_specs,
        out_specs=pl.BlockSpec((None, tm, d), lambda b, j, f: (b, j, 0)),
        out_shape=jax.ShapeDtypeStruct((nb, nblk * tm, d), F32),
        scratch_shapes=[pltpu.VMEM((tm + 2 * halo, d), BF16),
                        pltpu.VMEM((tf // sub, tm + 2 * halo, sub), F32),
                        pltpu.VMEM((tf // sub, tm, sub), F32),
                        pltpu.VMEM((tf // sub, 2, tm + 2 * SUBLANE, sub), F32),
                        pltpu.VMEM((tm, d), F32)],
        compiler_params=_cparams(("arbitrary", "arbitrary", "arbitrary")),
        name="ffn_lat" if grid_conv else "ffn_ctx",
    )(*args)


def kernel(x, c, ctx, c_ctx, w_ada, b_ada, w_in, w_out, ln1_g, ln1_b, ln2_g, ln2_b, s5_a_re, s5_a_im, s5_b_re, s5_b_im, s5_c_re, s5_c_im, s5_log_step, s5_d, s5_glu_w, s5_glu_b, sg_ln_g, sg_ln_b, sg_w, sg_b, pool_w, pool_b, pool_scale, m2_conv_w, m2_conv_b, m2_dt_bias, m2_a_log, m2_d, m2_norm_w, ffn_w_up, ffn_conv_w, ffn_conv_b, ffn_w_down):
    nb, nlat, d = x.shape
    nctx = ctx.shape[1]
    depth = w_ada.shape[0]
    gw = GROUP_W
    alpha = (2 * depth) ** 0.25
    assert nb + 1 <= MOD_ROWS and nlat % FFN_ROW_TILE == 0 and nctx % ROW_TILE == 0
    assert nlat % (GRID_W * SUBLANE) == 0 and FFN_ROW_TILE % GRID_W == 0
    assert w_in.shape[2] == 5 * gw + M2_XBC + 2 * M2_HEADS

    cmat = jnp.concatenate([c, c_ctx[None, :], jnp.zeros((MOD_ROWS - nb - 1, d), F32)], axis=0)
    mods = _ada(cmat, w_ada, b_ada).reshape(depth, MOD_ROWS, 1, 6 * d)

    def rows(v):
        return v.astype(F32).reshape(depth, 1, -1)

    fh = ffn_w_down.shape[1]
    lane_pad = LANE - 2 * M2_HEADS
    w_in_b = jnp.pad(w_in, ((0, 0), (0, 0), (0, lane_pad))).astype(BF16)
    w_out_b = w_out.astype(BF16)
    w_up_b = ffn_w_up.astype(BF16)
    w_down_b = ffn_w_down.astype(BF16)
    cw9 = ffn_conv_w.astype(F32).reshape(depth, 9, fh)
    s5_mats = _s5_mats(s5_a_re, s5_a_im, s5_b_re, s5_b_im, s5_c_re, s5_c_im, s5_log_step, s5_d)
    sg_btile = jnp.repeat(jnp.swapaxes(sg_b.astype(F32), 1, 2), gw // SG_HEADS, axis=2)
    mixer_params = (s5_glu_w.astype(BF16), rows(s5_glu_b), rows(sg_ln_g), rows(sg_ln_b), sg_w.astype(BF16),
                    sg_btile, pool_w.astype(BF16), rows(pool_b), rows(pool_scale))
    m2_a = -jnp.exp(m2_a_log.astype(F32))
    a_dir = (m2_a[:, :, None, :] * jnp.eye(2, dtype=F32)[None, :, :, None]).reshape(depth, 2, 1, 2 * M2_HEADS)
    a_dir = jnp.pad(a_dir, ((0, 0), (0, 0), (0, 0), (0, lane_pad)))
    ssd_params = (m2_conv_w.astype(F32), rows(m2_conv_b),
                  jnp.pad(m2_dt_bias.astype(F32).reshape(depth, 1, 2 * M2_HEADS), ((0, 0), (0, 0), (0, lane_pad))),
                  a_dir, rows(jnp.repeat(m2_d.astype(F32), M2_HEAD_DIM, axis=1)), rows(m2_norm_w))
    ln1 = rows(ln1_g), rows(ln1_b)
    ln2 = rows(ln2_g), rows(ln2_b)
    ffn_cb = rows(ffn_conv_b)

    h_lat, h_ctx = x, ctx
    for i in range(depth):
        need_ctx = i < depth - 1
        pa, pb, pd = _inproj(h_lat, h_ctx, mods, w_in_b, i)
        ya = _s5(pa, s5_mats, i, nlat)
        mabc = _mixers(ya, pb, mixer_params, i, nlat)
        md = _ssd(pd, ssd_params, i, nlat)
        rows_used = nlat + nctx if need_ctx else nlat
        h1 = _outproj(mabc, md, h_lat, h_ctx, mods, w_out_b, ln1[0], ln1[1], i, rows_used, alpha)
        h_lat = _ffn(h1, mods, w_up_b, cw9, ffn_cb, w_down_b, ln2[0], ln2[1], i, nlat, alpha, True)
        if need_ctx:
            h_ctx = _ffn(h1, mods, w_up_b, cw9, ffn_cb, w_down_b, ln2[0], ln2[1], i, nlat, alpha, False)
    return h_lat
```

```python
import functools
import math

import jax
import jax.numpy as jnp
from jax import lax
from jax.experimental import pallas as pl
from jax.experimental.pallas import tpu as pltpu

F32 = jnp.float32
BF16 = jnp.bfloat16
HIGHEST = lax.Precision.HIGHEST

GRID_W = 64
GROUP_W = 512
S5_CH = 16
S5_Q = 16
S5_RELAYOUT_UNROLL = 4
SG_HEADS = 4
SG_CHUNK = 128
POOL_WINDOWS = (2, 4, 8, 16)
POOL_DIM = GROUP_W // len(POOL_WINDOWS)
POOL_HALO = 8
M2_HEAD_DIM = 64
M2_HEADS = GROUP_W // M2_HEAD_DIM
M2_STATE = 128
M2_GROUPS = 2
M2_CONV = 4
M2_CHUNK = 128
M2_XBC = GROUP_W + 2 * M2_GROUPS * M2_STATE
M2_HALO = 8
LN_EPS = 1e-5
RMS_EPS = 1e-5

LANE = 128
SUBLANE = 8
VMEM_LIMIT = 56 * 1024 * 1024
ROW_TILE = 256
FFN_ROW_TILE = 512
FFN_COL_TILE = 512
FFN_SUB_TILE = 256
ADA_COL_TILE = 1024
MOD_ROWS = 8


def _cparams(sem):
    return pltpu.CompilerParams(dimension_semantics=sem, vmem_limit_bytes=VMEM_LIMIT)


def _bdot(a, b):
    return jnp.dot(a, b, preferred_element_type=F32)


def _hdot(a, b):
    return jnp.dot(a, b, preferred_element_type=F32, precision=HIGHEST)


def _gelu(x):
    return jax.nn.gelu(x)


def _silu(x):
    return x * jax.nn.sigmoid(x)


def _ada_kernel(c_ref, w_ref, b_ref, o_ref):
    c = c_ref[...]
    s = _silu(c).astype(BF16)
    o_ref[...] = _bdot(s, w_ref[...].astype(BF16)) + b_ref[...]


def _ada(cmat, w_ada, b_ada):
    depth, d, n = w_ada.shape
    tn = ADA_COL_TILE
    return pl.pallas_call(
        _ada_kernel,
        grid=(depth, n // tn),
        in_specs=[
            pl.BlockSpec((MOD_ROWS, d), lambda l, j: (0, 0)),
            pl.BlockSpec((None, d, tn), lambda l, j: (l, 0, j)),
            pl.BlockSpec((None, 1, tn), lambda l, j: (l, 0, j)),
        ],
        out_specs=pl.BlockSpec((None, MOD_ROWS, tn), lambda l, j: (l, 0, j)),
        out_shape=jax.ShapeDtypeStruct((depth, MOD_ROWS, n), F32),
        compiler_params=_cparams(("arbitrary", "arbitrary")),
        name="ada",
    )(cmat, w_ada, b_ada.reshape(depth, 1, n))


def _mod_spec(d, k, layer, nbatch, nlat_blocks):
    return pl.BlockSpec((None, None, 1, d),
                        lambda b, j, *_: (layer, jnp.where(j >= nlat_blocks, nbatch, b), 0, k))


def _layer_spec(arr, layer):
    tail = arr.shape[1:]
    return pl.BlockSpec((None,) + tail, lambda *_: (layer,) + (0,) * len(tail))


def _stream_specs(tm, d, nlat_blocks):
    lat = pl.BlockSpec((None, tm, d), lambda b, j: (b, jnp.minimum(j, nlat_blocks - 1), 0))
    ctx = pl.BlockSpec((None, tm, d), lambda b, j: (b, jnp.maximum(j - nlat_blocks, 0), 0))
    return lat, ctx


def _inproj_kernel(hl_ref, hc_ref, sh_ref, sc_ref, w_ref, wdt_ref, oa_ref, ob_ref, od_ref, xm_scr, *, nlat_blocks):
    j = pl.program_id(1)

    def modulated(h_ref):
        return (h_ref[...] * (1.0 + sc_ref[...]) + sh_ref[...]).astype(BF16)

    @pl.when(j < nlat_blocks)
    def _():
        xm_scr[...] = modulated(hl_ref)

    @pl.when(j >= nlat_blocks)
    def _():
        xm_scr[...] = modulated(hc_ref)

    xm = xm_scr[...]
    gw = GROUP_W
    oa_ref[...] = _bdot(xm, w_ref[:, 0:gw])
    ob_ref[...] = _bdot(xm, w_ref[:, gw:4 * gw])
    od_ref[:, 0:M2_XBC] = _bdot(xm, w_ref[:, 5 * gw:5 * gw + M2_XBC])
    od_ref[:, M2_XBC:M2_XBC + gw] = _bdot(xm, w_ref[:, 4 * gw:5 * gw])
    od_ref[:, M2_XBC + gw:M2_XBC + gw + LANE] = _bdot(xm, wdt_ref[...])


def _inproj(h_lat, h_ctx, mods, w, w_dt, layer):
    nb, nlat, d = h_lat.shape
    t = nlat + h_ctx.shape[1]
    tm = ROW_TILE
    na, nbw, nd = GROUP_W, 3 * GROUP_W, M2_XBC + GROUP_W + LANE
    lat_spec, ctx_spec = _stream_specs(tm, d, nlat // tm)

    def row(n):
        return pl.BlockSpec((None, tm, n), lambda b, j: (b, j, 0))

    return pl.pallas_call(
        functools.partial(_inproj_kernel, nlat_blocks=nlat // tm),
        grid=(nb, t // tm),
        in_specs=[lat_spec, ctx_spec, _mod_spec(d, 0, layer, nb, nlat // tm),
                  _mod_spec(d, 1, layer, nb, nlat // tm), _layer_spec(w, layer), _layer_spec(w_dt, layer)],
        out_specs=[row(na), row(nbw), row(nd)],
        out_shape=[jax.ShapeDtypeStruct((nb, t, n), F32) for n in (na, nbw, nd)],
        scratch_shapes=[pltpu.VMEM((tm, d), BF16)],
        compiler_params=_cparams(("arbitrary", "arbitrary")),
        name="inproj",
    )(h_lat, h_ctx, mods, mods, w, w_dt)


def _s5_mats(a_re, a_im, b_re, b_im, c_re, c_im, log_step, d_skip):
    q, hch = S5_Q, S5_CH
    nl, _, g, p = a_re.shape
    lre, lim = a_re.astype(F32), a_im.astype(F32)
    step = jnp.exp(log_step.astype(F32))[..., None]
    mag = jnp.exp(lre * step)
    ar, ai = mag * jnp.cos(lim * step), mag * jnp.sin(lim * step)
    den = lre * lre + lim * lim
    qr = ((ar - 1.0) * lre + ai * lim) / den
    qi = (ai * lre - (ar - 1.0) * lim) / den
    bre, bim = b_re.astype(F32), b_im.astype(F32)
    bbr = qr[..., None] * bre - qi[..., None] * bim
    bbi = qr[..., None] * bim + qi[..., None] * bre
    cre, cim = c_re.astype(F32), c_im.astype(F32)

    pw = [(jnp.ones_like(ar), jnp.zeros_like(ai))]
    for _ in range(q):
        pr, pi = pw[-1]
        pw.append((pr * ar - pi * ai, pr * ai + pi * ar))
    pk_re = jnp.stack([v[0] for v in pw], axis=3)
    pk_im = jnp.stack([v[1] for v in pw], axis=3)
    pt_re = jnp.stack([v[0] for v in pw], axis=4)
    pt_im = jnp.stack([v[1] for v in pw], axis=4)

    def lag_kernels(d, pr, pi):
        ctr = jnp.swapaxes(cre[:, d], -1, -2)[:, :, :, None]
        cti = jnp.swapaxes(cim[:, d], -1, -2)[:, :, :, None]
        ca_re = (ctr * pr[..., None] - cti * pi[..., None]).reshape(nl, g, p, 1, q * hch)
        ca_im = (ctr * pi[..., None] + cti * pr[..., None]).reshape(nl, g, p, 1, q * hch)
        return jnp.sum(bbr[:, d][..., None] * ca_re - bbi[:, d][..., None] * ca_im, axis=2)

    lag_f = lag_kernels(0, pt_re[:, 0, :, :, :q], pt_im[:, 0, :, :, :q])
    lag_r = lag_kernels(1, jnp.flip(pt_re[:, 1, :, :, :q], axis=3), jnp.flip(pt_im[:, 1, :, :, :q], axis=3))
    dmat = jnp.eye(hch, dtype=F32) * d_skip.astype(F32).reshape(nl, g, 1, hch)
    mid = lag_r[..., (q - 1) * hch:] + lag_f[..., :hch] + dmat
    kk = jnp.concatenate([lag_r[..., :(q - 1) * hch], mid, lag_f[..., hch:]], axis=-1)
    tmat = jnp.stack([kk[..., (q - 1 - s) * hch:(2 * q - 1 - s) * hch] for s in range(q)], axis=2)
    tmat = tmat.reshape(nl, g, q * hch, q * hch)

    def state_in(d, pr, pi):
        btr = jnp.swapaxes(bbr[:, d], -1, -2)[:, :, None]
        bti = jnp.swapaxes(bbi[:, d], -1, -2)[:, :, None]
        pr, pi = pr[:, :, :, None], pi[:, :, :, None]
        m = jnp.concatenate([pr * btr - pi * bti, pr * bti + pi * btr], axis=-1)
        return m.reshape(nl, g, q * hch, 2 * p)

    sb = jnp.concatenate([state_in(0, jnp.flip(pk_re[:, 0, :, :q], axis=2), jnp.flip(pk_im[:, 0, :, :q], axis=2)),
                          state_in(1, pk_re[:, 1, :, :q], pk_im[:, 1, :, :q])], axis=-1)

    def state_out(d, pr, pi):
        ctr = jnp.swapaxes(cre[:, d], -1, -2)[:, :, :, None]
        cti = jnp.swapaxes(cim[:, d], -1, -2)[:, :, :, None]
        pr, pi = pr[..., None], pi[..., None]
        m = jnp.concatenate([ctr * pr - cti * pi, -(ctr * pi + cti * pr)], axis=2)
        return m.reshape(nl, g, 2 * p, q * hch)

    w2 = jnp.concatenate([state_out(0, pt_re[:, 0, :, :, 1:], pt_im[:, 0, :, :, 1:]),
                          state_out(1, jnp.flip(pt_re[:, 1, :, :, 1:], axis=3), jnp.flip(pt_im[:, 1, :, :, 1:], axis=3))],
                         axis=2)

    aqr, aqi = pw[q]
    a1 = jnp.concatenate([aqr, aqr], axis=-1).reshape(nl, 2, g * 2 * p)
    a2 = jnp.concatenate([-aqi, aqi], axis=-1).reshape(nl, 2, g * 2 * p)
    acoef = jnp.stack([a1[:, 0], a2[:, 0], a1[:, 1], a2[:, 1]], axis=1)
    return tmat, sb, w2, acoef


def _granule_transpose(v):
    n = len(v)
    slot = lax.broadcasted_iota(jnp.int32, v[0].shape, 1) // S5_CH
    v = list(v)
    j = n // 2
    while j >= 1:
        keep = (slot & j) == 0
        for i in range(n):
            if i & j == 0:
                a, b = v[i], v[i + j]
                v[i] = jnp.where(keep, a, pltpu.roll(b, j * S5_CH, axis=1))
                v[i + j] = jnp.where(keep, pltpu.roll(a, LANE - j * S5_CH, axis=1), b)
        j //= 2
    return v


def _split_bf16(x):
    hi = x.astype(BF16)
    return hi, (x - hi.astype(F32)).astype(BF16)


def _dot3(xs, w):
    x_hi, x_lo = xs
    w_hi, w_lo = _split_bf16(w)
    return _bdot(x_hi, w_hi) + (_bdot(x_lo, w_hi) + _bdot(x_hi, w_lo))


def _s5_kernel(pa_ref, t_ref, sb_ref, w2_ref, ac_ref, ya_ref, x_scr, s_scr, h_scr, *, nbatch, nrows, nctx):
    gb = LANE // S5_CH
    halves = S5_Q // gb
    qh = S5_Q * S5_CH
    sw = LANE
    nlat = nrows - nctx
    nrc = nrows // SUBLANE
    tok_per_rc = SUBLANE * S5_Q

    def relayout(it, to_rows):
        for u in range(S5_RELAYOUT_UNROLL):
            idx = it * S5_RELAYOUT_UNROLL + u
            b = idx // nrc
            rc = idx % nrc
            r0 = pl.multiple_of(b * nrows + rc * SUBLANE, SUBLANE)
            for half in range(halves):
                lanes = slice(half * LANE, (half + 1) * LANE)

                def tok(s8, half=half, rc=rc):
                    return pl.ds(rc * tok_per_rc + half * gb + s8, SUBLANE, stride=S5_Q)

                if to_rows:
                    v = _granule_transpose([pa_ref[b, tok(s8), :] for s8 in range(gb)])
                    for gi in range(gb):
                        x_scr[gi, pl.ds(r0, SUBLANE), lanes] = v[gi]
                else:
                    v = _granule_transpose([x_scr[gi, pl.ds(r0, SUBLANE), lanes] for gi in range(gb)])
                    for s8 in range(gb):
                        ya_ref[b, tok(s8), :] = v[s8]
        return 0

    nblocks = nbatch * nrc // S5_RELAYOUT_UNROLL
    lax.fori_loop(0, nblocks, lambda it, c: relayout(it, True), 0)

    for gi in range(gb):
        xs = _split_bf16(x_scr[gi])
        x_scr[gi] = _dot3(xs, t_ref[gi])
        rs = _dot3(xs, sb_ref[gi])
        for d in range(2):
            sd = rs[:, d * sw:(d + 1) * sw]
            s_scr[d, gi] = sd
            s_scr[2 + d, gi] = pltpu.roll(sd, sw // 2, axis=1)

    zero = jnp.zeros((nbatch, sw), F32)

    def body(s, carry):
        f_rows = pl.ds(jnp.where(s < nctx, s + nlat, s - nctx), nbatch, stride=nrows)
        r_rows = pl.ds(nrows - 1 - s, nbatch, stride=nrows)
        out = []
        for gi in range(gb):
            hf, hfs, hr, hrs = carry[4 * gi:4 * gi + 4]
            a1f, a2f, a1r, a2r = (ac_ref[k:k + 1, gi * sw:(gi + 1) * sw] for k in range(4))
            h_scr[0, gi, f_rows, :] = hf
            h_scr[1, gi, r_rows, :] = hr
            out += [a1f * hf + a2f * hfs + s_scr[0, gi, f_rows, :],
                    a1f * hfs - a2f * hf + s_scr[2, gi, f_rows, :],
                    a1r * hr + a2r * hrs + s_scr[1, gi, r_rows, :],
                    a1r * hrs - a2r * hr + s_scr[3, gi, r_rows, :]]
        return tuple(out)

    lax.fori_loop(0, nrows, body, (zero,) * (4 * gb))

    for gi in range(gb):
        hin = jnp.concatenate([h_scr[0, gi], h_scr[1, gi]], axis=1)
        x_scr[gi] = x_scr[gi] + _dot3(_split_bf16(hin), w2_ref[gi])

    lax.fori_loop(0, nblocks, lambda it, c: relayout(it, False), 0)


def _s5(pa, mats, layer, nlat):
    tmat, sb, w2, acoef = mats
    nb, t, gw = pa.shape
    q = S5_Q
    nrows = t // q
    qh = q * S5_CH
    gb = LANE // S5_CH
    assert nrows % SUBLANE == 0 and q % gb == 0 and w2.shape[2] == 2 * LANE and sb.shape[3] == 2 * LANE
    assert (nb * nrows // SUBLANE) % S5_RELAYOUT_UNROLL == 0
    return pl.pallas_call(
        functools.partial(_s5_kernel, nbatch=nb, nrows=nrows, nctx=(t - nlat) // q),
        grid=(gw // LANE,),
        in_specs=[
            pl.BlockSpec((nb, t, LANE), lambda i: (0, 0, i)),
            pl.BlockSpec((None, gb, qh, qh), lambda i: (layer, i, 0, 0)),
            pl.BlockSpec((None, gb, qh, 2 * LANE), lambda i: (layer, i, 0, 0)),
            pl.BlockSpec((None, gb, 2 * LANE, qh), lambda i: (layer, i, 0, 0)),
            pl.BlockSpec((None, 4, gb * LANE), lambda i: (layer, 0, i)),
        ],
        out_specs=pl.BlockSpec((nb, t, LANE), lambda i: (0, 0, i)),
        out_shape=jax.ShapeDtypeStruct((nb, t, gw), F32),
        scratch_shapes=[pltpu.VMEM((gb, nb * nrows, qh), F32),
                        pltpu.VMEM((4, gb, nb * nrows, LANE), F32),
                        pltpu.VMEM((2, gb, nb * nrows, LANE), F32)],
        compiler_params=_cparams(("arbitrary",)),
        name="s5",
    )(pa, tmat, sb, w2, acoef)


def _mixers_kernel(ya_ref, pb_ref, prev_ref, next_ref, gluw_ref, glub_ref, lng_ref, lnb_ref, sgw_ref,
                   sgb_ref, pw_ref, pbias_ref, pscale_ref, o_ref, ext_scr, *, nlat_blocks, nblocks, nlat, nctx):
    j = pl.program_id(1)
    tm = ya_ref.shape[0]
    gw = GROUP_W

    z = _gelu(ya_ref[...])
    gate = jax.nn.sigmoid(_bdot(z.astype(BF16), gluw_ref[...]) + glub_ref[...])
    o_ref[:, 0:gw] = z * gate

    hd = gw // SG_HEADS
    u = _gelu(pb_ref[:, 0:gw])
    v = _gelu(pb_ref[:, gw:2 * gw])
    for hh in range(SG_HEADS):
        vh = v[:, hh * hd:(hh + 1) * hd]
        mu = jnp.mean(vh, axis=-1, keepdims=True)
        var = jnp.mean(jnp.square(vh - mu), axis=-1, keepdims=True)
        vn = ((vh - mu) * lax.rsqrt(var + LN_EPS) * lng_ref[:, hh * hd:(hh + 1) * hd]
              + lnb_ref[:, hh * hd:(hh + 1) * hd]).astype(BF16)
        for cchunk in range(tm // SG_CHUNK):
            rows = slice(cchunk * SG_CHUNK, (cchunk + 1) * SG_CHUNK)
            s = _bdot(sgw_ref[hh], vn[rows]) + sgb_ref[:, hh * hd:(hh + 1) * hd]
            o_ref[rows, gw + hh * hd:gw + (hh + 1) * hd] = u[rows, hh * hd:(hh + 1) * hd] * s

    is_first = jnp.logical_or(j == 0, j == nlat_blocks)
    is_last = jnp.logical_or(j == nlat_blocks - 1, j == nblocks - 1)
    p = pb_ref[:, 2 * gw:3 * gw]
    hal = POOL_HALO
    ext_scr[0:hal, :] = jnp.where(is_first, 0.0, prev_ref[...])
    ext_scr[hal:hal + tm, :] = p
    ext_scr[hal + tm:hal + tm + hal, :] = jnp.where(is_last, 0.0, next_ref[...])
    in_ctx = j >= nlat_blocks
    seq_len = jnp.where(in_ctx, nctx, nlat)
    t0 = (j - jnp.where(in_ctx, nlat_blocks, 0)) * tm
    tpos = t0 + lax.broadcasted_iota(jnp.int32, (tm, 1), 0)
    pd = POOL_DIM
    for gi, win in enumerate(POOL_WINDOWS):
        cols = slice(gi * pd, (gi + 1) * pd)
        tot = jnp.zeros((tm, pd), F32)
        for off in range(-(win // 2), win // 2):
            tot = tot + ext_scr[hal + off:hal + off + tm, cols]
        lo = jnp.maximum(tpos - win // 2, 0)
        hi = jnp.minimum(tpos + win // 2 - 1, seq_len - 1)
        mean = tot / (hi - lo + 1).astype(F32)
        yv = _bdot((mean - p[:, cols]).astype(BF16), pw_ref[gi]) + pbias_ref[:, cols]
        o_ref[:, 2 * gw + gi * pd:2 * gw + (gi + 1) * pd] = yv * pscale_ref[:, cols]


def _mixers(ya, pb, params, layer, nlat):
    nb, t, gw = ya.shape
    tm = ROW_TILE
    nblocks = t // tm
    nlat_blocks = nlat // tm
    hb = tm // POOL_HALO
    nh = t // POOL_HALO

    return pl.pallas_call(
        functools.partial(_mixers_kernel, nlat_blocks=nlat_blocks, nblocks=nblocks, nlat=nlat, nctx=t - nlat),
        grid=(nb, nblocks),
        in_specs=[
            pl.BlockSpec((None, tm, gw), lambda b, j: (b, j, 0)),
            pl.BlockSpec((None, tm, 3 * gw), lambda b, j: (b, j, 0)),
            pl.BlockSpec((None, POOL_HALO, gw), lambda b, j: (b, jnp.maximum(j * hb - 1, 0), 2)),
            pl.BlockSpec((None, POOL_HALO, gw), lambda b, j: (b, jnp.minimum((j + 1) * hb, nh - 1), 2)),
        ] + [_layer_spec(v, layer) for v in params],
        out_specs=pl.BlockSpec((None, tm, 3 * gw), lambda b, j: (b, j, 0)),
        out_shape=jax.ShapeDtypeStruct((nb, t, 3 * gw), F32),
        scratch_shapes=[pltpu.VMEM((tm + 2 * POOL_HALO, gw), F32)],
        compiler_params=_cparams(("arbitrary", "arbitrary")),
        name="mixers",
    )(ya, pb, pb, pb, *params)


def _ssd_prep_kernel(pd_ref, prev_ref, next_ref, cw_ref, cb_ref, dtb_ref, xc_ref, dt_ref, ext_scr,
                     *, nlat_blocks, nblocks):
    j = pl.program_id(1)
    tm = pd_ref.shape[0]
    nx = M2_XBC
    hal = M2_HALO
    is_first = jnp.logical_or(j == 0, j == nlat_blocks)
    is_last = jnp.logical_or(j == nlat_blocks - 1, j == nblocks - 1)
    ext_scr[0:hal, :] = jnp.where(is_first, 0.0, prev_ref[...])
    ext_scr[hal:hal + tm, :] = pd_ref[:, 0:nx]
    ext_scr[hal + tm:hal + tm + hal, :] = jnp.where(is_last, 0.0, next_ref[...])
    acc = jnp.zeros((tm, nx), F32) + cb_ref[...]
    for tap in range(M2_CONV):
        off = hal + tap - M2_CONV // 2
        acc = acc + ext_scr[off:off + tm, :] * cw_ref[tap:tap + 1, :]
    xc_ref[...] = _silu(acc)
    xdt = pd_ref[:, nx + GROUP_W:nx + GROUP_W + LANE] + dtb_ref[...]
    dt_ref[...] = jnp.maximum(xdt, 0.0) + jnp.log1p(jnp.exp(-jnp.abs(xdt)))


def _split3_bf16(x):
    x1 = x.astype(BF16)
    r1 = x - x1.astype(F32)
    x2 = r1.astype(BF16)
    return x1, x2, (r1 - x2.astype(F32)).astype(BF16)


def _ssd_direction(reverse, xc_ref, dt_ref, a_ref, dsk_ref, e_ref, st_scr, o_ref):
    qn = M2_CHUNK
    gw = GROUP_W
    lane0 = M2_HEADS if reverse else 0
    xs = xc_ref[:, 0:gw]
    nbc = M2_GROUPS * M2_STATE
    bm = xc_ref[:, gw:gw + nbc]
    cm = xc_ref[:, gw + nbc:gw + 2 * nbc]

    dt = dt_ref[...]
    da = dt * a_ref[...]
    ri = lax.broadcasted_iota(jnp.int32, (qn, qn), 0)
    ci = lax.broadcasted_iota(jnp.int32, (qn, qn), 1)
    causal = (ri <= ci) if reverse else (ri >= ci)
    a_col = _hdot(causal.astype(F32), da)
    a_row = a_col.T
    tot_row = a_col[0:1, :] if reverse else a_col[qn - 1:qn, :]

    per_head = jnp.concatenate([dt, jnp.exp(a_col), jnp.exp(tot_row - a_col)], axis=0)
    p1, p2, p3 = _split3_bf16(per_head)
    e = e_ref[...]
    per_ch = _bdot(p1, e) + (_bdot(p2, e) + _bdot(p3, e))
    dt_x, ea_x, de_x = per_ch[0:qn], per_ch[qn:2 * qn], per_ch[2 * qn:3 * qn]
    et_x = ea_x[0:1, :] if reverse else ea_x[qn - 1:qn, :]
    xd = xs * dt_x
    xdw = (xd * de_x).astype(BF16)
    xdb = xd.astype(BF16)

    hpg = M2_HEADS // M2_GROUPS
    gcols = hpg * M2_HEAD_DIM
    for g in range(M2_GROUPS):
        bg = bm[:, g * M2_STATE:(g + 1) * M2_STATE]
        cg = cm[:, g * M2_STATE:(g + 1) * M2_STATE].astype(BF16)
        bgt = bg.T.astype(BF16)
        scores = _bdot(cg, bgt)
        st_old = st_scr[:, g * gcols:(g + 1) * gcols]
        y_inter = _bdot(cg, st_old.astype(BF16)) * ea_x[:, g * gcols:(g + 1) * gcols]
        st_new = _bdot(bgt, xdw[:, g * gcols:(g + 1) * gcols])
        st_scr[:, g * gcols:(g + 1) * gcols] = et_x[:, g * gcols:(g + 1) * gcols] * st_old + st_new
        for hl in range(hpg):
            hh = g * hpg + hl
            seg = a_col[:, lane0 + hh:lane0 + hh + 1] - a_row[lane0 + hh:lane0 + hh + 1, :]
            decay = jnp.exp(jnp.where(causal, seg, -jnp.inf))
            cols = slice(hh * M2_HEAD_DIM, (hh + 1) * M2_HEAD_DIM)
            y_h = _bdot((scores * decay).astype(BF16), xdb[:, cols])
            y_h = y_h + y_inter[:, hl * M2_HEAD_DIM:(hl + 1) * M2_HEAD_DIM]
            if reverse:
                o_ref[:, cols] = y_h
            else:
                o_ref[:, cols] = y_h + dsk_ref[:, cols] * xs[:, cols]


def _ssd_kernel(xcf_ref, dtf_ref, xcr_ref, dtr_ref, af_ref, ar_ref, dsk_ref, ef_ref, er_ref,
                yf_ref, yr_ref, stf_scr, str_scr):
    @pl.when(pl.program_id(1) == 0)
    def _():
        stf_scr[...] = jnp.zeros_like(stf_scr)
        str_scr[...] = jnp.zeros_like(str_scr)

    _ssd_direction(False, xcf_ref, dtf_ref, af_ref, dsk_ref, ef_ref, stf_scr, yf_ref)
    _ssd_direction(True, xcr_ref, dtr_ref, ar_ref, dsk_ref, er_ref, str_scr, yr_ref)


def _ssd(pd, params, layer, nlat):
    conv_w, conv_b, dtb, a_dir, dsk, expand = params
    nb, t, wd = pd.shape
    gw = GROUP_W
    tm = ROW_TILE
    hb = tm // M2_HALO
    nh = t // M2_HALO
    xc, dtp = pl.pallas_call(
        functools.partial(_ssd_prep_kernel, nlat_blocks=nlat // tm, nblocks=t // tm),
        grid=(nb, t // tm),
        in_specs=[
            pl.BlockSpec((None, tm, wd), lambda b, j: (b, j, 0)),
            pl.BlockSpec((None, M2_HALO, M2_XBC), lambda b, j: (b, jnp.maximum(j * hb - 1, 0), 0)),
            pl.BlockSpec((None, M2_HALO, M2_XBC), lambda b, j: (b, jnp.minimum((j + 1) * hb, nh - 1), 0)),
            _layer_spec(conv_w, layer), _layer_spec(conv_b, layer), _layer_spec(dtb, layer),
        ],
        out_specs=[pl.BlockSpec((None, tm, M2_XBC), lambda b, j: (b, j, 0)),
                   pl.BlockSpec((None, tm, LANE), lambda b, j: (b, j, 0))],
        out_shape=[jax.ShapeDtypeStruct((nb, t, M2_XBC), F32), jax.ShapeDtypeStruct((nb, t, LANE), F32)],
        scratch_shapes=[pltpu.VMEM((tm + 2 * M2_HALO, M2_XBC), F32)],
        compiler_params=_cparams(("arbitrary", "arbitrary")),
        name="ssd_prep",
    )(pd, pd, pd, conv_w, conv_b, dtb)

    qn = M2_CHUNK
    nchunks = t // qn
    nctx_chunks = nchunks - nlat // qn

    def fwd_chunk(k):
        return jnp.where(k < nctx_chunks, k + (nchunks - nctx_chunks), k - nctx_chunks)

    def rev_chunk(k):
        return nchunks - 1 - k

    def rows(n, chunk_of):
        return pl.BlockSpec((None, qn, n), lambda b, k: (b, chunk_of(k), 0))

    def direction(d):
        return pl.BlockSpec((None, None, 1, LANE), lambda b, k: (layer, d, 0, 0))

    return pl.pallas_call(
        _ssd_kernel,
        grid=(nb, nchunks),
        in_specs=[rows(M2_XBC, fwd_chunk), rows(LANE, fwd_chunk), rows(M2_XBC, rev_chunk), rows(LANE, rev_chunk),
                  direction(0), direction(1), _layer_spec(dsk, layer),
                  pl.BlockSpec((None, LANE, gw), lambda b, k: (0, 0, 0)),
                  pl.BlockSpec((None, LANE, gw), lambda b, k: (1, 0, 0))],
        out_specs=[rows(gw, fwd_chunk), rows(gw, rev_chunk)],
        out_shape=[jax.ShapeDtypeStruct((nb, t, gw), F32), jax.ShapeDtypeStruct((nb, t, gw), F32)],
        scratch_shapes=[pltpu.VMEM((M2_STATE, gw), F32), pltpu.VMEM((M2_STATE, gw), F32)],
        compiler_params=_cparams(("arbitrary", "arbitrary")),
        name="ssd_scan",
    )(xc, dtp, xc, dtp, a_dir, a_dir, dsk, expand, expand)


def _layer_norm_rows(v, g, b):
    mu = jnp.mean(v, axis=-1, keepdims=True)
    var = jnp.mean(jnp.square(v - mu), axis=-1, keepdims=True)
    return (v - mu) * lax.rsqrt(var + LN_EPS) * g + b


def _outproj_kernel(mabc_ref, yf_ref, yr_ref, z_ref, nw_ref, hl_ref, hc_ref, gate_ref, w_ref, lg_ref, lb_ref,
                    o_ref, *, alpha, nlat_blocks):
    j = pl.program_id(1)
    k1 = mabc_ref.shape[1]
    gv = (yf_ref[...] + yr_ref[...]) * _silu(z_ref[...])
    gc = GROUP_W // M2_GROUPS
    md = []
    for g in range(M2_GROUPS):
        part = gv[:, g * gc:(g + 1) * gc]
        ms = jnp.mean(jnp.square(part), axis=-1, keepdims=True)
        md.append((part * lax.rsqrt(ms + RMS_EPS) * nw_ref[:, g * gc:(g + 1) * gc]).astype(BF16))
    mix = (_bdot(mabc_ref[...].astype(BF16), w_ref[0:k1, :])
           + _bdot(jnp.concatenate(md, axis=1), w_ref[k1:w_ref.shape[0], :]))
    gated = gate_ref[...] * mix

    @pl.when(j < nlat_blocks)
    def _():
        o_ref[...] = _layer_norm_rows(alpha * hl_ref[...] + gated, lg_ref[...], lb_ref[...])

    @pl.when(j >= nlat_blocks)
    def _():
        o_ref[...] = _layer_norm_rows(alpha * hc_ref[...] + gated, lg_ref[...], lb_ref[...])


def _outproj(mabc, yf, yr, pd, norm_w, h_lat, h_ctx, mods, w, ln_g, ln_b, layer, rows_used, alpha):
    nb, nlat, d = h_lat.shape
    tm = ROW_TILE
    k1, k2 = mabc.shape[2], yf.shape[2]
    z_block = M2_XBC // k2
    lat_spec, ctx_spec = _stream_specs(tm, d, nlat // tm)

    def row(n):
        return pl.BlockSpec((None, tm, n), lambda b, j: (b, j, 0))

    return pl.pallas_call(
        functools.partial(_outproj_kernel, alpha=alpha, nlat_blocks=nlat // tm),
        grid=(nb, rows_used // tm),
        in_specs=[row(k1), row(k2), row(k2), pl.BlockSpec((None, tm, k2), lambda b, j: (b, j, z_block)),
                  _layer_spec(norm_w, layer), lat_spec, ctx_spec, _mod_spec(d, 2, layer, nb, nlat // tm),
                  _layer_spec(w, layer), _layer_spec(ln_g, layer), _layer_spec(ln_b, layer)],
        out_specs=row(d),
        out_shape=jax.ShapeDtypeStruct((nb, rows_used, d), F32),
        compiler_params=_cparams(("arbitrary", "arbitrary")),
        name="outproj",
    )(mabc, yf, yr, pd, norm_w, h_lat, h_ctx, mods, w, ln_g, ln_b)


def _ffn_kernel(*refs, alpha, grid_conv, halo):
    if grid_conv:
        (h_ref, prev_ref, next_ref, sh_ref, sc_ref, gate_ref, wg_ref, wv_ref, cw_ref, cb_ref, wd_ref,
         lg_ref, lb_ref, o_ref, xm_scr, g_scr, v_scr, side_scr, acc_scr) = refs
    else:
        (h_ref, sh_ref, sc_ref, gate_ref, wg_ref, wv_ref, cw_ref, cb_ref, wd_ref,
         lg_ref, lb_ref, o_ref, xm_scr, g_scr, v_scr, side_scr, acc_scr) = refs
    j = pl.program_id(1)
    f = pl.program_id(2)
    nj = pl.num_programs(1)
    nf = pl.num_programs(2)
    tm = h_ref.shape[0]
    sub = g_scr.shape[2]
    nsub = g_scr.shape[0]
    pad = SUBLANE

    @pl.when(f == 0)
    def _():
        def modulated(v):
            return (v * (1.0 + sc_ref[...]) + sh_ref[...]).astype(BF16)
        xm_scr[halo:halo + tm, :] = modulated(h_ref[...])
        if grid_conv:
            xm_scr[0:halo, :] = modulated(prev_ref[...])
            xm_scr[halo + tm:halo + tm + halo, :] = modulated(next_ref[...])
        acc_scr[...] = jnp.zeros_like(acc_scr)
        side_scr[:, :, 0:pad, :] = jnp.zeros((nsub, 2, pad, sub), F32)
        side_scr[:, :, pad + tm:pad + tm + pad, :] = jnp.zeros((nsub, 2, pad, sub), F32)

    pos = lax.broadcasted_iota(jnp.int32, (tm, 1), 0)
    if grid_conv:
        keep_prev = (j > 0).astype(F32)
        keep_next = (j < nj - 1).astype(F32)
        pos = pos % GRID_W
        left_ok, right_ok = pos != 0, pos != GRID_W - 1
        row_taps = (-1, 0, 1)
    else:
        left_ok, right_ok = pos != 0, pos != tm - 1
        row_taps = (0,)

    for si in range(nsub):
        cs = slice(si * sub, (si + 1) * sub)
        ge = _bdot(xm_scr[...], wg_ref[:, cs])
        if grid_conv:
            g_scr[si, 0:halo, :] = ge[0:halo] * keep_prev
            g_scr[si, halo:halo + tm, :] = ge[halo:halo + tm]
            g_scr[si, halo + tm:halo + tm + halo, :] = ge[halo + tm:halo + tm + halo] * keep_next
        else:
            g_scr[si] = ge
        v_scr[si] = _bdot(xm_scr[halo:halo + tm, :], wv_ref[:, cs])

    acts = []
    for si in range(nsub):
        cs = slice(si * sub, (si + 1) * sub)

        def tap_sum(dc):
            tot = None
            for dr in row_taps:
                k = (dr + 1) * 3 + (dc + 1)
                r0 = halo + dr * GRID_W
                term = g_scr[si, r0:r0 + tm, :] * cw_ref[k:k + 1, cs]
                tot = term if tot is None else tot + term
            return tot

        side_scr[si, 0, pad:pad + tm, :] = tap_sum(-1)
        side_scr[si, 1, pad:pad + tm, :] = tap_sum(1)
        conv = tap_sum(0) + cb_ref[:, cs]
        conv = conv + jnp.where(left_ok, side_scr[si, 0, pad - 1:pad - 1 + tm, :], 0.0)
        conv = conv + jnp.where(right_ok, side_scr[si, 1, pad + 1:pad + 1 + tm, :], 0.0)
        acts.append((_gelu(conv) * v_scr[si]).astype(BF16))
    act = acts[0] if nsub == 1 else jnp.concatenate(acts, axis=1)
    acc_scr[...] += _bdot(act, wd_ref[...])

    @pl.when(f == nf - 1)
    def _():
        o_ref[...] = _layer_norm_rows(alpha * h_ref[...] + gate_ref[...] * acc_scr[...], lg_ref[...], lb_ref[...])


def _ffn(h, mods, w_up, conv_w9, conv_b, w_down, ln_g, ln_b, layer, nlat, alpha, grid_conv):
    nb, t, d = h.shape
    fh = w_down.shape[1]
    tf = FFN_COL_TILE
    sub = FFN_SUB_TILE
    nf = fh // tf
    if grid_conv:
        tm, halo = FFN_ROW_TILE, GRID_W
        nblk, blk0 = nlat // tm, 0
    else:
        tm, halo = t - nlat, 0
        nblk, blk0 = 1, nlat // tm
    hb = tm // GRID_W
    nhalo_blocks = nlat // GRID_W

    def vec(k):
        return pl.BlockSpec((None, None, 1, d), lambda b, j, f: (layer, b if grid_conv else nb, 0, k))

    in_specs = [pl.BlockSpec((None, tm, d), lambda b, j, f: (b, j + blk0, 0))]
    args = [h]
    if grid_conv:
        in_specs += [
            pl.BlockSpec((None, halo, d), lambda b, j, f: (b, jnp.maximum(j * hb - 1, 0), 0)),
            pl.BlockSpec((None, halo, d), lambda b, j, f: (b, jnp.minimum((j + 1) * hb, nhalo_blocks - 1), 0)),
        ]
        args += [h, h]
    in_specs += [
        vec(3), vec(4), vec(5),
        pl.BlockSpec((None, d, tf), lambda b, j, f: (layer, 0, f)),
        pl.BlockSpec((None, d, tf), lambda b, j, f: (layer, 0, nf + f)),
        pl.BlockSpec((None, 9, tf), lambda b, j, f: (layer, 0, f)),
        pl.BlockSpec((None, 1, tf), lambda b, j, f: (layer, 0, f)),
        pl.BlockSpec((None, tf, d), lambda b, j, f: (layer, f, 0)),
        _layer_spec(ln_g, layer), _layer_spec(ln_b, layer),
    ]
    args += [mods, mods, mods, w_up, w_up, conv_w9, conv_b, w_down, ln_g, ln_b]
    return pl.pallas_call(
        functools.partial(_ffn_kernel, alpha=alpha, grid_conv=grid_conv, halo=halo),
        grid=(nb, nblk, nf),
        in_specs=in_specs,
        out_specs=pl.BlockSpec((None, tm, d), lambda b, j, f: (b, j, 0)),
        out_shape=jax.ShapeDtypeStruct((nb, nblk * tm, d), F32),
        scratch_shapes=[pltpu.VMEM((tm + 2 * halo, d), BF16),
                        pltpu.VMEM((tf // sub, tm + 2 * halo, sub), F32),
                        pltpu.VMEM((tf // sub, tm, sub), F32),
                        pltpu.VMEM((tf // sub, 2, tm + 2 * SUBLANE, sub), F32),
                        pltpu.VMEM((tm, d), F32)],
        compiler_params=_cparams(("arbitrary", "arbitrary", "arbitrary")),
        name="ffn_lat" if grid_conv else "ffn_ctx",
    )(*args)


def kernel(x, c, ctx, c_ctx, w_ada, b_ada, w_in, w_out, ln1_g, ln1_b, ln2_g, ln2_b, s5_a_re, s5_a_im, s5_b_re, s5_b_im, s5_c_re, s5_c_im, s5_log_step, s5_d, s5_glu_w, s5_glu_b, sg_ln_g, sg_ln_b, sg_w, sg_b, pool_w, pool_b, pool_scale, m2_conv_w, m2_conv_b, m2_dt_bias, m2_a_log, m2_d, m2_norm_w, ffn_w_up, ffn_conv_w, ffn_conv_b, ffn_w_down):
    nb, nlat, d = x.shape
    nctx = ctx.shape[1]
    depth = w_ada.shape[0]
    gw = GROUP_W
    alpha = (2 * depth) ** 0.25
    assert nb + 1 <= MOD_ROWS and nlat % FFN_ROW_TILE == 0 and nctx % ROW_TILE == 0
    assert nlat % (GRID_W * SUBLANE) == 0 and FFN_ROW_TILE % GRID_W == 0
    assert w_in.shape[2] == 5 * gw + M2_XBC + 2 * M2_HEADS

    cmat = jnp.concatenate([c, c_ctx[None, :], jnp.zeros((MOD_ROWS - nb - 1, d), F32)], axis=0)
    mods = _ada(cmat, w_ada, b_ada).reshape(depth, MOD_ROWS, 1, 6 * d)

    def rows(v):
        return v.astype(F32).reshape(depth, 1, -1)

    fh = ffn_w_down.shape[1]
    lane_pad = LANE - 2 * M2_HEADS
    w_in_b = w_in.astype(BF16)
    w_dt_b = jnp.pad(w_in[:, :, w_in.shape[2] - 2 * M2_HEADS:], ((0, 0), (0, 0), (0, lane_pad))).astype(BF16)
    w_out_b = w_out.astype(BF16)
    w_up_b = ffn_w_up.astype(BF16)
    w_down_b = ffn_w_down.astype(BF16)
    cw9 = ffn_conv_w.astype(F32).reshape(depth, 9, fh)
    s5_mats = _s5_mats(s5_a_re, s5_a_im, s5_b_re, s5_b_im, s5_c_re, s5_c_im, s5_log_step, s5_d)
    sg_btile = jnp.repeat(jnp.swapaxes(sg_b.astype(F32), 1, 2), gw // SG_HEADS, axis=2)
    mixer_params = (s5_glu_w.astype(BF16), rows(s5_glu_b), rows(sg_ln_g), rows(sg_ln_b), sg_w.astype(BF16),
                    sg_btile, pool_w.astype(BF16), rows(pool_b), rows(pool_scale))
    m2_a = -jnp.exp(m2_a_log.astype(F32))
    a_dir = (m2_a[:, :, None, :] * jnp.eye(2, dtype=F32)[None, :, :, None]).reshape(depth, 2, 1, 2 * M2_HEADS)
    a_dir = jnp.pad(a_dir, ((0, 0), (0, 0), (0, 0), (0, lane_pad)))
    head_of_ch = jnp.arange(gw) // M2_HEAD_DIM
    expand = (jnp.arange(LANE)[None, :, None] == (jnp.arange(2)[:, None, None] * M2_HEADS + head_of_ch[None, None, :])
              ).astype(BF16)
    ssd_params = (m2_conv_w.astype(F32), rows(m2_conv_b),
                  jnp.pad(m2_dt_bias.astype(F32).reshape(depth, 1, 2 * M2_HEADS), ((0, 0), (0, 0), (0, lane_pad))),
                  a_dir, rows(jnp.repeat(m2_d.astype(F32), M2_HEAD_DIM, axis=1)), expand)
    m2_nw = rows(m2_norm_w)
    ln1 = rows(ln1_g), rows(ln1_b)
    ln2 = rows(ln2_g), rows(ln2_b)
    ffn_cb = rows(ffn_conv_b)

    h_lat, h_ctx = x, ctx
    for i in range(depth):
        need_ctx = i < depth - 1
        pa, pb, pd = _inproj(h_lat, h_ctx, mods, w_in_b, w_dt_b, i)
        ya = _s5(pa, s5_mats, i, nlat)
        mabc = _mixers(ya, pb, mixer_params, i, nlat)
        yf, yr = _ssd(pd, ssd_params, i, nlat)
        rows_used = nlat + nctx if need_ctx else nlat
        h1 = _outproj(mabc, yf, yr, pd, m2_nw, h_lat, h_ctx, mods, w_out_b, ln1[0], ln1[1], i, rows_used, alpha)
        h_lat = _ffn(h1, mods, w_up_b, cw9, ffn_cb, w_down_b, ln2[0], ln2[1], i, nlat, alpha, True)
        if need_ctx:
            h_ctx = _ffn(h1, mods, w_up_b, cw9, ffn_cb, w_down_b, ln2[0], ln2[1], i, nlat, alpha, False)
    return h_lat
```

```python
import functools

import jax
import jax.numpy as jnp
from jax import lax
from jax.experimental import pallas as pl
from jax.experimental.pallas import tpu as pltpu

F32 = jnp.float32
BF16 = jnp.bfloat16

GRID_W = 64
GROUP_W = 512
S5_CH = 16
S5_Q = 16
S5_RELAYOUT_UNROLL = 4
SG_HEADS = 4
SG_CHUNK = 128
POOL_WINDOWS = (2, 4, 8, 16)
POOL_DIM = GROUP_W // len(POOL_WINDOWS)
POOL_HALO = 8
M2_HEAD_DIM = 64
M2_HEADS = GROUP_W // M2_HEAD_DIM
M2_STATE = 128
M2_GROUPS = 2
M2_CONV = 4
M2_CHUNK = 128
M2_XBC = GROUP_W + 2 * M2_GROUPS * M2_STATE
M2_HALO = 8
LN_EPS = 1e-5
RMS_EPS = 1e-5

LANE = 128
SUBLANE = 8
VMEM_LIMIT = 56 * 1024 * 1024
ROW_TILE = 256
OUT_ROW_TILE = 512
FFN_ROW_TILE = 512
FFN_COL_TILE = 512
FFN_SUB_TILE = 256
ADA_COL_TILE = 1024
MOD_ROWS = 8


def _cparams(sem):
    return pltpu.CompilerParams(dimension_semantics=sem, vmem_limit_bytes=VMEM_LIMIT)


def _bdot(a, b):
    return jnp.dot(a, b, preferred_element_type=F32)


def _gelu(x):
    return jax.nn.gelu(x)


def _silu(x):
    return x * jax.nn.sigmoid(x)


def _ada_kernel(c_ref, w_ref, b_ref, o_ref):
    c = c_ref[...]
    s = _silu(c).astype(BF16)
    o_ref[...] = _bdot(s, w_ref[...].astype(BF16)) + b_ref[...]


def _ada(cmat, w_ada, b_ada):
    depth, d, n = w_ada.shape
    tn = ADA_COL_TILE
    return pl.pallas_call(
        _ada_kernel,
        grid=(depth, n // tn),
        in_specs=[
            pl.BlockSpec((MOD_ROWS, d), lambda l, j: (0, 0)),
            pl.BlockSpec((None, d, tn), lambda l, j: (l, 0, j)),
            pl.BlockSpec((None, 1, tn), lambda l, j: (l, 0, j)),
        ],
        out_specs=pl.BlockSpec((None, MOD_ROWS, tn), lambda l, j: (l, 0, j)),
        out_shape=jax.ShapeDtypeStruct((depth, MOD_ROWS, n), F32),
        compiler_params=_cparams(("arbitrary", "arbitrary")),
        name="ada",
    )(cmat, w_ada, b_ada.reshape(depth, 1, n))


def _mod_spec(d, k, layer, nbatch, nlat_blocks):
    return pl.BlockSpec((None, None, 1, d),
                        lambda b, j, *_: (layer, jnp.where(j >= nlat_blocks, nbatch, b), 0, k))


def _layer_spec(arr, layer):
    tail = arr.shape[1:]
    return pl.BlockSpec((None,) + tail, lambda *_: (layer,) + (0,) * len(tail))


def _stream_specs(tm, d, nlat_blocks):
    lat = pl.BlockSpec((None, tm, d), lambda b, j: (b, jnp.minimum(j, nlat_blocks - 1), 0))
    ctx = pl.BlockSpec((None, tm, d), lambda b, j: (b, jnp.maximum(j - nlat_blocks, 0), 0))
    return lat, ctx


def _inproj_kernel(hl_ref, hc_ref, sh_ref, sc_ref, w_ref, wdt_ref, oa_ref, ob_ref, od_ref, xm_scr, *, nlat_blocks):
    j = pl.program_id(1)

    def modulated(h_ref):
        return (h_ref[...] * (1.0 + sc_ref[...]) + sh_ref[...]).astype(BF16)

    @pl.when(j < nlat_blocks)
    def _():
        xm_scr[...] = modulated(hl_ref)

    @pl.when(j >= nlat_blocks)
    def _():
        xm_scr[...] = modulated(hc_ref)

    xm = xm_scr[...]
    gw = GROUP_W
    oa_ref[...] = _bdot(xm, w_ref[:, 0:gw])
    ob_ref[...] = _bdot(xm, w_ref[:, gw:4 * gw])
    od_ref[:, 0:M2_XBC] = _bdot(xm, w_ref[:, 5 * gw:5 * gw + M2_XBC])
    od_ref[:, M2_XBC:M2_XBC + gw] = _bdot(xm, w_ref[:, 4 * gw:5 * gw])
    od_ref[:, M2_XBC + gw:M2_XBC + gw + LANE] = _bdot(xm, wdt_ref[...])


def _inproj(h_lat, h_ctx, mods, w, w_dt, layer):
    nb, nlat, d = h_lat.shape
    t = nlat + h_ctx.shape[1]
    tm = ROW_TILE
    na, nbw, nd = GROUP_W, 3 * GROUP_W, M2_XBC + GROUP_W + LANE
    lat_spec, ctx_spec = _stream_specs(tm, d, nlat // tm)

    def row(n):
        return pl.BlockSpec((None, tm, n), lambda b, j: (b, j, 0))

    return pl.pallas_call(
        functools.partial(_inproj_kernel, nlat_blocks=nlat // tm),
        grid=(nb, t // tm),
        in_specs=[lat_spec, ctx_spec, _mod_spec(d, 0, layer, nb, nlat // tm),
                  _mod_spec(d, 1, layer, nb, nlat // tm), _layer_spec(w, layer), _layer_spec(w_dt, layer)],
        out_specs=[row(na), row(nbw), row(nd)],
        out_shape=[jax.ShapeDtypeStruct((nb, t, n), F32) for n in (na, nbw, nd)],
        scratch_shapes=[pltpu.VMEM((tm, d), BF16)],
        compiler_params=_cparams(("arbitrary", "arbitrary")),
        name="inproj",
    )(h_lat, h_ctx, mods, mods, w, w_dt)


def _s5_mats(a_re, a_im, b_re, b_im, c_re, c_im, log_step, d_skip):
    q, hch = S5_Q, S5_CH
    nl, _, g, p = a_re.shape
    lre, lim = a_re.astype(F32), a_im.astype(F32)
    step = jnp.exp(log_step.astype(F32))[..., None]
    mag = jnp.exp(lre * step)
    ar, ai = mag * jnp.cos(lim * step), mag * jnp.sin(lim * step)
    den = lre * lre + lim * lim
    qr = ((ar - 1.0) * lre + ai * lim) / den
    qi = (ai * lre - (ar - 1.0) * lim) / den
    bre, bim = b_re.astype(F32), b_im.astype(F32)
    bbr = qr[..., None] * bre - qi[..., None] * bim
    bbi = qr[..., None] * bim + qi[..., None] * bre
    cre, cim = c_re.astype(F32), c_im.astype(F32)

    pw = [(jnp.ones_like(ar), jnp.zeros_like(ai))]
    for _ in range(q):
        pr, pi = pw[-1]
        pw.append((pr * ar - pi * ai, pr * ai + pi * ar))
    pk_re = jnp.stack([v[0] for v in pw], axis=3)
    pk_im = jnp.stack([v[1] for v in pw], axis=3)
    pt_re = jnp.stack([v[0] for v in pw], axis=4)
    pt_im = jnp.stack([v[1] for v in pw], axis=4)

    def lag_kernels(d, pr, pi):
        ctr = jnp.swapaxes(cre[:, d], -1, -2)[:, :, :, None]
        cti = jnp.swapaxes(cim[:, d], -1, -2)[:, :, :, None]
        ca_re = (ctr * pr[..., None] - cti * pi[..., None]).reshape(nl, g, p, 1, q * hch)
        ca_im = (ctr * pi[..., None] + cti * pr[..., None]).reshape(nl, g, p, 1, q * hch)
        return jnp.sum(bbr[:, d][..., None] * ca_re - bbi[:, d][..., None] * ca_im, axis=2)

    lag_f = lag_kernels(0, pt_re[:, 0, :, :, :q], pt_im[:, 0, :, :, :q])
    lag_r = lag_kernels(1, jnp.flip(pt_re[:, 1, :, :, :q], axis=3), jnp.flip(pt_im[:, 1, :, :, :q], axis=3))
    dmat = jnp.eye(hch, dtype=F32) * d_skip.astype(F32).reshape(nl, g, 1, hch)
    mid = lag_r[..., (q - 1) * hch:] + lag_f[..., :hch] + dmat
    kk = jnp.concatenate([lag_r[..., :(q - 1) * hch], mid, lag_f[..., hch:]], axis=-1)
    tmat = jnp.stack([kk[..., (q - 1 - s) * hch:(2 * q - 1 - s) * hch] for s in range(q)], axis=2)
    tmat = tmat.reshape(nl, g, q * hch, q * hch)

    def state_in(d, pr, pi):
        btr = jnp.swapaxes(bbr[:, d], -1, -2)[:, :, None]
        bti = jnp.swapaxes(bbi[:, d], -1, -2)[:, :, None]
        pr, pi = pr[:, :, :, None], pi[:, :, :, None]
        m = jnp.concatenate([pr * btr - pi * bti, pr * bti + pi * btr], axis=-1)
        return m.reshape(nl, g, q * hch, 2 * p)

    sb = jnp.concatenate([state_in(0, jnp.flip(pk_re[:, 0, :, :q], axis=2), jnp.flip(pk_im[:, 0, :, :q], axis=2)),
                          state_in(1, pk_re[:, 1, :, :q], pk_im[:, 1, :, :q])], axis=-1)

    def state_out(d, pr, pi):
        ctr = jnp.swapaxes(cre[:, d], -1, -2)[:, :, :, None]
        cti = jnp.swapaxes(cim[:, d], -1, -2)[:, :, :, None]
        pr, pi = pr[..., None], pi[..., None]
        m = jnp.concatenate([ctr * pr - cti * pi, -(ctr * pi + cti * pr)], axis=2)
        return m.reshape(nl, g, 2 * p, q * hch)

    w2 = jnp.concatenate([state_out(0, pt_re[:, 0, :, :, 1:], pt_im[:, 0, :, :, 1:]),
                          state_out(1, jnp.flip(pt_re[:, 1, :, :, 1:], axis=3), jnp.flip(pt_im[:, 1, :, :, 1:], axis=3))],
                         axis=2)

    aqr, aqi = pw[q]
    a1 = jnp.concatenate([aqr, aqr], axis=-1).reshape(nl, 2, g * 2 * p)
    a2 = jnp.concatenate([-aqi, aqi], axis=-1).reshape(nl, 2, g * 2 * p)
    acoef = jnp.stack([a1[:, 0], a2[:, 0], a1[:, 1], a2[:, 1]], axis=1)
    return tmat, sb, w2, acoef


def _granule_transpose(v):
    n = len(v)
    slot = lax.broadcasted_iota(jnp.int32, v[0].shape, 1) // S5_CH
    v = list(v)
    j = n // 2
    while j >= 1:
        keep = (slot & j) == 0
        for i in range(n):
            if i & j == 0:
                a, b = v[i], v[i + j]
                v[i] = jnp.where(keep, a, pltpu.roll(b, j * S5_CH, axis=1))
                v[i + j] = jnp.where(keep, pltpu.roll(a, LANE - j * S5_CH, axis=1), b)
        j //= 2
    return v


def _split_bf16(x):
    hi = x.astype(BF16)
    return hi, (x - hi.astype(F32)).astype(BF16)


def _dot3(xs, w):
    x_hi, x_lo = xs
    w_hi, w_lo = _split_bf16(w)
    return _bdot(x_hi, w_hi) + (_bdot(x_lo, w_hi) + _bdot(x_hi, w_lo))


def _s5_kernel(pa_ref, t_ref, sb_ref, w2_ref, ac_ref, ya_ref, x_scr, s_scr, h_scr, *, nbatch, nrows, nctx):
    gb = LANE // S5_CH
    halves = S5_Q // gb
    qh = S5_Q * S5_CH
    sw = LANE
    nlat = nrows - nctx
    nrc = nrows // SUBLANE
    tok_per_rc = SUBLANE * S5_Q

    def relayout(it, to_rows):
        for u in range(S5_RELAYOUT_UNROLL):
            idx = it * S5_RELAYOUT_UNROLL + u
            b = idx // nrc
            rc = idx % nrc
            r0 = pl.multiple_of(b * nrows + rc * SUBLANE, SUBLANE)
            for half in range(halves):
                lanes = slice(half * LANE, (half + 1) * LANE)

                def tok(s8, half=half, rc=rc):
                    return pl.ds(rc * tok_per_rc + half * gb + s8, SUBLANE, stride=S5_Q)

                if to_rows:
                    v = _granule_transpose([pa_ref[b, tok(s8), :] for s8 in range(gb)])
                    for gi in range(gb):
                        x_scr[gi, pl.ds(r0, SUBLANE), lanes] = v[gi]
                else:
                    v = _granule_transpose([x_scr[gi, pl.ds(r0, SUBLANE), lanes] for gi in range(gb)])
                    for s8 in range(gb):
                        ya_ref[b, tok(s8), :] = v[s8]
        return 0

    nblocks = nbatch * nrc // S5_RELAYOUT_UNROLL
    lax.fori_loop(0, nblocks, lambda it, c: relayout(it, True), 0)

    for gi in range(gb):
        xs = _split_bf16(x_scr[gi])
        x_scr[gi] = _dot3(xs, t_ref[gi])
        rs = _dot3(xs, sb_ref[gi])
        for d in range(2):
            sd = rs[:, d * sw:(d + 1) * sw]
            s_scr[d, gi] = sd
            s_scr[2 + d, gi] = pltpu.roll(sd, sw // 2, axis=1)

    zero = jnp.zeros((nbatch, sw), F32)

    def body(s, carry):
        f_rows = pl.ds(jnp.where(s < nctx, s + nlat, s - nctx), nbatch, stride=nrows)
        r_rows = pl.ds(nrows - 1 - s, nbatch, stride=nrows)
        out = []
        for gi in range(gb):
            hf, hfs, hr, hrs = carry[4 * gi:4 * gi + 4]
            a1f, a2f, a1r, a2r = (ac_ref[k:k + 1, gi * sw:(gi + 1) * sw] for k in range(4))
            h_scr[0, gi, f_rows, :] = hf
            h_scr[1, gi, r_rows, :] = hr
            out += [a1f * hf + a2f * hfs + s_scr[0, gi, f_rows, :],
                    a1f * hfs - a2f * hf + s_scr[2, gi, f_rows, :],
                    a1r * hr + a2r * hrs + s_scr[1, gi, r_rows, :],
                    a1r * hrs - a2r * hr + s_scr[3, gi, r_rows, :]]
        return tuple(out)

    lax.fori_loop(0, nrows, body, (zero,) * (4 * gb))

    for gi in range(gb):
        hin = jnp.concatenate([h_scr[0, gi], h_scr[1, gi]], axis=1)
        x_scr[gi] = x_scr[gi] + _dot3(_split_bf16(hin), w2_ref[gi])

    lax.fori_loop(0, nblocks, lambda it, c: relayout(it, False), 0)


def _s5(pa, mats, layer, nlat):
    tmat, sb, w2, acoef = mats
    nb, t, gw = pa.shape
    q = S5_Q
    nrows = t // q
    qh = q * S5_CH
    gb = LANE // S5_CH
    assert nrows % SUBLANE == 0 and q % gb == 0 and w2.shape[2] == 2 * LANE and sb.shape[3] == 2 * LANE
    assert (nb * nrows // SUBLANE) % S5_RELAYOUT_UNROLL == 0
    return pl.pallas_call(
        functools.partial(_s5_kernel, nbatch=nb, nrows=nrows, nctx=(t - nlat) // q),
        grid=(gw // LANE,),
        in_specs=[
            pl.BlockSpec((nb, t, LANE), lambda i: (0, 0, i)),
            pl.BlockSpec((None, gb, qh, qh), lambda i: (layer, i, 0, 0)),
            pl.BlockSpec((None, gb, qh, 2 * LANE), lambda i: (layer, i, 0, 0)),
            pl.BlockSpec((None, gb, 2 * LANE, qh), lambda i: (layer, i, 0, 0)),
            pl.BlockSpec((None, 4, gb * LANE), lambda i: (layer, 0, i)),
        ],
        out_specs=pl.BlockSpec((nb, t, LANE), lambda i: (0, 0, i)),
        out_shape=jax.ShapeDtypeStruct((nb, t, gw), F32),
        scratch_shapes=[pltpu.VMEM((gb, nb * nrows, qh), F32),
                        pltpu.VMEM((4, gb, nb * nrows, LANE), F32),
                        pltpu.VMEM((2, gb, nb * nrows, LANE), F32)],
        compiler_params=_cparams(("arbitrary",)),
        name="s5",
    )(pa, tmat, sb, w2, acoef)


def _mixers_kernel(ya_ref, pb_ref, prev_ref, next_ref, gluw_ref, glub_ref, lng_ref, lnb_ref, sgw_ref,
                   sgb_ref, pw_ref, pbias_ref, pscale_ref, o_ref, ext_scr, *, nlat_blocks, nblocks, nlat, nctx):
    j = pl.program_id(1)
    tm = ya_ref.shape[0]
    gw = GROUP_W

    z = _gelu(ya_ref[...])
    gate = jax.nn.sigmoid(_bdot(z.astype(BF16), gluw_ref[...]) + glub_ref[...])
    o_ref[:, 0:gw] = z * gate

    hd = gw // SG_HEADS
    u = _gelu(pb_ref[:, 0:gw])
    v = _gelu(pb_ref[:, gw:2 * gw])
    for hh in range(SG_HEADS):
        vh = v[:, hh * hd:(hh + 1) * hd]
        mu = jnp.mean(vh, axis=-1, keepdims=True)
        var = jnp.mean(jnp.square(vh - mu), axis=-1, keepdims=True)
        vn = ((vh - mu) * lax.rsqrt(var + LN_EPS) * lng_ref[:, hh * hd:(hh + 1) * hd]
              + lnb_ref[:, hh * hd:(hh + 1) * hd]).astype(BF16)
        for cchunk in range(tm // SG_CHUNK):
            rows = slice(cchunk * SG_CHUNK, (cchunk + 1) * SG_CHUNK)
            s = _bdot(sgw_ref[hh], vn[rows]) + sgb_ref[:, hh * hd:(hh + 1) * hd]
            o_ref[rows, gw + hh * hd:gw + (hh + 1) * hd] = u[rows, hh * hd:(hh + 1) * hd] * s

    is_first = jnp.logical_or(j == 0, j == nlat_blocks)
    is_last = jnp.logical_or(j == nlat_blocks - 1, j == nblocks - 1)
    p = pb_ref[:, 2 * gw:3 * gw]
    hal = POOL_HALO
    ext_scr[0:hal, :] = jnp.where(is_first, 0.0, prev_ref[...])
    ext_scr[hal:hal + tm, :] = p
    ext_scr[hal + tm:hal + tm + hal, :] = jnp.where(is_last, 0.0, next_ref[...])
    in_ctx = j >= nlat_blocks
    seq_len = jnp.where(in_ctx, nctx, nlat)
    t0 = (j - jnp.where(in_ctx, nlat_blocks, 0)) * tm
    tpos = t0 + lax.broadcasted_iota(jnp.int32, (tm, 1), 0)
    pd = POOL_DIM
    for gi, win in enumerate(POOL_WINDOWS):
        cols = slice(gi * pd, (gi + 1) * pd)
        tot = jnp.zeros((tm, pd), F32)
        for off in range(-(win // 2), win // 2):
            tot = tot + ext_scr[hal + off:hal + off + tm, cols]
        lo = jnp.maximum(tpos - win // 2, 0)
        hi = jnp.minimum(tpos + win // 2 - 1, seq_len - 1)
        mean = tot / (hi - lo + 1).astype(F32)
        yv = _bdot((mean - p[:, cols]).astype(BF16), pw_ref[gi]) + pbias_ref[:, cols]
        o_ref[:, 2 * gw + gi * pd:2 * gw + (gi + 1) * pd] = yv * pscale_ref[:, cols]


def _mixers(ya, pb, params, layer, nlat):
    nb, t, gw = ya.shape
    tm = ROW_TILE
    nblocks = t // tm
    nlat_blocks = nlat // tm
    hb = tm // POOL_HALO
    nh = t // POOL_HALO

    return pl.pallas_call(
        functools.partial(_mixers_kernel, nlat_blocks=nlat_blocks, nblocks=nblocks, nlat=nlat, nctx=t - nlat),
        grid=(nb, nblocks),
        in_specs=[
            pl.BlockSpec((None, tm, gw), lambda b, j: (b, j, 0)),
            pl.BlockSpec((None, tm, 3 * gw), lambda b, j: (b, j, 0)),
            pl.BlockSpec((None, POOL_HALO, gw), lambda b, j: (b, jnp.maximum(j * hb - 1, 0), 2)),
            pl.BlockSpec((None, POOL_HALO, gw), lambda b, j: (b, jnp.minimum((j + 1) * hb, nh - 1), 2)),
        ] + [_layer_spec(v, layer) for v in params],
        out_specs=pl.BlockSpec((None, tm, 3 * gw), lambda b, j: (b, j, 0)),
        out_shape=jax.ShapeDtypeStruct((nb, t, 3 * gw), F32),
        scratch_shapes=[pltpu.VMEM((tm + 2 * POOL_HALO, gw), F32)],
        compiler_params=_cparams(("arbitrary", "arbitrary")),
        name="mixers",
    )(ya, pb, pb, pb, *params)


def _ssd_prep_kernel(pd_ref, prev_ref, next_ref, cw_ref, cb_ref, dtb_ref, xc_ref, dt_ref, ext_scr,
                     *, nlat_blocks, nblocks):
    j = pl.program_id(1)
    tm = pd_ref.shape[0]
    nx = M2_XBC
    hal = M2_HALO
    is_first = jnp.logical_or(j == 0, j == nlat_blocks)
    is_last = jnp.logical_or(j == nlat_blocks - 1, j == nblocks - 1)
    ext_scr[0:hal, :] = jnp.where(is_first, 0.0, prev_ref[...])
    ext_scr[hal:hal + tm, :] = pd_ref[:, 0:nx]
    ext_scr[hal + tm:hal + tm + hal, :] = jnp.where(is_last, 0.0, next_ref[...])
    acc = jnp.zeros((tm, nx), F32) + cb_ref[...]
    for tap in range(M2_CONV):
        off = hal + tap - M2_CONV // 2
        acc = acc + ext_scr[off:off + tm, :] * cw_ref[tap:tap + 1, :]
    xc_ref[...] = _silu(acc)
    xdt = pd_ref[:, nx + GROUP_W:nx + GROUP_W + LANE] + dtb_ref[...]
    dt_ref[...] = jnp.maximum(xdt, 0.0) + jnp.log1p(jnp.exp(-jnp.abs(xdt)))


def _split3_bf16(x):
    x1 = x.astype(BF16)
    r1 = x - x1.astype(F32)
    x2 = r1.astype(BF16)
    return x1, x2, (r1 - x2.astype(F32)).astype(BF16)


def _ssd_direction(reverse, xc_ref, dt_ref, a_ref, dsk_ref, e_ref, st_scr, o_ref):
    qn = M2_CHUNK
    gw = GROUP_W
    lane0 = M2_HEADS if reverse else 0
    xs = xc_ref[:, 0:gw]
    nbc = M2_GROUPS * M2_STATE
    bm = xc_ref[:, gw:gw + nbc]
    cm = xc_ref[:, gw + nbc:gw + 2 * nbc]

    dt = dt_ref[...]
    da = dt * a_ref[...]
    ri = lax.broadcasted_iota(jnp.int32, (qn, qn), 0)
    ci = lax.broadcasted_iota(jnp.int32, (qn, qn), 1)
    causal = (ri <= ci) if reverse else (ri >= ci)
    mask = causal.astype(BF16)
    d1, d2, d3 = _split3_bf16(da)
    a_col = _bdot(mask, d1) + (_bdot(mask, d2) + _bdot(mask, d3))
    a_row = a_col.T
    tot_row = a_col[0:1, :] if reverse else a_col[qn - 1:qn, :]

    per_head = jnp.concatenate([dt, jnp.exp(a_col), jnp.exp(tot_row - a_col)], axis=0)
    p1, p2 = _split_bf16(per_head)
    e = e_ref[...]
    per_ch = _bdot(p1, e) + _bdot(p2, e)
    dt_x, ea_x, de_x = per_ch[0:qn], per_ch[qn:2 * qn], per_ch[2 * qn:3 * qn]
    et_x = ea_x[0:1, :] if reverse else ea_x[qn - 1:qn, :]
    xd = xs * dt_x
    xdw = (xd * de_x).astype(BF16)
    xdb = xd.astype(BF16)

    hpg = M2_HEADS // M2_GROUPS
    gcols = hpg * M2_HEAD_DIM
    for g in range(M2_GROUPS):
        bg = bm[:, g * M2_STATE:(g + 1) * M2_STATE]
        cg = cm[:, g * M2_STATE:(g + 1) * M2_STATE].astype(BF16)
        bgt = bg.T.astype(BF16)
        scores = _bdot(cg, bgt)
        st_old = st_scr[:, g * gcols:(g + 1) * gcols]
        y_inter = _bdot(cg, st_old.astype(BF16)) * ea_x[:, g * gcols:(g + 1) * gcols]
        st_new = _bdot(bgt, xdw[:, g * gcols:(g + 1) * gcols])
        st_scr[:, g * gcols:(g + 1) * gcols] = et_x[:, g * gcols:(g + 1) * gcols] * st_old + st_new
        for hl in range(hpg):
            hh = g * hpg + hl
            seg = a_col[:, lane0 + hh:lane0 + hh + 1] - a_row[lane0 + hh:lane0 + hh + 1, :]
            decay = jnp.exp(jnp.where(causal, seg, -jnp.inf))
            cols = slice(hh * M2_HEAD_DIM, (hh + 1) * M2_HEAD_DIM)
            y_h = _bdot((scores * decay).astype(BF16), xdb[:, cols])
            y_h = y_h + y_inter[:, hl * M2_HEAD_DIM:(hl + 1) * M2_HEAD_DIM]
            if reverse:
                o_ref[:, cols] = y_h
            else:
                o_ref[:, cols] = y_h + dsk_ref[:, cols] * xs[:, cols]


def _ssd_kernel(xcf_ref, dtf_ref, xcr_ref, dtr_ref, af_ref, ar_ref, dsk_ref, ef_ref, er_ref,
                yf_ref, yr_ref, stf_scr, str_scr):
    @pl.when(pl.program_id(1) == 0)
    def _():
        stf_scr[...] = jnp.zeros_like(stf_scr)
        str_scr[...] = jnp.zeros_like(str_scr)

    _ssd_direction(False, xcf_ref, dtf_ref, af_ref, dsk_ref, ef_ref, stf_scr, yf_ref)
    _ssd_direction(True, xcr_ref, dtr_ref, ar_ref, dsk_ref, er_ref, str_scr, yr_ref)


def _ssd(pd, params, layer, nlat):
    conv_w, conv_b, dtb, a_dir, dsk, expand = params
    nb, t, wd = pd.shape
    gw = GROUP_W
    tm = ROW_TILE
    hb = tm // M2_HALO
    nh = t // M2_HALO
    xc, dtp = pl.pallas_call(
        functools.partial(_ssd_prep_kernel, nlat_blocks=nlat // tm, nblocks=t // tm),
        grid=(nb, t // tm),
        in_specs=[
            pl.BlockSpec((None, tm, wd), lambda b, j: (b, j, 0)),
            pl.BlockSpec((None, M2_HALO, M2_XBC), lambda b, j: (b, jnp.maximum(j * hb - 1, 0), 0)),
            pl.BlockSpec((None, M2_HALO, M2_XBC), lambda b, j: (b, jnp.minimum((j + 1) * hb, nh - 1), 0)),
            _layer_spec(conv_w, layer), _layer_spec(conv_b, layer), _layer_spec(dtb, layer),
        ],
        out_specs=[pl.BlockSpec((None, tm, M2_XBC), lambda b, j: (b, j, 0)),
                   pl.BlockSpec((None, tm, LANE), lambda b, j: (b, j, 0))],
        out_shape=[jax.ShapeDtypeStruct((nb, t, M2_XBC), F32), jax.ShapeDtypeStruct((nb, t, LANE), F32)],
        scratch_shapes=[pltpu.VMEM((tm + 2 * M2_HALO, M2_XBC), F32)],
        compiler_params=_cparams(("arbitrary", "arbitrary")),
        name="ssd_prep",
    )(pd, pd, pd, conv_w, conv_b, dtb)

    qn = M2_CHUNK
    nchunks = t // qn
    nctx_chunks = nchunks - nlat // qn

    def fwd_chunk(k):
        return jnp.where(k < nctx_chunks, k + (nchunks - nctx_chunks), k - nctx_chunks)

    def rev_chunk(k):
        return nchunks - 1 - k

    def rows(n, chunk_of):
        return pl.BlockSpec((None, qn, n), lambda b, k: (b, chunk_of(k), 0))

    def direction(d):
        return pl.BlockSpec((None, None, 1, LANE), lambda b, k: (layer, d, 0, 0))

    return pl.pallas_call(
        _ssd_kernel,
        grid=(nb, nchunks),
        in_specs=[rows(M2_XBC, fwd_chunk), rows(LANE, fwd_chunk), rows(M2_XBC, rev_chunk), rows(LANE, rev_chunk),
                  direction(0), direction(1), _layer_spec(dsk, layer),
                  pl.BlockSpec((None, LANE, gw), lambda b, k: (0, 0, 0)),
                  pl.BlockSpec((None, LANE, gw), lambda b, k: (1, 0, 0))],
        out_specs=[rows(gw, fwd_chunk), rows(gw, rev_chunk)],
        out_shape=[jax.ShapeDtypeStruct((nb, t, gw), F32), jax.ShapeDtypeStruct((nb, t, gw), F32)],
        scratch_shapes=[pltpu.VMEM((M2_STATE, gw), F32), pltpu.VMEM((M2_STATE, gw), F32)],
        compiler_params=_cparams(("arbitrary", "arbitrary")),
        name="ssd_scan",
    )(xc, dtp, xc, dtp, a_dir, a_dir, dsk, expand, expand)


def _layer_norm_rows(v, g, b):
    mu = jnp.mean(v, axis=-1, keepdims=True)
    var = jnp.mean(jnp.square(v - mu), axis=-1, keepdims=True)
    return (v - mu) * lax.rsqrt(var + LN_EPS) * g + b


def _outproj_kernel(mabc_ref, yf_ref, yr_ref, z_ref, nw_ref, h_ref, gate_ref, w_ref, lg_ref, lb_ref, o_ref,
                    *, alpha):
    k1 = mabc_ref.shape[1]
    gc = GROUP_W // M2_GROUPS
    for r0 in range(0, o_ref.shape[0], ROW_TILE):
        rows = slice(r0, r0 + ROW_TILE)
        gv = (yf_ref[rows, :] + yr_ref[rows, :]) * _silu(z_ref[rows, :])
        md = []
        for g in range(M2_GROUPS):
            part = gv[:, g * gc:(g + 1) * gc]
            ms = jnp.mean(jnp.square(part), axis=-1, keepdims=True)
            md.append((part * lax.rsqrt(ms + RMS_EPS) * nw_ref[:, g * gc:(g + 1) * gc]).astype(BF16))
        mix = (_bdot(mabc_ref[rows, :].astype(BF16), w_ref[0:k1, :])
               + _bdot(jnp.concatenate(md, axis=1), w_ref[k1:w_ref.shape[0], :]))
        o_ref[rows, :] = _layer_norm_rows(alpha * h_ref[rows, :] + gate_ref[...] * mix, lg_ref[...], lb_ref[...])


def _outproj(mabc, yf, yr, pd, norm_w, h, mods, w, ln_g, ln_b, layer, nlat, alpha, latent):
    nb, rows, d = h.shape
    tm = OUT_ROW_TILE if latent else rows
    blk0 = 0 if latent else nlat // tm
    k1, k2 = mabc.shape[2], yf.shape[2]
    z_block = M2_XBC // k2

    def row(n):
        return pl.BlockSpec((None, tm, n), lambda b, j: (b, j + blk0, 0))

    return pl.pallas_call(
        functools.partial(_outproj_kernel, alpha=alpha),
        grid=(nb, rows // tm),
        in_specs=[row(k1), row(k2), row(k2), pl.BlockSpec((None, tm, k2), lambda b, j: (b, j + blk0, z_block)),
                  _layer_spec(norm_w, layer), pl.BlockSpec((None, tm, d), lambda b, j: (b, j, 0)),
                  pl.BlockSpec((None, None, 1, d), lambda b, j: (layer, b if latent else nb, 0, 2)),
                  _layer_spec(w, layer), _layer_spec(ln_g, layer), _layer_spec(ln_b, layer)],
        out_specs=pl.BlockSpec((None, tm, d), lambda b, j: (b, j, 0)),
        out_shape=jax.ShapeDtypeStruct((nb, rows, d), F32),
        compiler_params=_cparams(("arbitrary", "arbitrary")),
        name="outproj_lat" if latent else "outproj_ctx",
    )(mabc, yf, yr, pd, norm_w, h, mods, w, ln_g, ln_b)


def _ffn_kernel(*refs, alpha, grid_conv, halo):
    if grid_conv:
        (h_ref, prev_ref, next_ref, sh_ref, sc_ref, gate_ref, wg_ref, wv_ref, cw_ref, cb_ref, wd_ref,
         lg_ref, lb_ref, o_ref, xm_scr, acc_scr, *chain_scr) = refs
    else:
        (h_ref, sh_ref, sc_ref, gate_ref, wg_ref, wv_ref, cw_ref, cb_ref, wd_ref,
         lg_ref, lb_ref, o_ref, xm_scr, acc_scr, *chain_scr) = refs
    g_scr, v_scr, side_scr = chain_scr[0::3], chain_scr[1::3], chain_scr[2::3]
    j = pl.program_id(1)
    f = pl.program_id(2)
    nj = pl.num_programs(1)
    nf = pl.num_programs(2)
    tm = h_ref.shape[0]
    sub = g_scr[0].shape[1]
    nsub = len(g_scr)
    pad = SUBLANE

    @pl.when(f == 0)
    def _():
        def modulated(v):
            return (v * (1.0 + sc_ref[...]) + sh_ref[...]).astype(BF16)
        xm_scr[halo:halo + tm, :] = modulated(h_ref[...])
        if grid_conv:
            xm_scr[0:halo, :] = modulated(prev_ref[...])
            xm_scr[halo + tm:halo + tm + halo, :] = modulated(next_ref[...])
        acc_scr[...] = jnp.zeros_like(acc_scr)
        for side in side_scr:
            side[:, 0:pad, :] = jnp.zeros((2, pad, sub), F32)
            side[:, pad + tm:pad + tm + pad, :] = jnp.zeros((2, pad, sub), F32)

    pos = lax.broadcasted_iota(jnp.int32, (tm, 1), 0)
    if grid_conv:
        keep_prev = (j > 0).astype(F32)
        keep_next = (j < nj - 1).astype(F32)
        pos = pos % GRID_W
        left_ok, right_ok = pos != 0, pos != GRID_W - 1
        row_taps = (-1, 0, 1)
    else:
        left_ok, right_ok = pos != 0, pos != tm - 1
        row_taps = (0,)

    for si in range(nsub):
        cs = slice(si * sub, (si + 1) * sub)
        ge = _bdot(xm_scr[...], wg_ref[:, cs])
        if grid_conv:
            g_scr[si][0:halo, :] = ge[0:halo] * keep_prev
            g_scr[si][halo:halo + tm, :] = ge[halo:halo + tm]
            g_scr[si][halo + tm:halo + tm + halo, :] = ge[halo + tm:halo + tm + halo] * keep_next
        else:
            g_scr[si][...] = ge
        v_scr[si][...] = _bdot(xm_scr[halo:halo + tm, :], wv_ref[:, cs])

    acts = []
    for si in range(nsub):
        cs = slice(si * sub, (si + 1) * sub)

        def tap_sum(dc):
            tot = None
            for dr in row_taps:
                k = (dr + 1) * 3 + (dc + 1)
                r0 = halo + dr * GRID_W
                term = g_scr[si][r0:r0 + tm, :] * cw_ref[k:k + 1, cs]
                tot = term if tot is None else tot + term
            return tot

        side_scr[si][0, pad:pad + tm, :] = tap_sum(-1)
        side_scr[si][1, pad:pad + tm, :] = tap_sum(1)
        conv = tap_sum(0) + cb_ref[:, cs]
        conv = conv + jnp.where(left_ok, side_scr[si][0, pad - 1:pad - 1 + tm, :], 0.0)
        conv = conv + jnp.where(right_ok, side_scr[si][1, pad + 1:pad + 1 + tm, :], 0.0)
        acts.append((_gelu(conv) * v_scr[si][...]).astype(BF16))
    for si in range(nsub):
        acc_scr[...] += _bdot(acts[si], wd_ref[si * sub:(si + 1) * sub, :])

    @pl.when(f == nf - 1)
    def _():
        o_ref[...] = _layer_norm_rows(alpha * h_ref[...] + gate_ref[...] * acc_scr[...], lg_ref[...], lb_ref[...])


def _ffn(h, mods, w_up, conv_w9, conv_b, w_down, ln_g, ln_b, layer, alpha, grid_conv):
    nb, rows, d = h.shape
    fh = w_down.shape[1]
    tf = FFN_COL_TILE
    sub = FFN_SUB_TILE
    nf = fh // tf
    if grid_conv:
        tm, halo = FFN_ROW_TILE, GRID_W
    else:
        tm, halo = rows, 0
    nblk = rows // tm
    hb = tm // GRID_W
    nhalo_blocks = rows // GRID_W

    def vec(k):
        return pl.BlockSpec((None, None, 1, d), lambda b, j, f: (layer, b if grid_conv else nb, 0, k))

    in_specs = [pl.BlockSpec((None, tm, d), lambda b, j, f: (b, j, 0))]
    args = [h]
    if grid_conv:
        in_specs += [
            pl.BlockSpec((None, halo, d), lambda b, j, f: (b, jnp.maximum(j * hb - 1, 0), 0)),
            pl.BlockSpec((None, halo, d), lambda b, j, f: (b, jnp.minimum((j + 1) * hb, nhalo_blocks - 1), 0)),
        ]
        args += [h, h]
    in_specs += [
        vec(3), vec(4), vec(5),
        pl.BlockSpec((None, d, tf), lambda b, j, f: (layer, 0, f)),
        pl.BlockSpec((None, d, tf), lambda b, j, f: (layer, 0, nf + f)),
        pl.BlockSpec((None, 9, tf), lambda b, j, f: (layer, 0, f)),
        pl.BlockSpec((None, 1, tf), lambda b, j, f: (layer, 0, f)),
        pl.BlockSpec((None, tf, d), lambda b, j, f: (layer, f, 0)),
        _layer_spec(ln_g, layer), _layer_spec(ln_b, layer),
    ]
    args += [mods, mods, mods, w_up, w_up, conv_w9, conv_b, w_down, ln_g, ln_b]
    return pl.pallas_call(
        functools.partial(_ffn_kernel, alpha=alpha, grid_conv=grid_conv, halo=halo),
        grid=(nb, nblk, nf),
        in_specs=in_specs,
        out_specs=pl.BlockSpec((None, tm, d), lambda b, j, f: (b, j, 0)),
        out_shape=jax.ShapeDtypeStruct((nb, nblk * tm, d), F32),
        scratch_shapes=[pltpu.VMEM((tm + 2 * halo, d), BF16), pltpu.VMEM((tm, d), F32)]
        + [pltpu.VMEM((tm + 2 * halo, sub), F32), pltpu.VMEM((tm, sub), F32),
           pltpu.VMEM((2, tm + 2 * SUBLANE, sub), F32)] * (tf // sub),
        compiler_params=_cparams(("arbitrary", "arbitrary", "arbitrary")),
        name="ffn_lat" if grid_conv else "ffn_ctx",
    )(*args)


def kernel(x, c, ctx, c_ctx, w_ada, b_ada, w_in, w_out, ln1_g, ln1_b, ln2_g, ln2_b, s5_a_re, s5_a_im, s5_b_re, s5_b_im, s5_c_re, s5_c_im, s5_log_step, s5_d, s5_glu_w, s5_glu_b, sg_ln_g, sg_ln_b, sg_w, sg_b, pool_w, pool_b, pool_scale, m2_conv_w, m2_conv_b, m2_dt_bias, m2_a_log, m2_d, m2_norm_w, ffn_w_up, ffn_conv_w, ffn_conv_b, ffn_w_down):
    nb, nlat, d = x.shape
    nctx = ctx.shape[1]
    depth = w_ada.shape[0]
    gw = GROUP_W
    alpha = (2 * depth) ** 0.25
    assert nb + 1 <= MOD_ROWS and nlat % FFN_ROW_TILE == 0 and nlat % OUT_ROW_TILE == 0 and nctx % ROW_TILE == 0
    assert nlat % (GRID_W * SUBLANE) == 0 and FFN_ROW_TILE % GRID_W == 0
    assert w_in.shape[2] == 5 * gw + M2_XBC + 2 * M2_HEADS

    cmat = jnp.concatenate([c, c_ctx[None, :], jnp.zeros((MOD_ROWS - nb - 1, d), F32)], axis=0)
    mods = _ada(cmat, w_ada, b_ada).reshape(depth, MOD_ROWS, 1, 6 * d)

    def rows(v):
        return v.astype(F32).reshape(depth, 1, -1)

    fh = ffn_w_down.shape[1]
    lane_pad = LANE - 2 * M2_HEADS
    w_in_b = w_in.astype(BF16)
    w_dt_b = jnp.pad(w_in[:, :, w_in.shape[2] - 2 * M2_HEADS:], ((0, 0), (0, 0), (0, lane_pad))).astype(BF16)
    w_out_b = w_out.astype(BF16)
    w_up_b = ffn_w_up.astype(BF16)
    w_down_b = ffn_w_down.astype(BF16)
    cw9 = ffn_conv_w.astype(F32).reshape(depth, 9, fh)
    s5_mats = _s5_mats(s5_a_re, s5_a_im, s5_b_re, s5_b_im, s5_c_re, s5_c_im, s5_log_step, s5_d)
    sg_btile = jnp.repeat(jnp.swapaxes(sg_b.astype(F32), 1, 2), gw // SG_HEADS, axis=2)
    mixer_params = (s5_glu_w.astype(BF16), rows(s5_glu_b), rows(sg_ln_g), rows(sg_ln_b), sg_w.astype(BF16),
                    sg_btile, pool_w.astype(BF16), rows(pool_b), rows(pool_scale))
    m2_a = -jnp.exp(m2_a_log.astype(F32))
    a_dir = (m2_a[:, :, None, :] * jnp.eye(2, dtype=F32)[None, :, :, None]).reshape(depth, 2, 1, 2 * M2_HEADS)
    a_dir = jnp.pad(a_dir, ((0, 0), (0, 0), (0, 0), (0, lane_pad)))
    head_of_ch = jnp.arange(gw) // M2_HEAD_DIM
    expand = (jnp.arange(LANE)[None, :, None] == (jnp.arange(2)[:, None, None] * M2_HEADS + head_of_ch[None, None, :])
              ).astype(BF16)
    ssd_params = (m2_conv_w.astype(F32), rows(m2_conv_b),
                  jnp.pad(m2_dt_bias.astype(F32).reshape(depth, 1, 2 * M2_HEADS), ((0, 0), (0, 0), (0, lane_pad))),
                  a_dir, rows(jnp.repeat(m2_d.astype(F32), M2_HEAD_DIM, axis=1)), expand)
    m2_nw = rows(m2_norm_w)
    ln1 = rows(ln1_g), rows(ln1_b)
    ln2 = rows(ln2_g), rows(ln2_b)
    ffn_cb = rows(ffn_conv_b)

    h_lat, h_ctx = x, ctx
    for i in range(depth):
        need_ctx = i < depth - 1
        pa, pb, pd = _inproj(h_lat, h_ctx, mods, w_in_b, w_dt_b, i)
        ya = _s5(pa, s5_mats, i, nlat)
        mabc = _mixers(ya, pb, mixer_params, i, nlat)
        yf, yr = _ssd(pd, ssd_params, i, nlat)
        mix_in = (mabc, yf, yr, pd, m2_nw)
        h1_lat = _outproj(*mix_in, h_lat, mods, w_out_b, ln1[0], ln1[1], i, nlat, alpha, True)
        if need_ctx:
            h1_ctx = _outproj(*mix_in, h_ctx, mods, w_out_b, ln1[0], ln1[1], i, nlat, alpha, False)
            h_ctx = _ffn(h1_ctx, mods, w_up_b, cw9, ffn_cb, w_down_b, ln2[0], ln2[1], i, alpha, False)
        h_lat = _ffn(h1_lat, mods, w_up_b, cw9, ffn_cb, w_down_b, ln2[0], ln2[1], i, alpha, True)
    return h_lat
```

```python
import functools

import jax
import jax.numpy as jnp
from jax import lax
from jax.experimental import pallas as pl
from jax.experimental.pallas import tpu as pltpu

F32 = jnp.float32
BF16 = jnp.bfloat16

GRID_W = 64
GROUP_W = 512
S5_CH = 16
S5_Q = 16
S5_RELAYOUT_UNROLL = 4
SG_HEADS = 4
SG_CHUNK = 128
POOL_WINDOWS = (2, 4, 8, 16)
POOL_DIM = GROUP_W // len(POOL_WINDOWS)
POOL_HALO = 8
M2_HEAD_DIM = 64
M2_HEADS = GROUP_W // M2_HEAD_DIM
M2_STATE = 128
M2_GROUPS = 2
M2_CONV = 4
M2_CHUNK = 128
M2_XBC = GROUP_W + 2 * M2_GROUPS * M2_STATE
M2_HALO = 8
LN_EPS = 1e-5
RMS_EPS = 1e-5

LANE = 128
SUBLANE = 8
VMEM_LIMIT = 56 * 1024 * 1024
ROW_TILE = 256
OUT_ROW_TILE = 512
FFN_ROW_TILE = 512
FFN_COL_TILE = 512
FFN_SUB_TILE = 256
ADA_COL_TILE = 1024
MOD_ROWS = 8


def _cparams(sem):
    return pltpu.CompilerParams(dimension_semantics=sem, vmem_limit_bytes=VMEM_LIMIT)


def _bdot(a, b):
    return jnp.dot(a, b, preferred_element_type=F32)


def _gelu(x):
    return jax.nn.gelu(x)


def _silu(x):
    return x * jax.nn.sigmoid(x)


def _ada_kernel(c_ref, w_ref, b_ref, o_ref):
    c = c_ref[...]
    s = _silu(c).astype(BF16)
    o_ref[...] = _bdot(s, w_ref[...].astype(BF16)) + b_ref[...]


def _ada(cmat, w_ada, b_ada):
    depth, d, n = w_ada.shape
    tn = ADA_COL_TILE
    return pl.pallas_call(
        _ada_kernel,
        grid=(depth, n // tn),
        in_specs=[
            pl.BlockSpec((MOD_ROWS, d), lambda l, j: (0, 0)),
            pl.BlockSpec((None, d, tn), lambda l, j: (l, 0, j)),
            pl.BlockSpec((None, 1, tn), lambda l, j: (l, 0, j)),
        ],
        out_specs=pl.BlockSpec((None, MOD_ROWS, tn), lambda l, j: (l, 0, j)),
        out_shape=jax.ShapeDtypeStruct((depth, MOD_ROWS, n), F32),
        compiler_params=_cparams(("arbitrary", "arbitrary")),
        name="ada",
    )(cmat, w_ada, b_ada.reshape(depth, 1, n))


def _mod_spec(d, k, layer, nbatch, nlat_blocks):
    return pl.BlockSpec((None, None, 1, d),
                        lambda b, j, *_: (layer, jnp.where(j >= nlat_blocks, nbatch, b), 0, k))


def _layer_spec(arr, layer):
    tail = arr.shape[1:]
    return pl.BlockSpec((None,) + tail, lambda *_: (layer,) + (0,) * len(tail))


def _cast_w_in_kernel(w_ref, wb_ref, wdt_ref):
    n = wb_ref.shape[1]
    wb_ref[...] = w_ref[:, 0:n].astype(BF16)
    tail = w_ref[:, n:w_ref.shape[1]].astype(BF16)
    wdt_ref[...] = jnp.concatenate([tail, jnp.zeros((tail.shape[0], LANE - tail.shape[1]), BF16)], axis=1)


def _cast_w_in(w_in):
    depth, d, n = w_in.shape
    nfull = n // LANE * LANE
    tm = ROW_TILE
    return pl.pallas_call(
        _cast_w_in_kernel,
        grid=(depth, d // tm),
        in_specs=[pl.BlockSpec((None, tm, n), lambda l, i: (l, i, 0))],
        out_specs=[pl.BlockSpec((None, tm, nfull), lambda l, i: (l, i, 0)),
                   pl.BlockSpec((None, tm, LANE), lambda l, i: (l, i, 0))],
        out_shape=[jax.ShapeDtypeStruct((depth, d, nfull), BF16), jax.ShapeDtypeStruct((depth, d, LANE), BF16)],
        compiler_params=_cparams(("arbitrary", "arbitrary")),
        name="cast_w_in",
    )(w_in)


def _stream_specs(tm, d, nlat_blocks):
    lat = pl.BlockSpec((None, tm, d), lambda b, j: (b, jnp.minimum(j, nlat_blocks - 1), 0))
    ctx = pl.BlockSpec((None, tm, d), lambda b, j: (b, jnp.maximum(j - nlat_blocks, 0), 0))
    return lat, ctx


def _inproj_kernel(hl_ref, hc_ref, sh_ref, sc_ref, w_ref, wdt_ref, oa_ref, ob_ref, od_ref, xm_scr, *, nlat_blocks):
    j = pl.program_id(1)

    def modulated(h_ref):
        return (h_ref[...] * (1.0 + sc_ref[...]) + sh_ref[...]).astype(BF16)

    @pl.when(j < nlat_blocks)
    def _():
        xm_scr[...] = modulated(hl_ref)

    @pl.when(j >= nlat_blocks)
    def _():
        xm_scr[...] = modulated(hc_ref)

    xm = xm_scr[...]
    gw = GROUP_W
    oa_ref[...] = _bdot(xm, w_ref[:, 0:gw])
    ob_ref[...] = _bdot(xm, w_ref[:, gw:4 * gw])
    od_ref[:, 0:M2_XBC] = _bdot(xm, w_ref[:, 5 * gw:5 * gw + M2_XBC])
    od_ref[:, M2_XBC:M2_XBC + gw] = _bdot(xm, w_ref[:, 4 * gw:5 * gw])
    od_ref[:, M2_XBC + gw:M2_XBC + gw + LANE] = _bdot(xm, wdt_ref[...])


def _inproj(h_lat, h_ctx, mods, w, w_dt, layer):
    nb, nlat, d = h_lat.shape
    t = nlat + h_ctx.shape[1]
    tm = ROW_TILE
    na, nbw, nd = GROUP_W, 3 * GROUP_W, M2_XBC + GROUP_W + LANE
    lat_spec, ctx_spec = _stream_specs(tm, d, nlat // tm)

    def row(n):
        return pl.BlockSpec((None, tm, n), lambda b, j: (b, j, 0))

    return pl.pallas_call(
        functools.partial(_inproj_kernel, nlat_blocks=nlat // tm),
        grid=(nb, t // tm),
        in_specs=[lat_spec, ctx_spec, _mod_spec(d, 0, layer, nb, nlat // tm),
                  _mod_spec(d, 1, layer, nb, nlat // tm), _layer_spec(w, layer), _layer_spec(w_dt, layer)],
        out_specs=[row(na), row(nbw), row(nd)],
        out_shape=[jax.ShapeDtypeStruct((nb, t, n), F32) for n in (na, nbw, nd)],
        scratch_shapes=[pltpu.VMEM((tm, d), BF16)],
        compiler_params=_cparams(("arbitrary", "arbitrary")),
        name="inproj",
    )(h_lat, h_ctx, mods, mods, w, w_dt)


def _s5_mats(a_re, a_im, b_re, b_im, c_re, c_im, log_step, d_skip):
    q, hch = S5_Q, S5_CH
    nl, _, g, p = a_re.shape
    lre, lim = a_re.astype(F32), a_im.astype(F32)
    step = jnp.exp(log_step.astype(F32))[..., None]
    mag = jnp.exp(lre * step)
    ar, ai = mag * jnp.cos(lim * step), mag * jnp.sin(lim * step)
    den = lre * lre + lim * lim
    qr = ((ar - 1.0) * lre + ai * lim) / den
    qi = (ai * lre - (ar - 1.0) * lim) / den
    bre, bim = b_re.astype(F32), b_im.astype(F32)
    bbr = qr[..., None] * bre - qi[..., None] * bim
    bbi = qr[..., None] * bim + qi[..., None] * bre
    cre, cim = c_re.astype(F32), c_im.astype(F32)

    pr, pi = ar[:, :, :, None], ai[:, :, :, None]
    while pr.shape[3] < q:
        lr, li = pr[:, :, :, -1:], pi[:, :, :, -1:]
        pr, pi = (jnp.concatenate([pr, pr * lr - pi * li], axis=3), jnp.concatenate([pi, pr * li + pi * lr], axis=3))
    pk_re = jnp.concatenate([jnp.ones_like(pr[:, :, :, :1]), pr], axis=3)
    pk_im = jnp.concatenate([jnp.zeros_like(pi[:, :, :, :1]), pi], axis=3)
    pt_re, pt_im = jnp.swapaxes(pk_re, 3, 4), jnp.swapaxes(pk_im, 3, 4)

    def lag_kernels(d, pr, pi):
        ctr = jnp.swapaxes(cre[:, d], -1, -2)[:, :, :, None]
        cti = jnp.swapaxes(cim[:, d], -1, -2)[:, :, :, None]
        ca_re = (ctr * pr[..., None] - cti * pi[..., None]).reshape(nl, g, p, 1, q * hch)
        ca_im = (ctr * pi[..., None] + cti * pr[..., None]).reshape(nl, g, p, 1, q * hch)
        return jnp.sum(bbr[:, d][..., None] * ca_re - bbi[:, d][..., None] * ca_im, axis=2)

    lag_f = lag_kernels(0, pt_re[:, 0, :, :, :q], pt_im[:, 0, :, :, :q])
    lag_r = lag_kernels(1, jnp.flip(pt_re[:, 1, :, :, :q], axis=3), jnp.flip(pt_im[:, 1, :, :, :q], axis=3))
    dmat = jnp.eye(hch, dtype=F32) * d_skip.astype(F32).reshape(nl, g, 1, hch)
    mid = lag_r[..., (q - 1) * hch:] + lag_f[..., :hch] + dmat
    kk = jnp.concatenate([lag_r[..., :(q - 1) * hch], mid, lag_f[..., hch:],
                          jnp.zeros(mid.shape[:-1] + (hch,), F32)], axis=-1)

    def state_in(d, pr, pi):
        btr = jnp.swapaxes(bbr[:, d], -1, -2)[:, :, None]
        bti = jnp.swapaxes(bbi[:, d], -1, -2)[:, :, None]
        pr, pi = pr[:, :, :, None], pi[:, :, :, None]
        m = jnp.concatenate([pr * btr - pi * bti, pr * bti + pi * btr], axis=-1)
        return m.reshape(nl, g, q * hch, 2 * p)

    sb = jnp.concatenate([state_in(0, jnp.flip(pk_re[:, 0, :, :q], axis=2), jnp.flip(pk_im[:, 0, :, :q], axis=2)),
                          state_in(1, pk_re[:, 1, :, :q], pk_im[:, 1, :, :q])], axis=-1)

    def state_out(d, pr, pi):
        ctr = jnp.swapaxes(cre[:, d], -1, -2)[:, :, :, None]
        cti = jnp.swapaxes(cim[:, d], -1, -2)[:, :, :, None]
        pr, pi = pr[..., None], pi[..., None]
        m = jnp.concatenate([ctr * pr - cti * pi, -(ctr * pi + cti * pr)], axis=2)
        return m.reshape(nl, g, 2 * p, q * hch)

    w2 = jnp.concatenate([state_out(0, pt_re[:, 0, :, :, 1:], pt_im[:, 0, :, :, 1:]),
                          state_out(1, jnp.flip(pt_re[:, 1, :, :, 1:], axis=3), jnp.flip(pt_im[:, 1, :, :, 1:], axis=3))],
                         axis=2)

    aqr, aqi = pk_re[:, :, :, q], pk_im[:, :, :, q]
    a1 = jnp.concatenate([aqr, aqr], axis=-1).reshape(nl, 2, g * 2 * p)
    a2 = jnp.concatenate([-aqi, aqi], axis=-1).reshape(nl, 2, g * 2 * p)
    acoef = jnp.stack([a1[:, 0], a2[:, 0], a1[:, 1], a2[:, 1]], axis=1)
    return kk, sb, w2, acoef


def _granule_transpose(v):
    n = len(v)
    slot = lax.broadcasted_iota(jnp.int32, v[0].shape, 1) // S5_CH
    v = list(v)
    j = n // 2
    while j >= 1:
        keep = (slot & j) == 0
        for i in range(n):
            if i & j == 0:
                a, b = v[i], v[i + j]
                v[i] = jnp.where(keep, a, pltpu.roll(b, j * S5_CH, axis=1))
                v[i + j] = jnp.where(keep, pltpu.roll(a, LANE - j * S5_CH, axis=1), b)
        j //= 2
    return v


def _split_bf16(x):
    hi = x.astype(BF16)
    return hi, (x - hi.astype(F32)).astype(BF16)


def _dot3(xs, w):
    x_hi, x_lo = xs
    w_hi, w_lo = _split_bf16(w)
    return _bdot(x_hi, w_hi) + (_bdot(x_lo, w_hi) + _bdot(x_hi, w_lo))


def _s5_kernel(pa_ref, kk_ref, sb_ref, w2_ref, ac_ref, ya_ref, x_scr, s_scr, h_scr, t_scr, *, nbatch, nrows, nctx):
    gb = LANE // S5_CH
    halves = S5_Q // gb
    qh = S5_Q * S5_CH
    sw = LANE
    nlat = nrows - nctx
    nrc = nrows // SUBLANE
    tok_per_rc = SUBLANE * S5_Q

    def relayout(it, to_rows):
        for u in range(S5_RELAYOUT_UNROLL):
            idx = it * S5_RELAYOUT_UNROLL + u
            b = idx // nrc
            rc = idx % nrc
            r0 = pl.multiple_of(b * nrows + rc * SUBLANE, SUBLANE)
            for half in range(halves):
                lanes = slice(half * LANE, (half + 1) * LANE)

                def tok(s8, half=half, rc=rc):
                    return pl.ds(rc * tok_per_rc + half * gb + s8, SUBLANE, stride=S5_Q)

                if to_rows:
                    v = _granule_transpose([pa_ref[b, tok(s8), :] for s8 in range(gb)])
                    for gi in range(gb):
                        x_scr[gi, pl.ds(r0, SUBLANE), lanes] = v[gi]
                else:
                    v = _granule_transpose([x_scr[gi, pl.ds(r0, SUBLANE), lanes] for gi in range(gb)])
                    for s8 in range(gb):
                        ya_ref[b, tok(s8), :] = v[s8]
        return 0

    nblocks = nbatch * nrc // S5_RELAYOUT_UNROLL
    lax.fori_loop(0, nblocks, lambda it, c: relayout(it, True), 0)

    for gi in range(gb):
        kkv = kk_ref[gi]
        for s in range(S5_Q):
            shift = (S5_Q - 1 - s) * S5_CH
            t_scr[s * S5_CH:(s + 1) * S5_CH, :] = pltpu.roll(kkv, kkv.shape[1] - shift, axis=1)[:, 0:qh]
        xs = _split_bf16(x_scr[gi])
        x_scr[gi] = _dot3(xs, t_scr[...])
        rs = _dot3(xs, sb_ref[gi])
        for d in range(2):
            sd = rs[:, d * sw:(d + 1) * sw]
            s_scr[d, gi] = sd
            s_scr[2 + d, gi] = pltpu.roll(sd, sw // 2, axis=1)

    zero = jnp.zeros((nbatch, sw), F32)

    def body(s, carry):
        f_rows = pl.ds(jnp.where(s < nctx, s + nlat, s - nctx), nbatch, stride=nrows)
        r_rows = pl.ds(nrows - 1 - s, nbatch, stride=nrows)
        out = []
        for gi in range(gb):
            hf, hfs, hr, hrs = carry[4 * gi:4 * gi + 4]
            a1f, a2f, a1r, a2r = (ac_ref[k:k + 1, gi * sw:(gi + 1) * sw] for k in range(4))
            h_scr[0, gi, f_rows, :] = hf
            h_scr[1, gi, r_rows, :] = hr
            out += [a1f * hf + a2f * hfs + s_scr[0, gi, f_rows, :],
                    a1f * hfs - a2f * hf + s_scr[2, gi, f_rows, :],
                    a1r * hr + a2r * hrs + s_scr[1, gi, r_rows, :],
                    a1r * hrs - a2r * hr + s_scr[3, gi, r_rows, :]]
        return tuple(out)

    lax.fori_loop(0, nrows, body, (zero,) * (4 * gb))

    for gi in range(gb):
        hin = jnp.concatenate([h_scr[0, gi], h_scr[1, gi]], axis=1)
        x_scr[gi] = x_scr[gi] + _dot3(_split_bf16(hin), w2_ref[gi])

    lax.fori_loop(0, nblocks, lambda it, c: relayout(it, False), 0)


def _s5(pa, mats, layer, nlat):
    kk, sb, w2, acoef = mats
    nb, t, gw = pa.shape
    q = S5_Q
    nrows = t // q
    qh = q * S5_CH
    gb = LANE // S5_CH
    assert nrows % SUBLANE == 0 and q % gb == 0 and w2.shape[2] == 2 * LANE and sb.shape[3] == 2 * LANE
    assert (nb * nrows // SUBLANE) % S5_RELAYOUT_UNROLL == 0
    return pl.pallas_call(
        functools.partial(_s5_kernel, nbatch=nb, nrows=nrows, nctx=(t - nlat) // q),
        grid=(gw // LANE,),
        in_specs=[
            pl.BlockSpec((nb, t, LANE), lambda i: (0, 0, i)),
            pl.BlockSpec((None, gb, S5_CH, 2 * qh), lambda i: (layer, i, 0, 0)),
            pl.BlockSpec((None, gb, qh, 2 * LANE), lambda i: (layer, i, 0, 0)),
            pl.BlockSpec((None, gb, 2 * LANE, qh), lambda i: (layer, i, 0, 0)),
            pl.BlockSpec((None, 4, gb * LANE), lambda i: (layer, 0, i)),
        ],
        out_specs=pl.BlockSpec((nb, t, LANE), lambda i: (0, 0, i)),
        out_shape=jax.ShapeDtypeStruct((nb, t, gw), F32),
        scratch_shapes=[pltpu.VMEM((gb, nb * nrows, qh), F32),
                        pltpu.VMEM((4, gb, nb * nrows, LANE), F32),
                        pltpu.VMEM((2, gb, nb * nrows, LANE), F32),
                        pltpu.VMEM((qh, qh), F32)],
        compiler_params=_cparams(("arbitrary",)),
        name="s5",
    )(pa, kk, sb, w2, acoef)


def _mixers_kernel(ya_ref, pb_ref, prev_ref, next_ref, gluw_ref, glub_ref, lng_ref, lnb_ref, sgw_ref,
                   sgb_ref, pw_ref, pbias_ref, pscale_ref, o_ref, ext_scr, *, nlat_blocks, nblocks, nlat, nctx):
    j = pl.program_id(1)
    tm = ya_ref.shape[0]
    gw = GROUP_W

    z = _gelu(ya_ref[...])
    gate = jax.nn.sigmoid(_bdot(z.astype(BF16), gluw_ref[...]) + glub_ref[...])
    o_ref[:, 0:gw] = z * gate

    hd = gw // SG_HEADS
    u = _gelu(pb_ref[:, 0:gw])
    v = _gelu(pb_ref[:, gw:2 * gw])
    for hh in range(SG_HEADS):
        vh = v[:, hh * hd:(hh + 1) * hd]
        mu = jnp.mean(vh, axis=-1, keepdims=True)
        var = jnp.mean(jnp.square(vh - mu), axis=-1, keepdims=True)
        vn = ((vh - mu) * lax.rsqrt(var + LN_EPS) * lng_ref[:, hh * hd:(hh + 1) * hd]
              + lnb_ref[:, hh * hd:(hh + 1) * hd]).astype(BF16)
        for cchunk in range(tm // SG_CHUNK):
            rows = slice(cchunk * SG_CHUNK, (cchunk + 1) * SG_CHUNK)
            s = _bdot(sgw_ref[hh], vn[rows]) + sgb_ref[:, hh * hd:(hh + 1) * hd]
            o_ref[rows, gw + hh * hd:gw + (hh + 1) * hd] = u[rows, hh * hd:(hh + 1) * hd] * s

    is_first = jnp.logical_or(j == 0, j == nlat_blocks)
    is_last = jnp.logical_or(j == nlat_blocks - 1, j == nblocks - 1)
    p = pb_ref[:, 2 * gw:3 * gw]
    hal = POOL_HALO
    ext_scr[0:hal, :] = jnp.where(is_first, 0.0, prev_ref[...])
    ext_scr[hal:hal + tm, :] = p
    ext_scr[hal + tm:hal + tm + hal, :] = jnp.where(is_last, 0.0, next_ref[...])
    in_ctx = j >= nlat_blocks
    seq_len = jnp.where(in_ctx, nctx, nlat)
    t0 = (j - jnp.where(in_ctx, nlat_blocks, 0)) * tm
    tpos = t0 + lax.broadcasted_iota(jnp.int32, (tm, 1), 0)
    pd = POOL_DIM
    for gi, win in enumerate(POOL_WINDOWS):
        cols = slice(gi * pd, (gi + 1) * pd)
        tot = jnp.zeros((tm, pd), F32)
        for off in range(-(win // 2), win // 2):
            tot = tot + ext_scr[hal + off:hal + off + tm, cols]
        lo = jnp.maximum(tpos - win // 2, 0)
        hi = jnp.minimum(tpos + win // 2 - 1, seq_len - 1)
        mean = tot / (hi - lo + 1).astype(F32)
        yv = _bdot((mean - p[:, cols]).astype(BF16), pw_ref[gi]) + pbias_ref[:, cols]
        o_ref[:, 2 * gw + gi * pd:2 * gw + (gi + 1) * pd] = yv * pscale_ref[:, cols]


def _mixers(ya, pb, params, layer, nlat):
    nb, t, gw = ya.shape
    tm = ROW_TILE
    nblocks = t // tm
    nlat_blocks = nlat // tm
    hb = tm // POOL_HALO
    nh = t // POOL_HALO

    return pl.pallas_call(
        functools.partial(_mixers_kernel, nlat_blocks=nlat_blocks, nblocks=nblocks, nlat=nlat, nctx=t - nlat),
        grid=(nb, nblocks),
        in_specs=[
            pl.BlockSpec((None, tm, gw), lambda b, j: (b, j, 0)),
            pl.BlockSpec((None, tm, 3 * gw), lambda b, j: (b, j, 0)),
            pl.BlockSpec((None, POOL_HALO, gw), lambda b, j: (b, jnp.maximum(j * hb - 1, 0), 2)),
            pl.BlockSpec((None, POOL_HALO, gw), lambda b, j: (b, jnp.minimum((j + 1) * hb, nh - 1), 2)),
        ] + [_layer_spec(v, layer) for v in params],
        out_specs=pl.BlockSpec((None, tm, 3 * gw), lambda b, j: (b, j, 0)),
        out_shape=jax.ShapeDtypeStruct((nb, t, 3 * gw), F32),
        scratch_shapes=[pltpu.VMEM((tm + 2 * POOL_HALO, gw), F32)],
        compiler_params=_cparams(("arbitrary", "arbitrary")),
        name="mixers",
    )(ya, pb, pb, pb, *params)


def _ssd_prep_kernel(pd_ref, prev_ref, next_ref, cw_ref, cb_ref, dtb_ref, xc_ref, dt_ref, ext_scr,
                     *, nlat_blocks, nblocks):
    j = pl.program_id(1)
    tm = pd_ref.shape[0]
    nx = M2_XBC
    hal = M2_HALO
    is_first = jnp.logical_or(j == 0, j == nlat_blocks)
    is_last = jnp.logical_or(j == nlat_blocks - 1, j == nblocks - 1)
    ext_scr[0:hal, :] = jnp.where(is_first, 0.0, prev_ref[...])
    ext_scr[hal:hal + tm, :] = pd_ref[:, 0:nx]
    ext_scr[hal + tm:hal + tm + hal, :] = jnp.where(is_last, 0.0, next_ref[...])
    acc = jnp.zeros((tm, nx), F32) + cb_ref[...]
    for tap in range(M2_CONV):
        off = hal + tap - M2_CONV // 2
        acc = acc + ext_scr[off:off + tm, :] * cw_ref[tap:tap + 1, :]
    xc_ref[...] = _silu(acc)
    xdt = pd_ref[:, nx + GROUP_W:nx + GROUP_W + LANE] + dtb_ref[...]
    dt_ref[...] = jnp.maximum(xdt, 0.0) + jnp.log1p(jnp.exp(-jnp.abs(xdt)))


def _split3_bf16(x):
    x1 = x.astype(BF16)
    r1 = x - x1.astype(F32)
    x2 = r1.astype(BF16)
    return x1, x2, (r1 - x2.astype(F32)).astype(BF16)


def _ssd_direction(reverse, xc_ref, dt_ref, a_ref, dsk_ref, e_ref, st_scr, o_ref):
    qn = M2_CHUNK
    gw = GROUP_W
    lane0 = M2_HEADS if reverse else 0
    xs = xc_ref[:, 0:gw]
    nbc = M2_GROUPS * M2_STATE
    bm = xc_ref[:, gw:gw + nbc]
    cm = xc_ref[:, gw + nbc:gw + 2 * nbc]

    dt = dt_ref[...]
    da = dt * a_ref[...]
    ri = lax.broadcasted_iota(jnp.int32, (qn, qn), 0)
    ci = lax.broadcasted_iota(jnp.int32, (qn, qn), 1)
    causal = (ri <= ci) if reverse else (ri >= ci)
    mask = causal.astype(BF16)
    d1, d2, d3 = _split3_bf16(da)
    a_col = _bdot(mask, d1) + (_bdot(mask, d2) + _bdot(mask, d3))
    a_row = a_col.T
    tot_row = a_col[0:1, :] if reverse else a_col[qn - 1:qn, :]

    per_head = jnp.concatenate([dt, jnp.exp(a_col), jnp.exp(tot_row - a_col)], axis=0)
    p1, p2 = _split_bf16(per_head)
    e = e_ref[...]
    per_ch = _bdot(p1, e) + _bdot(p2, e)
    dt_x, ea_x, de_x = per_ch[0:qn], per_ch[qn:2 * qn], per_ch[2 * qn:3 * qn]
    et_x = ea_x[0:1, :] if reverse else ea_x[qn - 1:qn, :]
    xd = xs * dt_x
    xdw = (xd * de_x).astype(BF16)
    xdb = xd.astype(BF16)

    hpg = M2_HEADS // M2_GROUPS
    gcols = hpg * M2_HEAD_DIM
    for g in range(M2_GROUPS):
        bg = bm[:, g * M2_STATE:(g + 1) * M2_STATE]
        cg = cm[:, g * M2_STATE:(g + 1) * M2_STATE].astype(BF16)
        bgt = bg.T.astype(BF16)
        scores = _bdot(cg, bgt)
        st_old = st_scr[:, g * gcols:(g + 1) * gcols]
        y_inter = _bdot(cg, st_old.astype(BF16)) * ea_x[:, g * gcols:(g + 1) * gcols]
        st_new = _bdot(bgt, xdw[:, g * gcols:(g + 1) * gcols])
        st_scr[:, g * gcols:(g + 1) * gcols] = et_x[:, g * gcols:(g + 1) * gcols] * st_old + st_new
        for hl in range(hpg):
            hh = g * hpg + hl
            seg = a_col[:, lane0 + hh:lane0 + hh + 1] - a_row[lane0 + hh:lane0 + hh + 1, :]
            decay = jnp.exp(jnp.where(causal, seg, -jnp.inf))
            cols = slice(hh * M2_HEAD_DIM, (hh + 1) * M2_HEAD_DIM)
            y_h = _bdot((scores * decay).astype(BF16), xdb[:, cols])
            y_h = y_h + y_inter[:, hl * M2_HEAD_DIM:(hl + 1) * M2_HEAD_DIM]
            if reverse:
                o_ref[:, cols] = y_h
            else:
                o_ref[:, cols] = y_h + dsk_ref[:, cols] * xs[:, cols]


def _ssd_kernel(xcf_ref, dtf_ref, xcr_ref, dtr_ref, af_ref, ar_ref, dsk_ref, ef_ref, er_ref,
                yf_ref, yr_ref, stf_scr, str_scr):
    @pl.when(pl.program_id(1) == 0)
    def _():
        stf_scr[...] = jnp.zeros_like(stf_scr)
        str_scr[...] = jnp.zeros_like(str_scr)

    _ssd_direction(False, xcf_ref, dtf_ref, af_ref, dsk_ref, ef_ref, stf_scr, yf_ref)
    _ssd_direction(True, xcr_ref, dtr_ref, ar_ref, dsk_ref, er_ref, str_scr, yr_ref)


def _ssd(pd, params, layer, nlat):
    conv_w, conv_b, dtb, a_dir, dsk, expand = params
    nb, t, wd = pd.shape
    gw = GROUP_W
    tm = ROW_TILE
    hb = tm // M2_HALO
    nh = t // M2_HALO
    xc, dtp = pl.pallas_call(
        functools.partial(_ssd_prep_kernel, nlat_blocks=nlat // tm, nblocks=t // tm),
        grid=(nb, t // tm),
        in_specs=[
            pl.BlockSpec((None, tm, wd), lambda b, j: (b, j, 0)),
            pl.BlockSpec((None, M2_HALO, M2_XBC), lambda b, j: (b, jnp.maximum(j * hb - 1, 0), 0)),
            pl.BlockSpec((None, M2_HALO, M2_XBC), lambda b, j: (b, jnp.minimum((j + 1) * hb, nh - 1), 0)),
            _layer_spec(conv_w, layer), _layer_spec(conv_b, layer), _layer_spec(dtb, layer),
        ],
        out_specs=[pl.BlockSpec((None, tm, M2_XBC), lambda b, j: (b, j, 0)),
                   pl.BlockSpec((None, tm, LANE), lambda b, j: (b, j, 0))],
        out_shape=[jax.ShapeDtypeStruct((nb, t, M2_XBC), F32), jax.ShapeDtypeStruct((nb, t, LANE), F32)],
        scratch_shapes=[pltpu.VMEM((tm + 2 * M2_HALO, M2_XBC), F32)],
        compiler_params=_cparams(("arbitrary", "arbitrary")),
        name="ssd_prep",
    )(pd, pd, pd, conv_w, conv_b, dtb)

    qn = M2_CHUNK
    nchunks = t // qn
    nctx_chunks = nchunks - nlat // qn

    def fwd_chunk(k):
        return jnp.where(k < nctx_chunks, k + (nchunks - nctx_chunks), k - nctx_chunks)

    def rev_chunk(k):
        return nchunks - 1 - k

    def rows(n, chunk_of):
        return pl.BlockSpec((None, qn, n), lambda b, k: (b, chunk_of(k), 0))

    def direction(d):
        return pl.BlockSpec((None, None, 1, LANE), lambda b, k: (layer, d, 0, 0))

    return pl.pallas_call(
        _ssd_kernel,
        grid=(nb, nchunks),
        in_specs=[rows(M2_XBC, fwd_chunk), rows(LANE, fwd_chunk), rows(M2_XBC, rev_chunk), rows(LANE, rev_chunk),
                  direction(0), direction(1), _layer_spec(dsk, layer),
                  pl.BlockSpec((None, LANE, gw), lambda b, k: (0, 0, 0)),
                  pl.BlockSpec((None, LANE, gw), lambda b, k: (1, 0, 0))],
        out_specs=[rows(gw, fwd_chunk), rows(gw, rev_chunk)],
        out_shape=[jax.ShapeDtypeStruct((nb, t, gw), F32), jax.ShapeDtypeStruct((nb, t, gw), F32)],
        scratch_shapes=[pltpu.VMEM((M2_STATE, gw), F32), pltpu.VMEM((M2_STATE, gw), F32)],
        compiler_params=_cparams(("arbitrary", "arbitrary")),
        name="ssd_scan",
    )(xc, dtp, xc, dtp, a_dir, a_dir, dsk, expand, expand)


def _layer_norm_rows(v, g, b):
    mu = jnp.mean(v, axis=-1, keepdims=True)
    var = jnp.mean(jnp.square(v - mu), axis=-1, keepdims=True)
    return (v - mu) * lax.rsqrt(var + LN_EPS) * g + b


def _outproj_kernel(mabc_ref, yf_ref, yr_ref, z_ref, nw_ref, h_ref, gate_ref, w_ref, lg_ref, lb_ref, o_ref,
                    *, alpha):
    k1 = mabc_ref.shape[1]
    gc = GROUP_W // M2_GROUPS
    for r0 in range(0, o_ref.shape[0], ROW_TILE):
        rows = slice(r0, r0 + ROW_TILE)
        gv = (yf_ref[rows, :] + yr_ref[rows, :]) * _silu(z_ref[rows, :])
        md = []
        for g in range(M2_GROUPS):
            part = gv[:, g * gc:(g + 1) * gc]
            ms = jnp.mean(jnp.square(part), axis=-1, keepdims=True)
            md.append((part * lax.rsqrt(ms + RMS_EPS) * nw_ref[:, g * gc:(g + 1) * gc]).astype(BF16))
        mix = (_bdot(mabc_ref[rows, :].astype(BF16), w_ref[0:k1, :])
               + _bdot(jnp.concatenate(md, axis=1), w_ref[k1:w_ref.shape[0], :]))
        o_ref[rows, :] = _layer_norm_rows(alpha * h_ref[rows, :] + gate_ref[...] * mix, lg_ref[...], lb_ref[...])


def _outproj(mabc, yf, yr, pd, norm_w, h, mods, w, ln_g, ln_b, layer, nlat, alpha, latent):
    nb, rows, d = h.shape
    tm = OUT_ROW_TILE if latent else rows
    blk0 = 0 if latent else nlat // tm
    k1, k2 = mabc.shape[2], yf.shape[2]
    z_block = M2_XBC // k2

    def row(n):
        return pl.BlockSpec((None, tm, n), lambda b, j: (b, j + blk0, 0))

    return pl.pallas_call(
        functools.partial(_outproj_kernel, alpha=alpha),
        grid=(nb, rows // tm),
        in_specs=[row(k1), row(k2), row(k2), pl.BlockSpec((None, tm, k2), lambda b, j: (b, j + blk0, z_block)),
                  _layer_spec(norm_w, layer), pl.BlockSpec((None, tm, d), lambda b, j: (b, j, 0)),
                  pl.BlockSpec((None, None, 1, d), lambda b, j: (layer, b if latent else nb, 0, 2)),
                  _layer_spec(w, layer), _layer_spec(ln_g, layer), _layer_spec(ln_b, layer)],
        out_specs=pl.BlockSpec((None, tm, d), lambda b, j: (b, j, 0)),
        out_shape=jax.ShapeDtypeStruct((nb, rows, d), F32),
        compiler_params=_cparams(("arbitrary", "arbitrary")),
        name="outproj_lat" if latent else "outproj_ctx",
    )(mabc, yf, yr, pd, norm_w, h, mods, w, ln_g, ln_b)


def _ffn_kernel(*refs, alpha, grid_conv, halo):
    if grid_conv:
        (h_ref, next_ref, sh_ref, sc_ref, gate_ref, wg_ref, wv_ref, cw_ref, cb_ref, wd_ref,
         lg_ref, lb_ref, o_ref, xm_scr, acc_scr, tail_scr, *chain_scr) = refs
    else:
        (h_ref, sh_ref, sc_ref, gate_ref, wg_ref, wv_ref, cw_ref, cb_ref, wd_ref,
         lg_ref, lb_ref, o_ref, xm_scr, acc_scr, tail_scr, *chain_scr) = refs
    g_scr, v_scr, side_scr = chain_scr[0::3], chain_scr[1::3], chain_scr[2::3]
    j = pl.program_id(1)
    f = pl.program_id(2)
    nj = pl.num_programs(1)
    nf = pl.num_programs(2)
    tm = h_ref.shape[0]
    sub = g_scr[0].shape[1]
    nsub = len(g_scr)
    pad = SUBLANE

    @pl.when(f == 0)
    def _():
        def modulated(v):
            return (v * (1.0 + sc_ref[...]) + sh_ref[...]).astype(BF16)
        xm_scr[0:tm, :] = modulated(h_ref[...])
        if grid_conv:
            xm_scr[tm:tm + halo, :] = modulated(next_ref[...])
        acc_scr[...] = jnp.zeros_like(acc_scr)
        for side in side_scr:
            side[:, 0:pad, :] = jnp.zeros((2, pad, sub), F32)
            side[:, pad + tm:pad + tm + pad, :] = jnp.zeros((2, pad, sub), F32)

    if grid_conv:
        @pl.when(j == 0)
        def _():
            tail_scr[f] = jnp.zeros(tail_scr.shape[1:], F32)

    pos = lax.broadcasted_iota(jnp.int32, (tm, 1), 0)
    if grid_conv:
        keep_next = (j < nj - 1).astype(F32)
        pos = pos % GRID_W
        left_ok, right_ok = pos != 0, pos != GRID_W - 1
        row_taps = (-1, 0, 1)
    else:
        left_ok, right_ok = pos != 0, pos != tm - 1
        row_taps = (0,)

    for si in range(nsub):
        cs = slice(si * sub, (si + 1) * sub)
        ge = _bdot(xm_scr[...], wg_ref[:, cs])
        if grid_conv:
            g_scr[si][0:halo, :] = tail_scr[f, :, cs]
            g_scr[si][halo:halo + tm, :] = ge[0:tm]
            g_scr[si][halo + tm:halo + tm + halo, :] = ge[tm:tm + halo] * keep_next
            tail_scr[f, :, cs] = ge[tm - halo:tm]
        else:
            g_scr[si][...] = ge
        v_scr[si][...] = _bdot(xm_scr[0:tm, :], wv_ref[:, cs])

    acts = []
    for si in range(nsub):
        cs = slice(si * sub, (si + 1) * sub)

        def tap_sum(dc):
            tot = None
            for dr in row_taps:
                k = (dr + 1) * 3 + (dc + 1)
                r0 = halo + dr * GRID_W
                term = g_scr[si][r0:r0 + tm, :] * cw_ref[k:k + 1, cs]
                tot = term if tot is None else tot + term
            return tot

        side_scr[si][0, pad:pad + tm, :] = tap_sum(-1)
        side_scr[si][1, pad:pad + tm, :] = tap_sum(1)
        conv = tap_sum(0) + cb_ref[:, cs]
        conv = conv + jnp.where(left_ok, side_scr[si][0, pad - 1:pad - 1 + tm, :], 0.0)
        conv = conv + jnp.where(right_ok, side_scr[si][1, pad + 1:pad + 1 + tm, :], 0.0)
        acts.append((_gelu(conv) * v_scr[si][...]).astype(BF16))
    for si in range(nsub):
        acc_scr[...] += _bdot(acts[si], wd_ref[si * sub:(si + 1) * sub, :])

    @pl.when(f == nf - 1)
    def _():
        o_ref[...] = _layer_norm_rows(alpha * h_ref[...] + gate_ref[...] * acc_scr[...], lg_ref[...], lb_ref[...])


def _ffn(h, mods, w_up, conv_w9, conv_b, w_down, ln_g, ln_b, layer, alpha, grid_conv):
    nb, rows, d = h.shape
    fh = w_down.shape[1]
    tf = FFN_COL_TILE
    sub = FFN_SUB_TILE
    nf = fh // tf
    if grid_conv:
        tm, halo = FFN_ROW_TILE, GRID_W
    else:
        tm, halo = rows, 0
    nblk = rows // tm
    hb = tm // GRID_W
    nhalo_blocks = rows // GRID_W

    def vec(k):
        return pl.BlockSpec((None, None, 1, d), lambda b, j, f: (layer, b if grid_conv else nb, 0, k))

    in_specs = [pl.BlockSpec((None, tm, d), lambda b, j, f: (b, j, 0))]
    args = [h]
    if grid_conv:
        in_specs += [
            pl.BlockSpec((None, halo, d), lambda b, j, f: (b, jnp.minimum((j + 1) * hb, nhalo_blocks - 1), 0)),
        ]
        args += [h]
    in_specs += [
        vec(3), vec(4), vec(5),
        pl.BlockSpec((None, d, tf), lambda b, j, f: (layer, 0, f)),
        pl.BlockSpec((None, d, tf), lambda b, j, f: (layer, 0, nf + f)),
        pl.BlockSpec((None, 9, tf), lambda b, j, f: (layer, 0, f)),
        pl.BlockSpec((None, 1, tf), lambda b, j, f: (layer, 0, f)),
        pl.BlockSpec((None, tf, d), lambda b, j, f: (layer, f, 0)),
        _layer_spec(ln_g, layer), _layer_spec(ln_b, layer),
    ]
    args += [mods, mods, mods, w_up, w_up, conv_w9, conv_b, w_down, ln_g, ln_b]
    return pl.pallas_call(
        functools.partial(_ffn_kernel, alpha=alpha, grid_conv=grid_conv, halo=halo),
        grid=(nb, nblk, nf),
        in_specs=in_specs,
        out_specs=pl.BlockSpec((None, tm, d), lambda b, j, f: (b, j, 0)),
        out_shape=jax.ShapeDtypeStruct((nb, nblk * tm, d), F32),
        scratch_shapes=[pltpu.VMEM((tm + halo, d), BF16), pltpu.VMEM((tm, d), F32),
                        pltpu.VMEM((nf, max(halo, SUBLANE), tf), F32)]
        + [pltpu.VMEM((tm + 2 * halo, sub), F32), pltpu.VMEM((tm, sub), F32),
           pltpu.VMEM((2, tm + 2 * SUBLANE, sub), F32)] * (tf // sub),
        compiler_params=_cparams(("arbitrary", "arbitrary", "arbitrary")),
        name="ffn_lat" if grid_conv else "ffn_ctx",
    )(*args)


def kernel(x, c, ctx, c_ctx, w_ada, b_ada, w_in, w_out, ln1_g, ln1_b, ln2_g, ln2_b, s5_a_re, s5_a_im, s5_b_re, s5_b_im, s5_c_re, s5_c_im, s5_log_step, s5_d, s5_glu_w, s5_glu_b, sg_ln_g, sg_ln_b, sg_w, sg_b, pool_w, pool_b, pool_scale, m2_conv_w, m2_conv_b, m2_dt_bias, m2_a_log, m2_d, m2_norm_w, ffn_w_up, ffn_conv_w, ffn_conv_b, ffn_w_down):
    nb, nlat, d = x.shape
    nctx = ctx.shape[1]
    depth = w_ada.shape[0]
    gw = GROUP_W
    alpha = (2 * depth) ** 0.25
    assert nb + 1 <= MOD_ROWS and nlat % FFN_ROW_TILE == 0 and nlat % OUT_ROW_TILE == 0 and nctx % ROW_TILE == 0
    assert nlat % (GRID_W * SUBLANE) == 0 and FFN_ROW_TILE % GRID_W == 0
    assert w_in.shape[2] == 5 * gw + M2_XBC + 2 * M2_HEADS and (5 * gw + M2_XBC) % LANE == 0

    cmat = jnp.concatenate([c, c_ctx[None, :], jnp.zeros((MOD_ROWS - nb - 1, d), F32)], axis=0)
    mods = _ada(cmat, w_ada, b_ada).reshape(depth, MOD_ROWS, 1, 6 * d)

    def rows(v):
        return v.astype(F32).reshape(depth, 1, -1)

    fh = ffn_w_down.shape[1]
    lane_pad = LANE - 2 * M2_HEADS
    w_in_b, w_dt_b = _cast_w_in(w_in)
    w_out_b = w_out.astype(BF16)
    w_up_b = ffn_w_up.astype(BF16)
    w_down_b = ffn_w_down.astype(BF16)
    cw9 = ffn_conv_w.astype(F32).reshape(depth, 9, fh)
    s5_mats = _s5_mats(s5_a_re, s5_a_im, s5_b_re, s5_b_im, s5_c_re, s5_c_im, s5_log_step, s5_d)
    sg_btile = jnp.repeat(jnp.swapaxes(sg_b.astype(F32), 1, 2), gw // SG_HEADS, axis=2)
    mixer_params = (s5_glu_w.astype(BF16), rows(s5_glu_b), rows(sg_ln_g), rows(sg_ln_b), sg_w.astype(BF16),
                    sg_btile, pool_w.astype(BF16), rows(pool_b), rows(pool_scale))
    m2_a = -jnp.exp(m2_a_log.astype(F32))
    a_dir = (m2_a[:, :, None, :] * jnp.eye(2, dtype=F32)[None, :, :, None]).reshape(depth, 2, 1, 2 * M2_HEADS)
    a_dir = jnp.pad(a_dir, ((0, 0), (0, 0), (0, 0), (0, lane_pad)))
    head_of_ch = jnp.arange(gw) // M2_HEAD_DIM
    expand = (jnp.arange(LANE)[None, :, None] == (jnp.arange(2)[:, None, None] * M2_HEADS + head_of_ch[None, None, :])
              ).astype(BF16)
    ssd_params = (m2_conv_w.astype(F32), rows(m2_conv_b),
                  jnp.pad(m2_dt_bias.astype(F32).reshape(depth, 1, 2 * M2_HEADS), ((0, 0), (0, 0), (0, lane_pad))),
                  a_dir, rows(jnp.repeat(m2_d.astype(F32), M2_HEAD_DIM, axis=1)), expand)
    m2_nw = rows(m2_norm_w)
    ln1 = rows(ln1_g), rows(ln1_b)
    ln2 = rows(ln2_g), rows(ln2_b)
    ffn_cb = rows(ffn_conv_b)

    h_lat, h_ctx = x, ctx
    for i in range(depth):
        need_ctx = i < depth - 1
        pa, pb, pd = _inproj(h_lat, h_ctx, mods, w_in_b, w_dt_b, i)
        ya = _s5(pa, s5_mats, i, nlat)
        mabc = _mixers(ya, pb, mixer_params, i, nlat)
        yf, yr = _ssd(pd, ssd_params, i, nlat)
        mix_in = (mabc, yf, yr, pd, m2_nw)
        h1_lat = _outproj(*mix_in, h_lat, mods, w_out_b, ln1[0], ln1[1], i, nlat, alpha, True)
        if need_ctx:
            h1_ctx = _outproj(*mix_in, h_ctx, mods, w_out_b, ln1[0], ln1[1], i, nlat, alpha, False)
            h_ctx = _ffn(h1_ctx, mods, w_up_b, cw9, ffn_cb, w_down_b, ln2[0], ln2[1], i, alpha, False)
        h_lat = _ffn(h1_lat, mods, w_up_b, cw9, ffn_cb, w_down_b, ln2[0], ln2[1], i, alpha, True)
    return h_lat
```

```python
import functools

import jax
import jax.numpy as jnp
from jax import lax
from jax.experimental import pallas as pl
from jax.experimental.pallas import tpu as pltpu

F32 = jnp.float32
BF16 = jnp.bfloat16

GRID_W = 64
GROUP_W = 512
S5_CH = 16
S5_Q = 16
S5_RELAYOUT_UNROLL = 4
SG_HEADS = 4
SG_CHUNK = 128
POOL_WINDOWS = (2, 4, 8, 16)
POOL_DIM = GROUP_W // len(POOL_WINDOWS)
POOL_HALO = 8
M2_HEAD_DIM = 64
M2_HEADS = GROUP_W // M2_HEAD_DIM
M2_STATE = 128
M2_GROUPS = 2
M2_CONV = 4
M2_CHUNK = 128
M2_XBC = GROUP_W + 2 * M2_GROUPS * M2_STATE
M2_HALO = 8
LN_EPS = 1e-5
RMS_EPS = 1e-5

LANE = 128
SUBLANE = 8
VMEM_LIMIT = 56 * 1024 * 1024
ROW_TILE = 256
OUT_ROW_TILE = 512
FFN_ROW_TILE = 512
FFN_COL_TILE = 512
FFN_SUB_TILE = 256
ADA_COL_TILE = 1024
MOD_ROWS = 8


def _cparams(sem):
    return pltpu.CompilerParams(dimension_semantics=sem, vmem_limit_bytes=VMEM_LIMIT)


def _bdot(a, b):
    return jnp.dot(a, b, preferred_element_type=F32)


def _gelu(x):
    return jax.nn.gelu(x)


def _silu(x):
    return x * jax.nn.sigmoid(x)


def _ada_kernel(c_ref, w_ref, b_ref, o_ref):
    c = c_ref[...]
    s = _silu(c).astype(BF16)
    o_ref[...] = _bdot(s, w_ref[...].astype(BF16)) + b_ref[...]


def _ada(cmat, w_ada, b_ada):
    depth, d, n = w_ada.shape
    tn = ADA_COL_TILE
    return pl.pallas_call(
        _ada_kernel,
        grid=(depth, n // tn),
        in_specs=[
            pl.BlockSpec((MOD_ROWS, d), lambda l, j: (0, 0)),
            pl.BlockSpec((None, d, tn), lambda l, j: (l, 0, j)),
            pl.BlockSpec((None, 1, tn), lambda l, j: (l, 0, j)),
        ],
        out_specs=pl.BlockSpec((None, MOD_ROWS, tn), lambda l, j: (l, 0, j)),
        out_shape=jax.ShapeDtypeStruct((depth, MOD_ROWS, n), F32),
        compiler_params=_cparams(("arbitrary", "arbitrary")),
        name="ada",
    )(cmat, w_ada, b_ada.reshape(depth, 1, n))


def _mod_spec(d, k, layer, nbatch, nlat_blocks):
    return pl.BlockSpec((None, None, 1, d),
                        lambda b, j, *_: (layer, jnp.where(j >= nlat_blocks, nbatch, b), 0, k))


def _layer_spec(arr, layer):
    tail = arr.shape[1:]
    return pl.BlockSpec((None,) + tail, lambda *_: (layer,) + (0,) * len(tail))


def _cast_w_in_kernel(wt_ref, wb_ref):
    wb_ref[...] = wt_ref[...].T.astype(BF16)


def _cast_w_in(w_in):
    depth, d, n = w_in.shape
    tn = ROW_TILE
    wt = jnp.swapaxes(w_in, 1, 2)
    return pl.pallas_call(
        _cast_w_in_kernel,
        grid=(depth, n // tn),
        in_specs=[pl.BlockSpec((None, tn, d), lambda l, i: (l, i, 0))],
        out_specs=pl.BlockSpec((None, d, tn), lambda l, i: (l, 0, i)),
        out_shape=jax.ShapeDtypeStruct((depth, d, n // tn * tn), BF16),
        compiler_params=_cparams(("arbitrary", "arbitrary")),
        name="cast_w_in",
    )(wt)


def _stream_specs(tm, d, nlat_blocks):
    lat = pl.BlockSpec((None, tm, d), lambda b, j: (b, jnp.minimum(j, nlat_blocks - 1), 0))
    ctx = pl.BlockSpec((None, tm, d), lambda b, j: (b, jnp.maximum(j - nlat_blocks, 0), 0))
    return lat, ctx


def _inproj_kernel(hl_ref, hc_ref, sh_ref, sc_ref, w_ref, wdt_ref, oa_ref, ob_ref, od_ref, xm_scr, *, nlat_blocks):
    j = pl.program_id(1)

    def modulated(h_ref):
        return (h_ref[...] * (1.0 + sc_ref[...]) + sh_ref[...]).astype(BF16)

    @pl.when(j < nlat_blocks)
    def _():
        xm_scr[...] = modulated(hl_ref)

    @pl.when(j >= nlat_blocks)
    def _():
        xm_scr[...] = modulated(hc_ref)

    xm = xm_scr[...]
    gw = GROUP_W
    oa_ref[...] = _bdot(xm, w_ref[:, 0:gw])
    ob_ref[...] = _bdot(xm, w_ref[:, gw:4 * gw])
    od_ref[:, 0:M2_XBC] = _bdot(xm, w_ref[:, 5 * gw:5 * gw + M2_XBC])
    od_ref[:, M2_XBC:M2_XBC + gw] = _bdot(xm, w_ref[:, 4 * gw:5 * gw])
    od_ref[:, M2_XBC + gw:M2_XBC + gw + LANE] = _bdot(xm, wdt_ref[...])


def _inproj(h_lat, h_ctx, mods, w, w_dt, layer):
    nb, nlat, d = h_lat.shape
    t = nlat + h_ctx.shape[1]
    tm = ROW_TILE
    na, nbw, nd = GROUP_W, 3 * GROUP_W, M2_XBC + GROUP_W + LANE
    lat_spec, ctx_spec = _stream_specs(tm, d, nlat // tm)

    def row(n):
        return pl.BlockSpec((None, tm, n), lambda b, j: (b, j, 0))

    return pl.pallas_call(
        functools.partial(_inproj_kernel, nlat_blocks=nlat // tm),
        grid=(nb, t // tm),
        in_specs=[lat_spec, ctx_spec, _mod_spec(d, 0, layer, nb, nlat // tm),
                  _mod_spec(d, 1, layer, nb, nlat // tm), _layer_spec(w, layer), _layer_spec(w_dt, layer)],
        out_specs=[row(na), row(nbw), row(nd)],
        out_shape=[jax.ShapeDtypeStruct((nb, t, n), F32) for n in (na, nbw, nd)],
        scratch_shapes=[pltpu.VMEM((tm, d), BF16)],
        compiler_params=_cparams(("arbitrary", "arbitrary")),
        name="inproj",
    )(h_lat, h_ctx, mods, mods, w, w_dt)


def _s5_mats(a_re, a_im, b_re, b_im, c_re, c_im, log_step, d_skip):
    q, hch = S5_Q, S5_CH
    nl, _, g, p = a_re.shape
    lre, lim = a_re.astype(F32), a_im.astype(F32)
    step = jnp.exp(log_step.astype(F32))[..., None]
    mag = jnp.exp(lre * step)
    ar, ai = mag * jnp.cos(lim * step), mag * jnp.sin(lim * step)
    den = lre * lre + lim * lim
    qr = ((ar - 1.0) * lre + ai * lim) / den
    qi = (ai * lre - (ar - 1.0) * lim) / den
    bre, bim = b_re.astype(F32), b_im.astype(F32)
    bbr = qr[..., None] * bre - qi[..., None] * bim
    bbi = qr[..., None] * bim + qi[..., None] * bre
    cre, cim = c_re.astype(F32), c_im.astype(F32)

    pr, pi = ar[:, :, :, None], ai[:, :, :, None]
    while pr.shape[3] < q:
        lr, li = pr[:, :, :, -1:], pi[:, :, :, -1:]
        pr, pi = (jnp.concatenate([pr, pr * lr - pi * li], axis=3), jnp.concatenate([pi, pr * li + pi * lr], axis=3))
    pk_re = jnp.concatenate([jnp.ones_like(pr[:, :, :, :1]), pr], axis=3)
    pk_im = jnp.concatenate([jnp.zeros_like(pi[:, :, :, :1]), pi], axis=3)
    pt_re, pt_im = jnp.swapaxes(pk_re, 3, 4), jnp.swapaxes(pk_im, 3, 4)

    def lag_kernels(d, pr, pi):
        ctr = jnp.swapaxes(cre[:, d], -1, -2)[:, :, :, None]
        cti = jnp.swapaxes(cim[:, d], -1, -2)[:, :, :, None]
        ca_re = (ctr * pr[..., None] - cti * pi[..., None]).reshape(nl, g, p, 1, q * hch)
        ca_im = (ctr * pi[..., None] + cti * pr[..., None]).reshape(nl, g, p, 1, q * hch)
        return jnp.sum(bbr[:, d][..., None] * ca_re - bbi[:, d][..., None] * ca_im, axis=2)

    lag_f = lag_kernels(0, pt_re[:, 0, :, :, :q], pt_im[:, 0, :, :, :q])
    lag_r = lag_kernels(1, jnp.flip(pt_re[:, 1, :, :, :q], axis=3), jnp.flip(pt_im[:, 1, :, :, :q], axis=3))
    dmat = jnp.eye(hch, dtype=F32) * d_skip.astype(F32).reshape(nl, g, 1, hch)
    mid = lag_r[..., (q - 1) * hch:] + lag_f[..., :hch] + dmat
    kk = jnp.concatenate([lag_r[..., :(q - 1) * hch], mid, lag_f[..., hch:],
                          jnp.zeros(mid.shape[:-1] + (hch,), F32)], axis=-1)

    def state_in(d, pr, pi):
        btr = jnp.swapaxes(bbr[:, d], -1, -2)[:, :, None]
        bti = jnp.swapaxes(bbi[:, d], -1, -2)[:, :, None]
        pr, pi = pr[:, :, :, None], pi[:, :, :, None]
        m = jnp.concatenate([pr * btr - pi * bti, pr * bti + pi * btr], axis=-1)
        return m.reshape(nl, g, q * hch, 2 * p)

    sb = jnp.concatenate([state_in(0, jnp.flip(pk_re[:, 0, :, :q], axis=2), jnp.flip(pk_im[:, 0, :, :q], axis=2)),
                          state_in(1, pk_re[:, 1, :, :q], pk_im[:, 1, :, :q])], axis=-1)

    def state_out(d, pr, pi):
        ctr = jnp.swapaxes(cre[:, d], -1, -2)[:, :, :, None]
        cti = jnp.swapaxes(cim[:, d], -1, -2)[:, :, :, None]
        pr, pi = pr[..., None], pi[..., None]
        m = jnp.concatenate([ctr * pr - cti * pi, -(ctr * pi + cti * pr)], axis=2)
        return m.reshape(nl, g, 2 * p, q * hch)

    w2 = jnp.concatenate([state_out(0, pt_re[:, 0, :, :, 1:], pt_im[:, 0, :, :, 1:]),
                          state_out(1, jnp.flip(pt_re[:, 1, :, :, 1:], axis=3), jnp.flip(pt_im[:, 1, :, :, 1:], axis=3))],
                         axis=2)

    aqr, aqi = pk_re[:, :, :, q], pk_im[:, :, :, q]
    a1 = jnp.concatenate([aqr, aqr], axis=-1).reshape(nl, 2, g * 2 * p)
    a2 = jnp.concatenate([-aqi, aqi], axis=-1).reshape(nl, 2, g * 2 * p)
    acoef = jnp.stack([a1[:, 0], a2[:, 0], a1[:, 1], a2[:, 1]], axis=1)
    return kk, sb, w2, acoef


def _granule_transpose(v):
    n = len(v)
    slot = lax.broadcasted_iota(jnp.int32, v[0].shape, 1) // S5_CH
    v = list(v)
    j = n // 2
    while j >= 1:
        keep = (slot & j) == 0
        for i in range(n):
            if i & j == 0:
                a, b = v[i], v[i + j]
                v[i] = jnp.where(keep, a, pltpu.roll(b, j * S5_CH, axis=1))
                v[i + j] = jnp.where(keep, pltpu.roll(a, LANE - j * S5_CH, axis=1), b)
        j //= 2
    return v


def _split_bf16(x):
    hi = x.astype(BF16)
    return hi, (x - hi.astype(F32)).astype(BF16)


def _dot3(xs, w):
    x_hi, x_lo = xs
    w_hi, w_lo = _split_bf16(w)
    return _bdot(x_hi, w_hi) + (_bdot(x_lo, w_hi) + _bdot(x_hi, w_lo))


def _s5_kernel(pa_ref, kk_ref, sb_ref, w2_ref, ac_ref, ya_ref, x_scr, s_scr, h_scr, t_scr, *, nbatch, nrows, nctx):
    gb = LANE // S5_CH
    halves = S5_Q // gb
    qh = S5_Q * S5_CH
    sw = LANE
    nlat = nrows - nctx
    nrc = nrows // SUBLANE
    tok_per_rc = SUBLANE * S5_Q

    def relayout(it, to_rows):
        for u in range(S5_RELAYOUT_UNROLL):
            idx = it * S5_RELAYOUT_UNROLL + u
            b = idx // nrc
            rc = idx % nrc
            r0 = pl.multiple_of(b * nrows + rc * SUBLANE, SUBLANE)
            for half in range(halves):
                lanes = slice(half * LANE, (half + 1) * LANE)

                def tok(s8, half=half, rc=rc):
                    return pl.ds(rc * tok_per_rc + half * gb + s8, SUBLANE, stride=S5_Q)

                if to_rows:
                    v = _granule_transpose([pa_ref[b, tok(s8), :] for s8 in range(gb)])
                    for gi in range(gb):
                        x_scr[gi, pl.ds(r0, SUBLANE), lanes] = v[gi]
                else:
                    v = _granule_transpose([x_scr[gi, pl.ds(r0, SUBLANE), lanes] for gi in range(gb)])
                    for s8 in range(gb):
                        ya_ref[b, tok(s8), :] = v[s8]
        return 0

    nblocks = nbatch * nrc // S5_RELAYOUT_UNROLL
    lax.fori_loop(0, nblocks, lambda it, c: relayout(it, True), 0)

    for gi in range(gb):
        kkv = kk_ref[gi]
        for s in range(S5_Q):
            shift = (S5_Q - 1 - s) * S5_CH
            t_scr[s * S5_CH:(s + 1) * S5_CH, :] = pltpu.roll(kkv, kkv.shape[1] - shift, axis=1)[:, 0:qh]
        xs = _split_bf16(x_scr[gi])
        x_scr[gi] = _dot3(xs, t_scr[...])
        rs = _dot3(xs, sb_ref[gi])
        for d in range(2):
            sd = rs[:, d * sw:(d + 1) * sw]
            s_scr[d, gi] = sd
            s_scr[2 + d, gi] = pltpu.roll(sd, sw // 2, axis=1)

    zero = jnp.zeros((nbatch, sw), F32)

    def body(s, carry):
        f_rows = pl.ds(jnp.where(s < nctx, s + nlat, s - nctx), nbatch, stride=nrows)
        r_rows = pl.ds(nrows - 1 - s, nbatch, stride=nrows)
        out = []
        for gi in range(gb):
            hf, hfs, hr, hrs = carry[4 * gi:4 * gi + 4]
            a1f, a2f, a1r, a2r = (ac_ref[k:k + 1, gi * sw:(gi + 1) * sw] for k in range(4))
            h_scr[0, gi, f_rows, :] = hf
            h_scr[1, gi, r_rows, :] = hr
            out += [a1f * hf + a2f * hfs + s_scr[0, gi, f_rows, :],
                    a1f * hfs - a2f * hf + s_scr[2, gi, f_rows, :],
                    a1r * hr + a2r * hrs + s_scr[1, gi, r_rows, :],
                    a1r * hrs - a2r * hr + s_scr[3, gi, r_rows, :]]
        return tuple(out)

    lax.fori_loop(0, nrows, body, (zero,) * (4 * gb))

    for gi in range(gb):
        hin = jnp.concatenate([h_scr[0, gi], h_scr[1, gi]], axis=1)
        x_scr[gi] = x_scr[gi] + _dot3(_split_bf16(hin), w2_ref[gi])

    lax.fori_loop(0, nblocks, lambda it, c: relayout(it, False), 0)


def _s5(pa, mats, layer, nlat):
    kk, sb, w2, acoef = mats
    nb, t, gw = pa.shape
    q = S5_Q
    nrows = t // q
    qh = q * S5_CH
    gb = LANE // S5_CH
    assert nrows % SUBLANE == 0 and q % gb == 0 and w2.shape[2] == 2 * LANE and sb.shape[3] == 2 * LANE
    assert (nb * nrows // SUBLANE) % S5_RELAYOUT_UNROLL == 0
    return pl.pallas_call(
        functools.partial(_s5_kernel, nbatch=nb, nrows=nrows, nctx=(t - nlat) // q),
        grid=(gw // LANE,),
        in_specs=[
            pl.BlockSpec((nb, t, LANE), lambda i: (0, 0, i)),
            pl.BlockSpec((None, gb, S5_CH, 2 * qh), lambda i: (layer, i, 0, 0)),
            pl.BlockSpec((None, gb, qh, 2 * LANE), lambda i: (layer, i, 0, 0)),
            pl.BlockSpec((None, gb, 2 * LANE, qh), lambda i: (layer, i, 0, 0)),
            pl.BlockSpec((None, 4, gb * LANE), lambda i: (layer, 0, i)),
        ],
        out_specs=pl.BlockSpec((nb, t, LANE), lambda i: (0, 0, i)),
        out_shape=jax.ShapeDtypeStruct((nb, t, gw), F32),
        scratch_shapes=[pltpu.VMEM((gb, nb * nrows, qh), F32),
                        pltpu.VMEM((4, gb, nb * nrows, LANE), F32),
                        pltpu.VMEM((2, gb, nb * nrows, LANE), F32),
                        pltpu.VMEM((qh, qh), F32)],
        compiler_params=_cparams(("arbitrary",)),
        name="s5",
    )(pa, kk, sb, w2, acoef)


def _mixers_kernel(ya_ref, pb_ref, prev_ref, next_ref, gluw_ref, glub_ref, lng_ref, lnb_ref, sgw_ref,
                   sgb_ref, pw_ref, pbias_ref, pscale_ref, o_ref, ext_scr, *, nlat_blocks, nblocks, nlat, nctx):
    j = pl.program_id(1)
    tm = ya_ref.shape[0]
    gw = GROUP_W

    z = _gelu(ya_ref[...])
    gate = jax.nn.sigmoid(_bdot(z.astype(BF16), gluw_ref[...]) + glub_ref[...])
    o_ref[:, 0:gw] = z * gate

    hd = gw // SG_HEADS
    u = _gelu(pb_ref[:, 0:gw])
    v = _gelu(pb_ref[:, gw:2 * gw])
    for hh in range(SG_HEADS):
        vh = v[:, hh * hd:(hh + 1) * hd]
        mu = jnp.mean(vh, axis=-1, keepdims=True)
        var = jnp.mean(jnp.square(vh - mu), axis=-1, keepdims=True)
        vn = ((vh - mu) * lax.rsqrt(var + LN_EPS) * lng_ref[:, hh * hd:(hh + 1) * hd]
              + lnb_ref[:, hh * hd:(hh + 1) * hd]).astype(BF16)
        for cchunk in range(tm // SG_CHUNK):
            rows = slice(cchunk * SG_CHUNK, (cchunk + 1) * SG_CHUNK)
            s = _bdot(sgw_ref[hh], vn[rows]) + sgb_ref[:, hh * hd:(hh + 1) * hd]
            o_ref[rows, gw + hh * hd:gw + (hh + 1) * hd] = u[rows, hh * hd:(hh + 1) * hd] * s

    is_first = jnp.logical_or(j == 0, j == nlat_blocks)
    is_last = jnp.logical_or(j == nlat_blocks - 1, j == nblocks - 1)
    p = pb_ref[:, 2 * gw:3 * gw]
    hal = POOL_HALO
    ext_scr[0:hal, :] = jnp.where(is_first, 0.0, prev_ref[...])
    ext_scr[hal:hal + tm, :] = p
    ext_scr[hal + tm:hal + tm + hal, :] = jnp.where(is_last, 0.0, next_ref[...])
    in_ctx = j >= nlat_blocks
    seq_len = jnp.where(in_ctx, nctx, nlat)
    t0 = (j - jnp.where(in_ctx, nlat_blocks, 0)) * tm
    tpos = t0 + lax.broadcasted_iota(jnp.int32, (tm, 1), 0)
    pd = POOL_DIM
    for gi, win in enumerate(POOL_WINDOWS):
        cols = slice(gi * pd, (gi + 1) * pd)
        tot = jnp.zeros((tm, pd), F32)
        for off in range(-(win // 2), win // 2):
            tot = tot + ext_scr[hal + off:hal + off + tm, cols]
        lo = jnp.maximum(tpos - win // 2, 0)
        hi = jnp.minimum(tpos + win // 2 - 1, seq_len - 1)
        mean = tot / (hi - lo + 1).astype(F32)
        yv = _bdot((mean - p[:, cols]).astype(BF16), pw_ref[gi]) + pbias_ref[:, cols]
        o_ref[:, 2 * gw + gi * pd:2 * gw + (gi + 1) * pd] = yv * pscale_ref[:, cols]


def _mixers(ya, pb, params, layer, nlat):
    nb, t, gw = ya.shape
    tm = ROW_TILE
    nblocks = t // tm
    nlat_blocks = nlat // tm
    hb = tm // POOL_HALO
    nh = t // POOL_HALO

    return pl.pallas_call(
        functools.partial(_mixers_kernel, nlat_blocks=nlat_blocks, nblocks=nblocks, nlat=nlat, nctx=t - nlat),
        grid=(nb, nblocks),
        in_specs=[
            pl.BlockSpec((None, tm, gw), lambda b, j: (b, j, 0)),
            pl.BlockSpec((None, tm, 3 * gw), lambda b, j: (b, j, 0)),
            pl.BlockSpec((None, POOL_HALO, gw), lambda b, j: (b, jnp.maximum(j * hb - 1, 0), 2)),
            pl.BlockSpec((None, POOL_HALO, gw), lambda b, j: (b, jnp.minimum((j + 1) * hb, nh - 1), 2)),
        ] + [_layer_spec(v, layer) for v in params],
        out_specs=pl.BlockSpec((None, tm, 3 * gw), lambda b, j: (b, j, 0)),
        out_shape=jax.ShapeDtypeStruct((nb, t, 3 * gw), F32),
        scratch_shapes=[pltpu.VMEM((tm + 2 * POOL_HALO, gw), F32)],
        compiler_params=_cparams(("arbitrary", "arbitrary")),
        name="mixers",
    )(ya, pb, pb, pb, *params)


def _ssd_prep_kernel(pd_ref, prev_ref, next_ref, cw_ref, cb_ref, dtb_ref, xc_ref, dt_ref, ext_scr,
                     *, nlat_blocks, nblocks):
    j = pl.program_id(1)
    tm = pd_ref.shape[0]
    nx = M2_XBC
    hal = M2_HALO
    is_first = jnp.logical_or(j == 0, j == nlat_blocks)
    is_last = jnp.logical_or(j == nlat_blocks - 1, j == nblocks - 1)
    ext_scr[0:hal, :] = jnp.where(is_first, 0.0, prev_ref[...])
    ext_scr[hal:hal + tm, :] = pd_ref[:, 0:nx]
    ext_scr[hal + tm:hal + tm + hal, :] = jnp.where(is_last, 0.0, next_ref[...])
    acc = jnp.zeros((tm, nx), F32) + cb_ref[...]
    for tap in range(M2_CONV):
        off = hal + tap - M2_CONV // 2
        acc = acc + ext_scr[off:off + tm, :] * cw_ref[tap:tap + 1, :]
    xc_ref[...] = _silu(acc)
    xdt = pd_ref[:, nx + GROUP_W:nx + GROUP_W + LANE] + dtb_ref[...]
    dt_ref[...] = jnp.maximum(xdt, 0.0) + jnp.log1p(jnp.exp(-jnp.abs(xdt)))


def _split3_bf16(x):
    x1 = x.astype(BF16)
    r1 = x - x1.astype(F32)
    x2 = r1.astype(BF16)
    return x1, x2, (r1 - x2.astype(F32)).astype(BF16)


def _ssd_direction(reverse, xc_ref, dt_ref, a_ref, dsk_ref, e_ref, st_scr, o_ref):
    qn = M2_CHUNK
    gw = GROUP_W
    lane0 = M2_HEADS if reverse else 0
    xs = xc_ref[:, 0:gw]
    nbc = M2_GROUPS * M2_STATE
    bm = xc_ref[:, gw:gw + nbc]
    cm = xc_ref[:, gw + nbc:gw + 2 * nbc]

    dt = dt_ref[...]
    da = dt * a_ref[...]
    ri = lax.broadcasted_iota(jnp.int32, (qn, qn), 0)
    ci = lax.broadcasted_iota(jnp.int32, (qn, qn), 1)
    causal = (ri <= ci) if reverse else (ri >= ci)
    mask = causal.astype(BF16)
    d1, d2, d3 = _split3_bf16(da)
    a_col = _bdot(mask, d1) + (_bdot(mask, d2) + _bdot(mask, d3))
    a_row = a_col.T
    tot_row = a_col[0:1, :] if reverse else a_col[qn - 1:qn, :]

    per_head = jnp.concatenate([dt, jnp.exp(a_col), jnp.exp(tot_row - a_col)], axis=0)
    p1, p2 = _split_bf16(per_head)
    e = e_ref[...]
    per_ch = _bdot(p1, e) + _bdot(p2, e)
    dt_x, ea_x, de_x = per_ch[0:qn], per_ch[qn:2 * qn], per_ch[2 * qn:3 * qn]
    et_x = ea_x[0:1, :] if reverse else ea_x[qn - 1:qn, :]
    xd = xs * dt_x
    xdw = (xd * de_x).astype(BF16)
    xdb = xd.astype(BF16)

    hpg = M2_HEADS // M2_GROUPS
    gcols = hpg * M2_HEAD_DIM
    for g in range(M2_GROUPS):
        bg = bm[:, g * M2_STATE:(g + 1) * M2_STATE]
        cg = cm[:, g * M2_STATE:(g + 1) * M2_STATE].astype(BF16)
        bgt = bg.T.astype(BF16)
        scores = _bdot(cg, bgt)
        st_old = st_scr[:, g * gcols:(g + 1) * gcols]
        y_inter = _bdot(cg, st_old.astype(BF16)) * ea_x[:, g * gcols:(g + 1) * gcols]
        st_new = _bdot(bgt, xdw[:, g * gcols:(g + 1) * gcols])
        st_scr[:, g * gcols:(g + 1) * gcols] = et_x[:, g * gcols:(g + 1) * gcols] * st_old + st_new
        for hl in range(hpg):
            hh = g * hpg + hl
            seg = a_col[:, lane0 + hh:lane0 + hh + 1] - a_row[lane0 + hh:lane0 + hh + 1, :]
            decay = jnp.exp(jnp.where(causal, seg, -jnp.inf))
            cols = slice(hh * M2_HEAD_DIM, (hh + 1) * M2_HEAD_DIM)
            y_h = _bdot((scores * decay).astype(BF16), xdb[:, cols])
            y_h = y_h + y_inter[:, hl * M2_HEAD_DIM:(hl + 1) * M2_HEAD_DIM]
            if reverse:
                o_ref[:, cols] = y_h
            else:
                o_ref[:, cols] = y_h + dsk_ref[:, cols] * xs[:, cols]


def _ssd_kernel(xcf_ref, dtf_ref, xcr_ref, dtr_ref, af_ref, ar_ref, dsk_ref, ef_ref, er_ref,
                yf_ref, yr_ref, stf_scr, str_scr):
    @pl.when(pl.program_id(1) == 0)
    def _():
        stf_scr[...] = jnp.zeros_like(stf_scr)
        str_scr[...] = jnp.zeros_like(str_scr)

    _ssd_direction(False, xcf_ref, dtf_ref, af_ref, dsk_ref, ef_ref, stf_scr, yf_ref)
    _ssd_direction(True, xcr_ref, dtr_ref, ar_ref, dsk_ref, er_ref, str_scr, yr_ref)


def _ssd(pd, params, layer, nlat):
    conv_w, conv_b, dtb, a_dir, dsk, expand = params
    nb, t, wd = pd.shape
    gw = GROUP_W
    tm = ROW_TILE
    hb = tm // M2_HALO
    nh = t // M2_HALO
    xc, dtp = pl.pallas_call(
        functools.partial(_ssd_prep_kernel, nlat_blocks=nlat // tm, nblocks=t // tm),
        grid=(nb, t // tm),
        in_specs=[
            pl.BlockSpec((None, tm, wd), lambda b, j: (b, j, 0)),
            pl.BlockSpec((None, M2_HALO, M2_XBC), lambda b, j: (b, jnp.maximum(j * hb - 1, 0), 0)),
            pl.BlockSpec((None, M2_HALO, M2_XBC), lambda b, j: (b, jnp.minimum((j + 1) * hb, nh - 1), 0)),
            _layer_spec(conv_w, layer), _layer_spec(conv_b, layer), _layer_spec(dtb, layer),
        ],
        out_specs=[pl.BlockSpec((None, tm, M2_XBC), lambda b, j: (b, j, 0)),
                   pl.BlockSpec((None, tm, LANE), lambda b, j: (b, j, 0))],
        out_shape=[jax.ShapeDtypeStruct((nb, t, M2_XBC), F32), jax.ShapeDtypeStruct((nb, t, LANE), F32)],
        scratch_shapes=[pltpu.VMEM((tm + 2 * M2_HALO, M2_XBC), F32)],
        compiler_params=_cparams(("arbitrary", "arbitrary")),
        name="ssd_prep",
    )(pd, pd, pd, conv_w, conv_b, dtb)

    qn = M2_CHUNK
    nchunks = t // qn
    nctx_chunks = nchunks - nlat // qn

    def fwd_chunk(k):
        return jnp.where(k < nctx_chunks, k + (nchunks - nctx_chunks), k - nctx_chunks)

    def rev_chunk(k):
        return nchunks - 1 - k

    def rows(n, chunk_of):
        return pl.BlockSpec((None, qn, n), lambda b, k: (b, chunk_of(k), 0))

    def direction(d):
        return pl.BlockSpec((None, None, 1, LANE), lambda b, k: (layer, d, 0, 0))

    return pl.pallas_call(
        _ssd_kernel,
        grid=(nb, nchunks),
        in_specs=[rows(M2_XBC, fwd_chunk), rows(LANE, fwd_chunk), rows(M2_XBC, rev_chunk), rows(LANE, rev_chunk),
                  direction(0), direction(1), _layer_spec(dsk, layer),
                  pl.BlockSpec((None, LANE, gw), lambda b, k: (0, 0, 0)),
                  pl.BlockSpec((None, LANE, gw), lambda b, k: (1, 0, 0))],
        out_specs=[rows(gw, fwd_chunk), rows(gw, rev_chunk)],
        out_shape=[jax.ShapeDtypeStruct((nb, t, gw), F32), jax.ShapeDtypeStruct((nb, t, gw), F32)],
        scratch_shapes=[pltpu.VMEM((M2_STATE, gw), F32), pltpu.VMEM((M2_STATE, gw), F32)],
        compiler_params=_cparams(("arbitrary", "arbitrary")),
        name="ssd_scan",
    )(xc, dtp, xc, dtp, a_dir, a_dir, dsk, expand, expand)


def _layer_norm_rows(v, g, b):
    mu = jnp.mean(v, axis=-1, keepdims=True)
    var = jnp.mean(jnp.square(v - mu), axis=-1, keepdims=True)
    return (v - mu) * lax.rsqrt(var + LN_EPS) * g + b


def _outproj_kernel(mabc_ref, yf_ref, yr_ref, z_ref, nw_ref, h_ref, gate_ref, w_ref, lg_ref, lb_ref, o_ref,
                    *, alpha):
    k1 = mabc_ref.shape[1]
    gc = GROUP_W // M2_GROUPS
    for r0 in range(0, o_ref.shape[0], ROW_TILE):
        rows = slice(r0, r0 + ROW_TILE)
        gv = (yf_ref[rows, :] + yr_ref[rows, :]) * _silu(z_ref[rows, :])
        md = []
        for g in range(M2_GROUPS):
            part = gv[:, g * gc:(g + 1) * gc]
            ms = jnp.mean(jnp.square(part), axis=-1, keepdims=True)
            md.append((part * lax.rsqrt(ms + RMS_EPS) * nw_ref[:, g * gc:(g + 1) * gc]).astype(BF16))
        mix = (_bdot(mabc_ref[rows, :].astype(BF16), w_ref[0:k1, :])
               + _bdot(jnp.concatenate(md, axis=1), w_ref[k1:w_ref.shape[0], :]))
        o_ref[rows, :] = _layer_norm_rows(alpha * h_ref[rows, :] + gate_ref[...] * mix, lg_ref[...], lb_ref[...])


def _outproj(mabc, yf, yr, pd, norm_w, h, mods, w, ln_g, ln_b, layer, nlat, alpha, latent):
    nb, rows, d = h.shape
    tm = OUT_ROW_TILE if latent else rows
    blk0 = 0 if latent else nlat // tm
    k1, k2 = mabc.shape[2], yf.shape[2]
    z_block = M2_XBC // k2

    def row(n):
        return pl.BlockSpec((None, tm, n), lambda b, j: (b, j + blk0, 0))

    return pl.pallas_call(
        functools.partial(_outproj_kernel, alpha=alpha),
        grid=(nb, rows // tm),
        in_specs=[row(k1), row(k2), row(k2), pl.BlockSpec((None, tm, k2), lambda b, j: (b, j + blk0, z_block)),
                  _layer_spec(norm_w, layer), pl.BlockSpec((None, tm, d), lambda b, j: (b, j, 0)),
                  pl.BlockSpec((None, None, 1, d), lambda b, j: (layer, b if latent else nb, 0, 2)),
                  _layer_spec(w, layer), _layer_spec(ln_g, layer), _layer_spec(ln_b, layer)],
        out_specs=pl.BlockSpec((None, tm, d), lambda b, j: (b, j, 0)),
        out_shape=jax.ShapeDtypeStruct((nb, rows, d), F32),
        compiler_params=_cparams(("arbitrary", "arbitrary")),
        name="outproj_lat" if latent else "outproj_ctx",
    )(mabc, yf, yr, pd, norm_w, h, mods, w, ln_g, ln_b)


def _ffn_kernel(*refs, alpha, grid_conv, halo):
    if grid_conv:
        (h_ref, next_ref, sh_ref, sc_ref, gate_ref, wg_ref, wv_ref, cw_ref, cb_ref, wd_ref,
         lg_ref, lb_ref, o_ref, xm_scr, acc_scr, tail_scr, *chain_scr) = refs
    else:
        (h_ref, sh_ref, sc_ref, gate_ref, wg_ref, wv_ref, cw_ref, cb_ref, wd_ref,
         lg_ref, lb_ref, o_ref, xm_scr, acc_scr, tail_scr, *chain_scr) = refs
    g_scr, v_scr, side_scr = chain_scr[0::3], chain_scr[1::3], chain_scr[2::3]
    j = pl.program_id(1)
    f = pl.program_id(2)
    nj = pl.num_programs(1)
    nf = pl.num_programs(2)
    tm = h_ref.shape[0]
    sub = g_scr[0].shape[1]
    nsub = len(g_scr)
    pad = SUBLANE

    @pl.when(f == 0)
    def _():
        def modulated(v):
            return (v * (1.0 + sc_ref[...]) + sh_ref[...]).astype(BF16)
        xm_scr[0:tm, :] = modulated(h_ref[...])
        if grid_conv:
            xm_scr[tm:tm + halo, :] = modulated(next_ref[...])
        acc_scr[...] = jnp.zeros_like(acc_scr)
        for side in side_scr:
            side[:, 0:pad, :] = jnp.zeros((2, pad, sub), F32)
            side[:, pad + tm:pad + tm + pad, :] = jnp.zeros((2, pad, sub), F32)

    if grid_conv:
        @pl.when(j == 0)
        def _():
            tail_scr[f] = jnp.zeros(tail_scr.shape[1:], F32)

    pos = lax.broadcasted_iota(jnp.int32, (tm, 1), 0)
    if grid_conv:
        keep_next = (j < nj - 1).astype(F32)
        pos = pos % GRID_W
        left_ok, right_ok = pos != 0, pos != GRID_W - 1
        row_taps = (-1, 0, 1)
    else:
        left_ok, right_ok = pos != 0, pos != tm - 1
        row_taps = (0,)

    for si in range(nsub):
        cs = slice(si * sub, (si + 1) * sub)
        ge = _bdot(xm_scr[...], wg_ref[:, cs])
        if grid_conv:
            g_scr[si][0:halo, :] = tail_scr[f, :, cs]
            g_scr[si][halo:halo + tm, :] = ge[0:tm]
            g_scr[si][halo + tm:halo + tm + halo, :] = ge[tm:tm + halo] * keep_next
            tail_scr[f, :, cs] = ge[tm - halo:tm]
        else:
            g_scr[si][...] = ge
        v_scr[si][...] = _bdot(xm_scr[0:tm, :], wv_ref[:, cs])

    acts = []
    for si in range(nsub):
        cs = slice(si * sub, (si + 1) * sub)

        def tap_sum(dc):
            tot = None
            for dr in row_taps:
                k = (dr + 1) * 3 + (dc + 1)
                r0 = halo + dr * GRID_W
                term = g_scr[si][r0:r0 + tm, :] * cw_ref[k:k + 1, cs]
                tot = term if tot is None else tot + term
            return tot

        side_scr[si][0, pad:pad + tm, :] = tap_sum(-1)
        side_scr[si][1, pad:pad + tm, :] = tap_sum(1)
        conv = tap_sum(0) + cb_ref[:, cs]
        conv = conv + jnp.where(left_ok, side_scr[si][0, pad - 1:pad - 1 + tm, :], 0.0)
        conv = conv + jnp.where(right_ok, side_scr[si][1, pad + 1:pad + 1 + tm, :], 0.0)
        acts.append((_gelu(conv) * v_scr[si][...]).astype(BF16))
    for si in range(nsub):
        acc_scr[...] += _bdot(acts[si], wd_ref[si * sub:(si + 1) * sub, :])

    @pl.when(f == nf - 1)
    def _():
        o_ref[...] = _layer_norm_rows(alpha * h_ref[...] + gate_ref[...] * acc_scr[...], lg_ref[...], lb_ref[...])


def _ffn(h, mods, w_up, conv_w9, conv_b, w_down, ln_g, ln_b, layer, alpha, grid_conv):
    nb, rows, d = h.shape
    fh = w_down.shape[1]
    tf = FFN_COL_TILE
    sub = FFN_SUB_TILE
    nf = fh // tf
    if grid_conv:
        tm, halo = FFN_ROW_TILE, GRID_W
    else:
        tm, halo = rows, 0
    nblk = rows // tm
    hb = tm // GRID_W
    nhalo_blocks = rows // GRID_W

    def vec(k):
        return pl.BlockSpec((None, None, 1, d), lambda b, j, f: (layer, b if grid_conv else nb, 0, k))

    in_specs = [pl.BlockSpec((None, tm, d), lambda b, j, f: (b, j, 0))]
    args = [h]
    if grid_conv:
        in_specs += [
            pl.BlockSpec((None, halo, d), lambda b, j, f: (b, jnp.minimum((j + 1) * hb, nhalo_blocks - 1), 0)),
        ]
        args += [h]
    in_specs += [
        vec(3), vec(4), vec(5),
        pl.BlockSpec((None, d, tf), lambda b, j, f: (layer, 0, f)),
        pl.BlockSpec((None, d, tf), lambda b, j, f: (layer, 0, nf + f)),
        pl.BlockSpec((None, 9, tf), lambda b, j, f: (layer, 0, f)),
        pl.BlockSpec((None, 1, tf), lambda b, j, f: (layer, 0, f)),
        pl.BlockSpec((None, tf, d), lambda b, j, f: (layer, f, 0)),
        _layer_spec(ln_g, layer), _layer_spec(ln_b, layer),
    ]
    args += [mods, mods, mods, w_up, w_up, conv_w9, conv_b, w_down, ln_g, ln_b]
    return pl.pallas_call(
        functools.partial(_ffn_kernel, alpha=alpha, grid_conv=grid_conv, halo=halo),
        grid=(nb, nblk, nf),
        in_specs=in_specs,
        out_specs=pl.BlockSpec((None, tm, d), lambda b, j, f: (b, j, 0)),
        out_shape=jax.ShapeDtypeStruct((nb, nblk * tm, d), F32),
        scratch_shapes=[pltpu.VMEM((tm + halo, d), BF16), pltpu.VMEM((tm, d), F32),
                        pltpu.VMEM((nf, max(halo, SUBLANE), tf), F32)]
        + [pltpu.VMEM((tm + 2 * halo, sub), F32), pltpu.VMEM((tm, sub), F32),
           pltpu.VMEM((2, tm + 2 * SUBLANE, sub), F32)] * (tf // sub),
        compiler_params=_cparams(("arbitrary", "arbitrary", "arbitrary")),
        name="ffn_lat" if grid_conv else "ffn_ctx",
    )(*args)


def kernel(x, c, ctx, c_ctx, w_ada, b_ada, w_in, w_out, ln1_g, ln1_b, ln2_g, ln2_b, s5_a_re, s5_a_im, s5_b_re, s5_b_im, s5_c_re, s5_c_im, s5_log_step, s5_d, s5_glu_w, s5_glu_b, sg_ln_g, sg_ln_b, sg_w, sg_b, pool_w, pool_b, pool_scale, m2_conv_w, m2_conv_b, m2_dt_bias, m2_a_log, m2_d, m2_norm_w, ffn_w_up, ffn_conv_w, ffn_conv_b, ffn_w_down):
    nb, nlat, d = x.shape
    nctx = ctx.shape[1]
    depth = w_ada.shape[0]
    gw = GROUP_W
    alpha = (2 * depth) ** 0.25
    assert nb + 1 <= MOD_ROWS and nlat % FFN_ROW_TILE == 0 and nlat % OUT_ROW_TILE == 0 and nctx % ROW_TILE == 0
    assert nlat % (GRID_W * SUBLANE) == 0 and FFN_ROW_TILE % GRID_W == 0
    assert w_in.shape[2] == 5 * gw + M2_XBC + 2 * M2_HEADS and (5 * gw + M2_XBC) % ROW_TILE == 0

    cmat = jnp.concatenate([c, c_ctx[None, :], jnp.zeros((MOD_ROWS - nb - 1, d), F32)], axis=0)
    mods = _ada(cmat, w_ada, b_ada).reshape(depth, MOD_ROWS, 1, 6 * d)

    def rows(v):
        return v.astype(F32).reshape(depth, 1, -1)

    fh = ffn_w_down.shape[1]
    lane_pad = LANE - 2 * M2_HEADS
    w_in_b = _cast_w_in(w_in)
    w_dt_b = jnp.pad(w_in[:, :, w_in.shape[2] - 2 * M2_HEADS:], ((0, 0), (0, 0), (0, lane_pad))).astype(BF16)
    w_out_b = w_out.astype(BF16)
    w_up_b = ffn_w_up.astype(BF16)
    w_down_b = ffn_w_down.astype(BF16)
    cw9 = ffn_conv_w.astype(F32).reshape(depth, 9, fh)
    s5_mats = _s5_mats(s5_a_re, s5_a_im, s5_b_re, s5_b_im, s5_c_re, s5_c_im, s5_log_step, s5_d)
    sg_btile = jnp.repeat(jnp.swapaxes(sg_b.astype(F32), 1, 2), gw // SG_HEADS, axis=2)
    mixer_params = (s5_glu_w.astype(BF16), rows(s5_glu_b), rows(sg_ln_g), rows(sg_ln_b), sg_w.astype(BF16),
                    sg_btile, pool_w.astype(BF16), rows(pool_b), rows(pool_scale))
    m2_a = -jnp.exp(m2_a_log.astype(F32))
    a_dir = (m2_a[:, :, None, :] * jnp.eye(2, dtype=F32)[None, :, :, None]).reshape(depth, 2, 1, 2 * M2_HEADS)
    a_dir = jnp.pad(a_dir, ((0, 0), (0, 0), (0, 0), (0, lane_pad)))
    head_of_ch = jnp.arange(gw) // M2_HEAD_DIM
    expand = (jnp.arange(LANE)[None, :, None] == (jnp.arange(2)[:, None, None] * M2_HEADS + head_of_ch[None, None, :])
              ).astype(BF16)
    ssd_params = (m2_conv_w.astype(F32), rows(m2_conv_b),
                  jnp.pad(m2_dt_bias.astype(F32).reshape(depth, 1, 2 * M2_HEADS), ((0, 0), (0, 0), (0, lane_pad))),
                  a_dir, rows(jnp.repeat(m2_d.astype(F32), M2_HEAD_DIM, axis=1)), expand)
    m2_nw = rows(m2_norm_w)
    ln1 = rows(ln1_g), rows(ln1_b)
    ln2 = rows(ln2_g), rows(ln2_b)
    ffn_cb = rows(ffn_conv_b)

    h_lat, h_ctx = x, ctx
    for i in range(depth):
        need_ctx = i < depth - 1
        pa, pb, pd = _inproj(h_lat, h_ctx, mods, w_in_b, w_dt_b, i)
        ya = _s5(pa, s5_mats, i, nlat)
        mabc = _mixers(ya, pb, mixer_params, i, nlat)
        yf, yr = _ssd(pd, ssd_params, i, nlat)
        mix_in = (mabc, yf, yr, pd, m2_nw)
        h1_lat = _outproj(*mix_in, h_lat, mods, w_out_b, ln1[0], ln1[1], i, nlat, alpha, True)
        if need_ctx:
            h1_ctx = _outproj(*mix_in, h_ctx, mods, w_out_b, ln1[0], ln1[1], i, nlat, alpha, False)
            h_ctx = _ffn(h1_ctx, mods, w_up_b, cw9, ffn_cb, w_down_b, ln2[0], ln2[1], i, alpha, False)
        h_lat = _ffn(h1_lat, mods, w_up_b, cw9, ffn_cb, w_down_b, ln2[0], ln2[1], i, alpha, True)
    return h_lat
```

```python
import functools

import jax
import jax.numpy as jnp
from jax import lax
from jax.experimental import pallas as pl
from jax.experimental.pallas import tpu as pltpu

F32 = jnp.float32
BF16 = jnp.bfloat16

GRID_W = 64
GROUP_W = 512
S5_CH = 16
S5_Q = 16
S5_RELAYOUT_UNROLL = 4
SG_HEADS = 4
SG_CHUNK = 128
POOL_WINDOWS = (2, 4, 8, 16)
POOL_DIM = GROUP_W // len(POOL_WINDOWS)
POOL_HALO = 8
M2_HEAD_DIM = 64
M2_HEADS = GROUP_W // M2_HEAD_DIM
M2_STATE = 128
M2_GROUPS = 2
M2_CONV = 4
M2_CHUNK = 128
M2_XBC = GROUP_W + 2 * M2_GROUPS * M2_STATE
M2_HALO = 8
LN_EPS = 1e-5
RMS_EPS = 1e-5

LANE = 128
SUBLANE = 8
VMEM_LIMIT = 56 * 1024 * 1024
ROW_TILE = 256
OUT_ROW_TILE = 512
FFN_ROW_TILE = 512
FFN_COL_TILE = 512
FFN_SUB_TILE = 256
ADA_COL_TILE = 1024
MOD_ROWS = 8


def _cparams(sem):
    return pltpu.CompilerParams(dimension_semantics=sem, vmem_limit_bytes=VMEM_LIMIT)


def _bdot(a, b):
    return jnp.dot(a, b, preferred_element_type=F32)


def _gelu(x):
    return jax.nn.gelu(x)


def _silu(x):
    return x * jax.nn.sigmoid(x)


def _ada_kernel(c_ref, w_ref, b_ref, o_ref):
    c = c_ref[...]
    s = _silu(c).astype(BF16)
    o_ref[...] = _bdot(s, w_ref[...].astype(BF16)) + b_ref[...]


def _ada(cmat, w_ada, b_ada):
    depth, d, n = w_ada.shape
    tn = ADA_COL_TILE
    return pl.pallas_call(
        _ada_kernel,
        grid=(depth, n // tn),
        in_specs=[
            pl.BlockSpec((MOD_ROWS, d), lambda l, j: (0, 0)),
            pl.BlockSpec((None, d, tn), lambda l, j: (l, 0, j)),
            pl.BlockSpec((None, 1, tn), lambda l, j: (l, 0, j)),
        ],
        out_specs=pl.BlockSpec((None, MOD_ROWS, tn), lambda l, j: (l, 0, j)),
        out_shape=jax.ShapeDtypeStruct((depth, MOD_ROWS, n), F32),
        compiler_params=_cparams(("arbitrary", "arbitrary")),
        name="ada",
    )(cmat, w_ada, b_ada.reshape(depth, 1, n))


def _mod_spec(d, k, layer, nbatch, nlat_blocks):
    return pl.BlockSpec((None, None, 1, d),
                        lambda b, j, *_: (layer, jnp.where(j >= nlat_blocks, nbatch, b), 0, k))


def _layer_spec(arr, layer):
    tail = arr.shape[1:]
    return pl.BlockSpec((None,) + tail, lambda *_: (layer,) + (0,) * len(tail))


def _cast_w_in_kernel(wt_ref, wb_ref):
    wb_ref[...] = wt_ref[...].T.astype(BF16)


def _cast_w_in(w_in):
    depth, d, n = w_in.shape
    tn = ROW_TILE
    wt = jnp.swapaxes(w_in, 1, 2)
    return pl.pallas_call(
        _cast_w_in_kernel,
        grid=(depth, n // tn),
        in_specs=[pl.BlockSpec((None, tn, d), lambda l, i: (l, i, 0))],
        out_specs=pl.BlockSpec((None, d, tn), lambda l, i: (l, 0, i)),
        out_shape=jax.ShapeDtypeStruct((depth, d, n // tn * tn), BF16),
        compiler_params=_cparams(("arbitrary", "arbitrary")),
        name="cast_w_in",
    )(wt)


def _stream_specs(tm, d, nlat_blocks):
    lat = pl.BlockSpec((None, tm, d), lambda b, j: (b, jnp.minimum(j, nlat_blocks - 1), 0))
    ctx = pl.BlockSpec((None, tm, d), lambda b, j: (b, jnp.maximum(j - nlat_blocks, 0), 0))
    return lat, ctx


def _inproj_kernel(hl_ref, hc_ref, sh_ref, sc_ref, w_ref, wdt_ref, oa_ref, ob_ref, od_ref, xm_scr, *, nlat_blocks):
    j = pl.program_id(1)

    def modulated(h_ref):
        return (h_ref[...] * (1.0 + sc_ref[...]) + sh_ref[...]).astype(BF16)

    @pl.when(j < nlat_blocks)
    def _():
        xm_scr[...] = modulated(hl_ref)

    @pl.when(j >= nlat_blocks)
    def _():
        xm_scr[...] = modulated(hc_ref)

    xm = xm_scr[...]
    gw = GROUP_W
    oa_ref[...] = _bdot(xm, w_ref[:, 0:gw])
    ob_ref[...] = _bdot(xm, w_ref[:, gw:4 * gw])
    od_ref[:, 0:M2_XBC] = _bdot(xm, w_ref[:, 5 * gw:5 * gw + M2_XBC])
    od_ref[:, M2_XBC:M2_XBC + gw] = _bdot(xm, w_ref[:, 4 * gw:5 * gw])
    od_ref[:, M2_XBC + gw:M2_XBC + gw + LANE] = _bdot(xm, wdt_ref[...])


def _inproj(h_lat, h_ctx, mods, w, w_dt, layer):
    nb, nlat, d = h_lat.shape
    t = nlat + h_ctx.shape[1]
    tm = ROW_TILE
    na, nbw, nd = GROUP_W, 3 * GROUP_W, M2_XBC + GROUP_W + LANE
    lat_spec, ctx_spec = _stream_specs(tm, d, nlat // tm)

    def row(n):
        return pl.BlockSpec((None, tm, n), lambda b, j: (b, j, 0))

    return pl.pallas_call(
        functools.partial(_inproj_kernel, nlat_blocks=nlat // tm),
        grid=(nb, t // tm),
        in_specs=[lat_spec, ctx_spec, _mod_spec(d, 0, layer, nb, nlat // tm),
                  _mod_spec(d, 1, layer, nb, nlat // tm), _layer_spec(w, layer), _layer_spec(w_dt, layer)],
        out_specs=[row(na), row(nbw), row(nd)],
        out_shape=[jax.ShapeDtypeStruct((nb, t, n), F32) for n in (na, nbw, nd)],
        scratch_shapes=[pltpu.VMEM((tm, d), BF16)],
        compiler_params=_cparams(("arbitrary", "arbitrary")),
        name="inproj",
    )(h_lat, h_ctx, mods, mods, w, w_dt)


def _s5_mats(a_re, a_im, b_re, b_im, c_re, c_im, log_step, d_skip):
    q, hch = S5_Q, S5_CH
    nl, _, g, p = a_re.shape
    lre, lim = a_re.astype(F32), a_im.astype(F32)
    step = jnp.exp(log_step.astype(F32))[..., None]
    mag = jnp.exp(lre * step)
    ar, ai = mag * jnp.cos(lim * step), mag * jnp.sin(lim * step)
    den = lre * lre + lim * lim
    qr = ((ar - 1.0) * lre + ai * lim) / den
    qi = (ai * lre - (ar - 1.0) * lim) / den
    bre, bim = b_re.astype(F32), b_im.astype(F32)
    bbr = qr[..., None] * bre - qi[..., None] * bim
    bbi = qr[..., None] * bim + qi[..., None] * bre
    cre, cim = c_re.astype(F32), c_im.astype(F32)

    pr, pi = ar[:, :, :, None], ai[:, :, :, None]
    while pr.shape[3] < q:
        lr, li = pr[:, :, :, -1:], pi[:, :, :, -1:]
        pr, pi = (jnp.concatenate([pr, pr * lr - pi * li], axis=3), jnp.concatenate([pi, pr * li + pi * lr], axis=3))
    pk_re = jnp.concatenate([jnp.ones_like(pr[:, :, :, :1]), pr], axis=3)
    pk_im = jnp.concatenate([jnp.zeros_like(pi[:, :, :, :1]), pi], axis=3)
    pt_re, pt_im = jnp.swapaxes(pk_re, 3, 4), jnp.swapaxes(pk_im, 3, 4)

    def lag_kernels(d, pr, pi):
        ctr = jnp.swapaxes(cre[:, d], -1, -2)[:, :, :, None]
        cti = jnp.swapaxes(cim[:, d], -1, -2)[:, :, :, None]
        ca_re = (ctr * pr[..., None] - cti * pi[..., None]).reshape(nl, g, p, 1, q * hch)
        ca_im = (ctr * pi[..., None] + cti * pr[..., None]).reshape(nl, g, p, 1, q * hch)
        return jnp.sum(bbr[:, d][..., None] * ca_re - bbi[:, d][..., None] * ca_im, axis=2)

    lag_f = lag_kernels(0, pt_re[:, 0, :, :, :q], pt_im[:, 0, :, :, :q])
    lag_r = lag_kernels(1, jnp.flip(pt_re[:, 1, :, :, :q], axis=3), jnp.flip(pt_im[:, 1, :, :, :q], axis=3))
    dmat = jnp.eye(hch, dtype=F32) * d_skip.astype(F32).reshape(nl, g, 1, hch)
    mid = lag_r[..., (q - 1) * hch:] + lag_f[..., :hch] + dmat
    kk = jnp.concatenate([lag_r[..., :(q - 1) * hch], mid, lag_f[..., hch:],
                          jnp.zeros(mid.shape[:-1] + (hch,), F32)], axis=-1)

    def state_in(d, pr, pi):
        btr = jnp.swapaxes(bbr[:, d], -1, -2)[:, :, None]
        bti = jnp.swapaxes(bbi[:, d], -1, -2)[:, :, None]
        pr, pi = pr[:, :, :, None], pi[:, :, :, None]
        m = jnp.concatenate([pr * btr - pi * bti, pr * bti + pi * btr], axis=-1)
        return m.reshape(nl, g, q * hch, 2 * p)

    sb = jnp.concatenate([state_in(0, jnp.flip(pk_re[:, 0, :, :q], axis=2), jnp.flip(pk_im[:, 0, :, :q], axis=2)),
                          state_in(1, pk_re[:, 1, :, :q], pk_im[:, 1, :, :q])], axis=-1)

    def state_out(d, pr, pi):
        ctr = jnp.swapaxes(cre[:, d], -1, -2)[:, :, :, None]
        cti = jnp.swapaxes(cim[:, d], -1, -2)[:, :, :, None]
        pr, pi = pr[..., None], pi[..., None]
        m = jnp.concatenate([ctr * pr - cti * pi, -(ctr * pi + cti * pr)], axis=2)
        return m.reshape(nl, g, 2 * p, q * hch)

    w2 = jnp.concatenate([state_out(0, pt_re[:, 0, :, :, 1:], pt_im[:, 0, :, :, 1:]),
                          state_out(1, jnp.flip(pt_re[:, 1, :, :, 1:], axis=3), jnp.flip(pt_im[:, 1, :, :, 1:], axis=3))],
                         axis=2)

    aqr, aqi = pk_re[:, :, :, q], pk_im[:, :, :, q]
    a1 = jnp.concatenate([aqr, aqr], axis=-1).reshape(nl, 2, g * 2 * p)
    a2 = jnp.concatenate([-aqi, aqi], axis=-1).reshape(nl, 2, g * 2 * p)
    acoef = jnp.stack([a1[:, 0], a2[:, 0], a1[:, 1], a2[:, 1]], axis=1)
    return kk, sb, w2, acoef


def _granule_transpose(v):
    n = len(v)
    slot = lax.broadcasted_iota(jnp.int32, v[0].shape, 1) // S5_CH
    v = list(v)
    j = n // 2
    while j >= 1:
        keep = (slot & j) == 0
        for i in range(n):
            if i & j == 0:
                a, b = v[i], v[i + j]
                v[i] = jnp.where(keep, a, pltpu.roll(b, j * S5_CH, axis=1))
                v[i + j] = jnp.where(keep, pltpu.roll(a, LANE - j * S5_CH, axis=1), b)
        j //= 2
    return v


def _split_bf16(x):
    hi = x.astype(BF16)
    return hi, (x - hi.astype(F32)).astype(BF16)


def _dot3(xs, w):
    x_hi, x_lo = xs
    w_hi, w_lo = _split_bf16(w)
    return _bdot(x_hi, w_hi) + (_bdot(x_lo, w_hi) + _bdot(x_hi, w_lo))


def _s5_kernel(pa_ref, kk_ref, sb_ref, w2_ref, ac_ref, ya_ref, x_scr, s_scr, h_scr, t_scr, *, nbatch, nrows, nctx):
    gb = LANE // S5_CH
    halves = S5_Q // gb
    qh = S5_Q * S5_CH
    sw = LANE
    nlat = nrows - nctx
    nrc = nrows // SUBLANE
    tok_per_rc = SUBLANE * S5_Q

    def relayout(it, to_rows):
        for u in range(S5_RELAYOUT_UNROLL):
            idx = it * S5_RELAYOUT_UNROLL + u
            b = idx // nrc
            rc = idx % nrc
            r0 = pl.multiple_of(b * nrows + rc * SUBLANE, SUBLANE)
            for half in range(halves):
                lanes = slice(half * LANE, (half + 1) * LANE)

                def tok(s8, half=half, rc=rc):
                    return pl.ds(rc * tok_per_rc + half * gb + s8, SUBLANE, stride=S5_Q)

                if to_rows:
                    v = _granule_transpose([pa_ref[b, tok(s8), :] for s8 in range(gb)])
                    for gi in range(gb):
                        x_scr[gi, pl.ds(r0, SUBLANE), lanes] = v[gi]
                else:
                    v = _granule_transpose([x_scr[gi, pl.ds(r0, SUBLANE), lanes] for gi in range(gb)])
                    for s8 in range(gb):
                        ya_ref[b, tok(s8), :] = v[s8]
        return 0

    nblocks = nbatch * nrc // S5_RELAYOUT_UNROLL
    lax.fori_loop(0, nblocks, lambda it, c: relayout(it, True), 0)

    for gi in range(gb):
        kkv = kk_ref[gi]
        for s in range(S5_Q):
            shift = (S5_Q - 1 - s) * S5_CH
            t_scr[s * S5_CH:(s + 1) * S5_CH, :] = pltpu.roll(kkv, kkv.shape[1] - shift, axis=1)[:, 0:qh]
        xs = _split_bf16(x_scr[gi])
        x_scr[gi] = _dot3(xs, t_scr[...])
        rs = _dot3(xs, sb_ref[gi])
        for d in range(2):
            sd = rs[:, d * sw:(d + 1) * sw]
            s_scr[d, gi] = sd
            s_scr[2 + d, gi] = pltpu.roll(sd, sw // 2, axis=1)

    zero = jnp.zeros((nbatch, sw), F32)

    def body(s, carry):
        f_rows = pl.ds(jnp.where(s < nctx, s + nlat, s - nctx), nbatch, stride=nrows)
        r_rows = pl.ds(nrows - 1 - s, nbatch, stride=nrows)
        out = []
        for gi in range(gb):
            hf, hfs, hr, hrs = carry[4 * gi:4 * gi + 4]
            a1f, a2f, a1r, a2r = (ac_ref[k:k + 1, gi * sw:(gi + 1) * sw] for k in range(4))
            h_scr[0, gi, f_rows, :] = hf
            h_scr[1, gi, r_rows, :] = hr
            out += [a1f * hf + a2f * hfs + s_scr[0, gi, f_rows, :],
                    a1f * hfs - a2f * hf + s_scr[2, gi, f_rows, :],
                    a1r * hr + a2r * hrs + s_scr[1, gi, r_rows, :],
                    a1r * hrs - a2r * hr + s_scr[3, gi, r_rows, :]]
        return tuple(out)

    lax.fori_loop(0, nrows, body, (zero,) * (4 * gb))

    for gi in range(gb):
        hin = jnp.concatenate([h_scr[0, gi], h_scr[1, gi]], axis=1)
        x_scr[gi] = x_scr[gi] + _dot3(_split_bf16(hin), w2_ref[gi])

    lax.fori_loop(0, nblocks, lambda it, c: relayout(it, False), 0)


def _s5(pa, mats, layer, nlat):
    kk, sb, w2, acoef = mats
    nb, t, gw = pa.shape
    q = S5_Q
    nrows = t // q
    qh = q * S5_CH
    gb = LANE // S5_CH
    assert nrows % SUBLANE == 0 and q % gb == 0 and w2.shape[2] == 2 * LANE and sb.shape[3] == 2 * LANE
    assert (nb * nrows // SUBLANE) % S5_RELAYOUT_UNROLL == 0
    return pl.pallas_call(
        functools.partial(_s5_kernel, nbatch=nb, nrows=nrows, nctx=(t - nlat) // q),
        grid=(gw // LANE,),
        in_specs=[
            pl.BlockSpec((nb, t, LANE), lambda i: (0, 0, i)),
            pl.BlockSpec((None, gb, S5_CH, 2 * qh), lambda i: (layer, i, 0, 0)),
            pl.BlockSpec((None, gb, qh, 2 * LANE), lambda i: (layer, i, 0, 0)),
            pl.BlockSpec((None, gb, 2 * LANE, qh), lambda i: (layer, i, 0, 0)),
            pl.BlockSpec((None, 4, gb * LANE), lambda i: (layer, 0, i)),
        ],
        out_specs=pl.BlockSpec((nb, t, LANE), lambda i: (0, 0, i)),
        out_shape=jax.ShapeDtypeStruct((nb, t, gw), F32),
        scratch_shapes=[pltpu.VMEM((gb, nb * nrows, qh), F32),
                        pltpu.VMEM((4, gb, nb * nrows, LANE), F32),
                        pltpu.VMEM((2, gb, nb * nrows, LANE), F32),
                        pltpu.VMEM((qh, qh), F32)],
        compiler_params=_cparams(("arbitrary",)),
        name="s5",
    )(pa, kk, sb, w2, acoef)


def _mixers_kernel(ya_ref, pb_ref, prev_ref, next_ref, gluw_ref, glub_ref, lng_ref, lnb_ref, sgw_ref,
                   sgb_ref, pw_ref, pbias_ref, pscale_ref, o_ref, ext_scr, *, nlat_blocks, nblocks, nlat, nctx):
    j = pl.program_id(1)
    tm = ya_ref.shape[0]
    gw = GROUP_W

    z = _gelu(ya_ref[...])
    gate = jax.nn.sigmoid(_bdot(z.astype(BF16), gluw_ref[...]) + glub_ref[...])
    o_ref[:, 0:gw] = z * gate

    hd = gw // SG_HEADS
    u = _gelu(pb_ref[:, 0:gw])
    v = _gelu(pb_ref[:, gw:2 * gw])
    for hh in range(SG_HEADS):
        vh = v[:, hh * hd:(hh + 1) * hd]
        mu = jnp.mean(vh, axis=-1, keepdims=True)
        var = jnp.mean(jnp.square(vh - mu), axis=-1, keepdims=True)
        vn = ((vh - mu) * lax.rsqrt(var + LN_EPS) * lng_ref[:, hh * hd:(hh + 1) * hd]
              + lnb_ref[:, hh * hd:(hh + 1) * hd]).astype(BF16)
        for cchunk in range(tm // SG_CHUNK):
            rows = slice(cchunk * SG_CHUNK, (cchunk + 1) * SG_CHUNK)
            s = _bdot(sgw_ref[hh], vn[rows]) + sgb_ref[:, hh * hd:(hh + 1) * hd]
            o_ref[rows, gw + hh * hd:gw + (hh + 1) * hd] = u[rows, hh * hd:(hh + 1) * hd] * s

    is_first = jnp.logical_or(j == 0, j == nlat_blocks)
    is_last = jnp.logical_or(j == nlat_blocks - 1, j == nblocks - 1)
    p = pb_ref[:, 2 * gw:3 * gw]
    hal = POOL_HALO
    ext_scr[0:hal, :] = jnp.where(is_first, 0.0, prev_ref[...])
    ext_scr[hal:hal + tm, :] = p
    ext_scr[hal + tm:hal + tm + hal, :] = jnp.where(is_last, 0.0, next_ref[...])
    in_ctx = j >= nlat_blocks
    seq_len = jnp.where(in_ctx, nctx, nlat)
    t0 = (j - jnp.where(in_ctx, nlat_blocks, 0)) * tm
    tpos = t0 + lax.broadcasted_iota(jnp.int32, (tm, 1), 0)
    pd = POOL_DIM
    for gi, win in enumerate(POOL_WINDOWS):
        cols = slice(gi * pd, (gi + 1) * pd)
        tot = jnp.zeros((tm, pd), F32)
        for off in range(-(win // 2), win // 2):
            tot = tot + ext_scr[hal + off:hal + off + tm, cols]
        lo = jnp.maximum(tpos - win // 2, 0)
        hi = jnp.minimum(tpos + win // 2 - 1, seq_len - 1)
        mean = tot / (hi - lo + 1).astype(F32)
        yv = _bdot((mean - p[:, cols]).astype(BF16), pw_ref[gi]) + pbias_ref[:, cols]
        o_ref[:, 2 * gw + gi * pd:2 * gw + (gi + 1) * pd] = yv * pscale_ref[:, cols]


def _mixers(ya, pb, params, layer, nlat):
    nb, t, gw = ya.shape
    tm = ROW_TILE
    nblocks = t // tm
    nlat_blocks = nlat // tm
    hb = tm // POOL_HALO
    nh = t // POOL_HALO

    return pl.pallas_call(
        functools.partial(_mixers_kernel, nlat_blocks=nlat_blocks, nblocks=nblocks, nlat=nlat, nctx=t - nlat),
        grid=(nb, nblocks),
        in_specs=[
            pl.BlockSpec((None, tm, gw), lambda b, j: (b, j, 0)),
            pl.BlockSpec((None, tm, 3 * gw), lambda b, j: (b, j, 0)),
            pl.BlockSpec((None, POOL_HALO, gw), lambda b, j: (b, jnp.maximum(j * hb - 1, 0), 2)),
            pl.BlockSpec((None, POOL_HALO, gw), lambda b, j: (b, jnp.minimum((j + 1) * hb, nh - 1), 2)),
        ] + [_layer_spec(v, layer) for v in params],
        out_specs=pl.BlockSpec((None, tm, 3 * gw), lambda b, j: (b, j, 0)),
        out_shape=jax.ShapeDtypeStruct((nb, t, 3 * gw), F32),
        scratch_shapes=[pltpu.VMEM((tm + 2 * POOL_HALO, gw), F32)],
        compiler_params=_cparams(("arbitrary", "arbitrary")),
        name="mixers",
    )(ya, pb, pb, pb, *params)


def _ssd_prep_kernel(pd_ref, prev_ref, next_ref, cw_ref, cb_ref, dtb_ref, xc_ref, dt_ref, ext_scr,
                     *, nlat_blocks, nblocks):
    j = pl.program_id(1)
    tm = pd_ref.shape[0]
    nx = M2_XBC
    hal = M2_HALO
    is_first = jnp.logical_or(j == 0, j == nlat_blocks)
    is_last = jnp.logical_or(j == nlat_blocks - 1, j == nblocks - 1)
    ext_scr[0:hal, :] = jnp.where(is_first, 0.0, prev_ref[...])
    ext_scr[hal:hal + tm, :] = pd_ref[:, 0:nx]
    ext_scr[hal + tm:hal + tm + hal, :] = jnp.where(is_last, 0.0, next_ref[...])
    acc = jnp.zeros((tm, nx), F32) + cb_ref[...]
    for tap in range(M2_CONV):
        off = hal + tap - M2_CONV // 2
        acc = acc + ext_scr[off:off + tm, :] * cw_ref[tap:tap + 1, :]
    xc_ref[...] = _silu(acc)
    xdt = pd_ref[:, nx + GROUP_W:nx + GROUP_W + LANE] + dtb_ref[...]
    dt_ref[...] = jnp.maximum(xdt, 0.0) + jnp.log1p(jnp.exp(-jnp.abs(xdt)))


def _split3_bf16(x):
    x1 = x.astype(BF16)
    r1 = x - x1.astype(F32)
    x2 = r1.astype(BF16)
    return x1, x2, (r1 - x2.astype(F32)).astype(BF16)


def _ssd_direction(reverse, xc_ref, dt_ref, a_ref, dsk_ref, e_ref, st_scr, o_ref):
    qn = M2_CHUNK
    gw = GROUP_W
    lane0 = M2_HEADS if reverse else 0
    xs = xc_ref[:, 0:gw]
    nbc = M2_GROUPS * M2_STATE
    bm = xc_ref[:, gw:gw + nbc]
    cm = xc_ref[:, gw + nbc:gw + 2 * nbc]

    dt = dt_ref[...]
    da = dt * a_ref[...]
    ri = lax.broadcasted_iota(jnp.int32, (qn, qn), 0)
    ci = lax.broadcasted_iota(jnp.int32, (qn, qn), 1)
    causal = (ri <= ci) if reverse else (ri >= ci)
    mask = causal.astype(BF16)
    d1, d2, d3 = _split3_bf16(da)
    a_col = _bdot(mask, d1) + (_bdot(mask, d2) + _bdot(mask, d3))
    a_row = a_col.T
    tot_row = a_col[0:1, :] if reverse else a_col[qn - 1:qn, :]

    per_head = jnp.concatenate([dt, jnp.exp(a_col), jnp.exp(tot_row - a_col)], axis=0)
    p1, p2 = _split_bf16(per_head)
    e = e_ref[...]
    per_ch = _bdot(p1, e) + _bdot(p2, e)
    dt_x, ea_x, de_x = per_ch[0:qn], per_ch[qn:2 * qn], per_ch[2 * qn:3 * qn]
    et_x = ea_x[0:1, :] if reverse else ea_x[qn - 1:qn, :]
    xd = xs * dt_x
    xdw = (xd * de_x).astype(BF16)
    xdb = xd.astype(BF16)

    hpg = M2_HEADS // M2_GROUPS
    gcols = hpg * M2_HEAD_DIM
    for g in range(M2_GROUPS):
        bg = bm[:, g * M2_STATE:(g + 1) * M2_STATE]
        cg = cm[:, g * M2_STATE:(g + 1) * M2_STATE].astype(BF16)
        bgt = bg.T.astype(BF16)
        scores = _bdot(cg, bgt)
        st_old = st_scr[:, g * gcols:(g + 1) * gcols]
        y_inter = _bdot(cg, st_old.astype(BF16)) * ea_x[:, g * gcols:(g + 1) * gcols]
        st_new = _bdot(bgt, xdw[:, g * gcols:(g + 1) * gcols])
        st_scr[:, g * gcols:(g + 1) * gcols] = et_x[:, g * gcols:(g + 1) * gcols] * st_old + st_new
        for hl in range(hpg):
            hh = g * hpg + hl
            seg = a_col[:, lane0 + hh:lane0 + hh + 1] - a_row[lane0 + hh:lane0 + hh + 1, :]
            decay = jnp.exp(jnp.where(causal, seg, -jnp.inf))
            cols = slice(hh * M2_HEAD_DIM, (hh + 1) * M2_HEAD_DIM)
            y_h = _bdot((scores * decay).astype(BF16), xdb[:, cols])
            y_h = y_h + y_inter[:, hl * M2_HEAD_DIM:(hl + 1) * M2_HEAD_DIM]
            if reverse:
                o_ref[:, cols] = y_h
            else:
                o_ref[:, cols] = y_h + dsk_ref[:, cols] * xs[:, cols]


def _ssd_kernel(xcf_ref, dtf_ref, xcr_ref, dtr_ref, af_ref, ar_ref, dsk_ref, ef_ref, er_ref,
                yf_ref, yr_ref, stf_scr, str_scr):
    @pl.when(pl.program_id(1) == 0)
    def _():
        stf_scr[...] = jnp.zeros_like(stf_scr)
        str_scr[...] = jnp.zeros_like(str_scr)

    _ssd_direction(False, xcf_ref, dtf_ref, af_ref, dsk_ref, ef_ref, stf_scr, yf_ref)
    _ssd_direction(True, xcr_ref, dtr_ref, ar_ref, dsk_ref, er_ref, str_scr, yr_ref)


def _ssd(pd, params, layer, nlat):
    conv_w, conv_b, dtb, a_dir, dsk, expand = params
    nb, t, wd = pd.shape
    gw = GROUP_W
    tm = ROW_TILE
    hb = tm // M2_HALO
    nh = t // M2_HALO
    xc, dtp = pl.pallas_call(
        functools.partial(_ssd_prep_kernel, nlat_blocks=nlat // tm, nblocks=t // tm),
        grid=(nb, t // tm),
        in_specs=[
            pl.BlockSpec((None, tm, wd), lambda b, j: (b, j, 0)),
            pl.BlockSpec((None, M2_HALO, M2_XBC), lambda b, j: (b, jnp.maximum(j * hb - 1, 0), 0)),
            pl.BlockSpec((None, M2_HALO, M2_XBC), lambda b, j: (b, jnp.minimum((j + 1) * hb, nh - 1), 0)),
            _layer_spec(conv_w, layer), _layer_spec(conv_b, layer), _layer_spec(dtb, layer),
        ],
        out_specs=[pl.BlockSpec((None, tm, M2_XBC), lambda b, j: (b, j, 0)),
                   pl.BlockSpec((None, tm, LANE), lambda b, j: (b, j, 0))],
        out_shape=[jax.ShapeDtypeStruct((nb, t, M2_XBC), F32), jax.ShapeDtypeStruct((nb, t, LANE), F32)],
        scratch_shapes=[pltpu.VMEM((tm + 2 * M2_HALO, M2_XBC), F32)],
        compiler_params=_cparams(("arbitrary", "arbitrary")),
        name="ssd_prep",
    )(pd, pd, pd, conv_w, conv_b, dtb)

    qn = M2_CHUNK
    nchunks = t // qn
    nctx_chunks = nchunks - nlat // qn

    def fwd_chunk(k):
        return jnp.where(k < nctx_chunks, k + (nchunks - nctx_chunks), k - nctx_chunks)

    def rev_chunk(k):
        return nchunks - 1 - k

    def rows(n, chunk_of):
        return pl.BlockSpec((None, qn, n), lambda b, k: (b, chunk_of(k), 0))

    def direction(d):
        return pl.BlockSpec((None, None, 1, LANE), lambda b, k: (layer, d, 0, 0))

    return pl.pallas_call(
        _ssd_kernel,
        grid=(nb, nchunks),
        in_specs=[rows(M2_XBC, fwd_chunk), rows(LANE, fwd_chunk), rows(M2_XBC, rev_chunk), rows(LANE, rev_chunk),
                  direction(0), direction(1), _layer_spec(dsk, layer),
                  pl.BlockSpec((None, LANE, gw), lambda b, k: (0, 0, 0)),
                  pl.BlockSpec((None, LANE, gw), lambda b, k: (1, 0, 0))],
        out_specs=[rows(gw, fwd_chunk), rows(gw, rev_chunk)],
        out_shape=[jax.ShapeDtypeStruct((nb, t, gw), F32), jax.ShapeDtypeStruct((nb, t, gw), F32)],
        scratch_shapes=[pltpu.VMEM((M2_STATE, gw), F32), pltpu.VMEM((M2_STATE, gw), F32)],
        compiler_params=_cparams(("arbitrary", "arbitrary")),
        name="ssd_scan",
    )(xc, dtp, xc, dtp, a_dir, a_dir, dsk, expand, expand)


def _layer_norm_rows(v, g, b):
    mu = jnp.mean(v, axis=-1, keepdims=True)
    var = jnp.mean(jnp.square(v - mu), axis=-1, keepdims=True)
    return (v - mu) * lax.rsqrt(var + LN_EPS) * g + b


def _outproj_kernel(mabc_ref, yf_ref, yr_ref, z_ref, nw_ref, h_ref, gate_ref, w_ref, lg_ref, lb_ref, o_ref,
                    *, alpha):
    k1 = mabc_ref.shape[1]
    gc = GROUP_W // M2_GROUPS
    for r0 in range(0, o_ref.shape[0], ROW_TILE):
        rows = slice(r0, r0 + ROW_TILE)
        gv = (yf_ref[rows, :] + yr_ref[rows, :]) * _silu(z_ref[rows, :])
        md = []
        for g in range(M2_GROUPS):
            part = gv[:, g * gc:(g + 1) * gc]
            ms = jnp.mean(jnp.square(part), axis=-1, keepdims=True)
            md.append((part * lax.rsqrt(ms + RMS_EPS) * nw_ref[:, g * gc:(g + 1) * gc]).astype(BF16))
        mix = (_bdot(mabc_ref[rows, :].astype(BF16), w_ref[0:k1, :])
               + _bdot(jnp.concatenate(md, axis=1), w_ref[k1:w_ref.shape[0], :]))
        o_ref[rows, :] = _layer_norm_rows(alpha * h_ref[rows, :] + gate_ref[...] * mix, lg_ref[...], lb_ref[...])


def _outproj(mabc, yf, yr, pd, norm_w, h, mods, w, ln_g, ln_b, layer, nlat, alpha, latent):
    nb, rows, d = h.shape
    tm = OUT_ROW_TILE if latent else rows
    blk0 = 0 if latent else nlat // tm
    k1, k2 = mabc.shape[2], yf.shape[2]
    z_block = M2_XBC // k2

    def row(n):
        return pl.BlockSpec((None, tm, n), lambda b, j: (b, j + blk0, 0))

    return pl.pallas_call(
        functools.partial(_outproj_kernel, alpha=alpha),
        grid=(nb, rows // tm),
        in_specs=[row(k1), row(k2), row(k2), pl.BlockSpec((None, tm, k2), lambda b, j: (b, j + blk0, z_block)),
                  _layer_spec(norm_w, layer), pl.BlockSpec((None, tm, d), lambda b, j: (b, j, 0)),
                  pl.BlockSpec((None, None, 1, d), lambda b, j: (layer, b if latent else nb, 0, 2)),
                  _layer_spec(w, layer), _layer_spec(ln_g, layer), _layer_spec(ln_b, layer)],
        out_specs=pl.BlockSpec((None, tm, d), lambda b, j: (b, j, 0)),
        out_shape=jax.ShapeDtypeStruct((nb, rows, d), F32),
        compiler_params=_cparams(("arbitrary", "arbitrary")),
        name="outproj_lat" if latent else "outproj_ctx",
    )(mabc, yf, yr, pd, norm_w, h, mods, w, ln_g, ln_b)


def _ffn_kernel(*refs, alpha, grid_conv, halo, seq_len):
    if grid_conv:
        (h_ref, next_ref, sh_ref, sc_ref, gate_ref, wg_ref, wv_ref, cw_ref, cb_ref, wd_ref,
         lg_ref, lb_ref, o_ref, xm_scr, acc_scr, tail_scr, *chain_scr) = refs
    else:
        (h_ref, sh_ref, sc_ref, gate_ref, wg_ref, wv_ref, cw_ref, cb_ref, wd_ref,
         lg_ref, lb_ref, o_ref, xm_scr, acc_scr, tail_scr, *chain_scr) = refs
    g_scr, v_scr, side_scr = chain_scr[0::3], chain_scr[1::3], chain_scr[2::3]
    j = pl.program_id(1)
    f = pl.program_id(2)
    nj = pl.num_programs(1)
    nf = pl.num_programs(2)
    tm = h_ref.shape[0]
    sub = g_scr[0].shape[1]
    nsub = len(g_scr)
    pad = SUBLANE

    @pl.when(f == 0)
    def _():
        def modulated(v):
            return (v * (1.0 + sc_ref[...]) + sh_ref[...]).astype(BF16)
        xm_scr[0:tm, :] = modulated(h_ref[...])
        if grid_conv:
            xm_scr[tm:tm + halo, :] = modulated(next_ref[...])
        acc_scr[...] = jnp.zeros_like(acc_scr)
        for side in side_scr:
            side[:, 0:pad, :] = jnp.zeros((2, pad, sub), F32)
            side[:, pad + tm:pad + tm + pad, :] = jnp.zeros((2, pad, sub), F32)

    if grid_conv:
        @pl.when(j == 0)
        def _():
            tail_scr[f] = jnp.zeros(tail_scr.shape[1:], F32)

    pos = lax.broadcasted_iota(jnp.int32, (tm, 1), 0)
    if grid_conv:
        keep_next = (j < nj - 1).astype(F32)
        pos = pos % GRID_W
        left_ok, right_ok = pos != 0, pos != GRID_W - 1
        row_taps = (-1, 0, 1)
    else:
        pos = pos % seq_len
        left_ok, right_ok = pos != 0, pos != seq_len - 1
        row_taps = (0,)

    for si in range(nsub):
        cs = slice(si * sub, (si + 1) * sub)
        ge = _bdot(xm_scr[...], wg_ref[:, cs])
        if grid_conv:
            g_scr[si][0:halo, :] = tail_scr[f, :, cs]
            g_scr[si][halo:halo + tm, :] = ge[0:tm]
            g_scr[si][halo + tm:halo + tm + halo, :] = ge[tm:tm + halo] * keep_next
            tail_scr[f, :, cs] = ge[tm - halo:tm]
        else:
            g_scr[si][...] = ge
        v_scr[si][...] = _bdot(xm_scr[0:tm, :], wv_ref[:, cs])

    acts = []
    for si in range(nsub):
        cs = slice(si * sub, (si + 1) * sub)

        def tap_sum(dc):
            tot = None
            for dr in row_taps:
                k = (dr + 1) * 3 + (dc + 1)
                r0 = halo + dr * GRID_W
                term = g_scr[si][r0:r0 + tm, :] * cw_ref[k:k + 1, cs]
                tot = term if tot is None else tot + term
            return tot

        side_scr[si][0, pad:pad + tm, :] = tap_sum(-1)
        side_scr[si][1, pad:pad + tm, :] = tap_sum(1)
        conv = tap_sum(0) + cb_ref[:, cs]
        conv = conv + jnp.where(left_ok, side_scr[si][0, pad - 1:pad - 1 + tm, :], 0.0)
        conv = conv + jnp.where(right_ok, side_scr[si][1, pad + 1:pad + 1 + tm, :], 0.0)
        acts.append((_gelu(conv) * v_scr[si][...]).astype(BF16))
    for si in range(nsub):
        acc_scr[...] += _bdot(acts[si], wd_ref[si * sub:(si + 1) * sub, :])

    @pl.when(f == nf - 1)
    def _():
        o_ref[...] = _layer_norm_rows(alpha * h_ref[...] + gate_ref[...] * acc_scr[...], lg_ref[...], lb_ref[...])


def _ffn(h, mods, w_up, conv_w9, conv_b, w_down, ln_g, ln_b, layer, alpha, grid_conv):
    nb, rows, d = h.shape
    fh = w_down.shape[1]
    tf = FFN_COL_TILE
    sub = FFN_SUB_TILE
    nf = fh // tf
    if grid_conv:
        hx = h
        tm, halo = FFN_ROW_TILE, GRID_W
    else:
        hx = h.reshape(1, nb * rows, d)
        tm, halo = nb * rows, 0
    nblk = hx.shape[1] // tm
    hb = tm // GRID_W
    nhalo_blocks = rows // GRID_W

    def vec(k):
        return pl.BlockSpec((None, None, 1, d), lambda b, j, f: (layer, b if grid_conv else nb, 0, k))

    in_specs = [pl.BlockSpec((None, tm, d), lambda b, j, f: (b, j, 0))]
    args = [hx]
    if grid_conv:
        in_specs += [
            pl.BlockSpec((None, halo, d), lambda b, j, f: (b, jnp.minimum((j + 1) * hb, nhalo_blocks - 1), 0)),
        ]
        args += [hx]
    in_specs += [
        vec(3), vec(4), vec(5),
        pl.BlockSpec((None, d, tf), lambda b, j, f: (layer, 0, f)),
        pl.BlockSpec((None, d, tf), lambda b, j, f: (layer, 0, nf + f)),
        pl.BlockSpec((None, 9, tf), lambda b, j, f: (layer, 0, f)),
        pl.BlockSpec((None, 1, tf), lambda b, j, f: (layer, 0, f)),
        pl.BlockSpec((None, tf, d), lambda b, j, f: (layer, f, 0)),
        _layer_spec(ln_g, layer), _layer_spec(ln_b, layer),
    ]
    args += [mods, mods, mods, w_up, w_up, conv_w9, conv_b, w_down, ln_g, ln_b]
    return pl.pallas_call(
        functools.partial(_ffn_kernel, alpha=alpha, grid_conv=grid_conv, halo=halo, seq_len=rows),
        grid=(hx.shape[0], nblk, nf),
        in_specs=in_specs,
        out_specs=pl.BlockSpec((None, tm, d), lambda b, j, f: (b, j, 0)),
        out_shape=jax.ShapeDtypeStruct(hx.shape, F32),
        scratch_shapes=[pltpu.VMEM((tm + halo, d), BF16), pltpu.VMEM((tm, d), F32),
                        pltpu.VMEM((nf, max(halo, SUBLANE), tf), F32)]
        + [pltpu.VMEM((tm + 2 * halo, sub), F32), pltpu.VMEM((tm, sub), F32),
           pltpu.VMEM((2, tm + 2 * SUBLANE, sub), F32)] * (tf // sub),
        compiler_params=_cparams(("arbitrary", "arbitrary", "arbitrary")),
        name="ffn_lat" if grid_conv else "ffn_ctx",
    )(*args).reshape(h.shape)


def kernel(x, c, ctx, c_ctx, w_ada, b_ada, w_in, w_out, ln1_g, ln1_b, ln2_g, ln2_b, s5_a_re, s5_a_im, s5_b_re, s5_b_im, s5_c_re, s5_c_im, s5_log_step, s5_d, s5_glu_w, s5_glu_b, sg_ln_g, sg_ln_b, sg_w, sg_b, pool_w, pool_b, pool_scale, m2_conv_w, m2_conv_b, m2_dt_bias, m2_a_log, m2_d, m2_norm_w, ffn_w_up, ffn_conv_w, ffn_conv_b, ffn_w_down):
    nb, nlat, d = x.shape
    nctx = ctx.shape[1]
    depth = w_ada.shape[0]
    gw = GROUP_W
    alpha = (2 * depth) ** 0.25
    assert nb + 1 <= MOD_ROWS and nlat % FFN_ROW_TILE == 0 and nlat % OUT_ROW_TILE == 0 and nctx % ROW_TILE == 0
    assert nlat % (GRID_W * SUBLANE) == 0 and FFN_ROW_TILE % GRID_W == 0
    assert w_in.shape[2] == 5 * gw + M2_XBC + 2 * M2_HEADS and (5 * gw + M2_XBC) % ROW_TILE == 0

    cmat = jnp.concatenate([c, c_ctx[None, :], jnp.zeros((MOD_ROWS - nb - 1, d), F32)], axis=0)
    mods = _ada(cmat, w_ada, b_ada).reshape(depth, MOD_ROWS, 1, 6 * d)

    def rows(v):
        return v.astype(F32).reshape(depth, 1, -1)

    fh = ffn_w_down.shape[1]
    lane_pad = LANE - 2 * M2_HEADS
    w_in_b = _cast_w_in(w_in)
    w_dt_b = jnp.pad(w_in[:, :, w_in.shape[2] - 2 * M2_HEADS:], ((0, 0), (0, 0), (0, lane_pad))).astype(BF16)
    w_out_b = w_out.astype(BF16)
    w_up_b = ffn_w_up.astype(BF16)
    w_down_b = ffn_w_down.astype(BF16)
    cw9 = ffn_conv_w.astype(F32).reshape(depth, 9, fh)
    s5_mats = _s5_mats(s5_a_re, s5_a_im, s5_b_re, s5_b_im, s5_c_re, s5_c_im, s5_log_step, s5_d)
    sg_btile = jnp.repeat(jnp.swapaxes(sg_b.astype(F32), 1, 2), gw // SG_HEADS, axis=2)
    mixer_params = (s5_glu_w.astype(BF16), rows(s5_glu_b), rows(sg_ln_g), rows(sg_ln_b), sg_w.astype(BF16),
                    sg_btile, pool_w.astype(BF16), rows(pool_b), rows(pool_scale))
    m2_a = -jnp.exp(m2_a_log.astype(F32))
    a_dir = (m2_a[:, :, None, :] * jnp.eye(2, dtype=F32)[None, :, :, None]).reshape(depth, 2, 1, 2 * M2_HEADS)
    a_dir = jnp.pad(a_dir, ((0, 0), (0, 0), (0, 0), (0, lane_pad)))
    head_of_ch = jnp.arange(gw) // M2_HEAD_DIM
    expand = (jnp.arange(LANE)[None, :, None] == (jnp.arange(2)[:, None, None] * M2_HEADS + head_of_ch[None, None, :])
              ).astype(BF16)
    ssd_params = (m2_conv_w.astype(F32), rows(m2_conv_b),
                  jnp.pad(m2_dt_bias.astype(F32).reshape(depth, 1, 2 * M2_HEADS), ((0, 0), (0, 0), (0, lane_pad))),
                  a_dir, rows(jnp.repeat(m2_d.astype(F32), M2_HEAD_DIM, axis=1)), expand)
    m2_nw = rows(m2_norm_w)
    ln1 = rows(ln1_g), rows(ln1_b)
    ln2 = rows(ln2_g), rows(ln2_b)
    ffn_cb = rows(ffn_conv_b)

    h_lat, h_ctx = x, ctx
    for i in range(depth):
        need_ctx = i < depth - 1
        pa, pb, pd = _inproj(h_lat, h_ctx, mods, w_in_b, w_dt_b, i)
        ya = _s5(pa, s5_mats, i, nlat)
        mabc = _mixers(ya, pb, mixer_params, i, nlat)
        yf, yr = _ssd(pd, ssd_params, i, nlat)
        mix_in = (mabc, yf, yr, pd, m2_nw)
        h1_lat = _outproj(*mix_in, h_lat, mods, w_out_b, ln1[0], ln1[1], i, nlat, alpha, True)
        if need_ctx:
            h1_ctx = _outproj(*mix_in, h_ctx, mods, w_out_b, ln1[0], ln1[1], i, nlat, alpha, False)
            h_ctx = _ffn(h1_ctx, mods, w_up_b, cw9, ffn_cb, w_down_b, ln2[0], ln2[1], i, alpha, False)
        h_lat = _ffn(h1_lat, mods, w_up_b, cw9, ffn_cb, w_down_b, ln2[0], ln2[1], i, alpha, True)
    return h_lat
```

```python
import functools

import jax
import jax.numpy as jnp
from jax import lax
from jax.experimental import pallas as pl
from jax.experimental.pallas import tpu as pltpu

F32 = jnp.float32
BF16 = jnp.bfloat16

GRID_W = 64
GROUP_W = 512
S5_CH = 16
S5_Q = 16
S5_RELAYOUT_UNROLL = 4
SG_HEADS = 4
SG_CHUNK = 128
POOL_WINDOWS = (2, 4, 8, 16)
POOL_DIM = GROUP_W // len(POOL_WINDOWS)
POOL_HALO = 8
M2_HEAD_DIM = 64
M2_HEADS = GROUP_W // M2_HEAD_DIM
M2_STATE = 128
M2_GROUPS = 2
M2_CONV = 4
M2_CHUNK = 128
M2_XBC = GROUP_W + 2 * M2_GROUPS * M2_STATE
M2_HALO = 8
SSD_BLOCK = 2 * M2_CHUNK
LN_EPS = 1e-5
RMS_EPS = 1e-5

LANE = 128
SUBLANE = 8
VMEM_LIMIT = 56 * 1024 * 1024
ROW_TILE = 256
OUT_ROW_TILE = 512
FFN_ROW_TILE = 512
FFN_COL_TILE = 512
FFN_SUB_TILE = 256
ADA_COL_TILE = 1024
MOD_ROWS = 8


def _cparams(sem):
    return pltpu.CompilerParams(dimension_semantics=sem, vmem_limit_bytes=VMEM_LIMIT)


def _bdot(a, b):
    return jnp.dot(a, b, preferred_element_type=F32)


def _gelu(x):
    return jax.nn.gelu(x)


def _silu(x):
    return x * jax.nn.sigmoid(x)


def _ada_kernel(c_ref, w_ref, b_ref, o_ref):
    c = c_ref[...]
    s = _silu(c).astype(BF16)
    o_ref[...] = _bdot(s, w_ref[...].astype(BF16)) + b_ref[...]


def _ada(cmat, w_ada, b_ada):
    depth, d, n = w_ada.shape
    tn = ADA_COL_TILE
    return pl.pallas_call(
        _ada_kernel,
        grid=(depth, n // tn),
        in_specs=[
            pl.BlockSpec((MOD_ROWS, d), lambda l, j: (0, 0)),
            pl.BlockSpec((None, d, tn), lambda l, j: (l, 0, j)),
            pl.BlockSpec((None, 1, tn), lambda l, j: (l, 0, j)),
        ],
        out_specs=pl.BlockSpec((None, MOD_ROWS, tn), lambda l, j: (l, 0, j)),
        out_shape=jax.ShapeDtypeStruct((depth, MOD_ROWS, n), F32),
        compiler_params=_cparams(("arbitrary", "arbitrary")),
        name="ada",
    )(cmat, w_ada, b_ada.reshape(depth, 1, n))


def _mod_spec(d, k, layer, nbatch, nlat_blocks):
    return pl.BlockSpec((None, None, 1, d),
                        lambda b, j, *_: (layer, jnp.where(j >= nlat_blocks, nbatch, b), 0, k))


def _layer_spec(arr, layer):
    tail = arr.shape[1:]
    return pl.BlockSpec((None,) + tail, lambda *_: (layer,) + (0,) * len(tail))


def _cast_w_in_kernel(wt_ref, wb_ref):
    wb_ref[...] = wt_ref[...].T.astype(BF16)


def _cast_w_in(w_in):
    depth, d, n = w_in.shape
    tn = ROW_TILE
    wt = jnp.swapaxes(w_in, 1, 2)
    return pl.pallas_call(
        _cast_w_in_kernel,
        grid=(depth, n // tn),
        in_specs=[pl.BlockSpec((None, tn, d), lambda l, i: (l, i, 0))],
        out_specs=pl.BlockSpec((None, d, tn), lambda l, i: (l, 0, i)),
        out_shape=jax.ShapeDtypeStruct((depth, d, n // tn * tn), BF16),
        compiler_params=_cparams(("arbitrary", "arbitrary")),
        name="cast_w_in",
    )(wt)


def _stream_specs(tm, d, nlat_blocks):
    lat = pl.BlockSpec((None, tm, d), lambda b, j: (b, jnp.minimum(j, nlat_blocks - 1), 0))
    ctx = pl.BlockSpec((None, tm, d), lambda b, j: (b, jnp.maximum(j - nlat_blocks, 0), 0))
    return lat, ctx


def _inproj_kernel(hl_ref, hc_ref, sh_ref, sc_ref, w_ref, wdt_ref, oa_ref, ob_ref, od_ref, xm_scr, *, nlat_blocks):
    j = pl.program_id(1)

    def modulated(h_ref):
        return (h_ref[...] * (1.0 + sc_ref[...]) + sh_ref[...]).astype(BF16)

    @pl.when(j < nlat_blocks)
    def _():
        xm_scr[...] = modulated(hl_ref)

    @pl.when(j >= nlat_blocks)
    def _():
        xm_scr[...] = modulated(hc_ref)

    xm = xm_scr[...]
    gw = GROUP_W
    oa_ref[...] = _bdot(xm, w_ref[:, 0:gw])
    ob_ref[...] = _bdot(xm, w_ref[:, gw:4 * gw])
    od_ref[:, 0:M2_XBC] = _bdot(xm, w_ref[:, 5 * gw:5 * gw + M2_XBC])
    od_ref[:, M2_XBC:M2_XBC + gw] = _bdot(xm, w_ref[:, 4 * gw:5 * gw])
    od_ref[:, M2_XBC + gw:M2_XBC + gw + LANE] = _bdot(xm, wdt_ref[...])


def _inproj(h_lat, h_ctx, mods, w, w_dt, layer):
    nb, nlat, d = h_lat.shape
    t = nlat + h_ctx.shape[1]
    tm = ROW_TILE
    na, nbw, nd = GROUP_W, 3 * GROUP_W, M2_XBC + GROUP_W + LANE
    lat_spec, ctx_spec = _stream_specs(tm, d, nlat // tm)

    def row(n):
        return pl.BlockSpec((None, tm, n), lambda b, j: (b, j, 0))

    return pl.pallas_call(
        functools.partial(_inproj_kernel, nlat_blocks=nlat // tm),
        grid=(nb, t // tm),
        in_specs=[lat_spec, ctx_spec, _mod_spec(d, 0, layer, nb, nlat // tm),
                  _mod_spec(d, 1, layer, nb, nlat // tm), _layer_spec(w, layer), _layer_spec(w_dt, layer)],
        out_specs=[row(na), row(nbw), row(nd)],
        out_shape=[jax.ShapeDtypeStruct((nb, t, n), F32) for n in (na, nbw, nd)],
        scratch_shapes=[pltpu.VMEM((tm, d), BF16)],
        compiler_params=_cparams(("arbitrary", "arbitrary")),
        name="inproj",
    )(h_lat, h_ctx, mods, mods, w, w_dt)


def _s5_mats(a_re, a_im, b_re, b_im, c_re, c_im, log_step, d_skip):
    q, hch = S5_Q, S5_CH
    nl, _, g, p = a_re.shape
    lre, lim = a_re.astype(F32), a_im.astype(F32)
    step = jnp.exp(log_step.astype(F32))[..., None]
    mag = jnp.exp(lre * step)
    ar, ai = mag * jnp.cos(lim * step), mag * jnp.sin(lim * step)
    den = lre * lre + lim * lim
    qr = ((ar - 1.0) * lre + ai * lim) / den
    qi = (ai * lre - (ar - 1.0) * lim) / den
    bre, bim = b_re.astype(F32), b_im.astype(F32)
    bbr = qr[..., None] * bre - qi[..., None] * bim
    bbi = qr[..., None] * bim + qi[..., None] * bre
    cre, cim = c_re.astype(F32), c_im.astype(F32)

    pr, pi = ar[:, :, :, None], ai[:, :, :, None]
    while pr.shape[3] < q:
        lr, li = pr[:, :, :, -1:], pi[:, :, :, -1:]
        pr, pi = (jnp.concatenate([pr, pr * lr - pi * li], axis=3), jnp.concatenate([pi, pr * li + pi * lr], axis=3))
    pk_re = jnp.concatenate([jnp.ones_like(pr[:, :, :, :1]), pr], axis=3)
    pk_im = jnp.concatenate([jnp.zeros_like(pi[:, :, :, :1]), pi], axis=3)
    pt_re, pt_im = jnp.swapaxes(pk_re, 3, 4), jnp.swapaxes(pk_im, 3, 4)

    def lag_kernels(d, pr, pi):
        ctr = jnp.swapaxes(cre[:, d], -1, -2)[:, :, :, None]
        cti = jnp.swapaxes(cim[:, d], -1, -2)[:, :, :, None]
        ca_re = (ctr * pr[..., None] - cti * pi[..., None]).reshape(nl, g, p, 1, q * hch)
        ca_im = (ctr * pi[..., None] + cti * pr[..., None]).reshape(nl, g, p, 1, q * hch)
        return jnp.sum(bbr[:, d][..., None] * ca_re - bbi[:, d][..., None] * ca_im, axis=2)

    lag_f = lag_kernels(0, pt_re[:, 0, :, :, :q], pt_im[:, 0, :, :, :q])
    lag_r = lag_kernels(1, jnp.flip(pt_re[:, 1, :, :, :q], axis=3), jnp.flip(pt_im[:, 1, :, :, :q], axis=3))
    dmat = jnp.eye(hch, dtype=F32) * d_skip.astype(F32).reshape(nl, g, 1, hch)
    mid = lag_r[..., (q - 1) * hch:] + lag_f[..., :hch] + dmat
    kk = jnp.concatenate([lag_r[..., :(q - 1) * hch], mid, lag_f[..., hch:],
                          jnp.zeros(mid.shape[:-1] + (hch,), F32)], axis=-1)

    def state_in(d, pr, pi):
        btr = jnp.swapaxes(bbr[:, d], -1, -2)[:, :, None]
        bti = jnp.swapaxes(bbi[:, d], -1, -2)[:, :, None]
        pr, pi = pr[:, :, :, None], pi[:, :, :, None]
        m = jnp.concatenate([pr * btr - pi * bti, pr * bti + pi * btr], axis=-1)
        return m.reshape(nl, g, q * hch, 2 * p)

    sb = jnp.concatenate([state_in(0, jnp.flip(pk_re[:, 0, :, :q], axis=2), jnp.flip(pk_im[:, 0, :, :q], axis=2)),
                          state_in(1, pk_re[:, 1, :, :q], pk_im[:, 1, :, :q])], axis=-1)

    def state_out(d, pr, pi):
        ctr = jnp.swapaxes(cre[:, d], -1, -2)[:, :, :, None]
        cti = jnp.swapaxes(cim[:, d], -1, -2)[:, :, :, None]
        pr, pi = pr[..., None], pi[..., None]
        m = jnp.concatenate([ctr * pr - cti * pi, -(ctr * pi + cti * pr)], axis=2)
        return m.reshape(nl, g, 2 * p, q * hch)

    w2 = jnp.concatenate([state_out(0, pt_re[:, 0, :, :, 1:], pt_im[:, 0, :, :, 1:]),
                          state_out(1, jnp.flip(pt_re[:, 1, :, :, 1:], axis=3), jnp.flip(pt_im[:, 1, :, :, 1:], axis=3))],
                         axis=2)

    aqr, aqi = pk_re[:, :, :, q], pk_im[:, :, :, q]
    a1 = jnp.concatenate([aqr, aqr], axis=-1).reshape(nl, 2, g * 2 * p)
    a2 = jnp.concatenate([-aqi, aqi], axis=-1).reshape(nl, 2, g * 2 * p)
    acoef = jnp.stack([a1[:, 0], a2[:, 0], a1[:, 1], a2[:, 1]], axis=1)
    return kk, sb, w2, acoef


def _granule_transpose(v):
    n = len(v)
    slot = lax.broadcasted_iota(jnp.int32, v[0].shape, 1) // S5_CH
    v = list(v)
    j = n // 2
    while j >= 1:
        keep = (slot & j) == 0
        for i in range(n):
            if i & j == 0:
                a, b = v[i], v[i + j]
                v[i] = jnp.where(keep, a, pltpu.roll(b, j * S5_CH, axis=1))
                v[i + j] = jnp.where(keep, pltpu.roll(a, LANE - j * S5_CH, axis=1), b)
        j //= 2
    return v


def _split_bf16(x):
    hi = x.astype(BF16)
    return hi, (x - hi.astype(F32)).astype(BF16)


def _dot3(xs, w):
    x_hi, x_lo = xs
    w_hi, w_lo = _split_bf16(w)
    return _bdot(x_hi, w_hi) + (_bdot(x_lo, w_hi) + _bdot(x_hi, w_lo))


def _s5_kernel(pa_ref, kk_ref, sb_ref, w2_ref, ac_ref, ya_ref, x_scr, s_scr, h_scr, t_scr, *, nbatch, nrows, nctx):
    gb = LANE // S5_CH
    halves = S5_Q // gb
    qh = S5_Q * S5_CH
    sw = LANE
    nlat = nrows - nctx
    nrc = nrows // SUBLANE
    tok_per_rc = SUBLANE * S5_Q

    def relayout(it, to_rows):
        for u in range(S5_RELAYOUT_UNROLL):
            idx = it * S5_RELAYOUT_UNROLL + u
            b = idx // nrc
            rc = idx % nrc
            r0 = pl.multiple_of(b * nrows + rc * SUBLANE, SUBLANE)
            for half in range(halves):
                lanes = slice(half * LANE, (half + 1) * LANE)

                def tok(s8, half=half, rc=rc):
                    return pl.ds(rc * tok_per_rc + half * gb + s8, SUBLANE, stride=S5_Q)

                if to_rows:
                    v = _granule_transpose([pa_ref[b, tok(s8), :] for s8 in range(gb)])
                    for gi in range(gb):
                        x_scr[gi, pl.ds(r0, SUBLANE), lanes] = v[gi]
                else:
                    v = _granule_transpose([x_scr[gi, pl.ds(r0, SUBLANE), lanes] for gi in range(gb)])
                    for s8 in range(gb):
                        ya_ref[b, tok(s8), :] = v[s8]
        return 0

    nblocks = nbatch * nrc // S5_RELAYOUT_UNROLL
    lax.fori_loop(0, nblocks, lambda it, c: relayout(it, True), 0)

    for gi in range(gb):
        kkv = kk_ref[gi]
        for s in range(S5_Q):
            shift = (S5_Q - 1 - s) * S5_CH
            t_scr[s * S5_CH:(s + 1) * S5_CH, :] = pltpu.roll(kkv, kkv.shape[1] - shift, axis=1)[:, 0:qh]
        xs = _split_bf16(x_scr[gi])
        x_scr[gi] = _dot3(xs, t_scr[...])
        rs = _dot3(xs, sb_ref[gi])
        for d in range(2):
            sd = rs[:, d * sw:(d + 1) * sw]
            s_scr[d, gi] = sd
            s_scr[2 + d, gi] = pltpu.roll(sd, sw // 2, axis=1)

    zero = jnp.zeros((nbatch, sw), F32)

    def body(s, carry):
        f_rows = pl.ds(jnp.where(s < nctx, s + nlat, s - nctx), nbatch, stride=nrows)
        r_rows = pl.ds(nrows - 1 - s, nbatch, stride=nrows)
        out = []
        for gi in range(gb):
            hf, hfs, hr, hrs = carry[4 * gi:4 * gi + 4]
            a1f, a2f, a1r, a2r = (ac_ref[k:k + 1, gi * sw:(gi + 1) * sw] for k in range(4))
            h_scr[0, gi, f_rows, :] = hf
            h_scr[1, gi, r_rows, :] = hr
            out += [a1f * hf + a2f * hfs + s_scr[0, gi, f_rows, :],
                    a1f * hfs - a2f * hf + s_scr[2, gi, f_rows, :],
                    a1r * hr + a2r * hrs + s_scr[1, gi, r_rows, :],
                    a1r * hrs - a2r * hr + s_scr[3, gi, r_rows, :]]
        return tuple(out)

    lax.fori_loop(0, nrows, body, (zero,) * (4 * gb))

    for gi in range(gb):
        hin = jnp.concatenate([h_scr[0, gi], h_scr[1, gi]], axis=1)
        x_scr[gi] = x_scr[gi] + _dot3(_split_bf16(hin), w2_ref[gi])

    lax.fori_loop(0, nblocks, lambda it, c: relayout(it, False), 0)


def _s5(pa, mats, layer, nlat):
    kk, sb, w2, acoef = mats
    nb, t, gw = pa.shape
    q = S5_Q
    nrows = t // q
    qh = q * S5_CH
    gb = LANE // S5_CH
    assert nrows % SUBLANE == 0 and q % gb == 0 and w2.shape[2] == 2 * LANE and sb.shape[3] == 2 * LANE
    assert (nb * nrows // SUBLANE) % S5_RELAYOUT_UNROLL == 0
    return pl.pallas_call(
        functools.partial(_s5_kernel, nbatch=nb, nrows=nrows, nctx=(t - nlat) // q),
        grid=(gw // LANE,),
        in_specs=[
            pl.BlockSpec((nb, t, LANE), lambda i: (0, 0, i)),
            pl.BlockSpec((None, gb, S5_CH, 2 * qh), lambda i: (layer, i, 0, 0)),
            pl.BlockSpec((None, gb, qh, 2 * LANE), lambda i: (layer, i, 0, 0)),
            pl.BlockSpec((None, gb, 2 * LANE, qh), lambda i: (layer, i, 0, 0)),
            pl.BlockSpec((None, 4, gb * LANE), lambda i: (layer, 0, i)),
        ],
        out_specs=pl.BlockSpec((nb, t, LANE), lambda i: (0, 0, i)),
        out_shape=jax.ShapeDtypeStruct((nb, t, gw), F32),
        scratch_shapes=[pltpu.VMEM((gb, nb * nrows, qh), F32),
                        pltpu.VMEM((4, gb, nb * nrows, LANE), F32),
                        pltpu.VMEM((2, gb, nb * nrows, LANE), F32),
                        pltpu.VMEM((qh, qh), F32)],
        compiler_params=_cparams(("arbitrary",)),
        name="s5",
    )(pa, kk, sb, w2, acoef)


def _mixers_kernel(ya_ref, pb_ref, prev_ref, next_ref, gluw_ref, glub_ref, lng_ref, lnb_ref, sgw_ref,
                   sgb_ref, pw_ref, pbias_ref, pscale_ref, o_ref, ext_scr, *, nlat_blocks, nblocks, nlat, nctx):
    j = pl.program_id(1)
    tm = ya_ref.shape[0]
    gw = GROUP_W

    z = _gelu(ya_ref[...])
    gate = jax.nn.sigmoid(_bdot(z.astype(BF16), gluw_ref[...]) + glub_ref[...])
    o_ref[:, 0:gw] = z * gate

    hd = gw // SG_HEADS
    u = _gelu(pb_ref[:, 0:gw])
    v = _gelu(pb_ref[:, gw:2 * gw])
    for hh in range(SG_HEADS):
        vh = v[:, hh * hd:(hh + 1) * hd]
        mu = jnp.mean(vh, axis=-1, keepdims=True)
        var = jnp.mean(jnp.square(vh - mu), axis=-1, keepdims=True)
        vn = ((vh - mu) * lax.rsqrt(var + LN_EPS) * lng_ref[:, hh * hd:(hh + 1) * hd]
              + lnb_ref[:, hh * hd:(hh + 1) * hd]).astype(BF16)
        for cchunk in range(tm // SG_CHUNK):
            rows = slice(cchunk * SG_CHUNK, (cchunk + 1) * SG_CHUNK)
            s = _bdot(sgw_ref[hh], vn[rows]) + sgb_ref[:, hh * hd:(hh + 1) * hd]
            o_ref[rows, gw + hh * hd:gw + (hh + 1) * hd] = u[rows, hh * hd:(hh + 1) * hd] * s

    is_first = jnp.logical_or(j == 0, j == nlat_blocks)
    is_last = jnp.logical_or(j == nlat_blocks - 1, j == nblocks - 1)
    p = pb_ref[:, 2 * gw:3 * gw]
    hal = POOL_HALO
    ext_scr[0:hal, :] = jnp.where(is_first, 0.0, prev_ref[...])
    ext_scr[hal:hal + tm, :] = p
    ext_scr[hal + tm:hal + tm + hal, :] = jnp.where(is_last, 0.0, next_ref[...])
    in_ctx = j >= nlat_blocks
    seq_len = jnp.where(in_ctx, nctx, nlat)
    t0 = (j - jnp.where(in_ctx, nlat_blocks, 0)) * tm
    tpos = t0 + lax.broadcasted_iota(jnp.int32, (tm, 1), 0)
    pd = POOL_DIM
    for gi, win in enumerate(POOL_WINDOWS):
        cols = slice(gi * pd, (gi + 1) * pd)
        tot = jnp.zeros((tm, pd), F32)
        for off in range(-(win // 2), win // 2):
            tot = tot + ext_scr[hal + off:hal + off + tm, cols]
        lo = jnp.maximum(tpos - win // 2, 0)
        hi = jnp.minimum(tpos + win // 2 - 1, seq_len - 1)
        mean = tot / (hi - lo + 1).astype(F32)
        yv = _bdot((mean - p[:, cols]).astype(BF16), pw_ref[gi]) + pbias_ref[:, cols]
        o_ref[:, 2 * gw + gi * pd:2 * gw + (gi + 1) * pd] = yv * pscale_ref[:, cols]


def _mixers(ya, pb, params, layer, nlat):
    nb, t, gw = ya.shape
    tm = ROW_TILE
    nblocks = t // tm
    nlat_blocks = nlat // tm
    hb = tm // POOL_HALO
    nh = t // POOL_HALO

    return pl.pallas_call(
        functools.partial(_mixers_kernel, nlat_blocks=nlat_blocks, nblocks=nblocks, nlat=nlat, nctx=t - nlat),
        grid=(nb, nblocks),
        in_specs=[
            pl.BlockSpec((None, tm, gw), lambda b, j: (b, j, 0)),
            pl.BlockSpec((None, tm, 3 * gw), lambda b, j: (b, j, 0)),
            pl.BlockSpec((None, POOL_HALO, gw), lambda b, j: (b, jnp.maximum(j * hb - 1, 0), 2)),
            pl.BlockSpec((None, POOL_HALO, gw), lambda b, j: (b, jnp.minimum((j + 1) * hb, nh - 1), 2)),
        ] + [_layer_spec(v, layer) for v in params],
        out_specs=pl.BlockSpec((None, tm, 3 * gw), lambda b, j: (b, j, 0)),
        out_shape=jax.ShapeDtypeStruct((nb, t, 3 * gw), F32),
        scratch_shapes=[pltpu.VMEM((tm + 2 * POOL_HALO, gw), F32)],
        compiler_params=_cparams(("arbitrary", "arbitrary")),
        name="mixers",
    )(ya, pb, pb, pb, *params)


def _ssd_prep_kernel(pd_ref, prev_ref, next_ref, cw_ref, cb_ref, dtb_ref, xc_ref, dt_ref, ext_scr,
                     *, nlat_blocks, nblocks):
    j = pl.program_id(1)
    tm = pd_ref.shape[0]
    nx = M2_XBC
    hal = M2_HALO
    is_first = jnp.logical_or(j == 0, j == nlat_blocks)
    is_last = jnp.logical_or(j == nlat_blocks - 1, j == nblocks - 1)
    ext_scr[0:hal, :] = jnp.where(is_first, 0.0, prev_ref[...])
    ext_scr[hal:hal + tm, :] = pd_ref[:, 0:nx]
    ext_scr[hal + tm:hal + tm + hal, :] = jnp.where(is_last, 0.0, next_ref[...])
    acc = jnp.zeros((tm, nx), F32) + cb_ref[...]
    for tap in range(M2_CONV):
        off = hal + tap - M2_CONV // 2
        acc = acc + ext_scr[off:off + tm, :] * cw_ref[tap:tap + 1, :]
    xc_ref[...] = _silu(acc)
    xdt = pd_ref[:, nx + GROUP_W:nx + GROUP_W + LANE] + dtb_ref[...]
    dt_ref[...] = jnp.maximum(xdt, 0.0) + jnp.log1p(jnp.exp(-jnp.abs(xdt)))


def _split3_bf16(x):
    x1 = x.astype(BF16)
    r1 = x - x1.astype(F32)
    x2 = r1.astype(BF16)
    return x1, x2, (r1 - x2.astype(F32)).astype(BF16)


def _ssd_chunk(reverse, rows, xc_ref, dt_ref, a_ref, dsk_ref, e_ref, st_scr, o_ref):
    qn = M2_CHUNK
    gw = GROUP_W
    lane0 = M2_HEADS if reverse else 0
    xs = xc_ref[rows, 0:gw]
    nbc = M2_GROUPS * M2_STATE
    bm = xc_ref[rows, gw:gw + nbc]
    cm = xc_ref[rows, gw + nbc:gw + 2 * nbc]

    dt = dt_ref[rows, :]
    da = dt * a_ref[...]
    ri = lax.broadcasted_iota(jnp.int32, (qn, qn), 0)
    ci = lax.broadcasted_iota(jnp.int32, (qn, qn), 1)
    causal = (ri <= ci) if reverse else (ri >= ci)
    mask = causal.astype(BF16)
    d1, d2, d3 = _split3_bf16(da)
    a_col = _bdot(mask, d1) + (_bdot(mask, d2) + _bdot(mask, d3))
    a_row = a_col.T
    tot_row = a_col[0:1, :] if reverse else a_col[qn - 1:qn, :]

    per_head = jnp.concatenate([dt, jnp.exp(a_col), jnp.exp(tot_row - a_col)], axis=0)
    p1, p2 = _split_bf16(per_head)
    e = e_ref[...]
    per_ch = _bdot(p1, e) + _bdot(p2, e)
    dt_x, ea_x, de_x = per_ch[0:qn], per_ch[qn:2 * qn], per_ch[2 * qn:3 * qn]
    et_x = ea_x[0:1, :] if reverse else ea_x[qn - 1:qn, :]
    xd = xs * dt_x
    xdw = (xd * de_x).astype(BF16)
    xdb = xd.astype(BF16)

    hpg = M2_HEADS // M2_GROUPS
    gcols = hpg * M2_HEAD_DIM
    for g in range(M2_GROUPS):
        bg = bm[:, g * M2_STATE:(g + 1) * M2_STATE]
        cg = cm[:, g * M2_STATE:(g + 1) * M2_STATE].astype(BF16)
        bgt = bg.T.astype(BF16)
        scores = _bdot(cg, bgt)
        st_old = st_scr[:, g * gcols:(g + 1) * gcols]
        y_inter = _bdot(cg, st_old.astype(BF16)) * ea_x[:, g * gcols:(g + 1) * gcols]
        st_new = _bdot(bgt, xdw[:, g * gcols:(g + 1) * gcols])
        st_scr[:, g * gcols:(g + 1) * gcols] = et_x[:, g * gcols:(g + 1) * gcols] * st_old + st_new
        for hl in range(hpg):
            hh = g * hpg + hl
            seg = a_col[:, lane0 + hh:lane0 + hh + 1] - a_row[lane0 + hh:lane0 + hh + 1, :]
            decay = jnp.exp(jnp.where(causal, seg, -jnp.inf))
            cols = slice(hh * M2_HEAD_DIM, (hh + 1) * M2_HEAD_DIM)
            y_h = _bdot((scores * decay).astype(BF16), xdb[:, cols])
            y_h = y_h + y_inter[:, hl * M2_HEAD_DIM:(hl + 1) * M2_HEAD_DIM]
            if reverse:
                o_ref[rows, cols] = y_h
            else:
                o_ref[rows, cols] = y_h + dsk_ref[:, cols] * xs[:, cols]


def _ssd_kernel(xcf_ref, dtf_ref, xcr_ref, dtr_ref, af_ref, ar_ref, dsk_ref, ef_ref, er_ref,
                yf_ref, yr_ref, stf_scr, str_scr):
    @pl.when(pl.program_id(1) == 0)
    def _():
        stf_scr[...] = jnp.zeros_like(stf_scr)
        str_scr[...] = jnp.zeros_like(str_scr)

    nchunk = yf_ref.shape[0] // M2_CHUNK
    for c in range(nchunk):
        rows_f = slice(c * M2_CHUNK, (c + 1) * M2_CHUNK)
        rows_r = slice((nchunk - 1 - c) * M2_CHUNK, (nchunk - c) * M2_CHUNK)
        _ssd_chunk(False, rows_f, xcf_ref, dtf_ref, af_ref, dsk_ref, ef_ref, stf_scr, yf_ref)
        _ssd_chunk(True, rows_r, xcr_ref, dtr_ref, ar_ref, dsk_ref, er_ref, str_scr, yr_ref)


def _ssd(pd, params, layer, nlat):
    conv_w, conv_b, dtb, a_dir, dsk, expand = params
    nb, t, wd = pd.shape
    gw = GROUP_W
    tm = ROW_TILE
    hb = tm // M2_HALO
    nh = t // M2_HALO
    xc, dtp = pl.pallas_call(
        functools.partial(_ssd_prep_kernel, nlat_blocks=nlat // tm, nblocks=t // tm),
        grid=(nb, t // tm),
        in_specs=[
            pl.BlockSpec((None, tm, wd), lambda b, j: (b, j, 0)),
            pl.BlockSpec((None, M2_HALO, M2_XBC), lambda b, j: (b, jnp.maximum(j * hb - 1, 0), 0)),
            pl.BlockSpec((None, M2_HALO, M2_XBC), lambda b, j: (b, jnp.minimum((j + 1) * hb, nh - 1), 0)),
            _layer_spec(conv_w, layer), _layer_spec(conv_b, layer), _layer_spec(dtb, layer),
        ],
        out_specs=[pl.BlockSpec((None, tm, M2_XBC), lambda b, j: (b, j, 0)),
                   pl.BlockSpec((None, tm, LANE), lambda b, j: (b, j, 0))],
        out_shape=[jax.ShapeDtypeStruct((nb, t, M2_XBC), F32), jax.ShapeDtypeStruct((nb, t, LANE), F32)],
        scratch_shapes=[pltpu.VMEM((tm + 2 * M2_HALO, M2_XBC), F32)],
        compiler_params=_cparams(("arbitrary", "arbitrary")),
        name="ssd_prep",
    )(pd, pd, pd, conv_w, conv_b, dtb)

    qn = SSD_BLOCK
    nchunks = t // qn
    nctx_chunks = nchunks - nlat // qn

    def fwd_chunk(k):
        return jnp.where(k < nctx_chunks, k + (nchunks - nctx_chunks), k - nctx_chunks)

    def rev_chunk(k):
        return nchunks - 1 - k

    def rows(n, chunk_of):
        return pl.BlockSpec((None, qn, n), lambda b, k: (b, chunk_of(k), 0))

    def direction(d):
        return pl.BlockSpec((None, None, 1, LANE), lambda b, k: (layer, d, 0, 0))

    return pl.pallas_call(
        _ssd_kernel,
        grid=(nb, nchunks),
        in_specs=[rows(M2_XBC, fwd_chunk), rows(LANE, fwd_chunk), rows(M2_XBC, rev_chunk), rows(LANE, rev_chunk),
                  direction(0), direction(1), _layer_spec(dsk, layer),
                  pl.BlockSpec((None, LANE, gw), lambda b, k: (0, 0, 0)),
                  pl.BlockSpec((None, LANE, gw), lambda b, k: (1, 0, 0))],
        out_specs=[rows(gw, fwd_chunk), rows(gw, rev_chunk)],
        out_shape=[jax.ShapeDtypeStruct((nb, t, gw), F32), jax.ShapeDtypeStruct((nb, t, gw), F32)],
        scratch_shapes=[pltpu.VMEM((M2_STATE, gw), F32), pltpu.VMEM((M2_STATE, gw), F32)],
        compiler_params=_cparams(("arbitrary", "arbitrary")),
        name="ssd_scan",
    )(xc, dtp, xc, dtp, a_dir, a_dir, dsk, expand, expand)


def _layer_norm_rows(v, g, b):
    mu = jnp.mean(v, axis=-1, keepdims=True)
    var = jnp.mean(jnp.square(v - mu), axis=-1, keepdims=True)
    return (v - mu) * lax.rsqrt(var + LN_EPS) * g + b


def _outproj_kernel(mabc_ref, yf_ref, yr_ref, z_ref, nw_ref, h_ref, gate_ref, w_ref, lg_ref, lb_ref, o_ref,
                    *, alpha):
    k1 = mabc_ref.shape[1]
    gc = GROUP_W // M2_GROUPS
    for r0 in range(0, o_ref.shape[0], ROW_TILE):
        rows = slice(r0, r0 + ROW_TILE)
        gv = (yf_ref[rows, :] + yr_ref[rows, :]) * _silu(z_ref[rows, :])
        md = []
        for g in range(M2_GROUPS):
            part = gv[:, g * gc:(g + 1) * gc]
            ms = jnp.mean(jnp.square(part), axis=-1, keepdims=True)
            md.append((part * lax.rsqrt(ms + RMS_EPS) * nw_ref[:, g * gc:(g + 1) * gc]).astype(BF16))
        mix = (_bdot(mabc_ref[rows, :].astype(BF16), w_ref[0:k1, :])
               + _bdot(jnp.concatenate(md, axis=1), w_ref[k1:w_ref.shape[0], :]))
        o_ref[rows, :] = _layer_norm_rows(alpha * h_ref[rows, :] + gate_ref[...] * mix, lg_ref[...], lb_ref[...])


def _outproj(mabc, yf, yr, pd, norm_w, h, mods, w, ln_g, ln_b, layer, nlat, alpha, latent):
    nb, rows, d = h.shape
    tm = OUT_ROW_TILE if latent else rows
    blk0 = 0 if latent else nlat // tm
    k1, k2 = mabc.shape[2], yf.shape[2]
    z_block = M2_XBC // k2

    def row(n):
        return pl.BlockSpec((None, tm, n), lambda b, j: (b, j + blk0, 0))

    return pl.pallas_call(
        functools.partial(_outproj_kernel, alpha=alpha),
        grid=(nb, rows // tm),
        in_specs=[row(k1), row(k2), row(k2), pl.BlockSpec((None, tm, k2), lambda b, j: (b, j + blk0, z_block)),
                  _layer_spec(norm_w, layer), pl.BlockSpec((None, tm, d), lambda b, j: (b, j, 0)),
                  pl.BlockSpec((None, None, 1, d), lambda b, j: (layer, b if latent else nb, 0, 2)),
                  _layer_spec(w, layer), _layer_spec(ln_g, layer), _layer_spec(ln_b, layer)],
        out_specs=pl.BlockSpec((None, tm, d), lambda b, j: (b, j, 0)),
        out_shape=jax.ShapeDtypeStruct((nb, rows, d), F32),
        compiler_params=_cparams(("arbitrary", "arbitrary")),
        name="outproj_lat" if latent else "outproj_ctx",
    )(mabc, yf, yr, pd, norm_w, h, mods, w, ln_g, ln_b)


def _ffn_kernel(*refs, alpha, grid_conv, halo, seq_len):
    if grid_conv:
        (h_ref, next_ref, sh_ref, sc_ref, gate_ref, wg_ref, wv_ref, cw_ref, cb_ref, wd_ref,
         lg_ref, lb_ref, o_ref, xm_scr, tail_scr, *chain_scr) = refs
    else:
        (h_ref, sh_ref, sc_ref, gate_ref, wg_ref, wv_ref, cw_ref, cb_ref, wd_ref,
         lg_ref, lb_ref, o_ref, xm_scr, tail_scr, *chain_scr) = refs
    g_scr, v_scr, side_scr = chain_scr[0::3], chain_scr[1::3], chain_scr[2::3]
    j = pl.program_id(1)
    f = pl.program_id(2)
    nj = pl.num_programs(1)
    nf = pl.num_programs(2)
    tm = h_ref.shape[0]
    sub = g_scr[0].shape[1]
    nsub = len(g_scr)
    pad = SUBLANE

    @pl.when(f == 0)
    def _():
        def modulated(v):
            return (v * (1.0 + sc_ref[...]) + sh_ref[...]).astype(BF16)
        xm_scr[0:tm, :] = modulated(h_ref[...])
        if grid_conv:
            xm_scr[tm:tm + halo, :] = modulated(next_ref[...])
        o_ref[...] = jnp.zeros_like(o_ref)
        for side in side_scr:
            side[:, 0:pad, :] = jnp.zeros((2, pad, sub), F32)
            side[:, pad + tm:pad + tm + pad, :] = jnp.zeros((2, pad, sub), F32)

    if grid_conv:
        @pl.when(j == 0)
        def _():
            tail_scr[f] = jnp.zeros(tail_scr.shape[1:], F32)

    pos = lax.broadcasted_iota(jnp.int32, (tm, 1), 0)
    if grid_conv:
        keep_next = (j < nj - 1).astype(F32)
        pos = pos % GRID_W
        left_ok, right_ok = pos != 0, pos != GRID_W - 1
        row_taps = (-1, 0, 1)
    else:
        pos = pos % seq_len
        left_ok, right_ok = pos != 0, pos != seq_len - 1
        row_taps = (0,)

    for si in range(nsub):
        cs = slice(si * sub, (si + 1) * sub)
        ge = _bdot(xm_scr[...], wg_ref[:, cs].astype(BF16))
        if grid_conv:
            g_scr[si][0:halo, :] = tail_scr[f, :, cs]
            g_scr[si][halo:halo + tm, :] = ge[0:tm]
            g_scr[si][halo + tm:halo + tm + halo, :] = ge[tm:tm + halo] * keep_next
            tail_scr[f, :, cs] = ge[tm - halo:tm]
        else:
            g_scr[si][...] = ge
        v_scr[si][...] = _bdot(xm_scr[0:tm, :], wv_ref[:, cs].astype(BF16))

    acts = []
    for si in range(nsub):
        cs = slice(si * sub, (si + 1) * sub)

        def tap_sum(dc):
            tot = None
            for dr in row_taps:
                k = (dr + 1) * 3 + (dc + 1)
                r0 = halo + dr * GRID_W
                term = g_scr[si][r0:r0 + tm, :] * cw_ref[k:k + 1, cs]
                tot = term if tot is None else tot + term
            return tot

        side_scr[si][0, pad:pad + tm, :] = tap_sum(-1)
        side_scr[si][1, pad:pad + tm, :] = tap_sum(1)
        conv = tap_sum(0) + cb_ref[:, cs]
        conv = conv + jnp.where(left_ok, side_scr[si][0, pad - 1:pad - 1 + tm, :], 0.0)
        conv = conv + jnp.where(right_ok, side_scr[si][1, pad + 1:pad + 1 + tm, :], 0.0)
        acts.append((_gelu(conv) * v_scr[si][...]).astype(BF16))
    for si in range(nsub):
        o_ref[...] += _bdot(acts[si], wd_ref[si * sub:(si + 1) * sub, :].astype(BF16))

    @pl.when(f == nf - 1)
    def _():
        o_ref[...] = _layer_norm_rows(alpha * h_ref[...] + gate_ref[...] * o_ref[...], lg_ref[...], lb_ref[...])


def _ffn(h, mods, w_up, conv_w9, conv_b, w_down, ln_g, ln_b, layer, alpha, grid_conv):
    nb, rows, d = h.shape
    fh = w_down.shape[1]
    tf = FFN_COL_TILE
    sub = FFN_SUB_TILE
    nf = fh // tf
    if grid_conv:
        hx = h
        tm, halo = FFN_ROW_TILE, GRID_W
    else:
        hx = h.reshape(1, nb * rows, d)
        tm, halo = nb * rows, 0
    nblk = hx.shape[1] // tm
    hb = tm // GRID_W
    nhalo_blocks = rows // GRID_W

    def vec(k):
        return pl.BlockSpec((None, None, 1, d), lambda b, j, f: (layer, b if grid_conv else nb, 0, k))

    in_specs = [pl.BlockSpec((None, tm, d), lambda b, j, f: (b, j, 0))]
    args = [hx]
    if grid_conv:
        in_specs += [
            pl.BlockSpec((None, halo, d), lambda b, j, f: (b, jnp.minimum((j + 1) * hb, nhalo_blocks - 1), 0)),
        ]
        args += [hx]
    in_specs += [
        vec(3), vec(4), vec(5),
        pl.BlockSpec((None, d, tf), lambda b, j, f: (layer, 0, f)),
        pl.BlockSpec((None, d, tf), lambda b, j, f: (layer, 0, nf + f)),
        pl.BlockSpec((None, 9, tf), lambda b, j, f: (layer, 0, f)),
        pl.BlockSpec((None, 1, tf), lambda b, j, f: (layer, 0, f)),
        pl.BlockSpec((None, tf, d), lambda b, j, f: (layer, f, 0)),
        _layer_spec(ln_g, layer), _layer_spec(ln_b, layer),
    ]
    args += [mods, mods, mods, w_up, w_up, conv_w9, conv_b, w_down, ln_g, ln_b]
    return pl.pallas_call(
        functools.partial(_ffn_kernel, alpha=alpha, grid_conv=grid_conv, halo=halo, seq_len=rows),
        grid=(hx.shape[0], nblk, nf),
        in_specs=in_specs,
        out_specs=pl.BlockSpec((None, tm, d), lambda b, j, f: (b, j, 0)),
        out_shape=jax.ShapeDtypeStruct(hx.shape, F32),
        scratch_shapes=[pltpu.VMEM((tm + halo, d), BF16), pltpu.VMEM((nf, max(halo, SUBLANE), tf), F32)]
        + [pltpu.VMEM((tm + 2 * halo, sub), F32), pltpu.VMEM((tm, sub), F32),
           pltpu.VMEM((2, tm + 2 * SUBLANE, sub), F32)] * (tf // sub),
        compiler_params=_cparams(("arbitrary", "arbitrary", "arbitrary")),
        name="ffn_lat" if grid_conv else "ffn_ctx",
    )(*args).reshape(h.shape)


def kernel(x, c, ctx, c_ctx, w_ada, b_ada, w_in, w_out, ln1_g, ln1_b, ln2_g, ln2_b, s5_a_re, s5_a_im, s5_b_re, s5_b_im, s5_c_re, s5_c_im, s5_log_step, s5_d, s5_glu_w, s5_glu_b, sg_ln_g, sg_ln_b, sg_w, sg_b, pool_w, pool_b, pool_scale, m2_conv_w, m2_conv_b, m2_dt_bias, m2_a_log, m2_d, m2_norm_w, ffn_w_up, ffn_conv_w, ffn_conv_b, ffn_w_down):
    nb, nlat, d = x.shape
    nctx = ctx.shape[1]
    depth = w_ada.shape[0]
    gw = GROUP_W
    alpha = (2 * depth) ** 0.25
    assert nb + 1 <= MOD_ROWS and nlat % FFN_ROW_TILE == 0 and nlat % OUT_ROW_TILE == 0 and nctx % ROW_TILE == 0
    assert nlat % (GRID_W * SUBLANE) == 0 and FFN_ROW_TILE % GRID_W == 0 and nlat % SSD_BLOCK == 0 and nctx % SSD_BLOCK == 0
    assert w_in.shape[2] == 5 * gw + M2_XBC + 2 * M2_HEADS and (5 * gw + M2_XBC) % ROW_TILE == 0

    cmat = jnp.concatenate([c, c_ctx[None, :], jnp.zeros((MOD_ROWS - nb - 1, d), F32)], axis=0)
    mods = _ada(cmat, w_ada, b_ada).reshape(depth, MOD_ROWS, 1, 6 * d)

    def rows(v):
        return v.astype(F32).reshape(depth, 1, -1)

    fh = ffn_w_down.shape[1]
    lane_pad = LANE - 2 * M2_HEADS
    w_in_b = _cast_w_in(w_in)
    w_dt_b = jnp.pad(w_in[:, :, w_in.shape[2] - 2 * M2_HEADS:], ((0, 0), (0, 0), (0, lane_pad))).astype(BF16)
    w_out_b = w_out.astype(BF16)
    cw9 = ffn_conv_w.astype(F32).reshape(depth, 9, fh)
    s5_mats = _s5_mats(s5_a_re, s5_a_im, s5_b_re, s5_b_im, s5_c_re, s5_c_im, s5_log_step, s5_d)
    sg_btile = jnp.repeat(jnp.swapaxes(sg_b.astype(F32), 1, 2), gw // SG_HEADS, axis=2)
    mixer_params = (s5_glu_w.astype(BF16), rows(s5_glu_b), rows(sg_ln_g), rows(sg_ln_b), sg_w.astype(BF16),
                    sg_btile, pool_w.astype(BF16), rows(pool_b), rows(pool_scale))
    m2_a = -jnp.exp(m2_a_log.astype(F32))
    a_dir = (m2_a[:, :, None, :] * jnp.eye(2, dtype=F32)[None, :, :, None]).reshape(depth, 2, 1, 2 * M2_HEADS)
    a_dir = jnp.pad(a_dir, ((0, 0), (0, 0), (0, 0), (0, lane_pad)))
    head_of_ch = jnp.arange(gw) // M2_HEAD_DIM
    expand = (jnp.arange(LANE)[None, :, None] == (jnp.arange(2)[:, None, None] * M2_HEADS + head_of_ch[None, None, :])
              ).astype(BF16)
    ssd_params = (m2_conv_w.astype(F32), rows(m2_conv_b),
                  jnp.pad(m2_dt_bias.astype(F32).reshape(depth, 1, 2 * M2_HEADS), ((0, 0), (0, 0), (0, lane_pad))),
                  a_dir, rows(jnp.repeat(m2_d.astype(F32), M2_HEAD_DIM, axis=1)), expand)
    m2_nw = rows(m2_norm_w)
    ln1 = rows(ln1_g), rows(ln1_b)
    ln2 = rows(ln2_g), rows(ln2_b)
    ffn_cb = rows(ffn_conv_b)

    h_lat, h_ctx = x, ctx
    for i in range(depth):
        need_ctx = i < depth - 1
        pa, pb, pd = _inproj(h_lat, h_ctx, mods, w_in_b, w_dt_b, i)
        ya = _s5(pa, s5_mats, i, nlat)
        mabc = _mixers(ya, pb, mixer_params, i, nlat)
        yf, yr = _ssd(pd, ssd_params, i, nlat)
        mix_in = (mabc, yf, yr, pd, m2_nw)
        h1_lat = _outproj(*mix_in, h_lat, mods, w_out_b, ln1[0], ln1[1], i, nlat, alpha, True)
        if need_ctx:
            h1_ctx = _outproj(*mix_in, h_ctx, mods, w_out_b, ln1[0], ln1[1], i, nlat, alpha, False)
            h_ctx = _ffn(h1_ctx, mods, ffn_w_up, cw9, ffn_cb, ffn_w_down, ln2[0], ln2[1], i, alpha, False)
        h_lat = _ffn(h1_lat, mods, ffn_w_up, cw9, ffn_cb, ffn_w_down, ln2[0], ln2[1], i, alpha, True)
    return h_lat
```

```python
import functools

import jax
import jax.numpy as jnp
from jax import lax
from jax.experimental import pallas as pl
from jax.experimental.pallas import tpu as pltpu

F32 = jnp.float32
BF16 = jnp.bfloat16

GRID_W = 64
GROUP_W = 512
S5_CH = 16
S5_Q = 16
S5_RELAYOUT_UNROLL = 4
SG_HEADS = 4
SG_CHUNK = 128
POOL_WINDOWS = (2, 4, 8, 16)
POOL_DIM = GROUP_W // len(POOL_WINDOWS)
POOL_HALO = 8
M2_HEAD_DIM = 64
M2_HEADS = GROUP_W // M2_HEAD_DIM
M2_STATE = 128
M2_GROUPS = 2
M2_CONV = 4
M2_CHUNK = 128
M2_XBC = GROUP_W + 2 * M2_GROUPS * M2_STATE
M2_HALO = 8
SSD_BLOCK = 2 * M2_CHUNK
LN_EPS = 1e-5
RMS_EPS = 1e-5

LANE = 128
SUBLANE = 8
VMEM_LIMIT = 56 * 1024 * 1024
ROW_TILE = 256
OUT_ROW_TILE = 512
FFN_ROW_TILE = 512
FFN_COL_TILE = 512
FFN_SUB_TILE = 256
ADA_COL_TILE = 1024
MOD_ROWS = 8


def _cparams(sem):
    return pltpu.CompilerParams(dimension_semantics=sem, vmem_limit_bytes=VMEM_LIMIT)


def _bdot(a, b):
    return jnp.dot(a, b, preferred_element_type=F32)


def _gelu(x):
    return jax.nn.gelu(x)


def _silu(x):
    return x * jax.nn.sigmoid(x)


def _ada_kernel(c_ref, w_ref, b_ref, o_ref):
    c = c_ref[...]
    s = _silu(c).astype(BF16)
    o_ref[...] = _bdot(s, w_ref[...].astype(BF16)) + b_ref[...]


def _ada(cmat, w_ada, b_ada):
    depth, d, n = w_ada.shape
    tn = ADA_COL_TILE
    return pl.pallas_call(
        _ada_kernel,
        grid=(depth, n // tn),
        in_specs=[
            pl.BlockSpec((MOD_ROWS, d), lambda l, j: (0, 0)),
            pl.BlockSpec((None, d, tn), lambda l, j: (l, 0, j)),
            pl.BlockSpec((None, 1, tn), lambda l, j: (l, 0, j)),
        ],
        out_specs=pl.BlockSpec((None, MOD_ROWS, tn), lambda l, j: (l, 0, j)),
        out_shape=jax.ShapeDtypeStruct((depth, MOD_ROWS, n), F32),
        compiler_params=_cparams(("arbitrary", "arbitrary")),
        name="ada",
    )(cmat, w_ada, b_ada.reshape(depth, 1, n))


def _mod_spec(d, k, layer, nbatch, nlat_blocks):
    return pl.BlockSpec((None, None, 1, d),
                        lambda b, j, *_: (layer, jnp.where(j >= nlat_blocks, nbatch, b), 0, k))


def _layer_spec(arr, layer):
    tail = arr.shape[1:]
    return pl.BlockSpec((None,) + tail, lambda *_: (layer,) + (0,) * len(tail))


def _cast_w_in_kernel(wt_ref, wb_ref):
    wb_ref[...] = wt_ref[...].T.astype(BF16)


def _cast_w_in(w_in):
    depth, d, n = w_in.shape
    tn = ROW_TILE
    wt = jnp.swapaxes(w_in, 1, 2)
    return pl.pallas_call(
        _cast_w_in_kernel,
        grid=(depth, n // tn),
        in_specs=[pl.BlockSpec((None, tn, d), lambda l, i: (l, i, 0))],
        out_specs=pl.BlockSpec((None, d, tn), lambda l, i: (l, 0, i)),
        out_shape=jax.ShapeDtypeStruct((depth, d, n // tn * tn), BF16),
        compiler_params=_cparams(("arbitrary", "arbitrary")),
        name="cast_w_in",
    )(wt)


def _stream_specs(tm, d, nlat_blocks):
    lat = pl.BlockSpec((None, tm, d), lambda b, j: (b, jnp.minimum(j, nlat_blocks - 1), 0))
    ctx = pl.BlockSpec((None, tm, d), lambda b, j: (b, jnp.maximum(j - nlat_blocks, 0), 0))
    return lat, ctx


def _inproj_kernel(hl_ref, hc_ref, sh_ref, sc_ref, w_ref, wdt_ref, oa_ref, ob_ref, od_ref, xm_scr, *, nlat_blocks):
    h = jnp.where(pl.program_id(1) < nlat_blocks, hl_ref[...], hc_ref[...])
    xm_scr[...] = (h * (1.0 + sc_ref[...]) + sh_ref[...]).astype(BF16)

    xm = xm_scr[...]
    gw = GROUP_W
    oa_ref[...] = _bdot(xm, w_ref[:, 0:gw])
    ob_ref[...] = _bdot(xm, w_ref[:, gw:4 * gw])
    od_ref[:, 0:M2_XBC] = _bdot(xm, w_ref[:, 5 * gw:5 * gw + M2_XBC])
    od_ref[:, M2_XBC:M2_XBC + gw] = _bdot(xm, w_ref[:, 4 * gw:5 * gw])
    od_ref[:, M2_XBC + gw:M2_XBC + gw + LANE] = _bdot(xm, wdt_ref[...])


def _inproj(h_lat, h_ctx, mods, w, w_dt, layer):
    nb, nlat, d = h_lat.shape
    t = nlat + h_ctx.shape[1]
    tm = ROW_TILE
    na, nbw, nd = GROUP_W, 3 * GROUP_W, M2_XBC + GROUP_W + LANE
    lat_spec, ctx_spec = _stream_specs(tm, d, nlat // tm)

    def row(n):
        return pl.BlockSpec((None, tm, n), lambda b, j: (b, j, 0))

    return pl.pallas_call(
        functools.partial(_inproj_kernel, nlat_blocks=nlat // tm),
        grid=(nb, t // tm),
        in_specs=[lat_spec, ctx_spec, _mod_spec(d, 0, layer, nb, nlat // tm),
                  _mod_spec(d, 1, layer, nb, nlat // tm), _layer_spec(w, layer), _layer_spec(w_dt, layer)],
        out_specs=[row(na), row(nbw), row(nd)],
        out_shape=[jax.ShapeDtypeStruct((nb, t, n), F32) for n in (na, nbw, nd)],
        scratch_shapes=[pltpu.VMEM((tm, d), BF16)],
        compiler_params=_cparams(("arbitrary", "arbitrary")),
        name="inproj",
    )(h_lat, h_ctx, mods, mods, w, w_dt)


def _s5_mats(a_re, a_im, b_re, b_im, c_re, c_im, log_step, d_skip):
    q, hch = S5_Q, S5_CH
    nl, _, g, p = a_re.shape
    lre, lim = a_re.astype(F32), a_im.astype(F32)
    step = jnp.exp(log_step.astype(F32))[..., None]
    mag = jnp.exp(lre * step)
    ar, ai = mag * jnp.cos(lim * step), mag * jnp.sin(lim * step)
    den = lre * lre + lim * lim
    qr = ((ar - 1.0) * lre + ai * lim) / den
    qi = (ai * lre - (ar - 1.0) * lim) / den
    bre, bim = b_re.astype(F32), b_im.astype(F32)
    bbr = qr[..., None] * bre - qi[..., None] * bim
    bbi = qr[..., None] * bim + qi[..., None] * bre
    cre, cim = c_re.astype(F32), c_im.astype(F32)

    pr, pi = ar[:, :, :, None], ai[:, :, :, None]
    while pr.shape[3] < q:
        lr, li = pr[:, :, :, -1:], pi[:, :, :, -1:]
        pr, pi = (jnp.concatenate([pr, pr * lr - pi * li], axis=3), jnp.concatenate([pi, pr * li + pi * lr], axis=3))
    pk_re = jnp.concatenate([jnp.ones_like(pr[:, :, :, :1]), pr], axis=3)
    pk_im = jnp.concatenate([jnp.zeros_like(pi[:, :, :, :1]), pi], axis=3)
    pt_re, pt_im = jnp.swapaxes(pk_re, 3, 4), jnp.swapaxes(pk_im, 3, 4)

    def lag_kernels(d, pr, pi):
        ctr = jnp.swapaxes(cre[:, d], -1, -2)[:, :, :, None]
        cti = jnp.swapaxes(cim[:, d], -1, -2)[:, :, :, None]
        ca_re = (ctr * pr[..., None] - cti * pi[..., None]).reshape(nl, g, p, 1, q * hch)
        ca_im = (ctr * pi[..., None] + cti * pr[..., None]).reshape(nl, g, p, 1, q * hch)
        return jnp.sum(bbr[:, d][..., None] * ca_re - bbi[:, d][..., None] * ca_im, axis=2)

    lag_f = lag_kernels(0, pt_re[:, 0, :, :, :q], pt_im[:, 0, :, :, :q])
    lag_r = lag_kernels(1, jnp.flip(pt_re[:, 1, :, :, :q], axis=3), jnp.flip(pt_im[:, 1, :, :, :q], axis=3))
    dmat = jnp.eye(hch, dtype=F32) * d_skip.astype(F32).reshape(nl, g, 1, hch)
    mid = lag_r[..., (q - 1) * hch:] + lag_f[..., :hch] + dmat
    kk = jnp.concatenate([lag_r[..., :(q - 1) * hch], mid, lag_f[..., hch:],
                          jnp.zeros(mid.shape[:-1] + (hch,), F32)], axis=-1)

    def state_in(d, pr, pi):
        btr = jnp.swapaxes(bbr[:, d], -1, -2)[:, :, None]
        bti = jnp.swapaxes(bbi[:, d], -1, -2)[:, :, None]
        pr, pi = pr[:, :, :, None], pi[:, :, :, None]
        m = jnp.concatenate([pr * btr - pi * bti, pr * bti + pi * btr], axis=-1)
        return m.reshape(nl, g, q * hch, 2 * p)

    sb = jnp.concatenate([state_in(0, jnp.flip(pk_re[:, 0, :, :q], axis=2), jnp.flip(pk_im[:, 0, :, :q], axis=2)),
                          state_in(1, pk_re[:, 1, :, :q], pk_im[:, 1, :, :q])], axis=-1)

    def state_out(d, pr, pi):
        ctr = jnp.swapaxes(cre[:, d], -1, -2)[:, :, :, None]
        cti = jnp.swapaxes(cim[:, d], -1, -2)[:, :, :, None]
        pr, pi = pr[..., None], pi[..., None]
        m = jnp.concatenate([ctr * pr - cti * pi, -(ctr * pi + cti * pr)], axis=2)
        return m.reshape(nl, g, 2 * p, q * hch)

    w2 = jnp.concatenate([state_out(0, pt_re[:, 0, :, :, 1:], pt_im[:, 0, :, :, 1:]),
                          state_out(1, jnp.flip(pt_re[:, 1, :, :, 1:], axis=3), jnp.flip(pt_im[:, 1, :, :, 1:], axis=3))],
                         axis=2)

    aqr, aqi = pk_re[:, :, :, q], pk_im[:, :, :, q]
    a1 = jnp.concatenate([aqr, aqr], axis=-1).reshape(nl, 2, g * 2 * p)
    a2 = jnp.concatenate([-aqi, aqi], axis=-1).reshape(nl, 2, g * 2 * p)
    acoef = jnp.stack([a1[:, 0], a2[:, 0], a1[:, 1], a2[:, 1]], axis=1)
    return kk, sb, w2, acoef


def _granule_transpose(v):
    n = len(v)
    slot = lax.broadcasted_iota(jnp.int32, v[0].shape, 1) // S5_CH
    v = list(v)
    j = n // 2
    while j >= 1:
        keep = (slot & j) == 0
        for i in range(n):
            if i & j == 0:
                a, b = v[i], v[i + j]
                v[i] = jnp.where(keep, a, pltpu.roll(b, j * S5_CH, axis=1))
                v[i + j] = jnp.where(keep, pltpu.roll(a, LANE - j * S5_CH, axis=1), b)
        j //= 2
    return v


def _split_bf16(x):
    hi = x.astype(BF16)
    return hi, (x - hi.astype(F32)).astype(BF16)


def _dot3(xs, w):
    x_hi, x_lo = xs
    w_hi, w_lo = _split_bf16(w)
    return _bdot(x_hi, w_hi) + (_bdot(x_lo, w_hi) + _bdot(x_hi, w_lo))


def _s5_kernel(pa_ref, kk_ref, sb_ref, w2_ref, ac_ref, ya_ref, x_scr, s_scr, h_scr, t_scr, *, nbatch, nrows, nctx):
    gb = LANE // S5_CH
    halves = S5_Q // gb
    qh = S5_Q * S5_CH
    sw = LANE
    nlat = nrows - nctx
    nrc = nrows // SUBLANE
    tok_per_rc = SUBLANE * S5_Q

    def relayout(it, to_rows):
        for u in range(S5_RELAYOUT_UNROLL):
            idx = it * S5_RELAYOUT_UNROLL + u
            b = idx // nrc
            rc = idx % nrc
            r0 = pl.multiple_of(b * nrows + rc * SUBLANE, SUBLANE)
            for half in range(halves):
                lanes = slice(half * LANE, (half + 1) * LANE)

                def tok(s8, half=half, rc=rc):
                    return pl.ds(rc * tok_per_rc + half * gb + s8, SUBLANE, stride=S5_Q)

                if to_rows:
                    v = _granule_transpose([pa_ref[b, tok(s8), :] for s8 in range(gb)])
                    for gi in range(gb):
                        x_scr[gi, pl.ds(r0, SUBLANE), lanes] = v[gi]
                else:
                    v = _granule_transpose([x_scr[gi, pl.ds(r0, SUBLANE), lanes] for gi in range(gb)])
                    for s8 in range(gb):
                        ya_ref[b, tok(s8), :] = v[s8]
        return 0

    nblocks = nbatch * nrc // S5_RELAYOUT_UNROLL
    lax.fori_loop(0, nblocks, lambda it, c: relayout(it, True), 0)

    for gi in range(gb):
        kkv = kk_ref[gi]
        for s in range(S5_Q):
            shift = (S5_Q - 1 - s) * S5_CH
            t_scr[s * S5_CH:(s + 1) * S5_CH, :] = pltpu.roll(kkv, kkv.shape[1] - shift, axis=1)[:, 0:qh]
        xs = _split_bf16(x_scr[gi])
        x_scr[gi] = _dot3(xs, t_scr[...])
        rs = _dot3(xs, sb_ref[gi])
        for d in range(2):
            sd = rs[:, d * sw:(d + 1) * sw]
            s_scr[d, gi] = sd
            s_scr[2 + d, gi] = pltpu.roll(sd, sw // 2, axis=1)

    zero = jnp.zeros((nbatch, sw), F32)

    def body(s, carry):
        f_rows = pl.ds(jnp.where(s < nctx, s + nlat, s - nctx), nbatch, stride=nrows)
        r_rows = pl.ds(nrows - 1 - s, nbatch, stride=nrows)
        out = []
        for gi in range(gb):
            hf, hfs, hr, hrs = carry[4 * gi:4 * gi + 4]
            a1f, a2f, a1r, a2r = (ac_ref[k:k + 1, gi * sw:(gi + 1) * sw] for k in range(4))
            h_scr[0, gi, f_rows, :] = hf
            h_scr[1, gi, r_rows, :] = hr
            out += [a1f * hf + a2f * hfs + s_scr[0, gi, f_rows, :],
                    a1f * hfs - a2f * hf + s_scr[2, gi, f_rows, :],
                    a1r * hr + a2r * hrs + s_scr[1, gi, r_rows, :],
                    a1r * hrs - a2r * hr + s_scr[3, gi, r_rows, :]]
        return tuple(out)

    lax.fori_loop(0, nrows, body, (zero,) * (4 * gb))

    for gi in range(gb):
        hin = jnp.concatenate([h_scr[0, gi], h_scr[1, gi]], axis=1)
        x_scr[gi] = x_scr[gi] + _dot3(_split_bf16(hin), w2_ref[gi])

    lax.fori_loop(0, nblocks, lambda it, c: relayout(it, False), 0)


def _s5(pa, mats, layer, nlat):
    kk, sb, w2, acoef = mats
    nb, t, gw = pa.shape
    q = S5_Q
    nrows = t // q
    qh = q * S5_CH
    gb = LANE // S5_CH
    assert nrows % SUBLANE == 0 and q % gb == 0 and w2.shape[2] == 2 * LANE and sb.shape[3] == 2 * LANE
    assert (nb * nrows // SUBLANE) % S5_RELAYOUT_UNROLL == 0
    return pl.pallas_call(
        functools.partial(_s5_kernel, nbatch=nb, nrows=nrows, nctx=(t - nlat) // q),
        grid=(gw // LANE,),
        in_specs=[
            pl.BlockSpec((nb, t, LANE), lambda i: (0, 0, i)),
            pl.BlockSpec((None, gb, S5_CH, 2 * qh), lambda i: (layer, i, 0, 0)),
            pl.BlockSpec((None, gb, qh, 2 * LANE), lambda i: (layer, i, 0, 0)),
            pl.BlockSpec((None, gb, 2 * LANE, qh), lambda i: (layer, i, 0, 0)),
            pl.BlockSpec((None, 4, gb * LANE), lambda i: (layer, 0, i)),
        ],
        out_specs=pl.BlockSpec((nb, t, LANE), lambda i: (0, 0, i)),
        out_shape=jax.ShapeDtypeStruct((nb, t, gw), F32),
        scratch_shapes=[pltpu.VMEM((gb, nb * nrows, qh), F32),
                        pltpu.VMEM((4, gb, nb * nrows, LANE), F32),
                        pltpu.VMEM((2, gb, nb * nrows, LANE), F32),
                        pltpu.VMEM((qh, qh), F32)],
        compiler_params=_cparams(("arbitrary",)),
        name="s5",
    )(pa, kk, sb, w2, acoef)


def _mixers_kernel(ya_ref, pb_ref, prev_ref, next_ref, gluw_ref, glub_ref, lng_ref, lnb_ref, sgw_ref,
                   sgb_ref, pw_ref, pbias_ref, pscale_ref, o_ref, ext_scr, *, nlat_blocks, nblocks, nlat, nctx):
    j = pl.program_id(1)
    tm = ya_ref.shape[0]
    gw = GROUP_W

    z = _gelu(ya_ref[...])
    gate = jax.nn.sigmoid(_bdot(z.astype(BF16), gluw_ref[...]) + glub_ref[...])
    o_ref[:, 0:gw] = z * gate

    hd = gw // SG_HEADS
    u = _gelu(pb_ref[:, 0:gw])
    v = _gelu(pb_ref[:, gw:2 * gw])
    for hh in range(SG_HEADS):
        vh = v[:, hh * hd:(hh + 1) * hd]
        mu = jnp.mean(vh, axis=-1, keepdims=True)
        var = jnp.mean(jnp.square(vh - mu), axis=-1, keepdims=True)
        vn = ((vh - mu) * lax.rsqrt(var + LN_EPS) * lng_ref[:, hh * hd:(hh + 1) * hd]
              + lnb_ref[:, hh * hd:(hh + 1) * hd]).astype(BF16)
        for cchunk in range(tm // SG_CHUNK):
            rows = slice(cchunk * SG_CHUNK, (cchunk + 1) * SG_CHUNK)
            s = _bdot(sgw_ref[hh], vn[rows]) + sgb_ref[:, hh * hd:(hh + 1) * hd]
            o_ref[rows, gw + hh * hd:gw + (hh + 1) * hd] = u[rows, hh * hd:(hh + 1) * hd] * s

    is_first = jnp.logical_or(j == 0, j == nlat_blocks)
    is_last = jnp.logical_or(j == nlat_blocks - 1, j == nblocks - 1)
    p = pb_ref[:, 2 * gw:3 * gw]
    hal = POOL_HALO
    ext_scr[0:hal, :] = jnp.where(is_first, 0.0, prev_ref[...])
    ext_scr[hal:hal + tm, :] = p
    ext_scr[hal + tm:hal + tm + hal, :] = jnp.where(is_last, 0.0, next_ref[...])
    in_ctx = j >= nlat_blocks
    seq_len = jnp.where(in_ctx, nctx, nlat)
    t0 = (j - jnp.where(in_ctx, nlat_blocks, 0)) * tm
    tpos = t0 + lax.broadcasted_iota(jnp.int32, (tm, 1), 0)
    pd = POOL_DIM
    for gi, win in enumerate(POOL_WINDOWS):
        cols = slice(gi * pd, (gi + 1) * pd)
        tot = jnp.zeros((tm, pd), F32)
        for off in range(-(win // 2), win // 2):
            tot = tot + ext_scr[hal + off:hal + off + tm, cols]
        lo = jnp.maximum(tpos - win // 2, 0)
        hi = jnp.minimum(tpos + win // 2 - 1, seq_len - 1)
        mean = tot / (hi - lo + 1).astype(F32)
        yv = _bdot((mean - p[:, cols]).astype(BF16), pw_ref[gi]) + pbias_ref[:, cols]
        o_ref[:, 2 * gw + gi * pd:2 * gw + (gi + 1) * pd] = yv * pscale_ref[:, cols]


def _mixers(ya, pb, params, layer, nlat):
    nb, t, gw = ya.shape
    tm = ROW_TILE
    nblocks = t // tm
    nlat_blocks = nlat // tm
    hb = tm // POOL_HALO
    nh = t // POOL_HALO

    return pl.pallas_call(
        functools.partial(_mixers_kernel, nlat_blocks=nlat_blocks, nblocks=nblocks, nlat=nlat, nctx=t - nlat),
        grid=(nb, nblocks),
        in_specs=[
            pl.BlockSpec((None, tm, gw), lambda b, j: (b, j, 0)),
            pl.BlockSpec((None, tm, 3 * gw), lambda b, j: (b, j, 0)),
            pl.BlockSpec((None, POOL_HALO, gw), lambda b, j: (b, jnp.maximum(j * hb - 1, 0), 2)),
            pl.BlockSpec((None, POOL_HALO, gw), lambda b, j: (b, jnp.minimum((j + 1) * hb, nh - 1), 2)),
        ] + [_layer_spec(v, layer) for v in params],
        out_specs=pl.BlockSpec((None, tm, 3 * gw), lambda b, j: (b, j, 0)),
        out_shape=jax.ShapeDtypeStruct((nb, t, 3 * gw), F32),
        scratch_shapes=[pltpu.VMEM((tm + 2 * POOL_HALO, gw), F32)],
        compiler_params=_cparams(("arbitrary", "arbitrary")),
        name="mixers",
    )(ya, pb, pb, pb, *params)


def _ssd_prep_kernel(pd_ref, prev_ref, next_ref, cw_ref, cb_ref, dtb_ref, xc_ref, dt_ref, ext_scr,
                     *, nlat_blocks, nblocks):
    j = pl.program_id(1)
    tm = pd_ref.shape[0]
    nx = M2_XBC
    hal = M2_HALO
    is_first = jnp.logical_or(j == 0, j == nlat_blocks)
    is_last = jnp.logical_or(j == nlat_blocks - 1, j == nblocks - 1)
    ext_scr[0:hal, :] = jnp.where(is_first, 0.0, prev_ref[...])
    ext_scr[hal:hal + tm, :] = pd_ref[:, 0:nx]
    ext_scr[hal + tm:hal + tm + hal, :] = jnp.where(is_last, 0.0, next_ref[...])
    acc = jnp.zeros((tm, nx), F32) + cb_ref[...]
    for tap in range(M2_CONV):
        off = hal + tap - M2_CONV // 2
        acc = acc + ext_scr[off:off + tm, :] * cw_ref[tap:tap + 1, :]
    xc_ref[...] = _silu(acc)
    xdt = pd_ref[:, nx + GROUP_W:nx + GROUP_W + LANE] + dtb_ref[...]
    dt_ref[...] = jnp.maximum(xdt, 0.0) + jnp.log1p(jnp.exp(-jnp.abs(xdt)))


def _split3_bf16(x):
    x1 = x.astype(BF16)
    r1 = x - x1.astype(F32)
    x2 = r1.astype(BF16)
    return x1, x2, (r1 - x2.astype(F32)).astype(BF16)


def _ssd_chunk(reverse, rows, xc_ref, dt_ref, a_ref, dsk_ref, e_ref, st_scr, o_ref):
    qn = M2_CHUNK
    gw = GROUP_W
    lane0 = M2_HEADS if reverse else 0
    xs = xc_ref[rows, 0:gw]
    nbc = M2_GROUPS * M2_STATE
    bm = xc_ref[rows, gw:gw + nbc]
    cm = xc_ref[rows, gw + nbc:gw + 2 * nbc]

    dt = dt_ref[rows, :]
    da = dt * a_ref[...]
    ri = lax.broadcasted_iota(jnp.int32, (qn, qn), 0)
    ci = lax.broadcasted_iota(jnp.int32, (qn, qn), 1)
    causal = (ri <= ci) if reverse else (ri >= ci)
    mask = causal.astype(BF16)
    d1, d2, d3 = _split3_bf16(da)
    a_col = _bdot(mask, d1) + (_bdot(mask, d2) + _bdot(mask, d3))
    a_row = a_col.T
    tot_row = a_col[0:1, :] if reverse else a_col[qn - 1:qn, :]

    per_head = jnp.concatenate([dt, jnp.exp(a_col), jnp.exp(tot_row - a_col)], axis=0)
    p1, p2 = _split_bf16(per_head)
    e = e_ref[...]
    per_ch = _bdot(p1, e) + _bdot(p2, e)
    dt_x, ea_x, de_x = per_ch[0:qn], per_ch[qn:2 * qn], per_ch[2 * qn:3 * qn]
    et_x = ea_x[0:1, :] if reverse else ea_x[qn - 1:qn, :]
    xd = xs * dt_x
    xdw = (xd * de_x).astype(BF16)
    xdb = xd.astype(BF16)

    hpg = M2_HEADS // M2_GROUPS
    gcols = hpg * M2_HEAD_DIM
    for g in range(M2_GROUPS):
        bg = bm[:, g * M2_STATE:(g + 1) * M2_STATE]
        cg = cm[:, g * M2_STATE:(g + 1) * M2_STATE].astype(BF16)
        bgt = bg.T.astype(BF16)
        scores = _bdot(cg, bgt)
        st_old = st_scr[:, g * gcols:(g + 1) * gcols]
        y_inter = _bdot(cg, st_old.astype(BF16)) * ea_x[:, g * gcols:(g + 1) * gcols]
        st_new = _bdot(bgt, xdw[:, g * gcols:(g + 1) * gcols])
        st_scr[:, g * gcols:(g + 1) * gcols] = et_x[:, g * gcols:(g + 1) * gcols] * st_old + st_new
        for hl in range(hpg):
            hh = g * hpg + hl
            seg = a_col[:, lane0 + hh:lane0 + hh + 1] - a_row[lane0 + hh:lane0 + hh + 1, :]
            decay = jnp.exp(jnp.where(causal, seg, -jnp.inf))
            cols = slice(hh * M2_HEAD_DIM, (hh + 1) * M2_HEAD_DIM)
            y_h = _bdot((scores * decay).astype(BF16), xdb[:, cols])
            y_h = y_h + y_inter[:, hl * M2_HEAD_DIM:(hl + 1) * M2_HEAD_DIM]
            if reverse:
                o_ref[rows, cols] = y_h
            else:
                o_ref[rows, cols] = y_h + dsk_ref[:, cols] * xs[:, cols]


def _ssd_kernel(xcf_ref, dtf_ref, xcr_ref, dtr_ref, af_ref, ar_ref, dsk_ref, ef_ref, er_ref,
                yf_ref, yr_ref, stf_scr, str_scr):
    @pl.when(pl.program_id(1) == 0)
    def _():
        stf_scr[...] = jnp.zeros_like(stf_scr)
        str_scr[...] = jnp.zeros_like(str_scr)

    nchunk = yf_ref.shape[0] // M2_CHUNK
    for c in range(nchunk):
        rows_f = slice(c * M2_CHUNK, (c + 1) * M2_CHUNK)
        rows_r = slice((nchunk - 1 - c) * M2_CHUNK, (nchunk - c) * M2_CHUNK)
        _ssd_chunk(False, rows_f, xcf_ref, dtf_ref, af_ref, dsk_ref, ef_ref, stf_scr, yf_ref)
        _ssd_chunk(True, rows_r, xcr_ref, dtr_ref, ar_ref, dsk_ref, er_ref, str_scr, yr_ref)


def _ssd(pd, params, layer, nlat):
    conv_w, conv_b, dtb, a_dir, dsk, expand = params
    nb, t, wd = pd.shape
    gw = GROUP_W
    tm = ROW_TILE
    hb = tm // M2_HALO
    nh = t // M2_HALO
    xc, dtp = pl.pallas_call(
        functools.partial(_ssd_prep_kernel, nlat_blocks=nlat // tm, nblocks=t // tm),
        grid=(nb, t // tm),
        in_specs=[
            pl.BlockSpec((None, tm, wd), lambda b, j: (b, j, 0)),
            pl.BlockSpec((None, M2_HALO, M2_XBC), lambda b, j: (b, jnp.maximum(j * hb - 1, 0), 0)),
            pl.BlockSpec((None, M2_HALO, M2_XBC), lambda b, j: (b, jnp.minimum((j + 1) * hb, nh - 1), 0)),
            _layer_spec(conv_w, layer), _layer_spec(conv_b, layer), _layer_spec(dtb, layer),
        ],
        out_specs=[pl.BlockSpec((None, tm, M2_XBC), lambda b, j: (b, j, 0)),
                   pl.BlockSpec((None, tm, LANE), lambda b, j: (b, j, 0))],
        out_shape=[jax.ShapeDtypeStruct((nb, t, M2_XBC), F32), jax.ShapeDtypeStruct((nb, t, LANE), F32)],
        scratch_shapes=[pltpu.VMEM((tm + 2 * M2_HALO, M2_XBC), F32)],
        compiler_params=_cparams(("arbitrary", "arbitrary")),
        name="ssd_prep",
    )(pd, pd, pd, conv_w, conv_b, dtb)

    qn = SSD_BLOCK
    nchunks = t // qn
    nctx_chunks = nchunks - nlat // qn

    def fwd_chunk(k):
        return jnp.where(k < nctx_chunks, k + (nchunks - nctx_chunks), k - nctx_chunks)

    def rev_chunk(k):
        return nchunks - 1 - k

    def rows(n, chunk_of):
        return pl.BlockSpec((None, qn, n), lambda b, k: (b, chunk_of(k), 0))

    def direction(d):
        return pl.BlockSpec((None, None, 1, LANE), lambda b, k: (layer, d, 0, 0))

    return pl.pallas_call(
        _ssd_kernel,
        grid=(nb, nchunks),
        in_specs=[rows(M2_XBC, fwd_chunk), rows(LANE, fwd_chunk), rows(M2_XBC, rev_chunk), rows(LANE, rev_chunk),
                  direction(0), direction(1), _layer_spec(dsk, layer),
                  pl.BlockSpec((None, LANE, gw), lambda b, k: (0, 0, 0)),
                  pl.BlockSpec((None, LANE, gw), lambda b, k: (1, 0, 0))],
        out_specs=[rows(gw, fwd_chunk), rows(gw, rev_chunk)],
        out_shape=[jax.ShapeDtypeStruct((nb, t, gw), F32), jax.ShapeDtypeStruct((nb, t, gw), F32)],
        scratch_shapes=[pltpu.VMEM((M2_STATE, gw), F32), pltpu.VMEM((M2_STATE, gw), F32)],
        compiler_params=_cparams(("arbitrary", "arbitrary")),
        name="ssd_scan",
    )(xc, dtp, xc, dtp, a_dir, a_dir, dsk, expand, expand)


def _layer_norm_rows(v, g, b):
    mu = jnp.mean(v, axis=-1, keepdims=True)
    var = jnp.mean(jnp.square(v - mu), axis=-1, keepdims=True)
    return (v - mu) * lax.rsqrt(var + LN_EPS) * g + b


def _outproj_kernel(mabc_ref, yf_ref, yr_ref, z_ref, nw_ref, h_ref, gate_ref, w_ref, lg_ref, lb_ref, o_ref,
                    *, alpha):
    k1 = mabc_ref.shape[1]
    gc = GROUP_W // M2_GROUPS
    for r0 in range(0, o_ref.shape[0], ROW_TILE):
        rows = slice(r0, r0 + ROW_TILE)
        gv = (yf_ref[rows, :] + yr_ref[rows, :]) * _silu(z_ref[rows, :])
        md = []
        for g in range(M2_GROUPS):
            part = gv[:, g * gc:(g + 1) * gc]
            ms = jnp.mean(jnp.square(part), axis=-1, keepdims=True)
            md.append((part * lax.rsqrt(ms + RMS_EPS) * nw_ref[:, g * gc:(g + 1) * gc]).astype(BF16))
        mix = (_bdot(mabc_ref[rows, :].astype(BF16), w_ref[0:k1, :])
               + _bdot(jnp.concatenate(md, axis=1), w_ref[k1:w_ref.shape[0], :]))
        o_ref[rows, :] = _layer_norm_rows(alpha * h_ref[rows, :] + gate_ref[...] * mix, lg_ref[...], lb_ref[...])


def _outproj(mabc, yf, yr, pd, norm_w, h, mods, w, ln_g, ln_b, layer, nlat, alpha, latent):
    nb, rows, d = h.shape
    tm = OUT_ROW_TILE if latent else rows
    blk0 = 0 if latent else nlat // tm
    k1, k2 = mabc.shape[2], yf.shape[2]
    z_block = M2_XBC // k2

    def row(n):
        return pl.BlockSpec((None, tm, n), lambda b, j: (b, j + blk0, 0))

    return pl.pallas_call(
        functools.partial(_outproj_kernel, alpha=alpha),
        grid=(nb, rows // tm),
        in_specs=[row(k1), row(k2), row(k2), pl.BlockSpec((None, tm, k2), lambda b, j: (b, j + blk0, z_block)),
                  _layer_spec(norm_w, layer), pl.BlockSpec((None, tm, d), lambda b, j: (b, j, 0)),
                  pl.BlockSpec((None, None, 1, d), lambda b, j: (layer, b if latent else nb, 0, 2)),
                  _layer_spec(w, layer), _layer_spec(ln_g, layer), _layer_spec(ln_b, layer)],
        out_specs=pl.BlockSpec((None, tm, d), lambda b, j: (b, j, 0)),
        out_shape=jax.ShapeDtypeStruct((nb, rows, d), F32),
        compiler_params=_cparams(("arbitrary", "arbitrary")),
        name="outproj_lat" if latent else "outproj_ctx",
    )(mabc, yf, yr, pd, norm_w, h, mods, w, ln_g, ln_b)


def _ffn_kernel(*refs, alpha, grid_conv, halo, seq_len):
    if grid_conv:
        (h_ref, next_ref, sh_ref, sc_ref, gate_ref, wg_ref, wv_ref, cw_ref, cb_ref, wd_ref,
         lg_ref, lb_ref, o_ref, xm_scr, tail_scr, *chain_scr) = refs
    else:
        (h_ref, sh_ref, sc_ref, gate_ref, wg_ref, wv_ref, cw_ref, cb_ref, wd_ref,
         lg_ref, lb_ref, o_ref, xm_scr, tail_scr, *chain_scr) = refs
    g_scr, v_scr, side_scr = chain_scr[0::3], chain_scr[1::3], chain_scr[2::3]
    j = pl.program_id(1)
    f = pl.program_id(2)
    nj = pl.num_programs(1)
    nf = pl.num_programs(2)
    tm = h_ref.shape[0]
    sub = g_scr[0].shape[1]
    nsub = len(g_scr)
    pad = SUBLANE

    @pl.when(f == 0)
    def _():
        def modulated(v):
            return (v * (1.0 + sc_ref[...]) + sh_ref[...]).astype(BF16)
        xm_scr[0:tm, :] = modulated(h_ref[...])
        if grid_conv:
            xm_scr[tm:tm + halo, :] = modulated(next_ref[...])
        o_ref[...] = jnp.zeros_like(o_ref)
        for side in side_scr:
            side[:, 0:pad, :] = jnp.zeros((2, pad, sub), F32)
            side[:, pad + tm:pad + tm + pad, :] = jnp.zeros((2, pad, sub), F32)

    if grid_conv:
        @pl.when(j == 0)
        def _():
            tail_scr[f] = jnp.zeros(tail_scr.shape[1:], F32)

    pos = lax.broadcasted_iota(jnp.int32, (tm, 1), 0)
    if grid_conv:
        keep_next = (j < nj - 1).astype(F32)
        pos = pos % GRID_W
        left_ok, right_ok = pos != 0, pos != GRID_W - 1
        row_taps = (-1, 0, 1)
    else:
        pos = pos % seq_len
        left_ok, right_ok = pos != 0, pos != seq_len - 1
        row_taps = (0,)

    for si in range(nsub):
        cs = slice(si * sub, (si + 1) * sub)
        ge = _bdot(xm_scr[...], wg_ref[:, cs].astype(BF16))
        if grid_conv:
            g_scr[si][0:halo, :] = tail_scr[f, :, cs]
            g_scr[si][halo:halo + tm, :] = ge[0:tm]
            g_scr[si][halo + tm:halo + tm + halo, :] = ge[tm:tm + halo] * keep_next
            tail_scr[f, :, cs] = ge[tm - halo:tm]
        else:
            g_scr[si][...] = ge
        v_scr[si][...] = _bdot(xm_scr[0:tm, :], wv_ref[:, cs].astype(BF16))

    acts = []
    for si in range(nsub):
        cs = slice(si * sub, (si + 1) * sub)

        def tap_sum(dc):
            tot = None
            for dr in row_taps:
                k = (dr + 1) * 3 + (dc + 1)
                r0 = halo + dr * GRID_W
                term = g_scr[si][r0:r0 + tm, :] * cw_ref[k:k + 1, cs]
                tot = term if tot is None else tot + term
            return tot

        side_scr[si][0, pad:pad + tm, :] = tap_sum(-1)
        side_scr[si][1, pad:pad + tm, :] = tap_sum(1)
        conv = tap_sum(0) + cb_ref[:, cs]
        conv = conv + jnp.where(left_ok, side_scr[si][0, pad - 1:pad - 1 + tm, :], 0.0)
        conv = conv + jnp.where(right_ok, side_scr[si][1, pad + 1:pad + 1 + tm, :], 0.0)
        acts.append((_gelu(conv) * v_scr[si][...]).astype(BF16))
    for si in range(nsub):
        o_ref[...] += _bdot(acts[si], wd_ref[si * sub:(si + 1) * sub, :].astype(BF16))

    @pl.when(f == nf - 1)
    def _():
        o_ref[...] = _layer_norm_rows(alpha * h_ref[...] + gate_ref[...] * o_ref[...], lg_ref[...], lb_ref[...])


def _ffn(h, mods, w_up, conv_w9, conv_b, w_down, ln_g, ln_b, layer, alpha, grid_conv):
    nb, rows, d = h.shape
    fh = w_down.shape[1]
    tf = FFN_COL_TILE
    sub = FFN_SUB_TILE
    nf = fh // tf
    if grid_conv:
        hx = h
        tm, halo = FFN_ROW_TILE, GRID_W
    else:
        hx = h.reshape(1, nb * rows, d)
        tm, halo = nb * rows, 0
    nblk = hx.shape[1] // tm
    hb = tm // GRID_W
    nhalo_blocks = rows // GRID_W

    def vec(k):
        return pl.BlockSpec((None, None, 1, d), lambda b, j, f: (layer, b if grid_conv else nb, 0, k))

    in_specs = [pl.BlockSpec((None, tm, d), lambda b, j, f: (b, j, 0))]
    args = [hx]
    if grid_conv:
        in_specs += [
            pl.BlockSpec((None, halo, d), lambda b, j, f: (b, jnp.minimum((j + 1) * hb, nhalo_blocks - 1), 0)),
        ]
        args += [hx]
    in_specs += [
        vec(3), vec(4), vec(5),
        pl.BlockSpec((None, d, tf), lambda b, j, f: (layer, 0, f)),
        pl.BlockSpec((None, d, tf), lambda b, j, f: (layer, 0, nf + f)),
        pl.BlockSpec((None, 9, tf), lambda b, j, f: (layer, 0, f)),
        pl.BlockSpec((None, 1, tf), lambda b, j, f: (layer, 0, f)),
        pl.BlockSpec((None, tf, d), lambda b, j, f: (layer, f, 0)),
        _layer_spec(ln_g, layer), _layer_spec(ln_b, layer),
    ]
    args += [mods, mods, mods, w_up, w_up, conv_w9, conv_b, w_down, ln_g, ln_b]
    return pl.pallas_call(
        functools.partial(_ffn_kernel, alpha=alpha, grid_conv=grid_conv, halo=halo, seq_len=rows),
        grid=(hx.shape[0], nblk, nf),
        in_specs=in_specs,
        out_specs=pl.BlockSpec((None, tm, d), lambda b, j, f: (b, j, 0)),
        out_shape=jax.ShapeDtypeStruct(hx.shape, F32),
        scratch_shapes=[pltpu.VMEM((tm + halo, d), BF16), pltpu.VMEM((nf, max(halo, SUBLANE), tf), F32)]
        + [pltpu.VMEM((tm + 2 * halo, sub), F32), pltpu.VMEM((tm, sub), F32),
           pltpu.VMEM((2, tm + 2 * SUBLANE, sub), F32)] * (tf // sub),
        compiler_params=_cparams(("arbitrary", "arbitrary", "arbitrary")),
        name="ffn_lat" if grid_conv else "ffn_ctx",
    )(*args).reshape(h.shape)


def kernel(x, c, ctx, c_ctx, w_ada, b_ada, w_in, w_out, ln1_g, ln1_b, ln2_g, ln2_b, s5_a_re, s5_a_im, s5_b_re, s5_b_im, s5_c_re, s5_c_im, s5_log_step, s5_d, s5_glu_w, s5_glu_b, sg_ln_g, sg_ln_b, sg_w, sg_b, pool_w, pool_b, pool_scale, m2_conv_w, m2_conv_b, m2_dt_bias, m2_a_log, m2_d, m2_norm_w, ffn_w_up, ffn_conv_w, ffn_conv_b, ffn_w_down):
    nb, nlat, d = x.shape
    nctx = ctx.shape[1]
    depth = w_ada.shape[0]
    gw = GROUP_W
    alpha = (2 * depth) ** 0.25
    assert nb + 1 <= MOD_ROWS and nlat % FFN_ROW_TILE == 0 and nlat % OUT_ROW_TILE == 0 and nctx % ROW_TILE == 0
    assert nlat % (GRID_W * SUBLANE) == 0 and FFN_ROW_TILE % GRID_W == 0 and nlat % SSD_BLOCK == 0 and nctx % SSD_BLOCK == 0
    assert w_in.shape[2] == 5 * gw + M2_XBC + 2 * M2_HEADS and (5 * gw + M2_XBC) % ROW_TILE == 0

    cmat = jnp.concatenate([c, c_ctx[None, :], jnp.zeros((MOD_ROWS - nb - 1, d), F32)], axis=0)
    mods = _ada(cmat, w_ada, b_ada).reshape(depth, MOD_ROWS, 1, 6 * d)

    def rows(v):
        return v.astype(F32).reshape(depth, 1, -1)

    fh = ffn_w_down.shape[1]
    lane_pad = LANE - 2 * M2_HEADS
    w_in_b = _cast_w_in(w_in)
    w_dt_b = jnp.pad(w_in[:, :, w_in.shape[2] - 2 * M2_HEADS:], ((0, 0), (0, 0), (0, lane_pad))).astype(BF16)
    w_out_b = w_out.astype(BF16)
    cw9 = ffn_conv_w.astype(F32).reshape(depth, 9, fh)
    s5_mats = _s5_mats(s5_a_re, s5_a_im, s5_b_re, s5_b_im, s5_c_re, s5_c_im, s5_log_step, s5_d)
    sg_btile = jnp.repeat(jnp.swapaxes(sg_b.astype(F32), 1, 2), gw // SG_HEADS, axis=2)
    mixer_params = (s5_glu_w.astype(BF16), rows(s5_glu_b), rows(sg_ln_g), rows(sg_ln_b), sg_w.astype(BF16),
                    sg_btile, pool_w.astype(BF16), rows(pool_b), rows(pool_scale))
    m2_a = -jnp.exp(m2_a_log.astype(F32))
    a_dir = (m2_a[:, :, None, :] * jnp.eye(2, dtype=F32)[None, :, :, None]).reshape(depth, 2, 1, 2 * M2_HEADS)
    a_dir = jnp.pad(a_dir, ((0, 0), (0, 0), (0, 0), (0, lane_pad)))
    head_of_ch = jnp.arange(gw) // M2_HEAD_DIM
    expand = (jnp.arange(LANE)[None, :, None] == (jnp.arange(2)[:, None, None] * M2_HEADS + head_of_ch[None, None, :])
              ).astype(BF16)
    ssd_params = (m2_conv_w.astype(F32), rows(m2_conv_b),
                  jnp.pad(m2_dt_bias.astype(F32).reshape(depth, 1, 2 * M2_HEADS), ((0, 0), (0, 0), (0, lane_pad))),
                  a_dir, rows(jnp.repeat(m2_d.astype(F32), M2_HEAD_DIM, axis=1)), expand)
    m2_nw = rows(m2_norm_w)
    ln1 = rows(ln1_g), rows(ln1_b)
    ln2 = rows(ln2_g), rows(ln2_b)
    ffn_cb = rows(ffn_conv_b)

    h_lat, h_ctx = x, ctx
    for i in range(depth):
        need_ctx = i < depth - 1
        pa, pb, pd = _inproj(h_lat, h_ctx, mods, w_in_b, w_dt_b, i)
        ya = _s5(pa, s5_mats, i, nlat)
        mabc = _mixers(ya, pb, mixer_params, i, nlat)
        yf, yr = _ssd(pd, ssd_params, i, nlat)
        mix_in = (mabc, yf, yr, pd, m2_nw)
        h1_lat = _outproj(*mix_in, h_lat, mods, w_out_b, ln1[0], ln1[1], i, nlat, alpha, True)
        if need_ctx:
            h1_ctx = _outproj(*mix_in, h_ctx, mods, w_out_b, ln1[0], ln1[1], i, nlat, alpha, False)
            h_ctx = _ffn(h1_ctx, mods, ffn_w_up, cw9, ffn_cb, ffn_w_down, ln2[0], ln2[1], i, alpha, False)
        h_lat = _ffn(h1_lat, mods, ffn_w_up, cw9, ffn_cb, ffn_w_down, ln2[0], ln2[1], i, alpha, True)
    return h_lat
```

```python
import functools

import jax
import jax.numpy as jnp
from jax import lax
from jax.experimental import pallas as pl
from jax.experimental.pallas import tpu as pltpu

F32 = jnp.float32
BF16 = jnp.bfloat16

GRID_W = 64
GROUP_W = 512
S5_CH = 16
S5_Q = 16
S5_RELAYOUT_UNROLL = 4
SG_HEADS = 4
SG_CHUNK = 128
POOL_WINDOWS = (2, 4, 8, 16)
POOL_DIM = GROUP_W // len(POOL_WINDOWS)
POOL_HALO = 8
M2_HEAD_DIM = 64
M2_HEADS = GROUP_W // M2_HEAD_DIM
M2_STATE = 128
M2_GROUPS = 2
M2_CONV = 4
M2_CHUNK = 128
M2_XBC = GROUP_W + 2 * M2_GROUPS * M2_STATE
M2_HALO = 8
SSD_BLOCK = 2 * M2_CHUNK
LN_EPS = 1e-5
RMS_EPS = 1e-5

LANE = 128
SUBLANE = 8
VMEM_LIMIT = 56 * 1024 * 1024
ROW_TILE = 256
OUT_ROW_TILE = 512
FFN_ROW_TILE = 512
FFN_COL_TILE = 512
FFN_SUB_TILE = 256
FFN_CONV_ROWS = 256
ADA_COL_TILE = 1024
MOD_ROWS = 8


def _cparams(sem):
    return pltpu.CompilerParams(dimension_semantics=sem, vmem_limit_bytes=VMEM_LIMIT)


def _bdot(a, b):
    return jnp.dot(a, b, preferred_element_type=F32)


def _gelu(x):
    return jax.nn.gelu(x)


def _silu(x):
    return x * jax.nn.sigmoid(x)


def _ada_kernel(c_ref, w_ref, b_ref, o_ref):
    c = c_ref[...]
    s = _silu(c).astype(BF16)
    o_ref[...] = _bdot(s, w_ref[...].astype(BF16)) + b_ref[...]


def _ada(cmat, w_ada, b_ada):
    depth, d, n = w_ada.shape
    tn = ADA_COL_TILE
    return pl.pallas_call(
        _ada_kernel,
        grid=(depth, n // tn),
        in_specs=[
            pl.BlockSpec((MOD_ROWS, d), lambda l, j: (0, 0)),
            pl.BlockSpec((None, d, tn), lambda l, j: (l, 0, j)),
            pl.BlockSpec((None, 1, tn), lambda l, j: (l, 0, j)),
        ],
        out_specs=pl.BlockSpec((None, MOD_ROWS, tn), lambda l, j: (l, 0, j)),
        out_shape=jax.ShapeDtypeStruct((depth, MOD_ROWS, n), F32),
        compiler_params=_cparams(("arbitrary", "arbitrary")),
        name="ada",
    )(cmat, w_ada, b_ada.reshape(depth, 1, n))


def _mod_spec(d, k, layer, nbatch, nlat_blocks):
    return pl.BlockSpec((None, None, 1, d),
                        lambda b, j, *_: (layer, jnp.where(j >= nlat_blocks, nbatch, b), 0, k))


def _layer_spec(arr, layer):
    tail = arr.shape[1:]
    return pl.BlockSpec((None,) + tail, lambda *_: (layer,) + (0,) * len(tail))


def _cast_w_in_kernel(wt_ref, wb_ref):
    wb_ref[...] = wt_ref[...].T.astype(BF16)


def _cast_w_in(w_in):
    depth, d, n = w_in.shape
    tn = ROW_TILE
    wt = jnp.swapaxes(w_in, 1, 2)
    return pl.pallas_call(
        _cast_w_in_kernel,
        grid=(depth, n // tn),
        in_specs=[pl.BlockSpec((None, tn, d), lambda l, i: (l, i, 0))],
        out_specs=pl.BlockSpec((None, d, tn), lambda l, i: (l, 0, i)),
        out_shape=jax.ShapeDtypeStruct((depth, d, n // tn * tn), BF16),
        compiler_params=_cparams(("arbitrary", "arbitrary")),
        name="cast_w_in",
    )(wt)


def _stream_specs(tm, d, nlat_blocks):
    lat = pl.BlockSpec((None, tm, d), lambda b, j: (b, jnp.minimum(j, nlat_blocks - 1), 0))
    ctx = pl.BlockSpec((None, tm, d), lambda b, j: (b, jnp.maximum(j - nlat_blocks, 0), 0))
    return lat, ctx


def _inproj_kernel(hl_ref, hc_ref, sh_ref, sc_ref, w_ref, wdt_ref, oa_ref, ob_ref, od_ref, xm_scr, *, nlat_blocks):
    h = jnp.where(pl.program_id(1) < nlat_blocks, hl_ref[...], hc_ref[...])
    xm_scr[...] = (h * (1.0 + sc_ref[...]) + sh_ref[...]).astype(BF16)

    xm = xm_scr[...]
    gw = GROUP_W
    oa_ref[...] = _bdot(xm, w_ref[:, 0:gw])
    ob_ref[...] = _bdot(xm, w_ref[:, gw:4 * gw])
    od_ref[:, 0:M2_XBC] = _bdot(xm, w_ref[:, 5 * gw:5 * gw + M2_XBC])
    od_ref[:, M2_XBC:M2_XBC + gw] = _bdot(xm, w_ref[:, 4 * gw:5 * gw])
    od_ref[:, M2_XBC + gw:M2_XBC + gw + LANE] = _bdot(xm, wdt_ref[...])


def _inproj(h_lat, h_ctx, mods, w, w_dt, layer):
    nb, nlat, d = h_lat.shape
    t = nlat + h_ctx.shape[1]
    tm = ROW_TILE
    na, nbw, nd = GROUP_W, 3 * GROUP_W, M2_XBC + GROUP_W + LANE
    lat_spec, ctx_spec = _stream_specs(tm, d, nlat // tm)

    def row(n):
        return pl.BlockSpec((None, tm, n), lambda b, j: (b, j, 0))

    return pl.pallas_call(
        functools.partial(_inproj_kernel, nlat_blocks=nlat // tm),
        grid=(nb, t // tm),
        in_specs=[lat_spec, ctx_spec, _mod_spec(d, 0, layer, nb, nlat // tm),
                  _mod_spec(d, 1, layer, nb, nlat // tm), _layer_spec(w, layer), _layer_spec(w_dt, layer)],
        out_specs=[row(na), row(nbw), row(nd)],
        out_shape=[jax.ShapeDtypeStruct((nb, t, n), F32) for n in (na, nbw, nd)],
        scratch_shapes=[pltpu.VMEM((tm, d), BF16)],
        compiler_params=_cparams(("arbitrary", "arbitrary")),
        name="inproj",
    )(h_lat, h_ctx, mods, mods, w, w_dt)


def _s5_mats(a_re, a_im, b_re, b_im, c_re, c_im, log_step, d_skip):
    q, hch = S5_Q, S5_CH
    nl, _, g, p = a_re.shape
    lre, lim = a_re.astype(F32), a_im.astype(F32)
    step = jnp.exp(log_step.astype(F32))[..., None]
    mag = jnp.exp(lre * step)
    ar, ai = mag * jnp.cos(lim * step), mag * jnp.sin(lim * step)
    den = lre * lre + lim * lim
    qr = ((ar - 1.0) * lre + ai * lim) / den
    qi = (ai * lre - (ar - 1.0) * lim) / den
    bre, bim = b_re.astype(F32), b_im.astype(F32)
    bbr = qr[..., None] * bre - qi[..., None] * bim
    bbi = qr[..., None] * bim + qi[..., None] * bre
    cre, cim = c_re.astype(F32), c_im.astype(F32)

    pr, pi = ar[:, :, :, None], ai[:, :, :, None]
    while pr.shape[3] < q:
        lr, li = pr[:, :, :, -1:], pi[:, :, :, -1:]
        pr, pi = (jnp.concatenate([pr, pr * lr - pi * li], axis=3), jnp.concatenate([pi, pr * li + pi * lr], axis=3))
    pk_re = jnp.concatenate([jnp.ones_like(pr[:, :, :, :1]), pr], axis=3)
    pk_im = jnp.concatenate([jnp.zeros_like(pi[:, :, :, :1]), pi], axis=3)
    pt_re, pt_im = jnp.swapaxes(pk_re, 3, 4), jnp.swapaxes(pk_im, 3, 4)

    def lag_kernels(d, pr, pi):
        ctr = jnp.swapaxes(cre[:, d], -1, -2)[:, :, :, None]
        cti = jnp.swapaxes(cim[:, d], -1, -2)[:, :, :, None]
        ca_re = (ctr * pr[..., None] - cti * pi[..., None]).reshape(nl, g, p, 1, q * hch)
        ca_im = (ctr * pi[..., None] + cti * pr[..., None]).reshape(nl, g, p, 1, q * hch)
        return jnp.sum(bbr[:, d][..., None] * ca_re - bbi[:, d][..., None] * ca_im, axis=2)

    lag_f = lag_kernels(0, pt_re[:, 0, :, :, :q], pt_im[:, 0, :, :, :q])
    lag_r = lag_kernels(1, jnp.flip(pt_re[:, 1, :, :, :q], axis=3), jnp.flip(pt_im[:, 1, :, :, :q], axis=3))
    dmat = jnp.eye(hch, dtype=F32) * d_skip.astype(F32).reshape(nl, g, 1, hch)
    mid = lag_r[..., (q - 1) * hch:] + lag_f[..., :hch] + dmat
    kk = jnp.concatenate([lag_r[..., :(q - 1) * hch], mid, lag_f[..., hch:],
                          jnp.zeros(mid.shape[:-1] + (hch,), F32)], axis=-1)

    def state_in(d, pr, pi):
        btr = jnp.swapaxes(bbr[:, d], -1, -2)[:, :, None]
        bti = jnp.swapaxes(bbi[:, d], -1, -2)[:, :, None]
        pr, pi = pr[:, :, :, None], pi[:, :, :, None]
        m = jnp.concatenate([pr * btr - pi * bti, pr * bti + pi * btr], axis=-1)
        return m.reshape(nl, g, q * hch, 2 * p)

    sb = jnp.concatenate([state_in(0, jnp.flip(pk_re[:, 0, :, :q], axis=2), jnp.flip(pk_im[:, 0, :, :q], axis=2)),
                          state_in(1, pk_re[:, 1, :, :q], pk_im[:, 1, :, :q])], axis=-1)

    def state_out(d, pr, pi):
        ctr = jnp.swapaxes(cre[:, d], -1, -2)[:, :, :, None]
        cti = jnp.swapaxes(cim[:, d], -1, -2)[:, :, :, None]
        pr, pi = pr[..., None], pi[..., None]
        m = jnp.concatenate([ctr * pr - cti * pi, -(ctr * pi + cti * pr)], axis=2)
        return m.reshape(nl, g, 2 * p, q * hch)

    w2 = jnp.concatenate([state_out(0, pt_re[:, 0, :, :, 1:], pt_im[:, 0, :, :, 1:]),
                          state_out(1, jnp.flip(pt_re[:, 1, :, :, 1:], axis=3), jnp.flip(pt_im[:, 1, :, :, 1:], axis=3))],
                         axis=2)

    aqr, aqi = pk_re[:, :, :, q], pk_im[:, :, :, q]
    a1 = jnp.concatenate([aqr, aqr], axis=-1).reshape(nl, 2, g * 2 * p)
    a2 = jnp.concatenate([-aqi, aqi], axis=-1).reshape(nl, 2, g * 2 * p)
    acoef = jnp.stack([a1[:, 0], a2[:, 0], a1[:, 1], a2[:, 1]], axis=1)
    return kk, sb, w2, acoef


def _granule_transpose(v):
    n = len(v)
    slot = lax.broadcasted_iota(jnp.int32, v[0].shape, 1) // S5_CH
    v = list(v)
    j = n // 2
    while j >= 1:
        keep = (slot & j) == 0
        for i in range(n):
            if i & j == 0:
                a, b = v[i], v[i + j]
                v[i] = jnp.where(keep, a, pltpu.roll(b, j * S5_CH, axis=1))
                v[i + j] = jnp.where(keep, pltpu.roll(a, LANE - j * S5_CH, axis=1), b)
        j //= 2
    return v


def _split_bf16(x):
    hi = x.astype(BF16)
    return hi, (x - hi.astype(F32)).astype(BF16)


def _dot3(xs, w):
    x_hi, x_lo = xs
    w_hi, w_lo = _split_bf16(w)
    return _bdot(x_hi, w_hi) + (_bdot(x_lo, w_hi) + _bdot(x_hi, w_lo))


def _s5_kernel(pa_ref, kk_ref, sb_ref, w2_ref, ac_ref, ya_ref, x_scr, s_scr, h_scr, t_scr, *, nbatch, nrows, nctx):
    gb = LANE // S5_CH
    halves = S5_Q // gb
    qh = S5_Q * S5_CH
    sw = LANE
    nlat = nrows - nctx
    nrc = nrows // SUBLANE
    tok_per_rc = SUBLANE * S5_Q

    def relayout(it, to_rows):
        for u in range(S5_RELAYOUT_UNROLL):
            idx = it * S5_RELAYOUT_UNROLL + u
            b = idx // nrc
            rc = idx % nrc
            r0 = pl.multiple_of(b * nrows + rc * SUBLANE, SUBLANE)
            for half in range(halves):
                lanes = slice(half * LANE, (half + 1) * LANE)

                def tok(s8, half=half, rc=rc):
                    return pl.ds(rc * tok_per_rc + half * gb + s8, SUBLANE, stride=S5_Q)

                if to_rows:
                    v = _granule_transpose([pa_ref[b, tok(s8), :] for s8 in range(gb)])
                    for gi in range(gb):
                        x_scr[gi, pl.ds(r0, SUBLANE), lanes] = v[gi]
                else:
                    v = _granule_transpose([x_scr[gi, pl.ds(r0, SUBLANE), lanes] for gi in range(gb)])
                    for s8 in range(gb):
                        ya_ref[b, tok(s8), :] = v[s8]
        return 0

    nblocks = nbatch * nrc // S5_RELAYOUT_UNROLL
    lax.fori_loop(0, nblocks, lambda it, c: relayout(it, True), 0)

    for gi in range(gb):
        kkv = kk_ref[gi]
        for s in range(S5_Q):
            shift = (S5_Q - 1 - s) * S5_CH
            t_scr[s * S5_CH:(s + 1) * S5_CH, :] = pltpu.roll(kkv, kkv.shape[1] - shift, axis=1)[:, 0:qh]
        xs = _split_bf16(x_scr[gi])
        x_scr[gi] = _dot3(xs, t_scr[...])
        rs = _dot3(xs, sb_ref[gi])
        for d in range(2):
            sd = rs[:, d * sw:(d + 1) * sw]
            s_scr[d, gi] = sd
            s_scr[2 + d, gi] = pltpu.roll(sd, sw // 2, axis=1)

    zero = jnp.zeros((nbatch, sw), F32)

    def body(s, carry):
        f_rows = pl.ds(jnp.where(s < nctx, s + nlat, s - nctx), nbatch, stride=nrows)
        r_rows = pl.ds(nrows - 1 - s, nbatch, stride=nrows)
        out = []
        for gi in range(gb):
            hf, hfs, hr, hrs = carry[4 * gi:4 * gi + 4]
            a1f, a2f, a1r, a2r = (ac_ref[k:k + 1, gi * sw:(gi + 1) * sw] for k in range(4))
            h_scr[0, gi, f_rows, :] = hf
            h_scr[1, gi, r_rows, :] = hr
            out += [a1f * hf + a2f * hfs + s_scr[0, gi, f_rows, :],
                    a1f * hfs - a2f * hf + s_scr[2, gi, f_rows, :],
                    a1r * hr + a2r * hrs + s_scr[1, gi, r_rows, :],
                    a1r * hrs - a2r * hr + s_scr[3, gi, r_rows, :]]
        return tuple(out)

    lax.fori_loop(0, nrows, body, (zero,) * (4 * gb))

    for gi in range(gb):
        hin = jnp.concatenate([h_scr[0, gi], h_scr[1, gi]], axis=1)
        x_scr[gi] = x_scr[gi] + _dot3(_split_bf16(hin), w2_ref[gi])

    lax.fori_loop(0, nblocks, lambda it, c: relayout(it, False), 0)


def _s5(pa, mats, layer, nlat):
    kk, sb, w2, acoef = mats
    nb, t, gw = pa.shape
    q = S5_Q
    nrows = t // q
    qh = q * S5_CH
    gb = LANE // S5_CH
    assert nrows % SUBLANE == 0 and q % gb == 0 and w2.shape[2] == 2 * LANE and sb.shape[3] == 2 * LANE
    assert (nb * nrows // SUBLANE) % S5_RELAYOUT_UNROLL == 0
    return pl.pallas_call(
        functools.partial(_s5_kernel, nbatch=nb, nrows=nrows, nctx=(t - nlat) // q),
        grid=(gw // LANE,),
        in_specs=[
            pl.BlockSpec((nb, t, LANE), lambda i: (0, 0, i)),
            pl.BlockSpec((None, gb, S5_CH, 2 * qh), lambda i: (layer, i, 0, 0)),
            pl.BlockSpec((None, gb, qh, 2 * LANE), lambda i: (layer, i, 0, 0)),
            pl.BlockSpec((None, gb, 2 * LANE, qh), lambda i: (layer, i, 0, 0)),
            pl.BlockSpec((None, 4, gb * LANE), lambda i: (layer, 0, i)),
        ],
        out_specs=pl.BlockSpec((nb, t, LANE), lambda i: (0, 0, i)),
        out_shape=jax.ShapeDtypeStruct((nb, t, gw), F32),
        scratch_shapes=[pltpu.VMEM((gb, nb * nrows, qh), F32),
                        pltpu.VMEM((4, gb, nb * nrows, LANE), F32),
                        pltpu.VMEM((2, gb, nb * nrows, LANE), F32),
                        pltpu.VMEM((qh, qh), F32)],
        compiler_params=_cparams(("arbitrary",)),
        name="s5",
    )(pa, kk, sb, w2, acoef)


def _mixers_kernel(ya_ref, pb_ref, prev_ref, next_ref, gluw_ref, glub_ref, lng_ref, lnb_ref, sgw_ref,
                   sgb_ref, pw_ref, pbias_ref, pscale_ref, o_ref, ext_scr, *, nlat_blocks, nblocks, nlat, nctx):
    j = pl.program_id(1)
    tm = ya_ref.shape[0]
    gw = GROUP_W

    z = _gelu(ya_ref[...])
    gate = jax.nn.sigmoid(_bdot(z.astype(BF16), gluw_ref[...]) + glub_ref[...])
    o_ref[:, 0:gw] = z * gate

    hd = gw // SG_HEADS
    u = _gelu(pb_ref[:, 0:gw])
    v = _gelu(pb_ref[:, gw:2 * gw])
    for hh in range(SG_HEADS):
        vh = v[:, hh * hd:(hh + 1) * hd]
        mu = jnp.mean(vh, axis=-1, keepdims=True)
        var = jnp.mean(jnp.square(vh - mu), axis=-1, keepdims=True)
        vn = ((vh - mu) * lax.rsqrt(var + LN_EPS) * lng_ref[:, hh * hd:(hh + 1) * hd]
              + lnb_ref[:, hh * hd:(hh + 1) * hd]).astype(BF16)
        for cchunk in range(tm // SG_CHUNK):
            rows = slice(cchunk * SG_CHUNK, (cchunk + 1) * SG_CHUNK)
            s = _bdot(sgw_ref[hh], vn[rows]) + sgb_ref[:, hh * hd:(hh + 1) * hd]
            o_ref[rows, gw + hh * hd:gw + (hh + 1) * hd] = u[rows, hh * hd:(hh + 1) * hd] * s

    is_first = jnp.logical_or(j == 0, j == nlat_blocks)
    is_last = jnp.logical_or(j == nlat_blocks - 1, j == nblocks - 1)
    p = pb_ref[:, 2 * gw:3 * gw]
    hal = POOL_HALO
    ext_scr[0:hal, :] = jnp.where(is_first, 0.0, prev_ref[...])
    ext_scr[hal:hal + tm, :] = p
    ext_scr[hal + tm:hal + tm + hal, :] = jnp.where(is_last, 0.0, next_ref[...])
    in_ctx = j >= nlat_blocks
    seq_len = jnp.where(in_ctx, nctx, nlat)
    t0 = (j - jnp.where(in_ctx, nlat_blocks, 0)) * tm
    tpos = t0 + lax.broadcasted_iota(jnp.int32, (tm, 1), 0)
    pd = POOL_DIM
    for gi, win in enumerate(POOL_WINDOWS):
        cols = slice(gi * pd, (gi + 1) * pd)
        tot = jnp.zeros((tm, pd), F32)
        for off in range(-(win // 2), win // 2):
            tot = tot + ext_scr[hal + off:hal + off + tm, cols]
        lo = jnp.maximum(tpos - win // 2, 0)
        hi = jnp.minimum(tpos + win // 2 - 1, seq_len - 1)
        mean = tot / (hi - lo + 1).astype(F32)
        yv = _bdot((mean - p[:, cols]).astype(BF16), pw_ref[gi]) + pbias_ref[:, cols]
        o_ref[:, 2 * gw + gi * pd:2 * gw + (gi + 1) * pd] = yv * pscale_ref[:, cols]


def _mixers(ya, pb, params, layer, nlat):
    nb, t, gw = ya.shape
    tm = ROW_TILE
    nblocks = t // tm
    nlat_blocks = nlat // tm
    hb = tm // POOL_HALO
    nh = t // POOL_HALO

    return pl.pallas_call(
        functools.partial(_mixers_kernel, nlat_blocks=nlat_blocks, nblocks=nblocks, nlat=nlat, nctx=t - nlat),
        grid=(nb, nblocks),
        in_specs=[
            pl.BlockSpec((None, tm, gw), lambda b, j: (b, j, 0)),
            pl.BlockSpec((None, tm, 3 * gw), lambda b, j: (b, j, 0)),
            pl.BlockSpec((None, POOL_HALO, gw), lambda b, j: (b, jnp.maximum(j * hb - 1, 0), 2)),
            pl.BlockSpec((None, POOL_HALO, gw), lambda b, j: (b, jnp.minimum((j + 1) * hb, nh - 1), 2)),
        ] + [_layer_spec(v, layer) for v in params],
        out_specs=pl.BlockSpec((None, tm, 3 * gw), lambda b, j: (b, j, 0)),
        out_shape=jax.ShapeDtypeStruct((nb, t, 3 * gw), F32),
        scratch_shapes=[pltpu.VMEM((tm + 2 * POOL_HALO, gw), F32)],
        compiler_params=_cparams(("arbitrary", "arbitrary")),
        name="mixers",
    )(ya, pb, pb, pb, *params)


def _ssd_prep_kernel(pd_ref, prev_ref, next_ref, cw_ref, cb_ref, dtb_ref, xc_ref, dt_ref, ext_scr,
                     *, nlat_blocks, nblocks):
    j = pl.program_id(1)
    tm = pd_ref.shape[0]
    nx = M2_XBC
    hal = M2_HALO
    is_first = jnp.logical_or(j == 0, j == nlat_blocks)
    is_last = jnp.logical_or(j == nlat_blocks - 1, j == nblocks - 1)
    ext_scr[0:hal, :] = jnp.where(is_first, 0.0, prev_ref[...])
    ext_scr[hal:hal + tm, :] = pd_ref[:, 0:nx]
    ext_scr[hal + tm:hal + tm + hal, :] = jnp.where(is_last, 0.0, next_ref[...])
    acc = jnp.zeros((tm, nx), F32) + cb_ref[...]
    for tap in range(M2_CONV):
        off = hal + tap - M2_CONV // 2
        acc = acc + ext_scr[off:off + tm, :] * cw_ref[tap:tap + 1, :]
    xc_ref[...] = _silu(acc)
    xdt = pd_ref[:, nx + GROUP_W:nx + GROUP_W + LANE] + dtb_ref[...]
    dt_ref[...] = jnp.maximum(xdt, 0.0) + jnp.log1p(jnp.exp(-jnp.abs(xdt)))


def _split3_bf16(x):
    x1 = x.astype(BF16)
    r1 = x - x1.astype(F32)
    x2 = r1.astype(BF16)
    return x1, x2, (r1 - x2.astype(F32)).astype(BF16)


def _ssd_chunk(reverse, rows, xc_ref, dt_ref, a_ref, dsk_ref, e_ref, st_scr, o_ref):
    qn = M2_CHUNK
    gw = GROUP_W
    lane0 = M2_HEADS if reverse else 0
    xs = xc_ref[rows, 0:gw]
    nbc = M2_GROUPS * M2_STATE
    bm = xc_ref[rows, gw:gw + nbc]
    cm = xc_ref[rows, gw + nbc:gw + 2 * nbc]

    dt = dt_ref[rows, :]
    da = dt * a_ref[...]
    ri = lax.broadcasted_iota(jnp.int32, (qn, qn), 0)
    ci = lax.broadcasted_iota(jnp.int32, (qn, qn), 1)
    causal = (ri <= ci) if reverse else (ri >= ci)
    mask = causal.astype(BF16)
    d1, d2, d3 = _split3_bf16(da)
    a_col = _bdot(mask, d1) + (_bdot(mask, d2) + _bdot(mask, d3))
    a_row = a_col.T
    tot_row = a_col[0:1, :] if reverse else a_col[qn - 1:qn, :]

    per_head = jnp.concatenate([dt, jnp.exp(a_col), jnp.exp(tot_row - a_col)], axis=0)
    p1, p2 = _split_bf16(per_head)
    e = e_ref[...]
    per_ch = _bdot(p1, e) + _bdot(p2, e)
    dt_x, ea_x, de_x = per_ch[0:qn], per_ch[qn:2 * qn], per_ch[2 * qn:3 * qn]
    et_x = ea_x[0:1, :] if reverse else ea_x[qn - 1:qn, :]
    xd = xs * dt_x
    xdw = (xd * de_x).astype(BF16)
    xdb = xd.astype(BF16)

    hpg = M2_HEADS // M2_GROUPS
    gcols = hpg * M2_HEAD_DIM
    for g in range(M2_GROUPS):
        bg = bm[:, g * M2_STATE:(g + 1) * M2_STATE]
        cg = cm[:, g * M2_STATE:(g + 1) * M2_STATE].astype(BF16)
        bgt = bg.T.astype(BF16)
        scores = _bdot(cg, bgt)
        st_old = st_scr[:, g * gcols:(g + 1) * gcols]
        y_inter = _bdot(cg, st_old.astype(BF16)) * ea_x[:, g * gcols:(g + 1) * gcols]
        st_new = _bdot(bgt, xdw[:, g * gcols:(g + 1) * gcols])
        st_scr[:, g * gcols:(g + 1) * gcols] = et_x[:, g * gcols:(g + 1) * gcols] * st_old + st_new
        for hl in range(hpg):
            hh = g * hpg + hl
            seg = a_col[:, lane0 + hh:lane0 + hh + 1] - a_row[lane0 + hh:lane0 + hh + 1, :]
            decay = jnp.exp(jnp.where(causal, seg, -jnp.inf))
            cols = slice(hh * M2_HEAD_DIM, (hh + 1) * M2_HEAD_DIM)
            y_h = _bdot((scores * decay).astype(BF16), xdb[:, cols])
            y_h = y_h + y_inter[:, hl * M2_HEAD_DIM:(hl + 1) * M2_HEAD_DIM]
            if reverse:
                o_ref[rows, cols] = y_h
            else:
                o_ref[rows, cols] = y_h + dsk_ref[:, cols] * xs[:, cols]


def _ssd_kernel(xcf_ref, dtf_ref, xcr_ref, dtr_ref, af_ref, ar_ref, dsk_ref, ef_ref, er_ref,
                yf_ref, yr_ref, stf_scr, str_scr):
    @pl.when(pl.program_id(1) == 0)
    def _():
        stf_scr[...] = jnp.zeros_like(stf_scr)
        str_scr[...] = jnp.zeros_like(str_scr)

    nchunk = yf_ref.shape[0] // M2_CHUNK
    for c in range(nchunk):
        rows_f = slice(c * M2_CHUNK, (c + 1) * M2_CHUNK)
        rows_r = slice((nchunk - 1 - c) * M2_CHUNK, (nchunk - c) * M2_CHUNK)
        _ssd_chunk(False, rows_f, xcf_ref, dtf_ref, af_ref, dsk_ref, ef_ref, stf_scr, yf_ref)
        _ssd_chunk(True, rows_r, xcr_ref, dtr_ref, ar_ref, dsk_ref, er_ref, str_scr, yr_ref)


def _ssd(pd, params, layer, nlat):
    conv_w, conv_b, dtb, a_dir, dsk, expand = params
    nb, t, wd = pd.shape
    gw = GROUP_W
    tm = ROW_TILE
    hb = tm // M2_HALO
    nh = t // M2_HALO
    xc, dtp = pl.pallas_call(
        functools.partial(_ssd_prep_kernel, nlat_blocks=nlat // tm, nblocks=t // tm),
        grid=(nb, t // tm),
        in_specs=[
            pl.BlockSpec((None, tm, wd), lambda b, j: (b, j, 0)),
            pl.BlockSpec((None, M2_HALO, M2_XBC), lambda b, j: (b, jnp.maximum(j * hb - 1, 0), 0)),
            pl.BlockSpec((None, M2_HALO, M2_XBC), lambda b, j: (b, jnp.minimum((j + 1) * hb, nh - 1), 0)),
            _layer_spec(conv_w, layer), _layer_spec(conv_b, layer), _layer_spec(dtb, layer),
        ],
        out_specs=[pl.BlockSpec((None, tm, M2_XBC), lambda b, j: (b, j, 0)),
                   pl.BlockSpec((None, tm, LANE), lambda b, j: (b, j, 0))],
        out_shape=[jax.ShapeDtypeStruct((nb, t, M2_XBC), F32), jax.ShapeDtypeStruct((nb, t, LANE), F32)],
        scratch_shapes=[pltpu.VMEM((tm + 2 * M2_HALO, M2_XBC), F32)],
        compiler_params=_cparams(("arbitrary", "arbitrary")),
        name="ssd_prep",
    )(pd, pd, pd, conv_w, conv_b, dtb)

    qn = SSD_BLOCK
    nchunks = t // qn
    nctx_chunks = nchunks - nlat // qn

    def fwd_chunk(k):
        return jnp.where(k < nctx_chunks, k + (nchunks - nctx_chunks), k - nctx_chunks)

    def rev_chunk(k):
        return nchunks - 1 - k

    def rows(n, chunk_of):
        return pl.BlockSpec((None, qn, n), lambda b, k: (b, chunk_of(k), 0))

    def direction(d):
        return pl.BlockSpec((None, None, 1, LANE), lambda b, k: (layer, d, 0, 0))

    return pl.pallas_call(
        _ssd_kernel,
        grid=(nb, nchunks),
        in_specs=[rows(M2_XBC, fwd_chunk), rows(LANE, fwd_chunk), rows(M2_XBC, rev_chunk), rows(LANE, rev_chunk),
                  direction(0), direction(1), _layer_spec(dsk, layer),
                  pl.BlockSpec((None, LANE, gw), lambda b, k: (0, 0, 0)),
                  pl.BlockSpec((None, LANE, gw), lambda b, k: (1, 0, 0))],
        out_specs=[rows(gw, fwd_chunk), rows(gw, rev_chunk)],
        out_shape=[jax.ShapeDtypeStruct((nb, t, gw), F32), jax.ShapeDtypeStruct((nb, t, gw), F32)],
        scratch_shapes=[pltpu.VMEM((M2_STATE, gw), F32), pltpu.VMEM((M2_STATE, gw), F32)],
        compiler_params=_cparams(("arbitrary", "arbitrary")),
        name="ssd_scan",
    )(xc, dtp, xc, dtp, a_dir, a_dir, dsk, expand, expand)


def _layer_norm_rows(v, g, b):
    mu = jnp.mean(v, axis=-1, keepdims=True)
    var = jnp.mean(jnp.square(v - mu), axis=-1, keepdims=True)
    return (v - mu) * lax.rsqrt(var + LN_EPS) * g + b


def _outproj_kernel(mabc_ref, yf_ref, yr_ref, z_ref, nw_ref, h_ref, gate_ref, w_ref, lg_ref, lb_ref, o_ref,
                    *, alpha):
    k1 = mabc_ref.shape[1]
    gc = GROUP_W // M2_GROUPS
    for r0 in range(0, o_ref.shape[0], ROW_TILE):
        rows = slice(r0, r0 + ROW_TILE)
        gv = (yf_ref[rows, :] + yr_ref[rows, :]) * _silu(z_ref[rows, :])
        md = []
        for g in range(M2_GROUPS):
            part = gv[:, g * gc:(g + 1) * gc]
            ms = jnp.mean(jnp.square(part), axis=-1, keepdims=True)
            md.append((part * lax.rsqrt(ms + RMS_EPS) * nw_ref[:, g * gc:(g + 1) * gc]).astype(BF16))
        mix = (_bdot(mabc_ref[rows, :].astype(BF16), w_ref[0:k1, :])
               + _bdot(jnp.concatenate(md, axis=1), w_ref[k1:w_ref.shape[0], :]))
        o_ref[rows, :] = _layer_norm_rows(alpha * h_ref[rows, :] + gate_ref[...] * mix, lg_ref[...], lb_ref[...])


def _outproj(mabc, yf, yr, pd, norm_w, h, mods, w, ln_g, ln_b, layer, nlat, alpha, latent):
    nb, rows, d = h.shape
    tm = OUT_ROW_TILE if latent else rows
    blk0 = 0 if latent else nlat // tm
    k1, k2 = mabc.shape[2], yf.shape[2]
    z_block = M2_XBC // k2

    def row(n):
        return pl.BlockSpec((None, tm, n), lambda b, j: (b, j + blk0, 0))

    return pl.pallas_call(
        functools.partial(_outproj_kernel, alpha=alpha),
        grid=(nb, rows // tm),
        in_specs=[row(k1), row(k2), row(k2), pl.BlockSpec((None, tm, k2), lambda b, j: (b, j + blk0, z_block)),
                  _layer_spec(norm_w, layer), pl.BlockSpec((None, tm, d), lambda b, j: (b, j, 0)),
                  pl.BlockSpec((None, None, 1, d), lambda b, j: (layer, b if latent else nb, 0, 2)),
                  _layer_spec(w, layer), _layer_spec(ln_g, layer), _layer_spec(ln_b, layer)],
        out_specs=pl.BlockSpec((None, tm, d), lambda b, j: (b, j, 0)),
        out_shape=jax.ShapeDtypeStruct((nb, rows, d), F32),
        compiler_params=_cparams(("arbitrary", "arbitrary")),
        name="outproj_lat" if latent else "outproj_ctx",
    )(mabc, yf, yr, pd, norm_w, h, mods, w, ln_g, ln_b)


def _ffn_kernel(*refs, alpha, grid_conv, halo, seq_len):
    if grid_conv:
        (h_ref, next_ref, sh_ref, sc_ref, gate_ref, wg_ref, wv_ref, cw_ref, cb_ref, wd_ref,
         lg_ref, lb_ref, o_ref, xm_scr, tail_scr, *chain_scr) = refs
    else:
        (h_ref, sh_ref, sc_ref, gate_ref, wg_ref, wv_ref, cw_ref, cb_ref, wd_ref,
         lg_ref, lb_ref, o_ref, xm_scr, tail_scr, *chain_scr) = refs
    g_scr, v_scr, side_scr, act_scr = chain_scr[0::4], chain_scr[1::4], chain_scr[2::4], chain_scr[3::4]
    j = pl.program_id(1)
    f = pl.program_id(2)
    nj = pl.num_programs(1)
    nf = pl.num_programs(2)
    tm = h_ref.shape[0]
    sub = g_scr[0].shape[1]
    nsub = len(g_scr)
    pad = SUBLANE

    @pl.when(f == 0)
    def _():
        def modulated(v):
            return (v * (1.0 + sc_ref[...]) + sh_ref[...]).astype(BF16)
        xm_scr[0:tm, :] = modulated(h_ref[...])
        if grid_conv:
            xm_scr[tm:tm + halo, :] = modulated(next_ref[...])
        o_ref[...] = jnp.zeros_like(o_ref)
        for side in side_scr:
            side[...] = jnp.zeros_like(side)

    if grid_conv:
        @pl.when(j == 0)
        def _():
            tail_scr[f] = jnp.zeros(tail_scr.shape[1:], F32)

    pos = lax.broadcasted_iota(jnp.int32, (tm, 1), 0)
    if grid_conv:
        keep_next = (j < nj - 1).astype(F32)
        pos = pos % GRID_W
        left_ok, right_ok = pos != 0, pos != GRID_W - 1
        row_taps = (-1, 0, 1)
    else:
        pos = pos % seq_len
        left_ok, right_ok = pos != 0, pos != seq_len - 1
        row_taps = (0,)

    for si in range(nsub):
        cs = slice(si * sub, (si + 1) * sub)
        ge = _bdot(xm_scr[...], wg_ref[:, cs].astype(BF16))
        if grid_conv:
            g_scr[si][0:halo, :] = tail_scr[f, :, cs]
            g_scr[si][halo:halo + tm, :] = ge[0:tm]
            g_scr[si][halo + tm:halo + tm + halo, :] = ge[tm:tm + halo] * keep_next
            tail_scr[f, :, cs] = ge[tm - halo:tm]
        else:
            g_scr[si][...] = ge
        v_scr[si][...] = _bdot(xm_scr[0:tm, :], wv_ref[:, cs].astype(BF16))

    rp = min(FFN_CONV_ROWS, tm) if grid_conv else tm
    for si in range(nsub):
        cs = slice(si * sub, (si + 1) * sub)
        for lo in range(0, tm, rp):
            rows = slice(lo, lo + rp)

            def tap_sum(dc, lo=lo):
                tot = None
                for dr in row_taps:
                    k = (dr + 1) * 3 + (dc + 1)
                    r0 = halo + dr * GRID_W + lo
                    term = g_scr[si][r0:r0 + rp, :] * cw_ref[k:k + 1, cs]
                    tot = term if tot is None else tot + term
                return tot

            side_scr[si][0, pad + lo:pad + lo + rp, :] = tap_sum(-1)
            side_scr[si][1, pad + lo:pad + lo + rp, :] = tap_sum(1)
            conv = tap_sum(0) + cb_ref[:, cs]
            conv = conv + jnp.where(left_ok[rows], side_scr[si][0, pad + lo - 1:pad + lo - 1 + rp, :], 0.0)
            conv = conv + jnp.where(right_ok[rows], side_scr[si][1, pad + lo + 1:pad + lo + 1 + rp, :], 0.0)
            act_scr[si][rows, :] = (_gelu(conv) * v_scr[si][rows, :]).astype(BF16)
    for si in range(nsub):
        o_ref[...] += _bdot(act_scr[si][...], wd_ref[si * sub:(si + 1) * sub, :].astype(BF16))

    @pl.when(f == nf - 1)
    def _():
        o_ref[...] = _layer_norm_rows(alpha * h_ref[...] + gate_ref[...] * o_ref[...], lg_ref[...], lb_ref[...])


def _ffn(h, mods, w_up, conv_w9, conv_b, w_down, ln_g, ln_b, layer, alpha, grid_conv):
    nb, rows, d = h.shape
    fh = w_down.shape[1]
    tf = FFN_COL_TILE
    sub = FFN_SUB_TILE
    nf = fh // tf
    if grid_conv:
        hx = h
        tm, halo = FFN_ROW_TILE, GRID_W
    else:
        hx = h.reshape(1, nb * rows, d)
        tm, halo = nb * rows, 0
    nblk = hx.shape[1] // tm
    hb = tm // GRID_W
    nhalo_blocks = rows // GRID_W

    def vec(k):
        return pl.BlockSpec((None, None, 1, d), lambda b, j, f: (layer, b if grid_conv else nb, 0, k))

    in_specs = [pl.BlockSpec((None, tm, d), lambda b, j, f: (b, j, 0))]
    args = [hx]
    if grid_conv:
        in_specs += [
            pl.BlockSpec((None, halo, d), lambda b, j, f: (b, jnp.minimum((j + 1) * hb, nhalo_blocks - 1), 0)),
        ]
        args += [hx]
    in_specs += [
        vec(3), vec(4), vec(5),
        pl.BlockSpec((None, d, tf), lambda b, j, f: (layer, 0, f)),
        pl.BlockSpec((None, d, tf), lambda b, j, f: (layer, 0, nf + f)),
        pl.BlockSpec((None, 9, tf), lambda b, j, f: (layer, 0, f)),
        pl.BlockSpec((None, 1, tf), lambda b, j, f: (layer, 0, f)),
        pl.BlockSpec((None, tf, d), lambda b, j, f: (layer, f, 0)),
        _layer_spec(ln_g, layer), _layer_spec(ln_b, layer),
    ]
    args += [mods, mods, mods, w_up, w_up, conv_w9, conv_b, w_down, ln_g, ln_b]
    return pl.pallas_call(
        functools.partial(_ffn_kernel, alpha=alpha, grid_conv=grid_conv, halo=halo, seq_len=rows),
        grid=(hx.shape[0], nblk, nf),
        in_specs=in_specs,
        out_specs=pl.BlockSpec((None, tm, d), lambda b, j, f: (b, j, 0)),
        out_shape=jax.ShapeDtypeStruct(hx.shape, F32),
        scratch_shapes=[pltpu.VMEM((tm + halo, d), BF16), pltpu.VMEM((nf, max(halo, SUBLANE), tf), F32)]
        + [pltpu.VMEM((tm + 2 * halo, sub), F32), pltpu.VMEM((tm, sub), F32),
           pltpu.VMEM((2, tm + 2 * SUBLANE, sub), F32), pltpu.VMEM((tm, sub), BF16)] * (tf // sub),
        compiler_params=_cparams(("arbitrary", "arbitrary", "arbitrary")),
        name="ffn_lat" if grid_conv else "ffn_ctx",
    )(*args).reshape(h.shape)


def kernel(x, c, ctx, c_ctx, w_ada, b_ada, w_in, w_out, ln1_g, ln1_b, ln2_g, ln2_b, s5_a_re, s5_a_im, s5_b_re, s5_b_im, s5_c_re, s5_c_im, s5_log_step, s5_d, s5_glu_w, s5_glu_b, sg_ln_g, sg_ln_b, sg_w, sg_b, pool_w, pool_b, pool_scale, m2_conv_w, m2_conv_b, m2_dt_bias, m2_a_log, m2_d, m2_norm_w, ffn_w_up, ffn_conv_w, ffn_conv_b, ffn_w_down):
    nb, nlat, d = x.shape
    nctx = ctx.shape[1]
    depth = w_ada.shape[0]
    gw = GROUP_W
    alpha = (2 * depth) ** 0.25
    assert nb + 1 <= MOD_ROWS and nlat % FFN_ROW_TILE == 0 and nlat % OUT_ROW_TILE == 0 and nctx % ROW_TILE == 0
    assert nlat % (GRID_W * SUBLANE) == 0 and FFN_ROW_TILE % GRID_W == 0 and nlat % SSD_BLOCK == 0 and nctx % SSD_BLOCK == 0
    assert w_in.shape[2] == 5 * gw + M2_XBC + 2 * M2_HEADS and (5 * gw + M2_XBC) % ROW_TILE == 0

    cmat = jnp.concatenate([c, c_ctx[None, :], jnp.zeros((MOD_ROWS - nb - 1, d), F32)], axis=0)
    mods = _ada(cmat, w_ada, b_ada).reshape(depth, MOD_ROWS, 1, 6 * d)

    def rows(v):
        return v.astype(F32).reshape(depth, 1, -1)

    fh = ffn_w_down.shape[1]
    lane_pad = LANE - 2 * M2_HEADS
    w_in_b = _cast_w_in(w_in)
    w_dt_b = jnp.pad(w_in[:, :, w_in.shape[2] - 2 * M2_HEADS:], ((0, 0), (0, 0), (0, lane_pad))).astype(BF16)
    w_out_b = w_out.astype(BF16)
    cw9 = ffn_conv_w.astype(F32).reshape(depth, 9, fh)
    s5_mats = _s5_mats(s5_a_re, s5_a_im, s5_b_re, s5_b_im, s5_c_re, s5_c_im, s5_log_step, s5_d)
    sg_btile = jnp.repeat(jnp.swapaxes(sg_b.astype(F32), 1, 2), gw // SG_HEADS, axis=2)
    mixer_params = (s5_glu_w.astype(BF16), rows(s5_glu_b), rows(sg_ln_g), rows(sg_ln_b), sg_w.astype(BF16),
                    sg_btile, pool_w.astype(BF16), rows(pool_b), rows(pool_scale))
    m2_a = -jnp.exp(m2_a_log.astype(F32))
    a_dir = (m2_a[:, :, None, :] * jnp.eye(2, dtype=F32)[None, :, :, None]).reshape(depth, 2, 1, 2 * M2_HEADS)
    a_dir = jnp.pad(a_dir, ((0, 0), (0, 0), (0, 0), (0, lane_pad)))
    head_of_ch = jnp.arange(gw) // M2_HEAD_DIM
    expand = (jnp.arange(LANE)[None, :, None] == (jnp.arange(2)[:, None, None] * M2_HEADS + head_of_ch[None, None, :])
              ).astype(BF16)
    ssd_params = (m2_conv_w.astype(F32), rows(m2_conv_b),
                  jnp.pad(m2_dt_bias.astype(F32).reshape(depth, 1, 2 * M2_HEADS), ((0, 0), (0, 0), (0, lane_pad))),
                  a_dir, rows(jnp.repeat(m2_d.astype(F32), M2_HEAD_DIM, axis=1)), expand)
    m2_nw = rows(m2_norm_w)
    ln1 = rows(ln1_g), rows(ln1_b)
    ln2 = rows(ln2_g), rows(ln2_b)
    ffn_cb = rows(ffn_conv_b)

    h_lat, h_ctx = x, ctx
    for i in range(depth):
        need_ctx = i < depth - 1
        pa, pb, pd = _inproj(h_lat, h_ctx, mods, w_in_b, w_dt_b, i)
        ya = _s5(pa, s5_mats, i, nlat)
        mabc = _mixers(ya, pb, mixer_params, i, nlat)
        yf, yr = _ssd(pd, ssd_params, i, nlat)
        mix_in = (mabc, yf, yr, pd, m2_nw)
        h1_lat = _outproj(*mix_in, h_lat, mods, w_out_b, ln1[0], ln1[1], i, nlat, alpha, True)
        if need_ctx:
            h1_ctx = _outproj(*mix_in, h_ctx, mods, w_out_b, ln1[0], ln1[1], i, nlat, alpha, False)
            h_ctx = _ffn(h1_ctx, mods, ffn_w_up, cw9, ffn_cb, ffn_w_down, ln2[0], ln2[1], i, alpha, False)
        h_lat = _ffn(h1_lat, mods, ffn_w_up, cw9, ffn_cb, ffn_w_down, ln2[0], ln2[1], i, alpha, True)
    return h_lat
```

```python
import functools

import jax
import jax.numpy as jnp
from jax import lax
from jax.experimental import pallas as pl
from jax.experimental.pallas import tpu as pltpu

F32 = jnp.float32
BF16 = jnp.bfloat16

GRID_W = 64
GROUP_W = 512
S5_CH = 16
S5_Q = 16
S5_RELAYOUT_UNROLL = 4
SG_HEADS = 4
SG_CHUNK = 128
POOL_WINDOWS = (2, 4, 8, 16)
POOL_DIM = GROUP_W // len(POOL_WINDOWS)
POOL_HALO = 8
M2_HEAD_DIM = 64
M2_HEADS = GROUP_W // M2_HEAD_DIM
M2_STATE = 128
M2_GROUPS = 2
M2_CONV = 4
M2_CHUNK = 128
M2_XBC = GROUP_W + 2 * M2_GROUPS * M2_STATE
M2_HALO = 8
SSD_BLOCK = 2 * M2_CHUNK
LN_EPS = 1e-5
RMS_EPS = 1e-5

LANE = 128
SUBLANE = 8
VMEM_LIMIT = 56 * 1024 * 1024
ROW_TILE = 256
OUT_ROW_TILE = 512
FFN_ROW_TILE = 512
FFN_COL_TILE = 512
FFN_SUB_TILE = 256
ADA_COL_TILE = 1024
MOD_ROWS = 8


def _cparams(sem):
    return pltpu.CompilerParams(dimension_semantics=sem, vmem_limit_bytes=VMEM_LIMIT)


def _bdot(a, b):
    return jnp.dot(a, b, preferred_element_type=F32)


def _gelu(x):
    return jax.nn.gelu(x)


def _silu(x):
    return x * jax.nn.sigmoid(x)


def _ada_kernel(c_ref, w_ref, b_ref, o_ref):
    c = c_ref[...]
    s = _silu(c).astype(BF16)
    o_ref[...] = _bdot(s, w_ref[...].astype(BF16)) + b_ref[...]


def _ada(cmat, w_ada, b_ada):
    depth, d, n = w_ada.shape
    tn = ADA_COL_TILE
    return pl.pallas_call(
        _ada_kernel,
        grid=(depth, n // tn),
        in_specs=[
            pl.BlockSpec((MOD_ROWS, d), lambda l, j: (0, 0)),
            pl.BlockSpec((None, d, tn), lambda l, j: (l, 0, j)),
            pl.BlockSpec((None, 1, tn), lambda l, j: (l, 0, j)),
        ],
        out_specs=pl.BlockSpec((None, MOD_ROWS, tn), lambda l, j: (l, 0, j)),
        out_shape=jax.ShapeDtypeStruct((depth, MOD_ROWS, n), F32),
        compiler_params=_cparams(("arbitrary", "arbitrary")),
        name="ada",
    )(cmat, w_ada, b_ada.reshape(depth, 1, n))


def _mod_spec(d, k, layer, nbatch, nlat_blocks):
    return pl.BlockSpec((None, None, 1, d),
                        lambda b, j, *_: (layer, jnp.where(j >= nlat_blocks, nbatch, b), 0, k))


def _layer_spec(arr, layer):
    tail = arr.shape[1:]
    return pl.BlockSpec((None,) + tail, lambda *_: (layer,) + (0,) * len(tail), pipeline_mode=pl.Buffered(1))


def _cast_w_in_kernel(wt_ref, wb_ref):
    wb_ref[...] = wt_ref[...].T.astype(BF16)


def _cast_w_in(w_in):
    depth, d, n = w_in.shape
    tn = ROW_TILE
    wt = jnp.swapaxes(w_in, 1, 2)
    return pl.pallas_call(
        _cast_w_in_kernel,
        grid=(depth, n // tn),
        in_specs=[pl.BlockSpec((None, tn, d), lambda l, i: (l, i, 0))],
        out_specs=pl.BlockSpec((None, d, tn), lambda l, i: (l, 0, i)),
        out_shape=jax.ShapeDtypeStruct((depth, d, n // tn * tn), BF16),
        compiler_params=_cparams(("arbitrary", "arbitrary")),
        name="cast_w_in",
    )(wt)


def _stream_specs(tm, d, nlat_blocks):
    lat = pl.BlockSpec((None, tm, d), lambda b, j: (b, jnp.minimum(j, nlat_blocks - 1), 0))
    ctx = pl.BlockSpec((None, tm, d), lambda b, j: (b, jnp.maximum(j - nlat_blocks, 0), 0))
    return lat, ctx


def _inproj_kernel(hl_ref, hc_ref, sh_ref, sc_ref, w_ref, wdt_ref, oa_ref, ob_ref, od_ref, xm_scr, *, nlat_blocks):
    h = jnp.where(pl.program_id(1) < nlat_blocks, hl_ref[...], hc_ref[...])
    xm_scr[...] = (h * (1.0 + sc_ref[...]) + sh_ref[...]).astype(BF16)

    xm = xm_scr[...]
    gw = GROUP_W
    oa_ref[...] = _bdot(xm, w_ref[:, 0:gw])
    ob_ref[...] = _bdot(xm, w_ref[:, gw:4 * gw])
    od_ref[:, 0:M2_XBC] = _bdot(xm, w_ref[:, 5 * gw:5 * gw + M2_XBC])
    od_ref[:, M2_XBC:M2_XBC + gw] = _bdot(xm, w_ref[:, 4 * gw:5 * gw])
    od_ref[:, M2_XBC + gw:M2_XBC + gw + LANE] = _bdot(xm, wdt_ref[...])


def _inproj(h_lat, h_ctx, mods, w, w_dt, layer):
    nb, nlat, d = h_lat.shape
    t = nlat + h_ctx.shape[1]
    tm = ROW_TILE
    na, nbw, nd = GROUP_W, 3 * GROUP_W, M2_XBC + GROUP_W + LANE
    lat_spec, ctx_spec = _stream_specs(tm, d, nlat // tm)

    def row(n):
        return pl.BlockSpec((None, tm, n), lambda b, j: (b, j, 0))

    return pl.pallas_call(
        functools.partial(_inproj_kernel, nlat_blocks=nlat // tm),
        grid=(nb, t // tm),
        in_specs=[lat_spec, ctx_spec, _mod_spec(d, 0, layer, nb, nlat // tm),
                  _mod_spec(d, 1, layer, nb, nlat // tm), _layer_spec(w, layer), _layer_spec(w_dt, layer)],
        out_specs=[row(na), row(nbw), row(nd)],
        out_shape=[jax.ShapeDtypeStruct((nb, t, n), F32) for n in (na, nbw, nd)],
        scratch_shapes=[pltpu.VMEM((tm, d), BF16)],
        compiler_params=_cparams(("arbitrary", "arbitrary")),
        name="inproj",
    )(h_lat, h_ctx, mods, mods, w, w_dt)


def _s5_mats(a_re, a_im, b_re, b_im, c_re, c_im, log_step, d_skip):
    q, hch = S5_Q, S5_CH
    nl, _, g, p = a_re.shape
    lre, lim = a_re.astype(F32), a_im.astype(F32)
    step = jnp.exp(log_step.astype(F32))[..., None]
    mag = jnp.exp(lre * step)
    ar, ai = mag * jnp.cos(lim * step), mag * jnp.sin(lim * step)
    den = lre * lre + lim * lim
    qr = ((ar - 1.0) * lre + ai * lim) / den
    qi = (ai * lre - (ar - 1.0) * lim) / den
    bre, bim = b_re.astype(F32), b_im.astype(F32)
    bbr = qr[..., None] * bre - qi[..., None] * bim
    bbi = qr[..., None] * bim + qi[..., None] * bre
    cre, cim = c_re.astype(F32), c_im.astype(F32)

    pr, pi = ar[:, :, :, None], ai[:, :, :, None]
    while pr.shape[3] < q:
        lr, li = pr[:, :, :, -1:], pi[:, :, :, -1:]
        pr, pi = (jnp.concatenate([pr, pr * lr - pi * li], axis=3), jnp.concatenate([pi, pr * li + pi * lr], axis=3))
    pk_re = jnp.concatenate([jnp.ones_like(pr[:, :, :, :1]), pr], axis=3)
    pk_im = jnp.concatenate([jnp.zeros_like(pi[:, :, :, :1]), pi], axis=3)
    pt_re, pt_im = jnp.swapaxes(pk_re, 3, 4), jnp.swapaxes(pk_im, 3, 4)

    def lag_kernels(d, pr, pi):
        ctr = jnp.swapaxes(cre[:, d], -1, -2)[:, :, :, None]
        cti = jnp.swapaxes(cim[:, d], -1, -2)[:, :, :, None]
        ca_re = (ctr * pr[..., None] - cti * pi[..., None]).reshape(nl, g, p, 1, q * hch)
        ca_im = (ctr * pi[..., None] + cti * pr[..., None]).reshape(nl, g, p, 1, q * hch)
        return jnp.sum(bbr[:, d][..., None] * ca_re - bbi[:, d][..., None] * ca_im, axis=2)

    lag_f = lag_kernels(0, pt_re[:, 0, :, :, :q], pt_im[:, 0, :, :, :q])
    lag_r = lag_kernels(1, jnp.flip(pt_re[:, 1, :, :, :q], axis=3), jnp.flip(pt_im[:, 1, :, :, :q], axis=3))
    dmat = jnp.eye(hch, dtype=F32) * d_skip.astype(F32).reshape(nl, g, 1, hch)
    mid = lag_r[..., (q - 1) * hch:] + lag_f[..., :hch] + dmat
    kk = jnp.concatenate([lag_r[..., :(q - 1) * hch], mid, lag_f[..., hch:],
                          jnp.zeros(mid.shape[:-1] + (hch,), F32)], axis=-1)

    def state_in(d, pr, pi):
        btr = jnp.swapaxes(bbr[:, d], -1, -2)[:, :, None]
        bti = jnp.swapaxes(bbi[:, d], -1, -2)[:, :, None]
        pr, pi = pr[:, :, :, None], pi[:, :, :, None]
        m = jnp.concatenate([pr * btr - pi * bti, pr * bti + pi * btr], axis=-1)
        return m.reshape(nl, g, q * hch, 2 * p)

    sb = jnp.concatenate([state_in(0, jnp.flip(pk_re[:, 0, :, :q], axis=2), jnp.flip(pk_im[:, 0, :, :q], axis=2)),
                          state_in(1, pk_re[:, 1, :, :q], pk_im[:, 1, :, :q])], axis=-1)

    def state_out(d, pr, pi):
        ctr = jnp.swapaxes(cre[:, d], -1, -2)[:, :, :, None]
        cti = jnp.swapaxes(cim[:, d], -1, -2)[:, :, :, None]
        pr, pi = pr[..., None], pi[..., None]
        m = jnp.concatenate([ctr * pr - cti * pi, -(ctr * pi + cti * pr)], axis=2)
        return m.reshape(nl, g, 2 * p, q * hch)

    w2 = jnp.concatenate([state_out(0, pt_re[:, 0, :, :, 1:], pt_im[:, 0, :, :, 1:]),
                          state_out(1, jnp.flip(pt_re[:, 1, :, :, 1:], axis=3), jnp.flip(pt_im[:, 1, :, :, 1:], axis=3))],
                         axis=2)

    aqr, aqi = pk_re[:, :, :, q], pk_im[:, :, :, q]
    a1 = jnp.concatenate([aqr, aqr], axis=-1).reshape(nl, 2, g * 2 * p)
    a2 = jnp.concatenate([-aqi, aqi], axis=-1).reshape(nl, 2, g * 2 * p)
    acoef = jnp.stack([a1[:, 0], a2[:, 0], a1[:, 1], a2[:, 1]], axis=1)
    return kk, sb, w2, acoef


def _granule_transpose(v):
    n = len(v)
    slot = lax.broadcasted_iota(jnp.int32, v[0].shape, 1) // S5_CH
    v = list(v)
    j = n // 2
    while j >= 1:
        keep = (slot & j) == 0
        for i in range(n):
            if i & j == 0:
                a, b = v[i], v[i + j]
                v[i] = jnp.where(keep, a, pltpu.roll(b, j * S5_CH, axis=1))
                v[i + j] = jnp.where(keep, pltpu.roll(a, LANE - j * S5_CH, axis=1), b)
        j //= 2
    return v


def _split_bf16(x):
    hi = x.astype(BF16)
    return hi, (x - hi.astype(F32)).astype(BF16)


def _dot3(xs, w):
    x_hi, x_lo = xs
    w_hi, w_lo = _split_bf16(w)
    return _bdot(x_hi, w_hi) + (_bdot(x_lo, w_hi) + _bdot(x_hi, w_lo))


def _s5_kernel(pa_ref, kk_ref, sb_ref, w2_ref, ac_ref, ya_ref, x_scr, s_scr, h_scr, t_scr, *, nbatch, nrows, nctx):
    gb = LANE // S5_CH
    halves = S5_Q // gb
    qh = S5_Q * S5_CH
    sw = LANE
    nlat = nrows - nctx
    nrc = nrows // SUBLANE
    tok_per_rc = SUBLANE * S5_Q

    def relayout(it, to_rows):
        for u in range(S5_RELAYOUT_UNROLL):
            idx = it * S5_RELAYOUT_UNROLL + u
            b = idx // nrc
            rc = idx % nrc
            r0 = pl.multiple_of(b * nrows + rc * SUBLANE, SUBLANE)
            for half in range(halves):
                lanes = slice(half * LANE, (half + 1) * LANE)

                def tok(s8, half=half, rc=rc):
                    return pl.ds(rc * tok_per_rc + half * gb + s8, SUBLANE, stride=S5_Q)

                if to_rows:
                    v = _granule_transpose([pa_ref[b, tok(s8), :] for s8 in range(gb)])
                    for gi in range(gb):
                        x_scr[gi, pl.ds(r0, SUBLANE), lanes] = v[gi]
                else:
                    v = _granule_transpose([x_scr[gi, pl.ds(r0, SUBLANE), lanes] for gi in range(gb)])
                    for s8 in range(gb):
                        ya_ref[b, tok(s8), :] = v[s8]
        return 0

    nblocks = nbatch * nrc // S5_RELAYOUT_UNROLL
    lax.fori_loop(0, nblocks, lambda it, c: relayout(it, True), 0)

    for gi in range(gb):
        kkv = kk_ref[gi]
        for s in range(S5_Q):
            shift = (S5_Q - 1 - s) * S5_CH
            t_scr[s * S5_CH:(s + 1) * S5_CH, :] = pltpu.roll(kkv, kkv.shape[1] - shift, axis=1)[:, 0:qh]
        xs = _split_bf16(x_scr[gi])
        x_scr[gi] = _dot3(xs, t_scr[...])
        rs = _dot3(xs, sb_ref[gi])
        for d in range(2):
            sd = rs[:, d * sw:(d + 1) * sw]
            s_scr[d, gi] = sd
            s_scr[2 + d, gi] = pltpu.roll(sd, sw // 2, axis=1)

    zero = jnp.zeros((nbatch, sw), F32)

    def body(s, carry):
        f_rows = pl.ds(jnp.where(s < nctx, s + nlat, s - nctx), nbatch, stride=nrows)
        r_rows = pl.ds(nrows - 1 - s, nbatch, stride=nrows)
        out = []
        for gi in range(gb):
            hf, hfs, hr, hrs = carry[4 * gi:4 * gi + 4]
            a1f, a2f, a1r, a2r = (ac_ref[k:k + 1, gi * sw:(gi + 1) * sw] for k in range(4))
            h_scr[0, gi, f_rows, :] = hf
            h_scr[1, gi, r_rows, :] = hr
            out += [a1f * hf + a2f * hfs + s_scr[0, gi, f_rows, :],
                    a1f * hfs - a2f * hf + s_scr[2, gi, f_rows, :],
                    a1r * hr + a2r * hrs + s_scr[1, gi, r_rows, :],
                    a1r * hrs - a2r * hr + s_scr[3, gi, r_rows, :]]
        return tuple(out)

    lax.fori_loop(0, nrows, body, (zero,) * (4 * gb))

    for gi in range(gb):
        hin = jnp.concatenate([h_scr[0, gi], h_scr[1, gi]], axis=1)
        x_scr[gi] = x_scr[gi] + _dot3(_split_bf16(hin), w2_ref[gi])

    lax.fori_loop(0, nblocks, lambda it, c: relayout(it, False), 0)


def _s5(pa, mats, layer, nlat):
    kk, sb, w2, acoef = mats
    nb, t, gw = pa.shape
    q = S5_Q
    nrows = t // q
    qh = q * S5_CH
    gb = LANE // S5_CH
    assert nrows % SUBLANE == 0 and q % gb == 0 and w2.shape[2] == 2 * LANE and sb.shape[3] == 2 * LANE
    assert (nb * nrows // SUBLANE) % S5_RELAYOUT_UNROLL == 0
    return pl.pallas_call(
        functools.partial(_s5_kernel, nbatch=nb, nrows=nrows, nctx=(t - nlat) // q),
        grid=(gw // LANE,),
        in_specs=[
            pl.BlockSpec((nb, t, LANE), lambda i: (0, 0, i)),
            pl.BlockSpec((None, gb, S5_CH, 2 * qh), lambda i: (layer, i, 0, 0)),
            pl.BlockSpec((None, gb, qh, 2 * LANE), lambda i: (layer, i, 0, 0)),
            pl.BlockSpec((None, gb, 2 * LANE, qh), lambda i: (layer, i, 0, 0)),
            pl.BlockSpec((None, 4, gb * LANE), lambda i: (layer, 0, i)),
        ],
        out_specs=pl.BlockSpec((nb, t, LANE), lambda i: (0, 0, i)),
        out_shape=jax.ShapeDtypeStruct((nb, t, gw), F32),
        scratch_shapes=[pltpu.VMEM((gb, nb * nrows, qh), F32),
                        pltpu.VMEM((4, gb, nb * nrows, LANE), F32),
                        pltpu.VMEM((2, gb, nb * nrows, LANE), F32),
                        pltpu.VMEM((qh, qh), F32)],
        compiler_params=_cparams(("arbitrary",)),
        name="s5",
    )(pa, kk, sb, w2, acoef)


def _mixers_kernel(ya_ref, pb_ref, prev_ref, next_ref, gluw_ref, glub_ref, lng_ref, lnb_ref, sgw_ref,
                   sgb_ref, pw_ref, pbias_ref, pscale_ref, o_ref, ext_scr, *, nlat_blocks, nblocks, nlat, nctx):
    j = pl.program_id(1)
    tm = ya_ref.shape[0]
    gw = GROUP_W

    z = _gelu(ya_ref[...])
    gate = jax.nn.sigmoid(_bdot(z.astype(BF16), gluw_ref[...]) + glub_ref[...])
    o_ref[:, 0:gw] = z * gate

    hd = gw // SG_HEADS
    u = _gelu(pb_ref[:, 0:gw])
    v = _gelu(pb_ref[:, gw:2 * gw])
    for hh in range(SG_HEADS):
        vh = v[:, hh * hd:(hh + 1) * hd]
        mu = jnp.mean(vh, axis=-1, keepdims=True)
        var = jnp.mean(jnp.square(vh - mu), axis=-1, keepdims=True)
        vn = ((vh - mu) * lax.rsqrt(var + LN_EPS) * lng_ref[:, hh * hd:(hh + 1) * hd]
              + lnb_ref[:, hh * hd:(hh + 1) * hd]).astype(BF16)
        for cchunk in range(tm // SG_CHUNK):
            rows = slice(cchunk * SG_CHUNK, (cchunk + 1) * SG_CHUNK)
            s = _bdot(sgw_ref[hh], vn[rows]) + sgb_ref[:, hh * hd:(hh + 1) * hd]
            o_ref[rows, gw + hh * hd:gw + (hh + 1) * hd] = u[rows, hh * hd:(hh + 1) * hd] * s

    is_first = jnp.logical_or(j == 0, j == nlat_blocks)
    is_last = jnp.logical_or(j == nlat_blocks - 1, j == nblocks - 1)
    p = pb_ref[:, 2 * gw:3 * gw]
    hal = POOL_HALO
    ext_scr[0:hal, :] = jnp.where(is_first, 0.0, prev_ref[...])
    ext_scr[hal:hal + tm, :] = p
    ext_scr[hal + tm:hal + tm + hal, :] = jnp.where(is_last, 0.0, next_ref[...])
    in_ctx = j >= nlat_blocks
    seq_len = jnp.where(in_ctx, nctx, nlat)
    t0 = (j - jnp.where(in_ctx, nlat_blocks, 0)) * tm
    tpos = t0 + lax.broadcasted_iota(jnp.int32, (tm, 1), 0)
    pd = POOL_DIM
    for gi, win in enumerate(POOL_WINDOWS):
        cols = slice(gi * pd, (gi + 1) * pd)
        tot = jnp.zeros((tm, pd), F32)
        for off in range(-(win // 2), win // 2):
            tot = tot + ext_scr[hal + off:hal + off + tm, cols]
        lo = jnp.maximum(tpos - win // 2, 0)
        hi = jnp.minimum(tpos + win // 2 - 1, seq_len - 1)
        mean = tot / (hi - lo + 1).astype(F32)
        yv = _bdot((mean - p[:, cols]).astype(BF16), pw_ref[gi]) + pbias_ref[:, cols]
        o_ref[:, 2 * gw + gi * pd:2 * gw + (gi + 1) * pd] = yv * pscale_ref[:, cols]


def _mixers(ya, pb, params, layer, nlat):
    nb, t, gw = ya.shape
    tm = ROW_TILE
    nblocks = t // tm
    nlat_blocks = nlat // tm
    hb = tm // POOL_HALO
    nh = t // POOL_HALO

    return pl.pallas_call(
        functools.partial(_mixers_kernel, nlat_blocks=nlat_blocks, nblocks=nblocks, nlat=nlat, nctx=t - nlat),
        grid=(nb, nblocks),
        in_specs=[
            pl.BlockSpec((None, tm, gw), lambda b, j: (b, j, 0)),
            pl.BlockSpec((None, tm, 3 * gw), lambda b, j: (b, j, 0)),
            pl.BlockSpec((None, POOL_HALO, gw), lambda b, j: (b, jnp.maximum(j * hb - 1, 0), 2)),
            pl.BlockSpec((None, POOL_HALO, gw), lambda b, j: (b, jnp.minimum((j + 1) * hb, nh - 1), 2)),
        ] + [_layer_spec(v, layer) for v in params],
        out_specs=pl.BlockSpec((None, tm, 3 * gw), lambda b, j: (b, j, 0)),
        out_shape=jax.ShapeDtypeStruct((nb, t, 3 * gw), F32),
        scratch_shapes=[pltpu.VMEM((tm + 2 * POOL_HALO, gw), F32)],
        compiler_params=_cparams(("arbitrary", "arbitrary")),
        name="mixers",
    )(ya, pb, pb, pb, *params)


def _ssd_prep_kernel(pd_ref, prev_ref, next_ref, cw_ref, cb_ref, dtb_ref, xc_ref, dt_ref, ext_scr,
                     *, nlat_blocks, nblocks):
    j = pl.program_id(1)
    tm = pd_ref.shape[0]
    nx = M2_XBC
    hal = M2_HALO
    is_first = jnp.logical_or(j == 0, j == nlat_blocks)
    is_last = jnp.logical_or(j == nlat_blocks - 1, j == nblocks - 1)
    ext_scr[0:hal, :] = jnp.where(is_first, 0.0, prev_ref[...])
    ext_scr[hal:hal + tm, :] = pd_ref[:, 0:nx]
    ext_scr[hal + tm:hal + tm + hal, :] = jnp.where(is_last, 0.0, next_ref[...])
    acc = jnp.zeros((tm, nx), F32) + cb_ref[...]
    for tap in range(M2_CONV):
        off = hal + tap - M2_CONV // 2
        acc = acc + ext_scr[off:off + tm, :] * cw_ref[tap:tap + 1, :]
    xc_ref[...] = _silu(acc)
    xdt = pd_ref[:, nx + GROUP_W:nx + GROUP_W + LANE] + dtb_ref[...]
    dt_ref[...] = jnp.maximum(xdt, 0.0) + jnp.log1p(jnp.exp(-jnp.abs(xdt)))


def _split3_bf16(x):
    x1 = x.astype(BF16)
    r1 = x - x1.astype(F32)
    x2 = r1.astype(BF16)
    return x1, x2, (r1 - x2.astype(F32)).astype(BF16)


def _ssd_chunk(reverse, rows, xc_ref, dt_ref, a_ref, dsk_ref, e_ref, st_scr, o_ref):
    qn = M2_CHUNK
    gw = GROUP_W
    lane0 = M2_HEADS if reverse else 0
    xs = xc_ref[rows, 0:gw]
    nbc = M2_GROUPS * M2_STATE
    bm = xc_ref[rows, gw:gw + nbc]
    cm = xc_ref[rows, gw + nbc:gw + 2 * nbc]

    dt = dt_ref[rows, :]
    da = dt * a_ref[...]
    ri = lax.broadcasted_iota(jnp.int32, (qn, qn), 0)
    ci = lax.broadcasted_iota(jnp.int32, (qn, qn), 1)
    causal = (ri <= ci) if reverse else (ri >= ci)
    mask = causal.astype(BF16)
    d1, d2, d3 = _split3_bf16(da)
    a_col = _bdot(mask, d1) + (_bdot(mask, d2) + _bdot(mask, d3))
    a_row = a_col.T
    tot_row = a_col[0:1, :] if reverse else a_col[qn - 1:qn, :]

    per_head = jnp.concatenate([dt, jnp.exp(a_col), jnp.exp(tot_row - a_col)], axis=0)
    p1, p2 = _split_bf16(per_head)
    e = e_ref[...]
    per_ch = _bdot(p1, e) + _bdot(p2, e)
    dt_x, ea_x, de_x = per_ch[0:qn], per_ch[qn:2 * qn], per_ch[2 * qn:3 * qn]
    et_x = ea_x[0:1, :] if reverse else ea_x[qn - 1:qn, :]
    xd = xs * dt_x
    xdw = (xd * de_x).astype(BF16)
    xdb = xd.astype(BF16)

    hpg = M2_HEADS // M2_GROUPS
    gcols = hpg * M2_HEAD_DIM
    for g in range(M2_GROUPS):
        bg = bm[:, g * M2_STATE:(g + 1) * M2_STATE]
        cg = cm[:, g * M2_STATE:(g + 1) * M2_STATE].astype(BF16)
        bgt = bg.T.astype(BF16)
        scores = _bdot(cg, bgt)
        st_old = st_scr[:, g * gcols:(g + 1) * gcols]
        y_inter = _bdot(cg, st_old.astype(BF16)) * ea_x[:, g * gcols:(g + 1) * gcols]
        st_new = _bdot(bgt, xdw[:, g * gcols:(g + 1) * gcols])
        st_scr[:, g * gcols:(g + 1) * gcols] = et_x[:, g * gcols:(g + 1) * gcols] * st_old + st_new
        for hl in range(hpg):
            hh = g * hpg + hl
            seg = a_col[:, lane0 + hh:lane0 + hh + 1] - a_row[lane0 + hh:lane0 + hh + 1, :]
            decay = jnp.exp(jnp.where(causal, seg, -jnp.inf))
            cols = slice(hh * M2_HEAD_DIM, (hh + 1) * M2_HEAD_DIM)
            y_h = _bdot((scores * decay).astype(BF16), xdb[:, cols])
            y_h = y_h + y_inter[:, hl * M2_HEAD_DIM:(hl + 1) * M2_HEAD_DIM]
            if reverse:
                o_ref[rows, cols] = y_h
            else:
                o_ref[rows, cols] = y_h + dsk_ref[:, cols] * xs[:, cols]


def _ssd_kernel(xcf_ref, dtf_ref, xcr_ref, dtr_ref, af_ref, ar_ref, dsk_ref, ef_ref, er_ref,
                yf_ref, yr_ref, stf_scr, str_scr):
    @pl.when(pl.program_id(1) == 0)
    def _():
        stf_scr[...] = jnp.zeros_like(stf_scr)
        str_scr[...] = jnp.zeros_like(str_scr)

    nchunk = yf_ref.shape[0] // M2_CHUNK
    for c in range(nchunk):
        rows_f = slice(c * M2_CHUNK, (c + 1) * M2_CHUNK)
        rows_r = slice((nchunk - 1 - c) * M2_CHUNK, (nchunk - c) * M2_CHUNK)
        _ssd_chunk(False, rows_f, xcf_ref, dtf_ref, af_ref, dsk_ref, ef_ref, stf_scr, yf_ref)
        _ssd_chunk(True, rows_r, xcr_ref, dtr_ref, ar_ref, dsk_ref, er_ref, str_scr, yr_ref)


def _ssd(pd, params, layer, nlat):
    conv_w, conv_b, dtb, a_dir, dsk, expand = params
    nb, t, wd = pd.shape
    gw = GROUP_W
    tm = ROW_TILE
    hb = tm // M2_HALO
    nh = t // M2_HALO
    xc, dtp = pl.pallas_call(
        functools.partial(_ssd_prep_kernel, nlat_blocks=nlat // tm, nblocks=t // tm),
        grid=(nb, t // tm),
        in_specs=[
            pl.BlockSpec((None, tm, wd), lambda b, j: (b, j, 0)),
            pl.BlockSpec((None, M2_HALO, M2_XBC), lambda b, j: (b, jnp.maximum(j * hb - 1, 0), 0)),
            pl.BlockSpec((None, M2_HALO, M2_XBC), lambda b, j: (b, jnp.minimum((j + 1) * hb, nh - 1), 0)),
            _layer_spec(conv_w, layer), _layer_spec(conv_b, layer), _layer_spec(dtb, layer),
        ],
        out_specs=[pl.BlockSpec((None, tm, M2_XBC), lambda b, j: (b, j, 0)),
                   pl.BlockSpec((None, tm, LANE), lambda b, j: (b, j, 0))],
        out_shape=[jax.ShapeDtypeStruct((nb, t, M2_XBC), F32), jax.ShapeDtypeStruct((nb, t, LANE), F32)],
        scratch_shapes=[pltpu.VMEM((tm + 2 * M2_HALO, M2_XBC), F32)],
        compiler_params=_cparams(("arbitrary", "arbitrary")),
        name="ssd_prep",
    )(pd, pd, pd, conv_w, conv_b, dtb)

    qn = SSD_BLOCK
    nchunks = t // qn
    nctx_chunks = nchunks - nlat // qn

    def fwd_chunk(k):
        return jnp.where(k < nctx_chunks, k + (nchunks - nctx_chunks), k - nctx_chunks)

    def rev_chunk(k):
        return nchunks - 1 - k

    def rows(n, chunk_of):
        return pl.BlockSpec((None, qn, n), lambda b, k: (b, chunk_of(k), 0))

    def direction(d):
        return pl.BlockSpec((None, None, 1, LANE), lambda b, k: (layer, d, 0, 0))

    return pl.pallas_call(
        _ssd_kernel,
        grid=(nb, nchunks),
        in_specs=[rows(M2_XBC, fwd_chunk), rows(LANE, fwd_chunk), rows(M2_XBC, rev_chunk), rows(LANE, rev_chunk),
                  direction(0), direction(1), _layer_spec(dsk, layer),
                  pl.BlockSpec((None, LANE, gw), lambda b, k: (0, 0, 0)),
                  pl.BlockSpec((None, LANE, gw), lambda b, k: (1, 0, 0))],
        out_specs=[rows(gw, fwd_chunk), rows(gw, rev_chunk)],
        out_shape=[jax.ShapeDtypeStruct((nb, t, gw), F32), jax.ShapeDtypeStruct((nb, t, gw), F32)],
        scratch_shapes=[pltpu.VMEM((M2_STATE, gw), F32), pltpu.VMEM((M2_STATE, gw), F32)],
        compiler_params=_cparams(("arbitrary", "arbitrary")),
        name="ssd_scan",
    )(xc, dtp, xc, dtp, a_dir, a_dir, dsk, expand, expand)


def _layer_norm_rows(v, g, b):
    mu = jnp.mean(v, axis=-1, keepdims=True)
    var = jnp.mean(jnp.square(v - mu), axis=-1, keepdims=True)
    return (v - mu) * lax.rsqrt(var + LN_EPS) * g + b


def _outproj_kernel(mabc_ref, yf_ref, yr_ref, z_ref, nw_ref, h_ref, gate_ref, w_ref, lg_ref, lb_ref, o_ref,
                    *, alpha):
    k1 = mabc_ref.shape[1]
    gc = GROUP_W // M2_GROUPS
    for r0 in range(0, o_ref.shape[0], ROW_TILE):
        rows = slice(r0, r0 + ROW_TILE)
        gv = (yf_ref[rows, :] + yr_ref[rows, :]) * _silu(z_ref[rows, :])
        md = []
        for g in range(M2_GROUPS):
            part = gv[:, g * gc:(g + 1) * gc]
            ms = jnp.mean(jnp.square(part), axis=-1, keepdims=True)
            md.append((part * lax.rsqrt(ms + RMS_EPS) * nw_ref[:, g * gc:(g + 1) * gc]).astype(BF16))
        mix = (_bdot(mabc_ref[rows, :].astype(BF16), w_ref[0:k1, :])
               + _bdot(jnp.concatenate(md, axis=1), w_ref[k1:w_ref.shape[0], :]))
        o_ref[rows, :] = _layer_norm_rows(alpha * h_ref[rows, :] + gate_ref[...] * mix, lg_ref[...], lb_ref[...])


def _outproj(mabc, yf, yr, pd, norm_w, h, mods, w, ln_g, ln_b, layer, nlat, alpha, latent):
    nb, rows, d = h.shape
    tm = OUT_ROW_TILE if latent else rows
    blk0 = 0 if latent else nlat // tm
    k1, k2 = mabc.shape[2], yf.shape[2]
    z_block = M2_XBC // k2

    def row(n):
        return pl.BlockSpec((None, tm, n), lambda b, j: (b, j + blk0, 0))

    return pl.pallas_call(
        functools.partial(_outproj_kernel, alpha=alpha),
        grid=(nb, rows // tm),
        in_specs=[row(k1), row(k2), row(k2), pl.BlockSpec((None, tm, k2), lambda b, j: (b, j + blk0, z_block)),
                  _layer_spec(norm_w, layer), pl.BlockSpec((None, tm, d), lambda b, j: (b, j, 0)),
                  pl.BlockSpec((None, None, 1, d), lambda b, j: (layer, b if latent else nb, 0, 2)),
                  _layer_spec(w, layer), _layer_spec(ln_g, layer), _layer_spec(ln_b, layer)],
        out_specs=pl.BlockSpec((None, tm, d), lambda b, j: (b, j, 0)),
        out_shape=jax.ShapeDtypeStruct((nb, rows, d), F32),
        compiler_params=_cparams(("arbitrary", "arbitrary")),
        name="outproj_lat" if latent else "outproj_ctx",
    )(mabc, yf, yr, pd, norm_w, h, mods, w, ln_g, ln_b)


def _ffn_kernel(*refs, alpha, grid_conv, halo, seq_len):
    if grid_conv:
        (h_ref, next_ref, sh_ref, sc_ref, gate_ref, wg_ref, wv_ref, cw_ref, cb_ref, wd_ref,
         lg_ref, lb_ref, o_ref, xm_scr, tail_scr, *chain_scr) = refs
    else:
        (h_ref, sh_ref, sc_ref, gate_ref, wg_ref, wv_ref, cw_ref, cb_ref, wd_ref,
         lg_ref, lb_ref, o_ref, xm_scr, tail_scr, *chain_scr) = refs
    g_scr, v_scr, side_scr = chain_scr[0::3], chain_scr[1::3], chain_scr[2::3]
    j = pl.program_id(1)
    f = pl.program_id(2)
    nj = pl.num_programs(1)
    nf = pl.num_programs(2)
    tm = h_ref.shape[0]
    sub = g_scr[0].shape[1]
    nsub = len(g_scr)
    pad = SUBLANE

    @pl.when(f == 0)
    def _():
        def modulated(v):
            return (v * (1.0 + sc_ref[...]) + sh_ref[...]).astype(BF16)
        xm_scr[0:tm, :] = modulated(h_ref[...])
        if grid_conv:
            xm_scr[tm:tm + halo, :] = modulated(next_ref[...])
        o_ref[...] = jnp.zeros_like(o_ref)
        for side in side_scr:
            side[:, 0:pad, :] = jnp.zeros((2, pad, sub), F32)
            side[:, pad + tm:pad + tm + pad, :] = jnp.zeros((2, pad, sub), F32)

    if grid_conv:
        @pl.when(j == 0)
        def _():
            tail_scr[f] = jnp.zeros(tail_scr.shape[1:], F32)

    pos = lax.broadcasted_iota(jnp.int32, (tm, 1), 0)
    if grid_conv:
        keep_next = (j < nj - 1).astype(F32)
        pos = pos % GRID_W
        left_ok, right_ok = pos != 0, pos != GRID_W - 1
        row_taps = (-1, 0, 1)
    else:
        pos = pos % seq_len
        left_ok, right_ok = pos != 0, pos != seq_len - 1
        row_taps = (0,)

    for si in range(nsub):
        cs = slice(si * sub, (si + 1) * sub)
        ge = _bdot(xm_scr[...], wg_ref[:, cs].astype(BF16))
        if grid_conv:
            g_scr[si][0:halo, :] = tail_scr[f, :, cs]
            g_scr[si][halo:halo + tm, :] = ge[0:tm]
            g_scr[si][halo + tm:halo + tm + halo, :] = ge[tm:tm + halo] * keep_next
            tail_scr[f, :, cs] = ge[tm - halo:tm]
        else:
            g_scr[si][...] = ge
        v_scr[si][...] = _bdot(xm_scr[0:tm, :], wv_ref[:, cs].astype(BF16))

    acts = []
    for si in range(nsub):
        cs = slice(si * sub, (si + 1) * sub)

        def tap_sum(dc):
            tot = None
            for dr in row_taps:
                k = (dr + 1) * 3 + (dc + 1)
                r0 = halo + dr * GRID_W
                term = g_scr[si][r0:r0 + tm, :] * cw_ref[k:k + 1, cs]
                tot = term if tot is None else tot + term
            return tot

        side_scr[si][0, pad:pad + tm, :] = tap_sum(-1)
        side_scr[si][1, pad:pad + tm, :] = tap_sum(1)
        conv = tap_sum(0) + cb_ref[:, cs]
        conv = conv + jnp.where(left_ok, side_scr[si][0, pad - 1:pad - 1 + tm, :], 0.0)
        conv = conv + jnp.where(right_ok, side_scr[si][1, pad + 1:pad + 1 + tm, :], 0.0)
        acts.append((_gelu(conv) * v_scr[si][...]).astype(BF16))
    for si in range(nsub):
        o_ref[...] += _bdot(acts[si], wd_ref[si * sub:(si + 1) * sub, :].astype(BF16))

    @pl.when(f == nf - 1)
    def _():
        o_ref[...] = _layer_norm_rows(alpha * h_ref[...] + gate_ref[...] * o_ref[...], lg_ref[...], lb_ref[...])


def _ffn(h, mods, w_up, conv_w9, conv_b, w_down, ln_g, ln_b, layer, alpha, grid_conv):
    nb, rows, d = h.shape
    fh = w_down.shape[1]
    tf = FFN_COL_TILE
    sub = FFN_SUB_TILE
    nf = fh // tf
    if grid_conv:
        hx = h
        tm, halo = FFN_ROW_TILE, GRID_W
    else:
        hx = h.reshape(1, nb * rows, d)
        tm, halo = nb * rows, 0
    nblk = hx.shape[1] // tm
    hb = tm // GRID_W
    nhalo_blocks = rows // GRID_W

    def vec(k):
        return pl.BlockSpec((None, None, 1, d), lambda b, j, f: (layer, b if grid_conv else nb, 0, k))

    in_specs = [pl.BlockSpec((None, tm, d), lambda b, j, f: (b, j, 0))]
    args = [hx]
    if grid_conv:
        in_specs += [
            pl.BlockSpec((None, halo, d), lambda b, j, f: (b, jnp.minimum((j + 1) * hb, nhalo_blocks - 1), 0)),
        ]
        args += [hx]
    in_specs += [
        vec(3), vec(4), vec(5),
        pl.BlockSpec((None, d, tf), lambda b, j, f: (layer, 0, f)),
        pl.BlockSpec((None, d, tf), lambda b, j, f: (layer, 0, nf + f)),
        pl.BlockSpec((None, 9, tf), lambda b, j, f: (layer, 0, f)),
        pl.BlockSpec((None, 1, tf), lambda b, j, f: (layer, 0, f)),
        pl.BlockSpec((None, tf, d), lambda b, j, f: (layer, f, 0)),
        _layer_spec(ln_g, layer), _layer_spec(ln_b, layer),
    ]
    args += [mods, mods, mods, w_up, w_up, conv_w9, conv_b, w_down, ln_g, ln_b]
    return pl.pallas_call(
        functools.partial(_ffn_kernel, alpha=alpha, grid_conv=grid_conv, halo=halo, seq_len=rows),
        grid=(hx.shape[0], nblk, nf),
        in_specs=in_specs,
        out_specs=pl.BlockSpec((None, tm, d), lambda b, j, f: (b, j, 0)),
        out_shape=jax.ShapeDtypeStruct(hx.shape, F32),
        scratch_shapes=[pltpu.VMEM((tm + halo, d), BF16), pltpu.VMEM((nf, max(halo, SUBLANE), tf), F32)]
        + [pltpu.VMEM((tm + 2 * halo, sub), F32), pltpu.VMEM((tm, sub), F32),
           pltpu.VMEM((2, tm + 2 * SUBLANE, sub), F32)] * (tf // sub),
        compiler_params=_cparams(("arbitrary", "arbitrary", "arbitrary")),
        name="ffn_lat" if grid_conv else "ffn_ctx",
    )(*args).reshape(h.shape)


def kernel(x, c, ctx, c_ctx, w_ada, b_ada, w_in, w_out, ln1_g, ln1_b, ln2_g, ln2_b, s5_a_re, s5_a_im, s5_b_re, s5_b_im, s5_c_re, s5_c_im, s5_log_step, s5_d, s5_glu_w, s5_glu_b, sg_ln_g, sg_ln_b, sg_w, sg_b, pool_w, pool_b, pool_scale, m2_conv_w, m2_conv_b, m2_dt_bias, m2_a_log, m2_d, m2_norm_w, ffn_w_up, ffn_conv_w, ffn_conv_b, ffn_w_down):
    nb, nlat, d = x.shape
    nctx = ctx.shape[1]
    depth = w_ada.shape[0]
    gw = GROUP_W
    alpha = (2 * depth) ** 0.25
    assert nb + 1 <= MOD_ROWS and nlat % FFN_ROW_TILE == 0 and nlat % OUT_ROW_TILE == 0 and nctx % ROW_TILE == 0
    assert nlat % (GRID_W * SUBLANE) == 0 and FFN_ROW_TILE % GRID_W == 0 and nlat % SSD_BLOCK == 0 and nctx % SSD_BLOCK == 0
    assert w_in.shape[2] == 5 * gw + M2_XBC + 2 * M2_HEADS and (5 * gw + M2_XBC) % ROW_TILE == 0

    cmat = jnp.concatenate([c, c_ctx[None, :], jnp.zeros((MOD_ROWS - nb - 1, d), F32)], axis=0)
    mods = _ada(cmat, w_ada, b_ada).reshape(depth, MOD_ROWS, 1, 6 * d)

    def rows(v):
        return v.astype(F32).reshape(depth, 1, -1)

    fh = ffn_w_down.shape[1]
    lane_pad = LANE - 2 * M2_HEADS
    w_in_b = _cast_w_in(w_in)
    w_dt_b = jnp.pad(w_in[:, :, w_in.shape[2] - 2 * M2_HEADS:], ((0, 0), (0, 0), (0, lane_pad))).astype(BF16)
    w_out_b = w_out.astype(BF16)
    cw9 = ffn_conv_w.astype(F32).reshape(depth, 9, fh)
    s5_mats = _s5_mats(s5_a_re, s5_a_im, s5_b_re, s5_b_im, s5_c_re, s5_c_im, s5_log_step, s5_d)
    sg_btile = jnp.repeat(jnp.swapaxes(sg_b.astype(F32), 1, 2), gw // SG_HEADS, axis=2)
    mixer_params = (s5_glu_w.astype(BF16), rows(s5_glu_b), rows(sg_ln_g), rows(sg_ln_b), sg_w.astype(BF16),
                    sg_btile, pool_w.astype(BF16), rows(pool_b), rows(pool_scale))
    m2_a = -jnp.exp(m2_a_log.astype(F32))
    a_dir = (m2_a[:, :, None, :] * jnp.eye(2, dtype=F32)[None, :, :, None]).reshape(depth, 2, 1, 2 * M2_HEADS)
    a_dir = jnp.pad(a_dir, ((0, 0), (0, 0), (0, 0), (0, lane_pad)))
    head_of_ch = jnp.arange(gw) // M2_HEAD_DIM
    expand = (jnp.arange(LANE)[None, :, None] == (jnp.arange(2)[:, None, None] * M2_HEADS + head_of_ch[None, None, :])
              ).astype(BF16)
    ssd_params = (m2_conv_w.astype(F32), rows(m2_conv_b),
                  jnp.pad(m2_dt_bias.astype(F32).reshape(depth, 1, 2 * M2_HEADS), ((0, 0), (0, 0), (0, lane_pad))),
                  a_dir, rows(jnp.repeat(m2_d.astype(F32), M2_HEAD_DIM, axis=1)), expand)
    m2_nw = rows(m2_norm_w)
    ln1 = rows(ln1_g), rows(ln1_b)
    ln2 = rows(ln2_g), rows(ln2_b)
    ffn_cb = rows(ffn_conv_b)

    h_lat, h_ctx = x, ctx
    for i in range(depth):
        need_ctx = i < depth - 1
        pa, pb, pd = _inproj(h_lat, h_ctx, mods, w_in_b, w_dt_b, i)
        ya = _s5(pa, s5_mats, i, nlat)
        mabc = _mixers(ya, pb, mixer_params, i, nlat)
        yf, yr = _ssd(pd, ssd_params, i, nlat)
        mix_in = (mabc, yf, yr, pd, m2_nw)
        h1_lat = _outproj(*mix_in, h_lat, mods, w_out_b, ln1[0], ln1[1], i, nlat, alpha, True)
        if need_ctx:
            h1_ctx = _outproj(*mix_in, h_ctx, mods, w_out_b, ln1[0], ln1[1], i, nlat, alpha, False)
            h_ctx = _ffn(h1_ctx, mods, ffn_w_up, cw9, ffn_cb, ffn_w_down, ln2[0], ln2[1], i, alpha, False)
        h_lat = _ffn(h1_lat, mods, ffn_w_up, cw9, ffn_cb, ffn_w_down, ln2[0], ln2[1], i, alpha, True)
    return h_lat
```
